```python
import math
import jax, jax.numpy as jnp
from jax import lax
import numpy as np

D_MODEL = 2048
BATCH = 8
SEQ = 4096
DEPTH = 1

HEAD_DIM = 64
D_RWKV = D_MODEL // 2
D_SB = D_MODEL - D_RWKV
N_RWKV_HEADS = D_RWKV // HEAD_DIM
N_SB_HEADS = D_SB // HEAD_DIM
D_IN_PROJ = 3 * D_RWKV + 3 * D_SB
DECAY_LORA = 64
AAA_LORA = 64
GATE_LORA = 160
D_FF = ((8 * D_MODEL + 3 * 256 - 1) // (3 * 256)) * 256
SB_BLOCK = 128
RMS_EPS = 1e-6
GN_EPS = 64e-5
L2_EPS = 1e-12

kernel_name = "hymba_rwkv7_stickbreaking_block"


def rms_norm(x, gain):
    xf = x.astype(jnp.float32)
    y = xf * lax.rsqrt(jnp.mean(xf * xf, axis=-1, keepdims=True) + RMS_EPS)
    return (y * gain.astype(jnp.float32)).astype(x.dtype)


def token_shift(x):
    return jnp.pad(x[:, :-1], ((0, 0), (1, 0), (0, 0)))


def rwkv7_time_mix(h, p_rkv, mu_rkv, mu_w, mu_a, mu_g, w0, w1, w2, a0, a1, a2,
                   g1, g2, k_k, k_a, r_k, ln_x_gain, ln_x_bias):
    B, T, _ = h.shape
    H, N = N_RWKV_HEADS, HEAD_DIM
    dh = token_shift(h) - h
    xw = h + dh * mu_w
    xa = h + dh * mu_a
    xg = h + dh * mu_g
    p = p_rkv + (token_shift(p_rkv) - p_rkv) * mu_rkv
    r, k, v = jnp.split(p, 3, axis=-1)
    w_log = -jax.nn.softplus(-(w0 + jnp.tanh(xw @ w1) @ w2)) - 0.5
    decay = jnp.exp(-jnp.exp(w_log.astype(jnp.float32)))
    a = jax.nn.sigmoid(a0 + (xa @ a1) @ a2)
    g = jax.nn.sigmoid(xg @ g1) @ g2
    kk = (k * k_k).reshape(B, T, H, N).astype(jnp.float32)
    kk = kk * lax.rsqrt(jnp.sum(kk * kk, axis=-1, keepdims=True) + L2_EPS)
    k = k * (1.0 + (a - 1.0) * k_a)
    rh = r.reshape(B, T, H, N)
    kh = k.reshape(B, T, H, N)
    vh = v.reshape(B, T, H, N)
    ah = a.reshape(B, T, H, N).astype(jnp.float32)
    wh = decay.reshape(B, T, H, N)

    def step(S, inp):
        r_t, w_t, k_t, v_t, rem_t, wr_t = inp
        sa = jnp.einsum('bhvk,bhk->bhv', S, rem_t)
        S = (S * w_t[:, :, None, :] + sa[..., None] * wr_t[:, :, None, :]
             + v_t[..., None] * k_t[:, :, None, :])
        y = jnp.einsum('bhvk,bhk->bhv', S, r_t)
        return S, y

    xs = tuple(jnp.moveaxis(t.astype(jnp.float32), 1, 0)
               for t in (rh, wh, kh, vh, -kk, kk * ah))
    S0 = jnp.zeros((B, H, N, N), jnp.float32)
    _, y = lax.scan(step, S0, xs)
    y = jnp.moveaxis(y, 0, 1)
    mean = jnp.mean(y, axis=-1, keepdims=True)
    var = jnp.mean(jnp.square(y - mean), axis=-1, keepdims=True)
    y = ((y - mean) * lax.rsqrt(var + GN_EPS)).reshape(B, T, D_RWKV)
    y = (y * ln_x_gain + ln_x_bias).astype(h.dtype)
    bonus = jnp.sum(rh * kh * r_k, axis=-1, keepdims=True) * vh
    return (y + bonus.reshape(B, T, D_RWKV)) * g


def stick_breaking_attention(q, k, v):
    B, H, T, d = q.shape
    nblk = T // SB_BLOCK
    qb = q.reshape(B, H, nblk, SB_BLOCK, d).transpose(2, 0, 1, 3, 4)
    kpos = jnp.arange(T)
    inv_sqrt_d = 1.0 / math.sqrt(d)

    def block(args):
        qi, i = args
        z = jnp.einsum('bhqd,bhkd->bhqk', qi, k).astype(jnp.float32) * inv_sqrt_d
        qpos = i * SB_BLOCK + jnp.arange(SB_BLOCK)
        causal = kpos[None, :] < qpos[:, None]
        log_1m_beta = jnp.where(causal, -jax.nn.softplus(z), 0.0)
        tail = lax.cumsum(log_1m_beta, axis=3, reverse=True) - log_1m_beta
        log_a = jax.nn.log_sigmoid(z) + tail
        attn = jnp.where(causal, jnp.exp(log_a), 0.0)
        return jnp.einsum('bhqk,bhkd->bhqd', attn.astype(v.dtype), v)

    out = lax.map(block, (qb, jnp.arange(nblk)))
    return out.transpose(1, 2, 0, 3, 4).reshape(B, H, T, d)


def _fwd_setup_inputs(seed: int = 0) -> dict:
    key = jax.random.key(seed)
    ks = jax.random.split(key, 32)
    L = DEPTH

    def nrm(k, shape, scale):
        return jax.random.normal(k, shape, jnp.float32) * scale

    def uni(k, shape, lo=0.0, hi=1.0):
        return jax.random.uniform(k, shape, jnp.float32, minval=lo, maxval=hi)

    Dm = D_MODEL
    return {
        "x": nrm(ks[0], (BATCH, SEQ, Dm), 1.0),
        "c": nrm(ks[1], (BATCH, Dm), 1.0),
        "w_ada": nrm(ks[2], (L, Dm, 6 * Dm), 0.5 * Dm ** -0.5),
        "b_ada": nrm(ks[3], (L, 6 * Dm), 0.02),
        "norm1_gain": 1.0 + nrm(ks[4], (L, Dm), 0.05),
        "norm2_gain": 1.0 + nrm(ks[5], (L, Dm), 0.05),
        "w_in": nrm(ks[6], (L, Dm, D_IN_PROJ), Dm ** -0.5),
        "mu_rkv": uni(ks[7], (L, 3 * D_RWKV)),
        "mu_w": uni(ks[8], (L, Dm)),
        "mu_a": uni(ks[9], (L, Dm)),
        "mu_g": uni(ks[10], (L, Dm)),
        "w0": uni(ks[11], (L, D_RWKV), -5.5, -0.5),
        "w1": nrm(ks[12], (L, Dm, DECAY_LORA), Dm ** -0.5),
        "w2": nrm(ks[13], (L, DECAY_LORA, D_RWKV), 0.3 * DECAY_LORA ** -0.5),
        "a0": nrm(ks[14], (L, D_RWKV), 0.1),
        "a1": nrm(ks[15], (L, Dm, AAA_LORA), Dm ** -0.5),
        "a2": nrm(ks[16], (L, AAA_LORA, D_RWKV), 0.3 * AAA_LORA ** -0.5),
        "g1": nrm(ks[17], (L, Dm, GATE_LORA), Dm ** -0.5),
        "g2": nrm(ks[18], (L, GATE_LORA, D_RWKV), GATE_LORA ** -0.5),
        "k_k": 0.85 + nrm(ks[19], (L, D_RWKV), 0.05),
        "k_a": 1.0 + nrm(ks[20], (L, D_RWKV), 0.05),
        "r_k": nrm(ks[21], (L, N_RWKV_HEADS, HEAD_DIM), 0.1),
        "ln_x_gain": 1.0 + nrm(ks[22], (L, D_RWKV), 0.05),
        "ln_x_bias": nrm(ks[23], (L, D_RWKV), 0.02),
        "q_norm_gain": 1.0 + nrm(ks[24], (L, HEAD_DIM), 0.05),
        "k_norm_gain": 1.0 + nrm(ks[25], (L, HEAD_DIM), 0.05),
        "w_out": nrm(ks[26], (L, Dm, Dm), Dm ** -0.5),
        "w_gate_up": nrm(ks[27], (L, Dm, 2 * D_FF), Dm ** -0.5),
        "w_down": nrm(ks[28], (L, D_FF, Dm), D_FF ** -0.5),
    }


def _fwd_reference(x, c, w_ada, b_ada, norm1_gain, norm2_gain, w_in, mu_rkv, mu_w, mu_a,
              mu_g, w0, w1, w2, a0, a1, a2, g1, g2, k_k, k_a, r_k, ln_x_gain,
              ln_x_bias, q_norm_gain, k_norm_gain, w_out, w_gate_up, w_down):
    B, T, _ = x.shape
    c_act = jax.nn.silu(c)
    for l in range(DEPTH):
        mod = c_act @ w_ada[l] + b_ada[l]
        sh1, sc1, gt1, sh2, sc2, gt2 = [m[:, None, :] for m in jnp.split(mod, 6, axis=-1)]

        h = rms_norm(x, norm1_gain[l]) * (1.0 + sc1) + sh1
        p = h @ w_in[l]
        p_rkv, p_sb = p[..., :3 * D_RWKV], p[..., 3 * D_RWKV:]

        y_rwkv = rwkv7_time_mix(h, p_rkv, mu_rkv[l], mu_w[l], mu_a[l], mu_g[l],
                                w0[l], w1[l], w2[l], a0[l], a1[l], a2[l], g1[l], g2[l],
                                k_k[l], k_a[l], r_k[l], ln_x_gain[l], ln_x_bias[l])

        q, k, v = jnp.split(p_sb, 3, axis=-1)
        q = rms_norm(q.reshape(B, T, N_SB_HEADS, HEAD_DIM), q_norm_gain[l])
        k = rms_norm(k.reshape(B, T, N_SB_HEADS, HEAD_DIM), k_norm_gain[l])
        v = v.reshape(B, T, N_SB_HEADS, HEAD_DIM)
        y_sb = stick_breaking_attention(q.transpose(0, 2, 1, 3), k.transpose(0, 2, 1, 3),
                                        v.transpose(0, 2, 1, 3))
        y_sb = y_sb.transpose(0, 2, 1, 3).reshape(B, T, D_SB)

        mix = jnp.concatenate([y_rwkv, y_sb], axis=-1) @ w_out[l]
        x = x + gt1 * mix

        h2 = rms_norm(x, norm2_gain[l]) * (1.0 + sc2) + sh2
        gate, up = jnp.split(h2 @ w_gate_up[l], 2, axis=-1)
        x = x + gt2 * ((jax.nn.silu(gate) * up) @ w_down[l])
    return x


import jax as _jax
import jax.numpy as _jnp

TWIN_FORMAT = 'train_step'
FWD_PARAMS = ['x', 'c', 'w_ada', 'b_ada', 'norm1_gain', 'norm2_gain', 'w_in', 'mu_rkv', 'mu_w', 'mu_a', 'mu_g', 'w0', 'w1', 'w2', 'a0', 'a1', 'a2', 'g1', 'g2', 'k_k', 'k_a', 'r_k', 'ln_x_gain', 'ln_x_bias', 'q_norm_gain', 'k_norm_gain', 'w_out', 'w_gate_up', 'w_down']
TWIN_WEIGHTS = ['w_ada', 'b_ada', 'norm1_gain', 'norm2_gain', 'w_in', 'mu_rkv', 'mu_w', 'mu_a', 'mu_g', 'w0', 'w1', 'w2', 'a0', 'a1', 'a2', 'g1', 'g2', 'k_k', 'k_a', 'r_k', 'ln_x_gain', 'ln_x_bias', 'q_norm_gain', 'k_norm_gain', 'w_out', 'w_gate_up', 'w_down']
TWIN_DIFF_INPUT = 'x'
TWIN_INPUTS = ['x', 'c', 'w_ada', 'b_ada', 'norm1_gain', 'norm2_gain', 'w_in', 'mu_rkv', 'mu_w', 'mu_a', 'mu_g', 'w0', 'w1', 'w2', 'a0', 'a1', 'a2', 'g1', 'g2', 'k_k', 'k_a', 'r_k', 'ln_x_gain', 'ln_x_bias', 'q_norm_gain', 'k_norm_gain', 'w_out', 'w_gate_up', 'w_down', 'loss_target', 'm_w_ada', 'm_b_ada', 'm_norm1_gain', 'm_norm2_gain', 'm_w_in', 'm_mu_rkv', 'm_mu_w', 'm_mu_a', 'm_mu_g', 'm_w0', 'm_w1', 'm_w2', 'm_a0', 'm_a1', 'm_a2', 'm_g1', 'm_g2', 'm_k_k', 'm_k_a', 'm_r_k', 'm_ln_x_gain', 'm_ln_x_bias', 'm_q_norm_gain', 'm_k_norm_gain', 'm_w_out', 'm_w_gate_up', 'm_w_down', 'v_w_ada', 'v_b_ada', 'v_norm1_gain', 'v_norm2_gain', 'v_w_in', 'v_mu_rkv', 'v_mu_w', 'v_mu_a', 'v_mu_g', 'v_w0', 'v_w1', 'v_w2', 'v_a0', 'v_a1', 'v_a2', 'v_g1', 'v_g2', 'v_k_k', 'v_k_a', 'v_r_k', 'v_ln_x_gain', 'v_ln_x_bias', 'v_q_norm_gain', 'v_k_norm_gain', 'v_w_out', 'v_w_gate_up', 'v_w_down']
TWIN_OUTPUTS = ['loss', 'grad_x', 'grad_w_ada', 'grad_b_ada', 'grad_norm1_gain', 'grad_norm2_gain', 'grad_w_in', 'grad_mu_rkv', 'grad_mu_w', 'grad_mu_a', 'grad_mu_g', 'grad_w0', 'grad_w1', 'grad_w2', 'grad_a0', 'grad_a1', 'grad_a2', 'grad_g1', 'grad_g2', 'grad_k_k', 'grad_k_a', 'grad_r_k', 'grad_ln_x_gain', 'grad_ln_x_bias', 'grad_q_norm_gain', 'grad_k_norm_gain', 'grad_w_out', 'grad_w_gate_up', 'grad_w_down', 'delta_w_ada', 'delta_b_ada', 'delta_norm1_gain', 'delta_norm2_gain', 'delta_w_in', 'delta_mu_rkv', 'delta_mu_w', 'delta_mu_a', 'delta_mu_g', 'delta_w0', 'delta_w1', 'delta_w2', 'delta_a0', 'delta_a1', 'delta_a2', 'delta_g1', 'delta_g2', 'delta_k_k', 'delta_k_a', 'delta_r_k', 'delta_ln_x_gain', 'delta_ln_x_bias', 'delta_q_norm_gain', 'delta_k_norm_gain', 'delta_w_out', 'delta_w_gate_up', 'delta_w_down', 'new_m_w_ada', 'new_m_b_ada', 'new_m_norm1_gain', 'new_m_norm2_gain', 'new_m_w_in', 'new_m_mu_rkv', 'new_m_mu_w', 'new_m_mu_a', 'new_m_mu_g', 'new_m_w0', 'new_m_w1', 'new_m_w2', 'new_m_a0', 'new_m_a1', 'new_m_a2', 'new_m_g1', 'new_m_g2', 'new_m_k_k', 'new_m_k_a', 'new_m_r_k', 'new_m_ln_x_gain', 'new_m_ln_x_bias', 'new_m_q_norm_gain', 'new_m_k_norm_gain', 'new_m_w_out', 'new_m_w_gate_up', 'new_m_w_down', 'new_v_w_ada', 'new_v_b_ada', 'new_v_norm1_gain', 'new_v_norm2_gain', 'new_v_w_in', 'new_v_mu_rkv', 'new_v_mu_w', 'new_v_mu_a', 'new_v_mu_g', 'new_v_w0', 'new_v_w1', 'new_v_w2', 'new_v_a0', 'new_v_a1', 'new_v_a2', 'new_v_g1', 'new_v_g2', 'new_v_k_k', 'new_v_k_a', 'new_v_r_k', 'new_v_ln_x_gain', 'new_v_ln_x_bias', 'new_v_q_norm_gain', 'new_v_k_norm_gain', 'new_v_w_out', 'new_v_w_gate_up', 'new_v_w_down']
TWIN_LEAF_KINDS = {'loss': 'loss', 'grad_x': 'grad_x', 'grad_w_ada': 'grad_w', 'grad_b_ada': 'grad_w', 'grad_norm1_gain': 'grad_w', 'grad_norm2_gain': 'grad_w', 'grad_w_in': 'grad_w', 'grad_mu_rkv': 'grad_w', 'grad_mu_w': 'grad_w', 'grad_mu_a': 'grad_w', 'grad_mu_g': 'grad_w', 'grad_w0': 'grad_w', 'grad_w1': 'grad_w', 'grad_w2': 'grad_w', 'grad_a0': 'grad_w', 'grad_a1': 'grad_w', 'grad_a2': 'grad_w', 'grad_g1': 'grad_w', 'grad_g2': 'grad_w', 'grad_k_k': 'grad_w', 'grad_k_a': 'grad_w', 'grad_r_k': 'grad_w', 'grad_ln_x_gain': 'grad_w', 'grad_ln_x_bias': 'grad_w', 'grad_q_norm_gain': 'grad_w', 'grad_k_norm_gain': 'grad_w', 'grad_w_out': 'grad_w', 'grad_w_gate_up': 'grad_w', 'grad_w_down': 'grad_w', 'delta_w_ada': 'delta_w', 'delta_b_ada': 'delta_w', 'delta_norm1_gain': 'delta_w', 'delta_norm2_gain': 'delta_w', 'delta_w_in': 'delta_w', 'delta_mu_rkv': 'delta_w', 'delta_mu_w': 'delta_w', 'delta_mu_a': 'delta_w', 'delta_mu_g': 'delta_w', 'delta_w0': 'delta_w', 'delta_w1': 'delta_w', 'delta_w2': 'delta_w', 'delta_a0': 'delta_w', 'delta_a1': 'delta_w', 'delta_a2': 'delta_w', 'delta_g1': 'delta_w', 'delta_g2': 'delta_w', 'delta_k_k': 'delta_w', 'delta_k_a': 'delta_w', 'delta_r_k': 'delta_w', 'delta_ln_x_gain': 'delta_w', 'delta_ln_x_bias': 'delta_w', 'delta_q_norm_gain': 'delta_w', 'delta_k_norm_gain': 'delta_w', 'delta_w_out': 'delta_w', 'delta_w_gate_up': 'delta_w', 'delta_w_down': 'delta_w', 'new_m_w_ada': 'new_m', 'new_m_b_ada': 'new_m', 'new_m_norm1_gain': 'new_m', 'new_m_norm2_gain': 'new_m', 'new_m_w_in': 'new_m', 'new_m_mu_rkv': 'new_m', 'new_m_mu_w': 'new_m', 'new_m_mu_a': 'new_m', 'new_m_mu_g': 'new_m', 'new_m_w0': 'new_m', 'new_m_w1': 'new_m', 'new_m_w2': 'new_m', 'new_m_a0': 'new_m', 'new_m_a1': 'new_m', 'new_m_a2': 'new_m', 'new_m_g1': 'new_m', 'new_m_g2': 'new_m', 'new_m_k_k': 'new_m', 'new_m_k_a': 'new_m', 'new_m_r_k': 'new_m', 'new_m_ln_x_gain': 'new_m', 'new_m_ln_x_bias': 'new_m', 'new_m_q_norm_gain': 'new_m', 'new_m_k_norm_gain': 'new_m', 'new_m_w_out': 'new_m', 'new_m_w_gate_up': 'new_m', 'new_m_w_down': 'new_m', 'new_v_w_ada': 'new_v', 'new_v_b_ada': 'new_v', 'new_v_norm1_gain': 'new_v', 'new_v_norm2_gain': 'new_v', 'new_v_w_in': 'new_v', 'new_v_mu_rkv': 'new_v', 'new_v_mu_w': 'new_v', 'new_v_mu_a': 'new_v', 'new_v_mu_g': 'new_v', 'new_v_w0': 'new_v', 'new_v_w1': 'new_v', 'new_v_w2': 'new_v', 'new_v_a0': 'new_v', 'new_v_a1': 'new_v', 'new_v_a2': 'new_v', 'new_v_g1': 'new_v', 'new_v_g2': 'new_v', 'new_v_k_k': 'new_v', 'new_v_k_a': 'new_v', 'new_v_r_k': 'new_v', 'new_v_ln_x_gain': 'new_v', 'new_v_ln_x_bias': 'new_v', 'new_v_q_norm_gain': 'new_v', 'new_v_k_norm_gain': 'new_v', 'new_v_w_out': 'new_v', 'new_v_w_gate_up': 'new_v', 'new_v_w_down': 'new_v'}


def _forward(args):
    return _fwd_reference(*[args[k] for k in FWD_PARAMS])


def _output_shape():
    def fwd():
        inp = _fwd_setup_inputs(0)
        return _fwd_reference(*[inp[k] for k in FWD_PARAMS])
    out = _jax.eval_shape(fwd)
    return out.shape, out.dtype

N_MICROBATCH = 1
ADAM_LR = 0.001
ADAM_B1 = 0.9
ADAM_B2 = 0.999
ADAM_EPS = 1e-08
ADAM_WD = 0.01
ADAM_STEP = 10
PER_EXAMPLE_BATCH_AXIS = {'x': 0, 'c': 0, 'loss_target': 0}
SHARED_INPUTS = []
_WEIGHT_DTYPES = {'w_ada': _jnp.float32, 'b_ada': _jnp.float32, 'norm1_gain': _jnp.float32, 'norm2_gain': _jnp.float32, 'w_in': _jnp.float32, 'mu_rkv': _jnp.float32, 'mu_w': _jnp.float32, 'mu_a': _jnp.float32, 'mu_g': _jnp.float32, 'w0': _jnp.float32, 'w1': _jnp.float32, 'w2': _jnp.float32, 'a0': _jnp.float32, 'a1': _jnp.float32, 'a2': _jnp.float32, 'g1': _jnp.float32, 'g2': _jnp.float32, 'k_k': _jnp.float32, 'k_a': _jnp.float32, 'r_k': _jnp.float32, 'ln_x_gain': _jnp.float32, 'ln_x_bias': _jnp.float32, 'q_norm_gain': _jnp.float32, 'k_norm_gain': _jnp.float32, 'w_out': _jnp.float32, 'w_gate_up': _jnp.float32, 'w_down': _jnp.float32}
MOMENT_SCALE = {'w_ada': 3.443742e-01, 'b_ada': 8.973397e-01, 'norm1_gain': 4.015285e-01, 'norm2_gain': 1.570157e+00, 'w_in': 3.910272e-02, 'mu_rkv': 2.092009e-01, 'mu_w': 2.295695e-04, 'mu_a': 3.882525e-03, 'mu_g': 1.741614e-02, 'w0': 1.507185e-02, 'w1': 3.523018e-03, 'w2': 3.026979e-03, 'a0': 3.816340e-02, 'a1': 1.436592e-02, 'a2': 1.194563e-02, 'g1': 1.631595e-01, 'g2': 4.085782e-01, 'k_k': 7.110420e-02, 'k_a': 1.014938e-01, 'r_k': 4.685841e-01, 'ln_x_gain': 7.052518e-01, 'ln_x_bias': 1.165506e-01, 'q_norm_gain': 1.541812e+00, 'k_norm_gain': 1.539554e+00, 'w_out': 4.679405e-02, 'w_gate_up': 2.601477e-02, 'w_down': 3.274211e-02}


def _to_microbatches(a, axis):
    t = _jnp.moveaxis(a, axis, 0)
    t = t.reshape((N_MICROBATCH, t.shape[0] // N_MICROBATCH) + t.shape[1:])
    return _jnp.moveaxis(t, 1, axis + 1)


def setup_inputs(seed: int = 0) -> dict:
    inp = _fwd_setup_inputs(seed)
    key = _jax.random.fold_in(_jax.random.key(seed), 7919)
    shape, _ = _output_shape()
    out = dict(inp)
    out["loss_target"] = _jax.random.normal(_jax.random.fold_in(key, 0), shape, _jnp.float32)
    for i, name in enumerate(TWIN_WEIGHTS):
        w = inp[name].astype(_jnp.float32)
        if MOMENT_SCALE is None:
            s = _jnp.sqrt(_jnp.mean(_jnp.square(w)) + 1e-30)
        else:
            s = MOMENT_SCALE[name]
        km, kv = _jax.random.split(_jax.random.fold_in(key, i + 1))
        out[name] = w
        out["m_" + name] = s * _jax.random.normal(km, w.shape, _jnp.float32)
        out["v_" + name] = (s * s) * _jax.random.uniform(kv, w.shape, _jnp.float32, 0.5, 1.5)
    if N_MICROBATCH > 1:
        for name, axis in PER_EXAMPLE_BATCH_AXIS.items():
            out[name] = _to_microbatches(out[name], axis)
    return {'x': out['x'], 'c': out['c'], 'w_ada': out['w_ada'], 'b_ada': out['b_ada'], 'norm1_gain': out['norm1_gain'], 'norm2_gain': out['norm2_gain'], 'w_in': out['w_in'], 'mu_rkv': out['mu_rkv'], 'mu_w': out['mu_w'], 'mu_a': out['mu_a'], 'mu_g': out['mu_g'], 'w0': out['w0'], 'w1': out['w1'], 'w2': out['w2'], 'a0': out['a0'], 'a1': out['a1'], 'a2': out['a2'], 'g1': out['g1'], 'g2': out['g2'], 'k_k': out['k_k'], 'k_a': out['k_a'], 'r_k': out['r_k'], 'ln_x_gain': out['ln_x_gain'], 'ln_x_bias': out['ln_x_bias'], 'q_norm_gain': out['q_norm_gain'], 'k_norm_gain': out['k_norm_gain'], 'w_out': out['w_out'], 'w_gate_up': out['w_gate_up'], 'w_down': out['w_down'], 'loss_target': out['loss_target'], 'm_w_ada': out['m_w_ada'], 'm_b_ada': out['m_b_ada'], 'm_norm1_gain': out['m_norm1_gain'], 'm_norm2_gain': out['m_norm2_gain'], 'm_w_in': out['m_w_in'], 'm_mu_rkv': out['m_mu_rkv'], 'm_mu_w': out['m_mu_w'], 'm_mu_a': out['m_mu_a'], 'm_mu_g': out['m_mu_g'], 'm_w0': out['m_w0'], 'm_w1': out['m_w1'], 'm_w2': out['m_w2'], 'm_a0': out['m_a0'], 'm_a1': out['m_a1'], 'm_a2': out['m_a2'], 'm_g1': out['m_g1'], 'm_g2': out['m_g2'], 'm_k_k': out['m_k_k'], 'm_k_a': out['m_k_a'], 'm_r_k': out['m_r_k'], 'm_ln_x_gain': out['m_ln_x_gain'], 'm_ln_x_bias': out['m_ln_x_bias'], 'm_q_norm_gain': out['m_q_norm_gain'], 'm_k_norm_gain': out['m_k_norm_gain'], 'm_w_out': out['m_w_out'], 'm_w_gate_up': out['m_w_gate_up'], 'm_w_down': out['m_w_down'], 'v_w_ada': out['v_w_ada'], 'v_b_ada': out['v_b_ada'], 'v_norm1_gain': out['v_norm1_gain'], 'v_norm2_gain': out['v_norm2_gain'], 'v_w_in': out['v_w_in'], 'v_mu_rkv': out['v_mu_rkv'], 'v_mu_w': out['v_mu_w'], 'v_mu_a': out['v_mu_a'], 'v_mu_g': out['v_mu_g'], 'v_w0': out['v_w0'], 'v_w1': out['v_w1'], 'v_w2': out['v_w2'], 'v_a0': out['v_a0'], 'v_a1': out['v_a1'], 'v_a2': out['v_a2'], 'v_g1': out['v_g1'], 'v_g2': out['v_g2'], 'v_k_k': out['v_k_k'], 'v_k_a': out['v_k_a'], 'v_r_k': out['v_r_k'], 'v_ln_x_gain': out['v_ln_x_gain'], 'v_ln_x_bias': out['v_ln_x_bias'], 'v_q_norm_gain': out['v_q_norm_gain'], 'v_k_norm_gain': out['v_k_norm_gain'], 'v_w_out': out['v_w_out'], 'v_w_gate_up': out['v_w_gate_up'], 'v_w_down': out['v_w_down']}


def _loss(weights, diff, rest, loss_target):
    with _jax.named_scope("forward"):
        args = {**rest, TWIN_DIFF_INPUT: diff, **{k: w.astype(_WEIGHT_DTYPES[k]) for k, w in weights.items()}}
        y = _forward(args)
    with _jax.named_scope("loss_head"):
        err = _jnp.square(y.astype(_jnp.float32) - loss_target)
        return 0.5 * _jnp.sum(_jnp.mean(err, axis=-1)) if err.ndim else 0.5 * err


def _adamw(w, g, m, v):
    m = ADAM_B1 * m + (1.0 - ADAM_B1) * g
    v = ADAM_B2 * v + (1.0 - ADAM_B2) * _jnp.square(g)
    m_hat = m / (1.0 - ADAM_B1 ** ADAM_STEP)
    v_hat = v / (1.0 - ADAM_B2 ** ADAM_STEP)
    delta = -ADAM_LR * (m_hat / (_jnp.sqrt(v_hat) + ADAM_EPS) + ADAM_WD * w)
    return delta, m, v


def reference(x, c, w_ada, b_ada, norm1_gain, norm2_gain, w_in, mu_rkv, mu_w, mu_a, mu_g, w0, w1, w2, a0, a1, a2, g1, g2, k_k, k_a, r_k, ln_x_gain, ln_x_bias, q_norm_gain, k_norm_gain, w_out, w_gate_up, w_down, loss_target, m_w_ada, m_b_ada, m_norm1_gain, m_norm2_gain, m_w_in, m_mu_rkv, m_mu_w, m_mu_a, m_mu_g, m_w0, m_w1, m_w2, m_a0, m_a1, m_a2, m_g1, m_g2, m_k_k, m_k_a, m_r_k, m_ln_x_gain, m_ln_x_bias, m_q_norm_gain, m_k_norm_gain, m_w_out, m_w_gate_up, m_w_down, v_w_ada, v_b_ada, v_norm1_gain, v_norm2_gain, v_w_in, v_mu_rkv, v_mu_w, v_mu_a, v_mu_g, v_w0, v_w1, v_w2, v_a0, v_a1, v_a2, v_g1, v_g2, v_k_k, v_k_a, v_r_k, v_ln_x_gain, v_ln_x_bias, v_q_norm_gain, v_k_norm_gain, v_w_out, v_w_gate_up, v_w_down):
    given = dict(x=x, c=c, w_ada=w_ada, b_ada=b_ada, norm1_gain=norm1_gain, norm2_gain=norm2_gain, w_in=w_in, mu_rkv=mu_rkv, mu_w=mu_w, mu_a=mu_a, mu_g=mu_g, w0=w0, w1=w1, w2=w2, a0=a0, a1=a1, a2=a2, g1=g1, g2=g2, k_k=k_k, k_a=k_a, r_k=r_k, ln_x_gain=ln_x_gain, ln_x_bias=ln_x_bias, q_norm_gain=q_norm_gain, k_norm_gain=k_norm_gain, w_out=w_out, w_gate_up=w_gate_up, w_down=w_down, loss_target=loss_target, m_w_ada=m_w_ada, m_b_ada=m_b_ada, m_norm1_gain=m_norm1_gain, m_norm2_gain=m_norm2_gain, m_w_in=m_w_in, m_mu_rkv=m_mu_rkv, m_mu_w=m_mu_w, m_mu_a=m_mu_a, m_mu_g=m_mu_g, m_w0=m_w0, m_w1=m_w1, m_w2=m_w2, m_a0=m_a0, m_a1=m_a1, m_a2=m_a2, m_g1=m_g1, m_g2=m_g2, m_k_k=m_k_k, m_k_a=m_k_a, m_r_k=m_r_k, m_ln_x_gain=m_ln_x_gain, m_ln_x_bias=m_ln_x_bias, m_q_norm_gain=m_q_norm_gain, m_k_norm_gain=m_k_norm_gain, m_w_out=m_w_out, m_w_gate_up=m_w_gate_up, m_w_down=m_w_down, v_w_ada=v_w_ada, v_b_ada=v_b_ada, v_norm1_gain=v_norm1_gain, v_norm2_gain=v_norm2_gain, v_w_in=v_w_in, v_mu_rkv=v_mu_rkv, v_mu_w=v_mu_w, v_mu_a=v_mu_a, v_mu_g=v_mu_g, v_w0=v_w0, v_w1=v_w1, v_w2=v_w2, v_a0=v_a0, v_a1=v_a1, v_a2=v_a2, v_g1=v_g1, v_g2=v_g2, v_k_k=v_k_k, v_k_a=v_k_a, v_r_k=v_r_k, v_ln_x_gain=v_ln_x_gain, v_ln_x_bias=v_ln_x_bias, v_q_norm_gain=v_q_norm_gain, v_k_norm_gain=v_k_norm_gain, v_w_out=v_w_out, v_w_gate_up=v_w_gate_up, v_w_down=v_w_down)
    weights = {n: given[n] for n in TWIN_WEIGHTS}
    shared = {n: given[n] for n in SHARED_INPUTS}
    per_example = {n: given[n] for n in ['x', 'c']}
    grad_fn = _jax.value_and_grad(_loss, argnums=(0, 1))

    def one_microbatch(ex, loss_target):
        ex = dict(ex)
        diff = ex.pop(TWIN_DIFF_INPUT)
        return grad_fn(weights, diff, {**shared, **ex}, loss_target)

    if N_MICROBATCH == 1:
        loss, (grad_w, grad_x) = one_microbatch(per_example, given["loss_target"])
    else:
        def body(carry, xs):
            loss_sum, grad_sum = carry
            l_k, (gw_k, gx_k) = one_microbatch(xs[0], xs[1])
            with _jax.named_scope("update"):
                return (loss_sum + l_k, _jax.tree.map(_jnp.add, grad_sum, gw_k)), gx_k

        init = (_jnp.zeros((), _jnp.float32), _jax.tree.map(_jnp.zeros_like, weights))
        (loss, grad_w), grad_x = _jax.lax.scan(body, init, (per_example, given["loss_target"]))
    with _jax.named_scope("update"):
        delta_w, new_m, new_v = {}, {}, {}
        for n in TWIN_WEIGHTS:
            delta_w[n], new_m[n], new_v[n] = _adamw(weights[n], grad_w[n], given["m_" + n], given["v_" + n])
    return (loss, grad_x, *[grad_w[n] for n in TWIN_WEIGHTS], *[delta_w[n] for n in TWIN_WEIGHTS],
            *[new_m[n] for n in TWIN_WEIGHTS], *[new_v[n] for n in TWIN_WEIGHTS])
```

```python
import functools
import math

import jax
import jax.numpy as jnp
from jax import lax
from jax.experimental import pallas as pl
from jax.experimental.pallas import tpu as pltpu

F32 = jnp.float32
BF16 = jnp.bfloat16
HEAD_DIM = 64
LANES = 128
RMS_EPS = 1e-6
GN_EPS = 64e-5
L2_EPS = 1e-12
ADAM_LR, ADAM_B1, ADAM_B2, ADAM_EPS, ADAM_WD, ADAM_STEP = 0.001, 0.9, 0.999, 1e-08, 0.01, 10
VMEM_LIMIT = 56 * 1024 * 1024
MESH = pl.DeviceIdType.MESH
HI = lax.Precision.HIGHEST
N_CHIPS = 4
N_DEV = 8
XY_MASKS = ((1, 0), (0, 1), (1, 1))


def _pick(dim, prefs):
    for p in prefs:
        if dim % p == 0:
            return p
    return dim


def _params(sem=None, vmem=VMEM_LIMIT):
    return pltpu.CompilerParams(dimension_semantics=sem, vmem_limit_bytes=vmem)


def _sigmoid(x):
    return 1.0 / (1.0 + jnp.exp(-x))


@jax.custom_vjp
def _softplus(x):
    return jnp.maximum(x, 0.0) + jnp.log(1.0 + jnp.exp(-jnp.abs(x)))


_softplus.defvjp(lambda x: (_softplus(x), x), lambda x, g: (g * _sigmoid(x),))


def _silu(x):
    return x * _sigmoid(x)


@jax.custom_vjp
def _bdot(a, b):
    return jnp.dot(a.astype(BF16), b.astype(BF16), preferred_element_type=F32)


def _bdot_bwd(res, g):
    a, b = res
    gb = g.astype(BF16)
    da = lax.dot_general(gb, b.astype(BF16), (((1,), (1,)), ((), ())), preferred_element_type=F32)
    db = lax.dot_general(a.astype(BF16), gb, (((0,), (0,)), ((), ())), preferred_element_type=F32)
    return da.astype(a.dtype), db.astype(b.dtype)


_bdot.defvjp(lambda a, b: (_bdot(a, b), (a, b)), _bdot_bwd)


def _head_ones():
    i = lax.broadcasted_iota(jnp.int32, (LANES, LANES), 0) // HEAD_DIM
    j = lax.broadcasted_iota(jnp.int32, (LANES, LANES), 1) // HEAD_DIM
    return (i == j).astype(F32)


def _hdot(x, ones):
    return jnp.dot(x, ones, precision=HI, preferred_element_type=F32)


@jax.custom_vjp
def _segsum(x):
    ones = _head_ones()
    parts = [_hdot(x[:, LANES * j:LANES * (j + 1)], ones) for j in range(x.shape[1] // LANES)]
    return parts[0] if len(parts) == 1 else jnp.concatenate(parts, axis=1)


_segsum.defvjp(lambda x: (_segsum(x), None), lambda _, g: (_segsum(g),))


def _rms(x, gain):
    return x * lax.rsqrt(jnp.mean(x * x, axis=-1, keepdims=True) + RMS_EPS) * gain


def _matmul(name, a, b, *, ta=False, tb=False, b_shards=False, out_shards=False, out_dtype=F32,
            tm=512, tn=512, tk=512):
    if ta:
        kdim, m = a.shape
    else:
        m, kdim = a.shape
    if b_shards:
        if tb:
            n, ks = b.shape[1], b.shape[2]
            assert ks * N_CHIPS == kdim
        else:
            ns = b.shape[2]
            n = ns * N_CHIPS
            assert b.shape[1] == kdim
    else:
        n = b.shape[0] if tb else b.shape[1]
    tm = _pick(m, (tm, 512, 256, 128))
    n_part = n // N_CHIPS if (out_shards or (b_shards and not tb)) else n
    tn = _pick(n_part, (tn, 512, 256, 128))
    k_part = kdim // N_CHIPS if (b_shards and tb) else kdim
    tk = _pick(k_part, (tk, 512, 256, 128))
    nb = n_part // tn
    kb = k_part // tk
    nk = kdim // tk
    grid = (m // tm, n // tn, nk)

    if ta:
        a_spec = pl.BlockSpec((tk, tm), lambda i, j, k: (k, i))
    else:
        a_spec = pl.BlockSpec((tm, tk), lambda i, j, k: (i, k))
    if b_shards and tb:
        b_spec = pl.BlockSpec((None, tn, tk), lambda i, j, k: (k // kb, j, k % kb))
    elif b_shards:
        b_spec = pl.BlockSpec((None, tk, tn), lambda i, j, k: (j // nb, k, j % nb))
    elif tb:
        b_spec = pl.BlockSpec((tn, tk), lambda i, j, k: (j, k))
    else:
        b_spec = pl.BlockSpec((tk, tn), lambda i, j, k: (k, j))
    if out_shards:
        o_spec = pl.BlockSpec((None, tm, tn), lambda i, j, k: (j // nb, i, j % nb))
        o_shape = jax.ShapeDtypeStruct((N_CHIPS, m, n_part), out_dtype)
    else:
        o_spec = pl.BlockSpec((tm, tn), lambda i, j, k: (i, j))
        o_shape = jax.ShapeDtypeStruct((m, n), out_dtype)
    dims = (((0 if ta else 1,), (1 if tb else 0,)), ((), ()))

    def body(a_ref, b_ref, o_ref, acc_ref):
        k = pl.program_id(2)

        @pl.when(k == 0)
        def _():
            acc_ref[...] = jnp.zeros_like(acc_ref)

        acc_ref[...] += lax.dot_general(a_ref[...].astype(BF16), b_ref[...].astype(BF16), dims,
                                        preferred_element_type=F32)

        @pl.when(k == nk - 1)
        def _():
            o_ref[...] = acc_ref[...].astype(o_ref.dtype)

    return pl.pallas_call(
        body, name=name, grid=grid, in_specs=[a_spec, b_spec], out_specs=o_spec, out_shape=o_shape,
        scratch_shapes=[pltpu.VMEM((tm, tn), F32)],
        compiler_params=_params(("parallel", "parallel", "arbitrary")),
    )(a, b)


def _row_in(spec, tile):
    if isinstance(spec, tuple):
        arr, width, cb = spec
    else:
        arr, width, cb = spec, spec.shape[1], 0
    return arr, pl.BlockSpec((tile, width), lambda i, cb=cb: (i, cb))


def _full_spec(arr):
    nd = arr.ndim
    return pl.BlockSpec(arr.shape, lambda i, nd=nd: (0,) * nd)


def _rowwise(name, fn, rows, params, outs, tile):
    t = (rows[0][0] if isinstance(rows[0], tuple) else rows[0]).shape[0]
    tile = _pick(t, (tile,))
    arrs, specs = zip(*[_row_in(s, tile) for s in rows])
    nr, npar = len(rows), len(params)

    def body(*refs):
        rv = [r[...].astype(F32) for r in refs[:nr]]
        pv = [p[...] for p in refs[nr:nr + npar]]
        res = fn(*rv, *pv)
        for o_ref, val in zip(refs[nr + npar:], res):
            o_ref[...] = val.astype(o_ref.dtype)

    return pl.pallas_call(
        body, name=name, grid=(t // tile,),
        in_specs=list(specs) + [_full_spec(p) for p in params],
        out_specs=[pl.BlockSpec((tile, w), lambda i: (i, 0)) for w, _ in outs],
        out_shape=[jax.ShapeDtypeStruct((t, w), d) for w, d in outs],
        compiler_params=_params(("parallel",)),
    )(*arrs, *params)


def _rowwise_bwd(name, fn, rows, params, cts, row_grads, param_grads, tile, add_to_first=()):
    t = (rows[0][0] if isinstance(rows[0], tuple) else rows[0]).shape[0]
    tile = _pick(t, (tile,))
    arrs, specs = zip(*[_row_in(s, tile) for s in rows])
    n_add = len(add_to_first)
    flat_cts = [c for group in cts for c in group] + list(add_to_first)
    c_arrs, c_specs = zip(*[_row_in(s, tile) for s in flat_cts])
    nr, npar, nc = len(rows), len(params), len(flat_cts)
    rg_idx = [i for i, d in enumerate(row_grads) if d is not None]
    pg_idx = [i for i, d in enumerate(param_grads) if d]

    def body(*refs):
        rv = [r[...].astype(F32) for r in refs[:nr]]
        pv = [p[...] for p in refs[nr:nr + npar]]
        cv = [c[...].astype(F32) for c in refs[nr + npar:nr + npar + nc]]
        o_refs = refs[nr + npar + nc:]
        outs, vjp = jax.vjp(fn, *rv, *pv)
        ct, pos = [], 0
        for group, o in zip(cts, outs):
            if group:
                acc = cv[pos]
                for extra in cv[pos + 1:pos + len(group)]:
                    acc = acc + extra
                pos += len(group)
            else:
                acc = jnp.zeros_like(o)
            ct.append(acc)
        grads = list(vjp(tuple(ct)))
        for extra in cv[nc - n_add:]:
            grads[rg_idx[0]] = grads[rg_idx[0]] + extra
        for o_ref, i in zip(o_refs[:len(rg_idx)], rg_idx):
            o_ref[...] = grads[i].astype(o_ref.dtype)
        first = pl.program_id(0) == 0
        for o_ref, i in zip(o_refs[len(rg_idx):], pg_idx):
            g = grads[nr + i].astype(F32)

            @pl.when(first)
            def _(o_ref=o_ref, g=g):
                o_ref[...] = g

            @pl.when(jnp.logical_not(first))
            def _(o_ref=o_ref, g=g):
                o_ref[...] += g

    def width(i):
        s = rows[i]
        return s[1] if isinstance(s, tuple) else s.shape[1]

    out_specs = [pl.BlockSpec((tile, width(i)), lambda i_: (i_, 0)) for i in rg_idx]
    out_shape = [jax.ShapeDtypeStruct((t, width(i)), row_grads[i]) for i in rg_idx]
    out_specs += [_full_spec(params[i]) for i in pg_idx]
    out_shape += [jax.ShapeDtypeStruct(params[i].shape, F32) for i in pg_idx]
    res = pl.pallas_call(
        body, name=name, grid=(t // tile,),
        in_specs=list(specs) + [_full_spec(p) for p in params] + list(c_specs),
        out_specs=out_specs, out_shape=out_shape,
        compiler_params=_params(("arbitrary",)),
    )(*arrs, *params, *c_arrs)
    return res[:len(rg_idx)], res[len(rg_idx):]


def _fn_norm1(x, gain, sc, sh):
    h = _rms(x, gain) * (1.0 + sc) + sh
    return h, h


def _fn_rwkv_pre(h, hp, p, pp, mu_rkv, mu_w, mu_a, mu_g, w0, a0, k_k, k_a, w1, w2, a1, a2, g1, g2):
    d = p.shape[1] // 3
    dh = hp - h
    xw = h + dh * mu_w
    xa = h + dh * mu_a
    xg = h + dh * mu_g
    pr = p + (pp - p) * mu_rkv
    r, k, v = pr[:, :d], pr[:, d:2 * d], pr[:, 2 * d:]
    w_log = -_softplus(-(w0 + _bdot(jnp.tanh(_bdot(xw, w1)), w2))) - 0.5
    decay = jnp.exp(-jnp.exp(w_log))
    a = _sigmoid(a0 + _bdot(_bdot(xa, a1), a2))
    g = _bdot(_sigmoid(_bdot(xg, g1)), g2)
    kk = k * k_k
    kk = kk * lax.rsqrt(_segsum(kk * kk) + L2_EPS)
    k2 = k * (1.0 + (a - 1.0) * k_a)
    return r, decay, k2, v, -kk, kk * a, g


def _fn_rwkv_post(y, r, k2, v, g, ln_g, ln_b, r_k):
    inv = 1.0 / HEAD_DIM
    mean = _segsum(y) * inv
    yc = y - mean
    var = _segsum(yc * yc) * inv
    yn = yc * lax.rsqrt(var + GN_EPS) * ln_g + ln_b
    bonus = _segsum(r * k2 * r_k) * v
    return ((yn + bonus) * g,)


def _fn_qk_norm(p, qg, kg):
    d = p.shape[1] // 3
    q, k, v = p[:, :d], p[:, d:2 * d], p[:, 2 * d:]
    inv = 1.0 / HEAD_DIM
    qn = q * lax.rsqrt(_segsum(q * q) * inv + RMS_EPS) * qg
    kn = k * lax.rsqrt(_segsum(k * k) * inv + RMS_EPS) * kg
    return qn, kn, v


def _fn_mix_norm2(x, mix, gt1, gain, sc, sh):
    x1 = x + gt1 * mix
    h2 = _rms(x1, gain) * (1.0 + sc) + sh
    return x1, h2


def _fn_swiglu(gate, up):
    return (_silu(gate) * up,)


def _loss_head(name, x1, dn, target, gt2, tile=256):
    t, d = x1.shape
    tile = _pick(t, (tile,))

    def body(x1_ref, dn_ref, tg_ref, gt_ref, loss_ref, dout_ref, ddn_ref, dgt_ref):
        dnv = dn_ref[...]
        gt = gt_ref[...]
        err = x1_ref[...] + gt * dnv - tg_ref[...]
        dout = err * (1.0 / d)
        dout_ref[...] = dout
        ddn_ref[...] = (dout * gt).astype(ddn_ref.dtype)
        part = 0.5 * jnp.sum(jnp.sum(err * dout, axis=-1, keepdims=True), axis=0, keepdims=True)
        dgt = jnp.sum(dout * dnv, axis=0, keepdims=True)
        first = pl.program_id(0) == 0

        @pl.when(first)
        def _():
            loss_ref[...] = jnp.broadcast_to(part, loss_ref.shape)
            dgt_ref[...] = dgt

        @pl.when(jnp.logical_not(first))
        def _():
            loss_ref[...] += jnp.broadcast_to(part, loss_ref.shape)
            dgt_ref[...] += dgt

    row = pl.BlockSpec((tile, d), lambda i: (i, 0))
    vec = pl.BlockSpec((1, d), lambda i: (0, 0))
    return pl.pallas_call(
        body, name=name, grid=(t // tile,),
        in_specs=[row, row, row, vec],
        out_specs=[pl.BlockSpec((1, LANES), lambda i: (0, 0)), row, row, vec],
        out_shape=[jax.ShapeDtypeStruct((1, LANES), F32), jax.ShapeDtypeStruct((t, d), F32),
                   jax.ShapeDtypeStruct((t, d), BF16), jax.ShapeDtypeStruct((1, d), F32)],
        compiler_params=_params(("arbitrary",)),
    )(x1, dn, target, gt2)


SCAN_BLOCK = 32


def _pair_eye():
    i = lax.broadcasted_iota(jnp.int32, (HEAD_DIM, LANES), 0)
    j = lax.broadcasted_iota(jnp.int32, (HEAD_DIM, LANES), 1) % HEAD_DIM
    return (i == j).astype(F32)


def _rwkv_step(s, r, w, k, v, rem, wr, ones, eye):
    sa = _hdot(s * rem, ones)
    vcol = _hdot(eye * v, ones)
    s2 = s * w + sa * wr + vcol * k
    ycol = _hdot(s2 * r, ones)
    y = jnp.sum(eye * ycol, axis=0, keepdims=True)
    return s2, y


def _scan_fwd(ins):
    t, npair, _ = ins[0].shape
    tb = _pick(t, (SCAN_BLOCK,))

    def body(r_ref, w_ref, k_ref, v_ref, rem_ref, wr_ref, y_ref, hist_ref, s_ref):
        @pl.when(pl.program_id(0) == 0)
        def _():
            s_ref[...] = jnp.zeros_like(s_ref)

        ones, eye = _head_ones(), _pair_eye()

        def step(i, carry):
            for p in range(npair):
                s = s_ref[p]
                hist_ref[i, p] = s
                row = [ref[i, pl.ds(p, 1), :] for ref in (r_ref, w_ref, k_ref, v_ref, rem_ref, wr_ref)]
                s2, y = _rwkv_step(s, *row, ones, eye)
                s_ref[p] = s2
                y_ref[i, pl.ds(p, 1), :] = y
            return carry

        lax.fori_loop(0, tb, step, 0)

    blk = pl.BlockSpec((tb, npair, LANES), lambda i: (i, 0, 0))
    return pl.pallas_call(
        body, name="rwkv_scan_fwd", grid=(t // tb,),
        in_specs=[blk] * 6,
        out_specs=[blk, pl.BlockSpec((tb, npair, HEAD_DIM, LANES), lambda i: (i, 0, 0, 0))],
        out_shape=[jax.ShapeDtypeStruct((t, npair, LANES), F32),
                   jax.ShapeDtypeStruct((t, npair, HEAD_DIM, LANES), F32)],
        scratch_shapes=[pltpu.VMEM((npair, HEAD_DIM, LANES), F32)],
        compiler_params=_params(("arbitrary",)),
    )(*ins)


def _scan_bwd(ins, hist, dy):
    t, npair, _ = ins[0].shape
    tb = _pick(t, (SCAN_BLOCK,))
    nblk = t // tb

    def body(r_ref, w_ref, k_ref, v_ref, rem_ref, wr_ref, hist_ref, dy_ref,
             dr_ref, dw_ref, dk_ref, dv_ref, drem_ref, dwr_ref, ds_ref):
        @pl.when(pl.program_id(0) == 0)
        def _():
            ds_ref[...] = jnp.zeros_like(ds_ref)

        ones, eye = _head_ones(), _pair_eye()
        step_fn = functools.partial(_rwkv_step, ones=ones, eye=eye)
        in_refs = (r_ref, w_ref, k_ref, v_ref, rem_ref, wr_ref)
        out_refs = (dr_ref, dw_ref, dk_ref, dv_ref, drem_ref, dwr_ref)

        def step(j, carry):
            i = tb - 1 - j
            for p in range(npair):
                row = [ref[i, pl.ds(p, 1), :] for ref in in_refs]
                _, vjp = jax.vjp(step_fn, hist_ref[i, p], *row)
                grads = vjp((ds_ref[p], dy_ref[i, pl.ds(p, 1), :]))
                ds_ref[p] = grads[0]
                for o_ref, g in zip(out_refs, grads[1:]):
                    o_ref[i, pl.ds(p, 1), :] = g
            return carry

        lax.fori_loop(0, tb, step, 0)

    blk = pl.BlockSpec((tb, npair, LANES), lambda i: (nblk - 1 - i, 0, 0))
    return pl.pallas_call(
        body, name="rwkv_scan_bwd", grid=(nblk,),
        in_specs=[blk] * 6 + [pl.BlockSpec((tb, npair, HEAD_DIM, LANES), lambda i: (nblk - 1 - i, 0, 0, 0)), blk],
        out_specs=[blk] * 6,
        out_shape=[jax.ShapeDtypeStruct((t, npair, LANES), F32)] * 6,
        scratch_shapes=[pltpu.VMEM((npair, HEAD_DIM, LANES), F32)],
        compiler_params=_params(("arbitrary",)),
    )(*ins, hist, dy)


SB_BQ = 256
SB_BK = 128
NT_DIMS = (((1,), (1,)), ((), ()))
TN_DIMS = (((0,), (0,)), ((), ()))


def _dot3(x, tri):
    hi = x.astype(BF16)
    r1 = x - hi.astype(F32)
    mid = r1.astype(BF16)
    lo = (r1 - mid.astype(F32)).astype(BF16)
    out = jnp.dot(hi, tri, preferred_element_type=F32)
    out += jnp.dot(mid, tri, preferred_element_type=F32)
    out += jnp.dot(lo, tri, preferred_element_type=F32)
    return out


def _sb_scores(q, kb, qpos, j, bk):
    z = lax.dot_general(q, kb, NT_DIMS, preferred_element_type=F32) * (1.0 / math.sqrt(HEAD_DIM))
    kpos = j * bk + lax.broadcasted_iota(jnp.int32, (1, bk), 1)
    mask = kpos < qpos
    sp = _softplus(z)
    return z, mask, sp, jnp.where(mask, -sp, 0.0)


def _sb_fwd(q, k, v):
    h, t, d = q.shape
    bq = _pick(t, (SB_BQ,))
    bk = _pick(bq, (SB_BK,))

    def body(q_ref, k_ref, v_ref, o_ref, l_ref):
        qi = pl.program_id(1)
        qv = q_ref[0]
        qpos = qi * bq + lax.broadcasted_iota(jnp.int32, (bq, 1), 0)
        ri = lax.broadcasted_iota(jnp.int32, (bk, bk), 0)
        ci = lax.broadcasted_iota(jnp.int32, (bk, bk), 1)
        tri_ge = (ri >= ci).astype(BF16)
        nkb = (qi + 1) * (bq // bk)

        def kstep(n, carry):
            acc, tailc = carry
            j = nkb - 1 - n
            rows = pl.ds(pl.multiple_of(j * bk, bk), bk)
            kb, vb = k_ref[0, rows, :], v_ref[0, rows, :]
            z, mask, sp, l1m = _sb_scores(qv, kb, qpos, j, bk)
            cs = _dot3(l1m, tri_ge)
            a = jnp.where(mask, jnp.exp(z - sp + (cs - l1m + tailc)), 0.0)
            acc = acc + jnp.dot(a.astype(BF16), vb, preferred_element_type=F32)
            return acc, tailc + cs[:, 0:1]

        acc, tailc = lax.fori_loop(0, nkb, kstep, (jnp.zeros((bq, d), F32), jnp.zeros((bq, 1), F32)))
        o_ref[0] = acc
        l_ref[0] = jnp.broadcast_to(tailc, (bq, d))

    qs = pl.BlockSpec((1, bq, d), lambda hh, i: (hh, i, 0))
    ks = pl.BlockSpec((1, t, d), lambda hh, i: (hh, 0, 0))
    return pl.pallas_call(
        body, name="sb_attn_fwd", grid=(h, t // bq),
        in_specs=[qs, ks, ks], out_specs=[qs, qs],
        out_shape=[jax.ShapeDtypeStruct((h, t, d), F32)] * 2,
        compiler_params=_params(("parallel", "arbitrary")),
    )(q, k, v)


def _sb_bwd(q, k, v, lsum, do):
    h, t, d = q.shape
    bq = _pick(t, (SB_BQ,))
    bk = _pick(bq, (SB_BK,))

    def body(q_ref, k_ref, v_ref, l_ref, do_ref, dq_ref, dk_ref, dv_ref):
        qi = pl.program_id(1)

        @pl.when(qi == 0)
        def _():
            dk_ref[...] = jnp.zeros_like(dk_ref)
            dv_ref[...] = jnp.zeros_like(dv_ref)

        qv = q_ref[0]
        dov = do_ref[0]
        dob = dov.astype(BF16)
        ltot = l_ref[0][:, 0:1]
        qpos = qi * bq + lax.broadcasted_iota(jnp.int32, (bq, 1), 0)
        ri = lax.broadcasted_iota(jnp.int32, (bk, bk), 0)
        ci = lax.broadcasted_iota(jnp.int32, (bk, bk), 1)
        tri_le = (ri <= ci).astype(BF16)
        tri_lt = (ri < ci).astype(BF16)
        nkb = (qi + 1) * (bq // bk)

        def kstep(j, carry):
            dq, pc, ec = carry
            rows = pl.ds(pl.multiple_of(j * bk, bk), bk)
            kb, vb = k_ref[0, rows, :], v_ref[0, rows, :]
            z, mask, sp, l1m = _sb_scores(qv, kb, qpos, j, bk)
            pin = _dot3(l1m, tri_le) + pc
            a = jnp.where(mask, jnp.exp(z - sp + (ltot - pin)), 0.0)
            da = lax.dot_general(dob, vb, NT_DIMS, preferred_element_type=F32)
            e = a * da
            epre = _dot3(e, tri_lt) + ec
            dz = jnp.where(mask, e * jnp.exp(-sp) - jnp.exp(z - sp) * epre, 0.0)
            dzb = (dz * (1.0 / math.sqrt(HEAD_DIM))).astype(BF16)
            dq = dq + jnp.dot(dzb, kb, preferred_element_type=F32)
            dk_ref[0, rows, :] += lax.dot_general(dzb, qv, TN_DIMS, preferred_element_type=F32)
            dv_ref[0, rows, :] += lax.dot_general(a.astype(BF16), dob, TN_DIMS, preferred_element_type=F32)
            pc = pc + jnp.sum(l1m, axis=1, keepdims=True)
            ec = ec + jnp.sum(e, axis=1, keepdims=True)
            return dq, pc, ec

        zcol = jnp.zeros((bq, 1), F32)
        dq, _, _ = lax.fori_loop(0, nkb, kstep, (jnp.zeros((bq, d), F32), zcol, zcol))
        dq_ref[0] = dq

    qs = pl.BlockSpec((1, bq, d), lambda hh, i: (hh, i, 0))
    ks = pl.BlockSpec((1, t, d), lambda hh, i: (hh, 0, 0))
    return pl.pallas_call(
        body, name="sb_attn_bwd", grid=(h, t // bq),
        in_specs=[qs, ks, ks, qs, qs], out_specs=[qs, ks, ks],
        out_shape=[jax.ShapeDtypeStruct((h, t, d), F32)] * 3,
        compiler_params=_params(("parallel", "arbitrary")),
    )(q, k, v, lsum, do)


ANY = pl.BlockSpec(memory_space=pl.ANY)
IN_VMEM = pl.BlockSpec(memory_space=pltpu.VMEM)


def _coords():
    return lax.axis_index("x"), lax.axis_index("y"), lax.axis_index("c")


def _flip(v, bit):
    return 1 - v if bit else v


def _remote(src, dst, send_sem, recv_sem, device):
    return pltpu.make_async_remote_copy(src_ref=src, dst_ref=dst, send_sem=send_sem, recv_sem=recv_sem,
                                        device_id=device, device_id_type=MESH)


def _all_gather8(name, blk):
    m, n = blk.shape

    def body(x_ref, o_ref, send_sems, recv_sems, local_sem):
        x, y, c = _coords()
        own = pltpu.make_async_copy(x_ref, o_ref.at[4 * x + 2 * y + c], local_sem)
        own.start()
        peers = []
        for bits in range(1, N_DEV):
            px, py, pc = _flip(x, (bits >> 2) & 1), _flip(y, (bits >> 1) & 1), _flip(c, bits & 1)
            peers.append((px, py, pc))
        sends = []
        for k, peer in enumerate(peers):
            cp = _remote(x_ref, o_ref.at[4 * x + 2 * y + c], send_sems.at[k], recv_sems.at[k], peer)
            cp.start()
            sends.append(cp)
        for k, (px, py, pc) in enumerate(peers):
            slot = o_ref.at[4 * px + 2 * py + pc]
            _remote(slot, slot, send_sems.at[k], recv_sems.at[k], (px, py, pc)).wait_recv()
        for cp in sends:
            cp.wait_send()
        own.wait()

    return pl.pallas_call(
        body, name=name, out_shape=jax.ShapeDtypeStruct((N_DEV, m, n), blk.dtype),
        in_specs=[IN_VMEM], out_specs=IN_VMEM,
        scratch_shapes=[pltpu.SemaphoreType.DMA((N_DEV - 1,)), pltpu.SemaphoreType.DMA((N_DEV - 1,)),
                        pltpu.SemaphoreType.DMA],
        compiler_params=_params(),
    )(blk)


def _gather_weights(shards):
    nw = len(shards)

    def body(*refs):
        ins, outs = refs[:nw], refs[nw:2 * nw]
        ici_send, ici_recv, d2d_send, d2d_recv, local_sems = refs[2 * nw:]
        x, y, c = _coords()
        chip = 2 * x + y
        sibling = (x, y, 1 - c)
        started = []
        for w in range(nw):
            cp = pltpu.make_async_copy(ins[w], outs[w].at[chip], local_sems.at[w])
            cp.start()
            started.append(cp)
        for w in range(nw):
            half = shards[w].shape[0] // 2
            mine = pl.ds(c * half, half)
            for j, (a, b) in enumerate(XY_MASKS):
                cp = _remote(ins[w].at[mine], outs[w].at[chip, mine], ici_send.at[3 * w + j],
                             ici_recv.at[3 * w + j], (_flip(x, a), _flip(y, b), c))
                cp.start()
                started.append(cp)
        for w in range(nw):
            half = shards[w].shape[0] // 2
            mine = pl.ds(c * half, half)
            for j, (a, b) in enumerate(XY_MASKS):
                px, py = _flip(x, a), _flip(y, b)
                landed = outs[w].at[2 * px + py, mine]
                _remote(landed, landed, ici_send.at[3 * w + j], ici_recv.at[3 * w + j], (px, py, c)).wait_recv()
                cp = _remote(landed, landed, d2d_send.at[3 * w + j], d2d_recv.at[3 * w + j], sibling)
                cp.start()
                started.append(cp)
        for w in range(nw):
            half = shards[w].shape[0] // 2
            theirs = pl.ds((1 - c) * half, half)
            for j, (a, b) in enumerate(XY_MASKS):
                slot = outs[w].at[2 * _flip(x, a) + _flip(y, b), theirs]
                _remote(slot, slot, d2d_send.at[3 * w + j], d2d_recv.at[3 * w + j], sibling).wait_recv()
        for cp in started[nw:]:
            cp.wait_send()
        for cp in started[:nw]:
            cp.wait()

    return pl.pallas_call(
        body, name="gather_weights",
        out_shape=[jax.ShapeDtypeStruct((N_CHIPS,) + s.shape, s.dtype) for s in shards],
        in_specs=[ANY] * nw, out_specs=[ANY] * nw,
        scratch_shapes=[pltpu.SemaphoreType.DMA((3 * nw,))] * 4 + [pltpu.SemaphoreType.DMA((nw,))],
        compiler_params=_params(),
    )(*shards)


def _pair_exchange(grads):
    nw = len(grads)

    def body(*refs):
        ins, outs = refs[:nw], refs[nw:2 * nw]
        send_sems, recv_sems = refs[2 * nw:]
        x, y, c = _coords()
        sibling = (x, y, 1 - c)
        sends = []
        for w in range(nw):
            half = grads[w].shape[1] // 2
            cp = _remote(ins[w].at[:, pl.ds((1 - c) * half, half)], outs[w], send_sems.at[w], recv_sems.at[w], sibling)
            cp.start()
            sends.append(cp)
        for w in range(nw):
            _remote(outs[w], outs[w], send_sems.at[w], recv_sems.at[w], sibling).wait_recv()
        for cp in sends:
            cp.wait_send()

    return pl.pallas_call(
        body, name="grad_pair_exchange",
        out_shape=[jax.ShapeDtypeStruct((N_CHIPS, g.shape[1] // 2, g.shape[2]), g.dtype) for g in grads],
        in_specs=[ANY] * nw, out_specs=[ANY] * nw,
        scratch_shapes=[pltpu.SemaphoreType.DMA((nw,))] * 2,
        compiler_params=_params(),
    )(*grads)


def _chip_exchange(parts):
    nw = len(parts)

    def body(*refs):
        ins, outs = refs[:nw], refs[nw:2 * nw]
        send_sems, recv_sems, local_sems = refs[2 * nw:]
        x, y, c = _coords()
        chip = 2 * x + y
        started = []
        for w in range(nw):
            cp = pltpu.make_async_copy(ins[w].at[chip], outs[w].at[chip], local_sems.at[w])
            cp.start()
            started.append(cp)
        for w in range(nw):
            for j, (a, b) in enumerate(XY_MASKS):
                px, py = _flip(x, a), _flip(y, b)
                cp = _remote(ins[w].at[2 * px + py], outs[w].at[chip], send_sems.at[3 * w + j],
                             recv_sems.at[3 * w + j], (px, py, c))
                cp.start()
                started.append(cp)
        for w in range(nw):
            for j, (a, b) in enumerate(XY_MASKS):
                px, py = _flip(x, a), _flip(y, b)
                slot = outs[w].at[2 * px + py]
                _remote(slot, slot, send_sems.at[3 * w + j], recv_sems.at[3 * w + j], (px, py, c)).wait_recv()
        for cp in started[nw:]:
            cp.wait_send()
        for cp in started[:nw]:
            cp.wait()

    return pl.pallas_call(
        body, name="grad_chip_exchange",
        out_shape=[jax.ShapeDtypeStruct(p.shape, p.dtype) for p in parts],
        in_specs=[ANY] * nw, out_specs=[ANY] * nw,
        scratch_shapes=[pltpu.SemaphoreType.DMA((3 * nw,))] * 2 + [pltpu.SemaphoreType.DMA((nw,))],
        compiler_params=_params(),
    )(*parts)


def _pair_share(halves):
    nw = len(halves)

    def body(*refs):
        ins, outs = refs[:nw], refs[nw:2 * nw]
        send_sems, recv_sems, local_sems = refs[2 * nw:]
        x, y, c = _coords()
        sibling = (x, y, 1 - c)
        started = []
        for w in range(nw):
            half = halves[w].shape[0]
            cp = pltpu.make_async_copy(ins[w], outs[w].at[pl.ds(c * half, half)], local_sems.at[w])
            cp.start()
            started.append(cp)
        for w in range(nw):
            half = halves[w].shape[0]
            cp = _remote(ins[w], outs[w].at[pl.ds(c * half, half)], send_sems.at[w], recv_sems.at[w], sibling)
            cp.start()
            started.append(cp)
        for w in range(nw):
            half = halves[w].shape[0]
            slot = outs[w].at[pl.ds((1 - c) * half, half)]
            _remote(slot, slot, send_sems.at[w], recv_sems.at[w], sibling).wait_recv()
        for cp in started[nw:]:
            cp.wait_send()
        for cp in started[:nw]:
            cp.wait()

    return pl.pallas_call(
        body, name="grad_pair_share",
        out_shape=[jax.ShapeDtypeStruct((2 * h.shape[0], h.shape[1]), h.dtype) for h in halves],
        in_specs=[ANY] * nw, out_specs=[ANY] * nw,
        scratch_shapes=[pltpu.SemaphoreType.DMA((nw,))] * 3,
        compiler_params=_params(),
    )(*halves)


TILE_BYTES = 2 * 1024 * 1024


def _row_tile(rows, cols, mult=8):
    best = None
    for tr in range(mult, rows + 1, mult):
        if rows % tr == 0 and tr * cols * 4 <= TILE_BYTES:
            best = tr
    return best if best is not None else rows


def _pair_add(name, grad, other, core):
    _, half, cols = other.shape
    tr = _row_tile(half, cols)
    nb = half // tr

    def body(core_ref, g_ref, o_ref, out_ref):
        out_ref[...] = g_ref[...] + o_ref[...]

    return pl.pallas_call(
        body, name=name, out_shape=jax.ShapeDtypeStruct(other.shape, F32),
        grid_spec=pltpu.PrefetchScalarGridSpec(
            num_scalar_prefetch=1, grid=(N_CHIPS, nb),
            in_specs=[pl.BlockSpec((None, tr, cols), lambda s, i, core_ref: (s, core_ref[0] * nb + i, 0)),
                      pl.BlockSpec((None, tr, cols), lambda s, i, core_ref: (s, i, 0))],
            out_specs=pl.BlockSpec((None, tr, cols), lambda s, i, core_ref: (s, i, 0))),
        compiler_params=_params(("parallel", "parallel")),
    )(core, grad, other)


def _sum_chips(name, parts):
    _, half, cols = parts.shape
    tr = _row_tile(half, cols)

    def body(p_ref, out_ref):
        out_ref[...] = ((p_ref[0] + p_ref[1]) + p_ref[2]) + p_ref[3]

    return pl.pallas_call(
        body, name=name, out_shape=jax.ShapeDtypeStruct((half, cols), F32), grid=(half // tr,),
        in_specs=[pl.BlockSpec((N_CHIPS, tr, cols), lambda i: (0, i, 0))],
        out_specs=pl.BlockSpec((tr, cols), lambda i: (i, 0)),
        compiler_params=_params(("parallel",)),
    )(parts)


def _adamw_math(w, g, m, v):
    m2 = ADAM_B1 * m + (1.0 - ADAM_B1) * g
    v2 = ADAM_B2 * v + (1.0 - ADAM_B2) * (g * g)
    m_hat = m2 / (1.0 - ADAM_B1 ** ADAM_STEP)
    v_hat = v2 / (1.0 - ADAM_B2 ** ADAM_STEP)
    delta = -ADAM_LR * (m_hat / (jnp.sqrt(v_hat) + ADAM_EPS) + ADAM_WD * w)
    return delta, m2, v2


def _adamw(name, w, g, m, v):
    rows, cols = w.shape
    tr = _row_tile(rows, cols)

    def body(w_ref, g_ref, m_ref, v_ref, d_ref, m2_ref, v2_ref):
        d_ref[...], m2_ref[...], v2_ref[...] = _adamw_math(w_ref[...], g_ref[...], m_ref[...], v_ref[...])

    blk = pl.BlockSpec((tr, cols), lambda i: (i, 0))
    return pl.pallas_call(
        body, name=name, out_shape=[jax.ShapeDtypeStruct(w.shape, F32)] * 3, grid=(rows // tr,),
        in_specs=[blk] * 4, out_specs=[blk] * 3,
        compiler_params=_params(("parallel",)),
    )(w, g, m, v)


def _small_update(gathered, w, m, v):
    def body(gs_ref, w_ref, m_ref, v_ref, g_ref, d_ref, m2_ref, v2_ref):
        g = gs_ref[0]
        for dev in range(1, N_DEV):
            g = g + gs_ref[dev]
        g_ref[...] = g
        d_ref[...], m2_ref[...], v2_ref[...] = _adamw_math(w_ref[...], g, m_ref[...], v_ref[...])

    return pl.pallas_call(
        body, name="small_update", out_shape=[jax.ShapeDtypeStruct(w.shape, F32)] * 4,
        compiler_params=_params(),
    )(gathered, w, m, v)


def _ada_mod(c_all, w_shard, b_shard):
    d, n = w_shard.shape
    tn = _pick(n, (512, 256, 128))

    def body(c_ref, w_ref, b_ref, o_ref):
        act = _silu(c_ref[...]).astype(BF16)
        o_ref[...] = jnp.dot(act, w_ref[...].astype(BF16), preferred_element_type=F32) + b_ref[...]

    return pl.pallas_call(
        body, name="ada_mod", out_shape=jax.ShapeDtypeStruct((c_all.shape[0], n), F32), grid=(n // tn,),
        in_specs=[pl.BlockSpec(c_all.shape, lambda j: (0, 0)), pl.BlockSpec((d, tn), lambda j: (0, j)),
                  pl.BlockSpec((1, tn), lambda j: (0, j))],
        out_specs=pl.BlockSpec((c_all.shape[0], tn), lambda j: (0, j)),
        compiler_params=_params(("parallel",)),
    )(c_all, w_shard, b_shard)


def _ada_grad(c_pad, dmod_pad):
    rows, d = c_pad.shape
    n = dmod_pad.shape[1]
    tn = _pick(n, (512, 256, 128))

    def body(c_ref, g_ref, o_ref):
        act = _silu(c_ref[...]).astype(BF16)
        o_ref[...] = lax.dot_general(act, g_ref[...].astype(BF16), TN_DIMS, preferred_element_type=F32)

    return pl.pallas_call(
        body, name="ada_grad", out_shape=jax.ShapeDtypeStruct((d, n), F32), grid=(n // tn,),
        in_specs=[pl.BlockSpec((rows, d), lambda j: (0, 0)), pl.BlockSpec((rows, tn), lambda j: (0, j))],
        out_specs=pl.BlockSpec((d, tn), lambda j: (0, j)),
        compiler_params=_params(("parallel",)),
    )(c_pad, dmod_pad)


WEIGHTS = ['w_ada', 'b_ada', 'norm1_gain', 'norm2_gain', 'w_in', 'mu_rkv', 'mu_w', 'mu_a', 'mu_g', 'w0', 'w1',
           'w2', 'a0', 'a1', 'a2', 'g1', 'g2', 'k_k', 'k_a', 'r_k', 'ln_x_gain', 'ln_x_bias', 'q_norm_gain',
           'k_norm_gain', 'w_out', 'w_gate_up', 'w_down']
SMALL = ['b_ada', 'norm1_gain', 'norm2_gain', 'mu_rkv', 'mu_w', 'mu_a', 'mu_g', 'w0', 'a0', 'k_k', 'k_a', 'r_k',
         'ln_x_gain', 'ln_x_bias', 'q_norm_gain', 'k_norm_gain']
PACK_ROWS = 8
LOSS_SLOT = LANES


def _shift_down(a):
    return jnp.pad(a[:-1], ((1, 0), (0, 0)))


def _shift_up(a):
    return jnp.pad(a[1:], ((0, 1), (0, 0)))


def _pack_small(vals):
    flat = jnp.concatenate([v.reshape(1, -1) for v in vals], axis=1)
    unit = PACK_ROWS * LANES
    total = -(-flat.shape[1] // unit) * unit
    flat = jnp.pad(flat, ((0, 0), (0, total - flat.shape[1])))
    return flat.reshape(PACK_ROWS, total // PACK_ROWS)


def kernel(x, c, w_ada, b_ada, norm1_gain, norm2_gain, w_in, mu_rkv, mu_w, mu_a, mu_g, w0, w1, w2, a0, a1, a2, g1, g2, k_k, k_a, r_k, ln_x_gain, ln_x_bias, q_norm_gain, k_norm_gain, w_out, w_gate_up, w_down, loss_target, m_w_ada, m_b_ada, m_norm1_gain, m_norm2_gain, m_w_in, m_mu_rkv, m_mu_w, m_mu_a, m_mu_g, m_w0, m_w1, m_w2, m_a0, m_a1, m_a2, m_g1, m_g2, m_k_k, m_k_a, m_r_k, m_ln_x_gain, m_ln_x_bias, m_q_norm_gain, m_k_norm_gain, m_w_out, m_w_gate_up, m_w_down, v_w_ada, v_b_ada, v_norm1_gain, v_norm2_gain, v_w_in, v_mu_rkv, v_mu_w, v_mu_a, v_mu_g, v_w0, v_w1, v_w2, v_a0, v_a1, v_a2, v_g1, v_g2, v_k_k, v_k_a, v_r_k, v_ln_x_gain, v_ln_x_bias, v_q_norm_gain, v_k_norm_gain, v_w_out, v_w_gate_up, v_w_down):
    given = dict(locals())
    wt = {n: given[n][0] for n in WEIGHTS}
    mom = {n: given["m_" + n][0] for n in WEIGHTS}
    var = {n: given["v_" + n][0] for n in WEIGHTS}
    for tree in (wt, mom, var):
        tree["b_ada"] = tree["b_ada"].reshape(1, -1)
        for n in SMALL[1:]:
            tree[n] = tree[n].reshape(1, -1)

    ax, ay, ac = _coords()
    chip = 2 * ax + ay
    dev = 4 * ax + 2 * ay + ac
    xs, target = x[0], loss_target[0]
    t, d = xs.shape
    dr = wt["w0"].shape[1]
    ds = d - dr
    nh = ds // HEAD_DIM
    dff = wt["w_down"].shape[0] * N_CHIPS
    n_ada = wt["w_ada"].shape[1]
    lw, la, lg = wt["w1"].shape[1], wt["a1"].shape[1], wt["g1"].shape[1]

    def lora_a(tree):
        return jnp.concatenate([tree["w1"], tree["a1"], tree["g1"]], axis=1)

    def lora_b(tree):
        return jnp.concatenate([tree["w2"], tree["a2"], tree["g2"]], axis=0)

    shards = [wt["w_in"], wt["w_out"], wt["w_gate_up"], wt["w_down"], lora_a(wt), lora_b(wt)]
    full_in, full_out, full_gu, full_down, full_la, full_lb = _gather_weights([s.astype(BF16) for s in shards])
    full_out = full_out.reshape(d, d)
    full_down = full_down.reshape(dff, d)
    full_la = full_la.reshape(d, lw + la + lg).astype(F32)
    full_lb = full_lb.transpose(1, 0, 2).reshape(lw + la + lg, dr).astype(F32)
    w1f, a1f, g1f = full_la[:, :lw], full_la[:, lw:lw + la], full_la[:, lw + la:]
    w2f, a2f, g2f = full_lb[:lw], full_lb[lw:lw + la], full_lb[lw + la:]

    c_all = _all_gather8("gather_c", c.reshape(PACK_ROWS, d // PACK_ROWS)).reshape(N_DEV, d)
    b_shard = lax.dynamic_slice(wt["b_ada"], (0, chip * n_ada), (1, n_ada))
    mod_part = _ada_mod(c_all, wt["w_ada"], b_shard)
    mod_all = _all_gather8("gather_mod", mod_part)[::2]
    mod = lax.dynamic_slice(mod_all, (0, dev, 0), (N_CHIPS, 1, n_ada)).reshape(1, N_CHIPS * n_ada)
    sh1, sc1, gt1, sh2, sc2, gt2 = [mod[:, i * d:(i + 1) * d] for i in range(6)]

    h, h_bf = _rowwise("norm1", _fn_norm1, [xs], [wt["norm1_gain"], sc1, sh1], [(d, F32), (d, BF16)], 256)
    hp = _shift_down(h)
    p = _matmul("mm_in", h_bf, full_in, b_shards=True, tm=512, tn=512, tk=2048)
    p_rkv, p_sb = (p, 3 * dr, 0), (p, 3 * ds, 1)
    pp = _shift_down(p[:, :3 * dr])
    pre_rows = [h, hp, p_rkv, pp]
    pre_params = [wt["mu_rkv"], wt["mu_w"], wt["mu_a"], wt["mu_g"], wt["w0"], wt["a0"], wt["k_k"], wt["k_a"],
                  w1f, w2f, a1f, a2f, g1f, g2f]
    pre = _rowwise("rwkv_pre", _fn_rwkv_pre, pre_rows, pre_params, [(dr, F32)] * 7, 64)
    r_, w_, k2, v_, rem, wr, g_ = pre
    npair = dr // LANES
    scan_in = [a.reshape(t, npair, LANES) for a in (r_, w_, k2, v_, rem, wr)]
    y3, hist = _scan_fwd(scan_in)
    y_raw = y3.reshape(t, dr)
    post_rows = [y_raw, r_, k2, v_, g_]
    post_params = [wt["ln_x_gain"], wt["ln_x_bias"], wt["r_k"]]
    (y_rwkv,) = _rowwise("rwkv_post", _fn_rwkv_post, post_rows, post_params, [(dr, BF16)], 256)

    qg = jnp.tile(wt["q_norm_gain"], (1, nh))
    kg = jnp.tile(wt["k_norm_gain"], (1, nh))
    qn, kn, vs = _rowwise("qk_norm", _fn_qk_norm, [p_sb], [qg, kg], [(ds, BF16)] * 3, 256)

    def to_heads(a):
        return a.reshape(t, nh, HEAD_DIM).transpose(1, 0, 2)

    def from_heads(a):
        return a.transpose(1, 0, 2).reshape(t, ds)

    qh, kh, vh = to_heads(qn), to_heads(kn), to_heads(vs)
    o_h, lsum = _sb_fwd(qh, kh, vh)
    ycat = jnp.concatenate([y_rwkv, from_heads(o_h).astype(BF16)], axis=1)
    mix = _matmul("mm_out", ycat, full_out, tm=512, tn=512, tk=2048)
    norm2_params = [gt1, wt["norm2_gain"], sc2, sh2]
    x1, h2 = _rowwise("mix_norm2", _fn_mix_norm2, [xs, mix], norm2_params, [(d, F32), (d, BF16)], 256)
    gu = _matmul("mm_gate_up", h2, full_gu, b_shards=True, tm=512, tn=256, tk=2048)
    gate_up = [(gu, dff, 0), (gu, dff, 1)]
    (act,) = _rowwise("swiglu", _fn_swiglu, gate_up, [], [(dff, BF16)], 256)
    dn = _matmul("mm_down", act, full_down, tm=512, tn=512, tk=512)
    loss_vec, dout, ddn, dgt2 = _loss_head("loss_head", x1, dn, target, gt2)

    dact = _matmul("mm_down_dx", ddn, full_down, tb=True, tm=512, tn=512, tk=2048)
    gw_down = _matmul("mm_down_dw", act, ddn, ta=True, tm=512, tn=512, tk=512)
    (dgate, dup), _ = _rowwise_bwd("swiglu_bwd", _fn_swiglu, gate_up, [], [[dact]], [BF16, BF16], [], 256)
    dgu = jnp.concatenate([dgate, dup], axis=1)
    dh2 = _matmul("mm_gate_up_dx", dgu, full_gu, tb=True, b_shards=True, tm=512, tn=512, tk=256)
    gw_gu = _matmul("mm_gate_up_dw", h2, dgu, ta=True, out_shards=True, tm=512, tn=256, tk=512)
    (dx_a, dmix), (dgt1, dgain2, dsc2, dsh2) = _rowwise_bwd(
        "mix_norm2_bwd", _fn_mix_norm2, [xs, mix], norm2_params, [[dout], [dh2]], [F32, BF16], [True] * 4, 128)
    dycat = _matmul("mm_out_dx", dmix, full_out, tb=True, tm=512, tn=512, tk=2048)
    gw_out = _matmul("mm_out_dw", ycat, dmix, ta=True, tm=512, tn=512, tk=512)
    (dy_raw, dr_f, dk_f, dv_f, dg), (dlng, dlnb, drk) = _rowwise_bwd(
        "rwkv_post_bwd", _fn_rwkv_post, post_rows, post_params, [[(dycat, dr, 0)]], [F32] * 5, [True] * 3, 128)
    scan_g = _scan_bwd(scan_in, hist, dy_raw.reshape(t, npair, LANES))
    dr_s, dw_s, dk_s, dv_s, drem_s, dwr_s = [a.reshape(t, dr) for a in scan_g]
    do_h = to_heads(dycat[:, dr:])
    dqh, dkh, dvh = _sb_bwd(qh, kh, vh, lsum, do_h)
    (dp_sb,), (dqg, dkg) = _rowwise_bwd(
        "qk_norm_bwd", _fn_qk_norm, [p_sb], [qg, kg], [[from_heads(dqh)], [from_heads(dkh)], [from_heads(dvh)]],
        [F32], [True, True], 128)
    pre_cts = [[dr_s, dr_f], [dw_s], [dk_s, dk_f], [dv_s, dv_f], [drem_s], [dwr_s], [dg]]
    (dh_a, dhp, dp_rkv, dpp), pre_g = _rowwise_bwd(
        "rwkv_pre_bwd", _fn_rwkv_pre, pre_rows, pre_params, pre_cts, [F32] * 4, [True] * 14, 64)
    dp = jnp.concatenate([dp_rkv + _shift_up(dpp), dp_sb], axis=1).astype(BF16)
    dh_mm = _matmul("mm_in_dx", dp, full_in, tb=True, b_shards=True, tm=512, tn=512, tk=512)
    gw_in = _matmul("mm_in_dw", h_bf, dp, ta=True, out_shards=True, tm=512, tn=512, tk=512)
    (grad_x,), (dgain1, dsc1, dsh1) = _rowwise_bwd(
        "norm1_bwd", _fn_norm1, [xs], [wt["norm1_gain"], sc1, sh1], [[dh_a, dh_mm, _shift_up(dhp)], []],
        [F32], [True] * 3, 128, add_to_first=[dx_a])

    g_mu_rkv, g_mu_w, g_mu_a, g_mu_g, g_w0, g_a0, g_kk, g_ka, gw1, gw2, ga1, ga2, gg1, gg2 = pre_g
    g_la = jnp.concatenate([gw1, ga1, gg1], axis=1).reshape(N_CHIPS, d // N_CHIPS, lw + la + lg)
    g_lb = jnp.concatenate([gw2, ga2, gg2], axis=0)
    g_lb = g_lb.reshape(lw + la + lg, N_CHIPS, dr // N_CHIPS).transpose(1, 0, 2)
    local = [gw_in, gw_out.reshape(N_CHIPS, d // N_CHIPS, d), gw_gu, gw_down.reshape(N_CHIPS, dff // N_CHIPS, d),
             g_la, g_lb]
    names = ["w_in", "w_out", "w_gate_up", "w_down", "lora_a", "lora_b"]
    core = ac.reshape(1).astype(jnp.int32)
    from_sibling = _pair_exchange(local)
    pair_sums = [_pair_add("pair_add_" + n, g, o, core) for n, g, o in zip(names, local, from_sibling)]
    from_chips = _chip_exchange(pair_sums)
    halves = [_sum_chips("chip_sum_" + n, q) for n, q in zip(names, from_chips)]
    r_in, r_out, r_gu, r_down, r_la, r_lb = _pair_share(halves)

    dmod = jnp.concatenate([dsh1, dsc1, dgt1, dsh2, dsc2, dgt2], axis=1)
    dqg = dqg.reshape(nh, HEAD_DIM).sum(axis=0, keepdims=True)
    dkg = dkg.reshape(nh, HEAD_DIM).sum(axis=0, keepdims=True)
    small_g = [dmod, dgain1, dgain2, g_mu_rkv, g_mu_w, g_mu_a, g_mu_g, g_w0, g_a0, g_kk, g_ka, drk, dlng, dlnb,
               dqg, dkg]
    lead = jnp.zeros((1, LOSS_SLOT), F32)
    packed = _pack_small([loss_vec] + small_g)
    gathered = _all_gather8("gather_small", packed)
    sm_g, sm_d, sm_m, sm_v = _small_update(gathered, _pack_small([lead] + [wt[n] for n in SMALL]),
                                           _pack_small([lead] + [mom[n] for n in SMALL]),
                                           _pack_small([lead] + [var[n] for n in SMALL]))
    loss = sm_g.reshape(-1)[0]

    def unpack(packed_arr):
        flat, out, pos = packed_arr.reshape(-1), {}, LOSS_SLOT
        for n in SMALL:
            size = wt[n].size
            out[n] = flat[pos:pos + size]
            pos += size
        return out

    res = {"grad": unpack(sm_g), "delta": unpack(sm_d), "m": unpack(sm_m), "v": unpack(sm_v)}

    dmod_all = gathered.reshape(N_DEV, -1)[:, LOSS_SLOT:LOSS_SLOT + N_CHIPS * n_ada]
    dmod_cols = lax.dynamic_slice(dmod_all, (0, chip * n_ada), (N_DEV, n_ada))
    pad8 = ((0, N_DEV), (0, 0))
    res["grad"]["w_ada"] = _ada_grad(jnp.pad(c_all, pad8), jnp.pad(dmod_cols, pad8))

    res["grad"].update(w_in=r_in, w_out=r_out, w_gate_up=r_gu, w_down=r_down)
    for n in ("w_ada", "w_in", "w_out", "w_gate_up", "w_down"):
        res["delta"][n], res["m"][n], res["v"][n] = _adamw("adamw_" + n, wt[n], res["grad"][n], mom[n], var[n])
    la_d, la_m, la_v = _adamw("adamw_lora_a", lora_a(wt), r_la, lora_a(mom), lora_a(var))
    lb_d, lb_m, lb_v = _adamw("adamw_lora_b", lora_b(wt), r_lb, lora_b(mom), lora_b(var))
    for key, pa, pb in (("grad", r_la, r_lb), ("delta", la_d, lb_d), ("m", la_m, lb_m), ("v", la_v, lb_v)):
        res[key].update(w1=pa[:, :lw], a1=pa[:, lw:lw + la], g1=pa[:, lw + la:],
                        w2=pb[:lw], a2=pb[lw:lw + la], g2=pb[lw + la:])

    outs = [loss, grad_x[None]]
    for key in ("grad", "delta", "m", "v"):
        outs += [res[key][n].reshape(given[n].shape) for n in WEIGHTS]
    return tuple(outs)
```

```python
import functools
import math

import jax
import jax.numpy as jnp
from jax import lax
from jax.experimental import pallas as pl
from jax.experimental.pallas import tpu as pltpu

F32 = jnp.float32
BF16 = jnp.bfloat16
HEAD_DIM = 64
LANES = 128
RMS_EPS = 1e-6
GN_EPS = 64e-5
L2_EPS = 1e-12
ADAM_LR, ADAM_B1, ADAM_B2, ADAM_EPS, ADAM_WD, ADAM_STEP = 0.001, 0.9, 0.999, 1e-08, 0.01, 10
VMEM_LIMIT = 56 * 1024 * 1024
MESH = pl.DeviceIdType.MESH
HI = lax.Precision.HIGHEST
N_CHIPS = 4
N_DEV = 8
XY_MASKS = ((1, 0), (0, 1), (1, 1))


def _pick(dim, prefs):
    for p in prefs:
        if dim % p == 0:
            return p
    return dim


def _params(sem=None, vmem=VMEM_LIMIT):
    return pltpu.CompilerParams(dimension_semantics=sem, vmem_limit_bytes=vmem)


def _sigmoid(x):
    return 1.0 / (1.0 + jnp.exp(-x))


@jax.custom_vjp
def _softplus(x):
    return jnp.maximum(x, 0.0) + jnp.log(1.0 + jnp.exp(-jnp.abs(x)))


_softplus.defvjp(lambda x: (_softplus(x), x), lambda x, g: (g * _sigmoid(x),))


def _silu(x):
    return x * _sigmoid(x)


@jax.custom_vjp
def _bdot(a, b):
    return jnp.dot(a.astype(BF16), b.astype(BF16), preferred_element_type=F32)


def _bdot_bwd(res, g):
    a, b = res
    gb = g.astype(BF16)
    da = lax.dot_general(gb, b.astype(BF16), (((1,), (1,)), ((), ())), preferred_element_type=F32)
    db = lax.dot_general(a.astype(BF16), gb, (((0,), (0,)), ((), ())), preferred_element_type=F32)
    return da.astype(a.dtype), db.astype(b.dtype)


_bdot.defvjp(lambda a, b: (_bdot(a, b), (a, b)), _bdot_bwd)


def _head_ones():
    i = lax.broadcasted_iota(jnp.int32, (LANES, LANES), 0) // HEAD_DIM
    j = lax.broadcasted_iota(jnp.int32, (LANES, LANES), 1) // HEAD_DIM
    return (i == j).astype(F32)


def _hdot(x, ones):
    return jnp.dot(x, ones, precision=HI, preferred_element_type=F32)


@jax.custom_vjp
def _segsum(x):
    ones = _head_ones()
    parts = [_hdot(x[:, LANES * j:LANES * (j + 1)], ones) for j in range(x.shape[1] // LANES)]
    return parts[0] if len(parts) == 1 else jnp.concatenate(parts, axis=1)


_segsum.defvjp(lambda x: (_segsum(x), None), lambda _, g: (_segsum(g),))


def _rms(x, gain):
    return x * lax.rsqrt(jnp.mean(x * x, axis=-1, keepdims=True) + RMS_EPS) * gain


def _matmul(name, a, b, *, ta=False, tb=False, b_shards=False, out_shards=False, out_dtype=F32,
            tm=512, tn=512, tk=512):
    if ta:
        kdim, m = a.shape
    else:
        m, kdim = a.shape
    if b_shards:
        if tb:
            n, ks = b.shape[1], b.shape[2]
            assert ks * N_CHIPS == kdim
        else:
            ns = b.shape[2]
            n = ns * N_CHIPS
            assert b.shape[1] == kdim
    else:
        n = b.shape[0] if tb else b.shape[1]
    tm = _pick(m, (tm, 512, 256, 128))
    n_part = n // N_CHIPS if (out_shards or (b_shards and not tb)) else n
    tn = _pick(n_part, (tn, 512, 256, 128))
    k_part = kdim // N_CHIPS if (b_shards and tb) else kdim
    tk = _pick(k_part, (tk, 512, 256, 128))
    nb = n_part // tn
    kb = k_part // tk
    nk = kdim // tk
    grid = (m // tm, n // tn, nk)

    if ta:
        a_spec = pl.BlockSpec((tk, tm), lambda i, j, k: (k, i))
    else:
        a_spec = pl.BlockSpec((tm, tk), lambda i, j, k: (i, k))
    if b_shards and tb:
        b_spec = pl.BlockSpec((None, tn, tk), lambda i, j, k: (k // kb, j, k % kb))
    elif b_shards:
        b_spec = pl.BlockSpec((None, tk, tn), lambda i, j, k: (j // nb, k, j % nb))
    elif tb:
        b_spec = pl.BlockSpec((tn, tk), lambda i, j, k: (j, k))
    else:
        b_spec = pl.BlockSpec((tk, tn), lambda i, j, k: (k, j))
    if out_shards:
        o_spec = pl.BlockSpec((None, tm, tn), lambda i, j, k: (j // nb, i, j % nb))
        o_shape = jax.ShapeDtypeStruct((N_CHIPS, m, n_part), out_dtype)
    else:
        o_spec = pl.BlockSpec((tm, tn), lambda i, j, k: (i, j))
        o_shape = jax.ShapeDtypeStruct((m, n), out_dtype)
    dims = (((0 if ta else 1,), (1 if tb else 0,)), ((), ()))

    def body(a_ref, b_ref, o_ref, acc_ref):
        k = pl.program_id(2)

        @pl.when(k == 0)
        def _():
            acc_ref[...] = jnp.zeros_like(acc_ref)

        acc_ref[...] += lax.dot_general(a_ref[...].astype(BF16), b_ref[...].astype(BF16), dims,
                                        preferred_element_type=F32)

        @pl.when(k == nk - 1)
        def _():
            o_ref[...] = acc_ref[...].astype(o_ref.dtype)

    return pl.pallas_call(
        body, name=name, grid=grid, in_specs=[a_spec, b_spec], out_specs=o_spec, out_shape=o_shape,
        scratch_shapes=[pltpu.VMEM((tm, tn), F32)],
        compiler_params=_params(("parallel", "parallel", "arbitrary")),
    )(a, b)


def _row_in(spec, tile):
    if isinstance(spec, tuple):
        arr, width, cb = spec
    else:
        arr, width, cb = spec, spec.shape[1], 0
    return arr, pl.BlockSpec((tile, width), lambda i, cb=cb: (i, cb))


def _full_spec(arr):
    nd = arr.ndim
    return pl.BlockSpec(arr.shape, lambda i, nd=nd: (0,) * nd)


def _rowwise(name, fn, rows, params, outs, tile):
    t = (rows[0][0] if isinstance(rows[0], tuple) else rows[0]).shape[0]
    tile = _pick(t, (tile,))
    arrs, specs = zip(*[_row_in(s, tile) for s in rows])
    nr, npar = len(rows), len(params)

    def body(*refs):
        rv = [r[...].astype(F32) for r in refs[:nr]]
        pv = [p[...] for p in refs[nr:nr + npar]]
        res = fn(*rv, *pv)
        for o_ref, val in zip(refs[nr + npar:], res):
            o_ref[...] = val.astype(o_ref.dtype)

    return pl.pallas_call(
        body, name=name, grid=(t // tile,),
        in_specs=list(specs) + [_full_spec(p) for p in params],
        out_specs=[pl.BlockSpec((tile, w), lambda i: (i, 0)) for w, _ in outs],
        out_shape=[jax.ShapeDtypeStruct((t, w), d) for w, d in outs],
        compiler_params=_params(("parallel",)),
    )(*arrs, *params)


def _rowwise_bwd(name, fn, rows, params, cts, row_grads, param_grads, tile, add_to_first=()):
    t = (rows[0][0] if isinstance(rows[0], tuple) else rows[0]).shape[0]
    tile = _pick(t, (tile,))
    arrs, specs = zip(*[_row_in(s, tile) for s in rows])
    n_add = len(add_to_first)
    flat_cts = [c for group in cts for c in group] + list(add_to_first)
    c_arrs, c_specs = zip(*[_row_in(s, tile) for s in flat_cts])
    nr, npar, nc = len(rows), len(params), len(flat_cts)
    rg_idx = [i for i, d in enumerate(row_grads) if d is not None]
    pg_idx = [i for i, d in enumerate(param_grads) if d]

    def body(*refs):
        rv = [r[...].astype(F32) for r in refs[:nr]]
        pv = [p[...] for p in refs[nr:nr + npar]]
        cv = [c[...].astype(F32) for c in refs[nr + npar:nr + npar + nc]]
        o_refs = refs[nr + npar + nc:]
        outs, vjp = jax.vjp(fn, *rv, *pv)
        ct, pos = [], 0
        for group, o in zip(cts, outs):
            if group:
                acc = cv[pos]
                for extra in cv[pos + 1:pos + len(group)]:
                    acc = acc + extra
                pos += len(group)
            else:
                acc = jnp.zeros_like(o)
            ct.append(acc)
        grads = list(vjp(tuple(ct)))
        for extra in cv[nc - n_add:]:
            grads[rg_idx[0]] = grads[rg_idx[0]] + extra
        for o_ref, i in zip(o_refs[:len(rg_idx)], rg_idx):
            o_ref[...] = grads[i].astype(o_ref.dtype)
        first = pl.program_id(0) == 0
        for o_ref, i in zip(o_refs[len(rg_idx):], pg_idx):
            g = grads[nr + i].astype(F32)

            @pl.when(first)
            def _(o_ref=o_ref, g=g):
                o_ref[...] = g

            @pl.when(jnp.logical_not(first))
            def _(o_ref=o_ref, g=g):
                o_ref[...] += g

    def width(i):
        s = rows[i]
        return s[1] if isinstance(s, tuple) else s.shape[1]

    out_specs = [pl.BlockSpec((tile, width(i)), lambda i_: (i_, 0)) for i in rg_idx]
    out_shape = [jax.ShapeDtypeStruct((t, width(i)), row_grads[i]) for i in rg_idx]
    out_specs += [_full_spec(params[i]) for i in pg_idx]
    out_shape += [jax.ShapeDtypeStruct(params[i].shape, F32) for i in pg_idx]
    res = pl.pallas_call(
        body, name=name, grid=(t // tile,),
        in_specs=list(specs) + [_full_spec(p) for p in params] + list(c_specs),
        out_specs=out_specs, out_shape=out_shape,
        compiler_params=_params(("arbitrary",)),
    )(*arrs, *params, *c_arrs)
    return res[:len(rg_idx)], res[len(rg_idx):]


def _fn_norm1(x, gain, sc, sh):
    h = _rms(x, gain) * (1.0 + sc) + sh
    return h, h


def _fn_rwkv_pre(h, hp, p, pp, mu_rkv, mu_w, mu_a, mu_g, w0, a0, k_k, k_a, w1, w2, a1, a2, g1, g2):
    d = p.shape[1] // 3
    dh = hp - h
    xw = h + dh * mu_w
    xa = h + dh * mu_a
    xg = h + dh * mu_g
    pr = p + (pp - p) * mu_rkv
    r, k, v = pr[:, :d], pr[:, d:2 * d], pr[:, 2 * d:]
    w_log = -_softplus(-(w0 + _bdot(jnp.tanh(_bdot(xw, w1)), w2))) - 0.5
    decay = jnp.exp(-jnp.exp(w_log))
    a = _sigmoid(a0 + _bdot(_bdot(xa, a1), a2))
    g = _bdot(_sigmoid(_bdot(xg, g1)), g2)
    kk = k * k_k
    kk = kk * lax.rsqrt(_segsum(kk * kk) + L2_EPS)
    k2 = k * (1.0 + (a - 1.0) * k_a)
    return r, decay, k2, v, -kk, kk * a, g


def _fn_rwkv_post(y, r, k2, v, g, ln_g, ln_b, r_k):
    inv = 1.0 / HEAD_DIM
    mean = _segsum(y) * inv
    yc = y - mean
    var = _segsum(yc * yc) * inv
    yn = yc * lax.rsqrt(var + GN_EPS) * ln_g + ln_b
    bonus = _segsum(r * k2 * r_k) * v
    return ((yn + bonus) * g,)


def _fn_qk_norm(p, qg, kg):
    d = p.shape[1] // 3
    q, k, v = p[:, :d], p[:, d:2 * d], p[:, 2 * d:]
    inv = 1.0 / HEAD_DIM
    qn = q * lax.rsqrt(_segsum(q * q) * inv + RMS_EPS) * qg
    kn = k * lax.rsqrt(_segsum(k * k) * inv + RMS_EPS) * kg
    return qn, kn, v


def _fn_mix_norm2(x, mix, gt1, gain, sc, sh):
    x1 = x + gt1 * mix
    h2 = _rms(x1, gain) * (1.0 + sc) + sh
    return x1, h2


def _fn_swiglu(gate, up):
    return (_silu(gate) * up,)


def _loss_head(name, x1, dn, target, gt2, tile=256):
    t, d = x1.shape
    tile = _pick(t, (tile,))

    def body(x1_ref, dn_ref, tg_ref, gt_ref, loss_ref, dout_ref, ddn_ref, dgt_ref):
        dnv = dn_ref[...]
        gt = gt_ref[...]
        err = x1_ref[...] + gt * dnv - tg_ref[...]
        dout = err * (1.0 / d)
        dout_ref[...] = dout
        ddn_ref[...] = (dout * gt).astype(ddn_ref.dtype)
        part = 0.5 * jnp.sum(jnp.sum(err * dout, axis=-1, keepdims=True), axis=0, keepdims=True)
        dgt = jnp.sum(dout * dnv, axis=0, keepdims=True)
        first = pl.program_id(0) == 0

        @pl.when(first)
        def _():
            loss_ref[...] = jnp.broadcast_to(part, loss_ref.shape)
            dgt_ref[...] = dgt

        @pl.when(jnp.logical_not(first))
        def _():
            loss_ref[...] += jnp.broadcast_to(part, loss_ref.shape)
            dgt_ref[...] += dgt

    row = pl.BlockSpec((tile, d), lambda i: (i, 0))
    vec = pl.BlockSpec((1, d), lambda i: (0, 0))
    return pl.pallas_call(
        body, name=name, grid=(t // tile,),
        in_specs=[row, row, row, vec],
        out_specs=[pl.BlockSpec((1, LANES), lambda i: (0, 0)), row, row, vec],
        out_shape=[jax.ShapeDtypeStruct((1, LANES), F32), jax.ShapeDtypeStruct((t, d), F32),
                   jax.ShapeDtypeStruct((t, d), BF16), jax.ShapeDtypeStruct((1, d), F32)],
        compiler_params=_params(("arbitrary",)),
    )(x1, dn, target, gt2)


SCAN_BLOCK = 32
N_COL = 5


WIDE = 2 * LANES


def _wide_eye():
    i = lax.broadcasted_iota(jnp.int32, (HEAD_DIM, WIDE), 0)
    j = lax.broadcasted_iota(jnp.int32, (HEAD_DIM, WIDE), 1) % HEAD_DIM
    return i == j


def _wide_ones():
    i = lax.broadcasted_iota(jnp.int32, (WIDE, WIDE), 0) // HEAD_DIM
    j = lax.broadcasted_iota(jnp.int32, (WIDE, WIDE), 1) // HEAD_DIM
    return (i == j).astype(BF16)


def _col_tiles(rows, eye, ones_bf):
    n = rows.shape[0]
    hi = rows.astype(BF16)
    rest = rows - hi.astype(F32)
    mid = rest.astype(BF16)
    lo = (rest - mid.astype(F32)).astype(BF16)
    out = None
    for part in (hi, mid, lo):
        lhs = jnp.where(eye[None], part.astype(F32)[:, None, :], 0.0).astype(BF16)
        term = jnp.dot(lhs.reshape(n * HEAD_DIM, WIDE), ones_bf, preferred_element_type=F32)
        out = term if out is None else out + term
    return out.reshape(n, HEAD_DIM, WIDE)


def _pack_cols(vectors, t, npair):
    nq = npair // 2
    cols = jnp.stack([a.reshape(t, 2, nq, LANES) for a in vectors], axis=1)
    return cols.transpose(0, 1, 3, 2, 4).reshape(t, N_COL * nq, WIDE)


def _unpack_cols(cols, t, npair):
    nq = npair // 2
    cols = cols.reshape(t, N_COL, nq, 2, LANES).transpose(0, 1, 3, 2, 4).reshape(t, N_COL, npair * LANES)
    return [cols[:, n] for n in range(N_COL)]


def _pair_tile(tiles_ref, n, p, npair):
    nq = npair // 2
    return tiles_ref[n * nq + p % nq, :, (p // nq) * LANES:(p // nq + 1) * LANES]


def _scan_fwd(cols, v):
    t, npair, _ = v.shape
    ncol = cols.shape[1]
    tb = _pick(t, (SCAN_BLOCK,))

    def body(cols_ref, v_ref, y_ref, hist_ref, s_ref, tiles_a, tiles_b):
        @pl.when(pl.program_id(0) == 0)
        def _():
            s_ref[...] = jnp.zeros_like(s_ref)

        eye, ones_bf = _wide_eye(), _wide_ones()

        def step(i, tiles_ref):
            for p in range(npair):
                s = s_ref[p]
                hist_ref[i, p] = s
                c_rem, c_w, c_wr, c_k, c_r = [_pair_tile(tiles_ref, n, p, npair) for n in range(N_COL)]
                sa = jnp.sum(s * c_rem, axis=0, keepdims=True)
                s2 = s * c_w + c_wr * sa + c_k * v_ref[i, pl.ds(p, 1), :]
                y_ref[i, pl.ds(p, 1), :] = jnp.sum(s2 * c_r, axis=0, keepdims=True)
                s_ref[p] = s2

        tiles_a[...] = _col_tiles(cols_ref[0], eye, ones_bf)

        def two_steps(m, carry):
            i = 2 * m
            tiles_b[...] = _col_tiles(cols_ref[i + 1], eye, ones_bf)
            step(i, tiles_a)
            tiles_a[...] = _col_tiles(cols_ref[jnp.minimum(i + 2, tb - 1)], eye, ones_bf)
            step(i + 1, tiles_b)
            return carry

        lax.fori_loop(0, tb // 2, two_steps, 0)

    blk = pl.BlockSpec((tb, npair, LANES), lambda i: (i, 0, 0))
    return pl.pallas_call(
        body, name="rwkv_scan_fwd", grid=(t // tb,),
        in_specs=[pl.BlockSpec((tb, ncol, WIDE), lambda i: (i, 0, 0)), blk],
        out_specs=[blk, pl.BlockSpec((tb, npair, HEAD_DIM, LANES), lambda i: (i, 0, 0, 0))],
        out_shape=[jax.ShapeDtypeStruct((t, npair, LANES), F32),
                   jax.ShapeDtypeStruct((t, npair, HEAD_DIM, LANES), F32)],
        scratch_shapes=[pltpu.VMEM((npair, HEAD_DIM, LANES), F32),
                        pltpu.VMEM((ncol, HEAD_DIM, WIDE), F32), pltpu.VMEM((ncol, HEAD_DIM, WIDE), F32)],
        compiler_params=_params(("arbitrary",)),
    )(cols, v)


def _scan_bwd(cols, v, hist, dy):
    t, npair, _ = v.shape
    ncol = cols.shape[1]
    nq = npair // 2
    tb = _pick(t, (SCAN_BLOCK,))
    nblk = t // tb

    def body(cols_ref, v_ref, hist_ref, dy_ref, dcols_ref, dv_ref, ds_ref, tiles_a, tiles_b):
        @pl.when(pl.program_id(0) == 0)
        def _():
            ds_ref[...] = jnp.zeros_like(ds_ref)

        eye, ones_bf = _wide_eye(), _wide_ones()

        def step(i, tiles_ref):
            grads = [[None] * npair for _ in range(N_COL)]
            for p in range(npair):
                s = hist_ref[i, p]
                c_rem, c_w, c_wr, c_k, c_r = [_pair_tile(tiles_ref, n, p, npair) for n in range(N_COL)]
                v_row = v_ref[i, pl.ds(p, 1), :]
                dy_row = dy_ref[i, pl.ds(p, 1), :]
                sa = jnp.sum(s * c_rem, axis=0, keepdims=True)
                s2 = s * c_w + c_wr * sa + c_k * v_row
                d2 = ds_ref[p] + c_r * dy_row
                dsa = jnp.sum(d2 * c_wr, axis=0, keepdims=True)
                dv_ref[i, pl.ds(p, 1), :] = jnp.sum(d2 * c_k, axis=0, keepdims=True)
                ds_ref[p] = d2 * c_w + c_rem * dsa
                for n, tile in enumerate((s * dsa, d2 * s, d2 * sa, d2 * v_row, s2 * dy_row)):
                    grads[n][p] = tile.astype(BF16)
            wide = [jnp.concatenate([grads[n][q], grads[n][q + nq]], axis=1) for n in range(N_COL) for q in range(nq)]
            sums = jnp.dot(jnp.concatenate(wide, axis=0), ones_bf, preferred_element_type=F32)
            sums = sums.reshape(ncol, HEAD_DIM, WIDE)
            dcols_ref[i] = jnp.sum(jnp.where(eye[None], sums, 0.0), axis=1)

        tiles_a[...] = _col_tiles(cols_ref[tb - 1], eye, ones_bf)

        def two_steps(m, carry):
            i = tb - 1 - 2 * m
            tiles_b[...] = _col_tiles(cols_ref[i - 1], eye, ones_bf)
            step(i, tiles_a)
            tiles_a[...] = _col_tiles(cols_ref[jnp.maximum(i - 2, 0)], eye, ones_bf)
            step(i - 1, tiles_b)
            return carry

        lax.fori_loop(0, tb // 2, two_steps, 0)

    blk = pl.BlockSpec((tb, npair, LANES), lambda i: (nblk - 1 - i, 0, 0))
    cblk = pl.BlockSpec((tb, ncol, WIDE), lambda i: (nblk - 1 - i, 0, 0))
    return pl.pallas_call(
        body, name="rwkv_scan_bwd", grid=(nblk,),
        in_specs=[cblk, blk, pl.BlockSpec((tb, npair, HEAD_DIM, LANES), lambda i: (nblk - 1 - i, 0, 0, 0)), blk],
        out_specs=[cblk, blk],
        out_shape=[jax.ShapeDtypeStruct((t, ncol, WIDE), F32), jax.ShapeDtypeStruct((t, npair, LANES), F32)],
        scratch_shapes=[pltpu.VMEM((npair, HEAD_DIM, LANES), F32),
                        pltpu.VMEM((ncol, HEAD_DIM, WIDE), F32), pltpu.VMEM((ncol, HEAD_DIM, WIDE), F32)],
        compiler_params=_params(("arbitrary",)),
    )(cols, v, hist, dy)


SB_BLOCK = 256
SB_HEADS = 2
NT_DIMS = (((1,), (1,)), ((), ()))
TN_DIMS = (((0,), (0,)), ((), ()))
SB_SCALE = 1.0 / math.sqrt(HEAD_DIM)


def _dot2(x, tri):
    hi = x.astype(BF16)
    mid = (x - hi.astype(F32)).astype(BF16)
    return jnp.dot(hi, tri, preferred_element_type=F32) + jnp.dot(mid, tri, preferred_element_type=F32)


def _sb_block_iotas(bs):
    return lax.broadcasted_iota(jnp.int32, (bs, bs), 0), lax.broadcasted_iota(jnp.int32, (bs, bs), 1)


def _sb_fwd(q, k, v):
    h, t, d = q.shape
    bs = _pick(t, (SB_BLOCK,))
    nh = _pick(h, (SB_HEADS,))

    def body(q_ref, k_ref, v_ref, o_ref, l_ref):
        qi = pl.program_id(1)
        ri, ci = _sb_block_iotas(bs)
        tri_ge = (ri >= ci).astype(BF16)
        causal = ci < ri
        qv = [q_ref[hh] for hh in range(nh)]

        def block(hh, j, masked, acc, tailc):
            rows = pl.ds(pl.multiple_of(j * bs, bs), bs)
            z = lax.dot_general(qv[hh], k_ref[hh, rows, :], NT_DIMS, preferred_element_type=F32) * SB_SCALE
            log1m = -_softplus(z)
            if masked:
                log1m = jnp.where(causal, log1m, 0.0)
            cs = _dot2(log1m, tri_ge)
            a = jnp.exp(z + cs + tailc)
            if masked:
                a = jnp.where(causal, a, 0.0)
            acc = acc + jnp.dot(a.astype(BF16), v_ref[hh, rows, :], preferred_element_type=F32)
            return acc, tailc + cs[:, 0:1]

        carry = []
        for hh in range(nh):
            carry += block(hh, qi, True, jnp.zeros((bs, d), F32), jnp.zeros((bs, 1), F32))

        def kstep(n, carry):
            out = []
            for hh in range(nh):
                out += block(hh, qi - 1 - n, False, carry[2 * hh], carry[2 * hh + 1])
            return tuple(out)

        carry = lax.fori_loop(0, qi, kstep, tuple(carry))
        for hh in range(nh):
            o_ref[hh] = carry[2 * hh]
            l_ref[hh] = jnp.broadcast_to(carry[2 * hh + 1], (bs, d))

    qs = pl.BlockSpec((nh, bs, d), lambda hh, i: (hh, i, 0))
    ks = pl.BlockSpec((nh, t, d), lambda hh, i: (hh, 0, 0))
    return pl.pallas_call(
        body, name="sb_attn_fwd", grid=(h // nh, t // bs),
        in_specs=[qs, ks, ks], out_specs=[qs, qs],
        out_shape=[jax.ShapeDtypeStruct((h, t, d), F32)] * 2,
        compiler_params=_params(("parallel", "arbitrary")),
    )(q, k, v)


def _sb_bwd(q, k, v, lsum, do):
    h, t, d = q.shape
    bs = _pick(t, (SB_BLOCK,))
    nh = _pick(h, (SB_HEADS,))

    def body(q_ref, k_ref, v_ref, l_ref, do_ref, dq_ref, dk_ref, dv_ref):
        qi = pl.program_id(1)

        @pl.when(qi == 0)
        def _():
            dk_ref[...] = jnp.zeros_like(dk_ref)
            dv_ref[...] = jnp.zeros_like(dv_ref)

        ri, ci = _sb_block_iotas(bs)
        tri_lt = (ri < ci).astype(BF16)
        causal = ci < ri
        qv = [q_ref[hh] for hh in range(nh)]
        dob = [do_ref[hh].astype(BF16) for hh in range(nh)]
        ltot = [l_ref[hh][:, 0:1] for hh in range(nh)]

        def block(hh, j, masked, dq, pc, ec):
            rows = pl.ds(pl.multiple_of(j * bs, bs), bs)
            kb, vb = k_ref[hh, rows, :], v_ref[hh, rows, :]
            z = lax.dot_general(qv[hh], kb, NT_DIMS, preferred_element_type=F32) * SB_SCALE
            nsp = -_softplus(z)
            log1m = jnp.where(causal, nsp, 0.0) if masked else nsp
            below = _dot2(log1m, tri_lt) + pc
            a = jnp.exp(z + (ltot[hh] - below))
            if masked:
                a = jnp.where(causal, a, 0.0)
            da = lax.dot_general(dob[hh], vb, NT_DIMS, preferred_element_type=F32)
            e = a * da
            ebelow = _dot2(e, tri_lt) + ec
            dz = e * jnp.exp(nsp) - jnp.exp(z + nsp) * ebelow
            if masked:
                dz = jnp.where(causal, dz, 0.0)
            dzb = (dz * SB_SCALE).astype(BF16)
            dq = dq + jnp.dot(dzb, kb, preferred_element_type=F32)
            dk_ref[hh, rows, :] += lax.dot_general(dzb, qv[hh], TN_DIMS, preferred_element_type=F32)
            dv_ref[hh, rows, :] += lax.dot_general(a.astype(BF16), dob[hh], TN_DIMS, preferred_element_type=F32)
            pc = pc + jnp.sum(log1m, axis=1, keepdims=True)
            ec = ec + jnp.sum(e, axis=1, keepdims=True)
            return [dq, pc, ec]

        def kstep(j, carry):
            out = []
            for hh in range(nh):
                out += block(hh, j, False, *carry[3 * hh:3 * hh + 3])
            return tuple(out)

        zcol = jnp.zeros((bs, 1), F32)
        carry = lax.fori_loop(0, qi, kstep, (jnp.zeros((bs, d), F32), zcol, zcol) * nh)
        for hh in range(nh):
            dq_ref[hh] = block(hh, qi, True, *carry[3 * hh:3 * hh + 3])[0]

    qs = pl.BlockSpec((nh, bs, d), lambda hh, i: (hh, i, 0))
    ks = pl.BlockSpec((nh, t, d), lambda hh, i: (hh, 0, 0))
    return pl.pallas_call(
        body, name="sb_attn_bwd", grid=(h // nh, t // bs),
        in_specs=[qs, ks, ks, qs, qs], out_specs=[qs, ks, ks],
        out_shape=[jax.ShapeDtypeStruct((h, t, d), F32)] * 3,
        compiler_params=_params(("parallel", "arbitrary")),
    )(q, k, v, lsum, do)


ANY = pl.BlockSpec(memory_space=pl.ANY)
IN_VMEM = pl.BlockSpec(memory_space=pltpu.VMEM)


def _coords():
    return lax.axis_index("x"), lax.axis_index("y"), lax.axis_index("c")


def _flip(v, bit):
    return 1 - v if bit else v


def _remote(src, dst, send_sem, recv_sem, device):
    return pltpu.make_async_remote_copy(src_ref=src, dst_ref=dst, send_sem=send_sem, recv_sem=recv_sem,
                                        device_id=device, device_id_type=MESH)


def _all_gather8(name, blk):
    m, n = blk.shape

    def body(x_ref, o_ref, send_sems, recv_sems, local_sem):
        x, y, c = _coords()
        own = pltpu.make_async_copy(x_ref, o_ref.at[4 * x + 2 * y + c], local_sem)
        own.start()
        peers = []
        for bits in range(1, N_DEV):
            px, py, pc = _flip(x, (bits >> 2) & 1), _flip(y, (bits >> 1) & 1), _flip(c, bits & 1)
            peers.append((px, py, pc))
        sends = []
        for k, peer in enumerate(peers):
            cp = _remote(x_ref, o_ref.at[4 * x + 2 * y + c], send_sems.at[k], recv_sems.at[k], peer)
            cp.start()
            sends.append(cp)
        for k, (px, py, pc) in enumerate(peers):
            slot = o_ref.at[4 * px + 2 * py + pc]
            _remote(slot, slot, send_sems.at[k], recv_sems.at[k], (px, py, pc)).wait_recv()
        for cp in sends:
            cp.wait_send()
        own.wait()

    return pl.pallas_call(
        body, name=name, out_shape=jax.ShapeDtypeStruct((N_DEV, m, n), blk.dtype),
        in_specs=[IN_VMEM], out_specs=IN_VMEM,
        scratch_shapes=[pltpu.SemaphoreType.DMA((N_DEV - 1,)), pltpu.SemaphoreType.DMA((N_DEV - 1,)),
                        pltpu.SemaphoreType.DMA],
        compiler_params=_params(),
    )(blk)


def _gather_weights(shards):
    nw = len(shards)

    def body(*refs):
        ins, outs = refs[:nw], refs[nw:2 * nw]
        ici_send, ici_recv, d2d_send, d2d_recv, local_sems = refs[2 * nw:]
        x, y, c = _coords()
        chip = 2 * x + y
        sibling = (x, y, 1 - c)
        started = []
        for w in range(nw):
            cp = pltpu.make_async_copy(ins[w], outs[w].at[chip], local_sems.at[w])
            cp.start()
            started.append(cp)
        for w in range(nw):
            half = shards[w].shape[0] // 2
            mine = pl.ds(c * half, half)
            for j, (a, b) in enumerate(XY_MASKS):
                cp = _remote(ins[w].at[mine], outs[w].at[chip, mine], ici_send.at[3 * w + j],
                             ici_recv.at[3 * w + j], (_flip(x, a), _flip(y, b), c))
                cp.start()
                started.append(cp)
        for w in range(nw):
            half = shards[w].shape[0] // 2
            mine = pl.ds(c * half, half)
            for j, (a, b) in enumerate(XY_MASKS):
                px, py = _flip(x, a), _flip(y, b)
                landed = outs[w].at[2 * px + py, mine]
                _remote(landed, landed, ici_send.at[3 * w + j], ici_recv.at[3 * w + j], (px, py, c)).wait_recv()
                cp = _remote(landed, landed, d2d_send.at[3 * w + j], d2d_recv.at[3 * w + j], sibling)
                cp.start()
                started.append(cp)
        for w in range(nw):
            half = shards[w].shape[0] // 2
            theirs = pl.ds((1 - c) * half, half)
            for j, (a, b) in enumerate(XY_MASKS):
                slot = outs[w].at[2 * _flip(x, a) + _flip(y, b), theirs]
                _remote(slot, slot, d2d_send.at[3 * w + j], d2d_recv.at[3 * w + j], sibling).wait_recv()
        for cp in started[nw:]:
            cp.wait_send()
        for cp in started[:nw]:
            cp.wait()

    return pl.pallas_call(
        body, name="gather_weights",
        out_shape=[jax.ShapeDtypeStruct((N_CHIPS,) + s.shape, s.dtype) for s in shards],
        in_specs=[ANY] * nw, out_specs=[ANY] * nw,
        scratch_shapes=[pltpu.SemaphoreType.DMA((3 * nw,))] * 4 + [pltpu.SemaphoreType.DMA((nw,))],
        compiler_params=_params(),
    )(*shards)


def _pair_exchange(grads):
    nw = len(grads)

    def body(*refs):
        ins, outs = refs[:nw], refs[nw:2 * nw]
        send_sems, recv_sems = refs[2 * nw:]
        x, y, c = _coords()
        sibling = (x, y, 1 - c)
        sends = []
        for w in range(nw):
            half = grads[w].shape[1] // 2
            cp = _remote(ins[w].at[:, pl.ds((1 - c) * half, half)], outs[w], send_sems.at[w], recv_sems.at[w], sibling)
            cp.start()
            sends.append(cp)
        for w in range(nw):
            _remote(outs[w], outs[w], send_sems.at[w], recv_sems.at[w], sibling).wait_recv()
        for cp in sends:
            cp.wait_send()

    return pl.pallas_call(
        body, name="grad_pair_exchange",
        out_shape=[jax.ShapeDtypeStruct((N_CHIPS, g.shape[1] // 2, g.shape[2]), g.dtype) for g in grads],
        in_specs=[ANY] * nw, out_specs=[ANY] * nw,
        scratch_shapes=[pltpu.SemaphoreType.DMA((nw,))] * 2,
        compiler_params=_params(),
    )(*grads)


def _chip_exchange(parts):
    nw = len(parts)

    def body(*refs):
        ins, outs = refs[:nw], refs[nw:2 * nw]
        send_sems, recv_sems, local_sems = refs[2 * nw:]
        x, y, c = _coords()
        chip = 2 * x + y
        started = []
        for w in range(nw):
            cp = pltpu.make_async_copy(ins[w].at[chip], outs[w].at[chip], local_sems.at[w])
            cp.start()
            started.append(cp)
        for w in range(nw):
            for j, (a, b) in enumerate(XY_MASKS):
                px, py = _flip(x, a), _flip(y, b)
                cp = _remote(ins[w].at[2 * px + py], outs[w].at[chip], send_sems.at[3 * w + j],
                             recv_sems.at[3 * w + j], (px, py, c))
                cp.start()
                started.append(cp)
        for w in range(nw):
            for j, (a, b) in enumerate(XY_MASKS):
                px, py = _flip(x, a), _flip(y, b)
                slot = outs[w].at[2 * px + py]
                _remote(slot, slot, send_sems.at[3 * w + j], recv_sems.at[3 * w + j], (px, py, c)).wait_recv()
        for cp in started[nw:]:
            cp.wait_send()
        for cp in started[:nw]:
            cp.wait()

    return pl.pallas_call(
        body, name="grad_chip_exchange",
        out_shape=[jax.ShapeDtypeStruct(p.shape, p.dtype) for p in parts],
        in_specs=[ANY] * nw, out_specs=[ANY] * nw,
        scratch_shapes=[pltpu.SemaphoreType.DMA((3 * nw,))] * 2 + [pltpu.SemaphoreType.DMA((nw,))],
        compiler_params=_params(),
    )(*parts)


def _pair_share(halves):
    nw = len(halves)

    def body(*refs):
        ins, outs = refs[:nw], refs[nw:2 * nw]
        send_sems, recv_sems, local_sems = refs[2 * nw:]
        x, y, c = _coords()
        sibling = (x, y, 1 - c)
        started = []
        for w in range(nw):
            half = halves[w].shape[0]
            cp = pltpu.make_async_copy(ins[w], outs[w].at[pl.ds(c * half, half)], local_sems.at[w])
            cp.start()
            started.append(cp)
        for w in range(nw):
            half = halves[w].shape[0]
            cp = _remote(ins[w], outs[w].at[pl.ds(c * half, half)], send_sems.at[w], recv_sems.at[w], sibling)
            cp.start()
            started.append(cp)
        for w in range(nw):
            half = halves[w].shape[0]
            slot = outs[w].at[pl.ds((1 - c) * half, half)]
            _remote(slot, slot, send_sems.at[w], recv_sems.at[w], sibling).wait_recv()
        for cp in started[nw:]:
            cp.wait_send()
        for cp in started[:nw]:
            cp.wait()

    return pl.pallas_call(
        body, name="grad_pair_share",
        out_shape=[jax.ShapeDtypeStruct((2 * h.shape[0], h.shape[1]), h.dtype) for h in halves],
        in_specs=[ANY] * nw, out_specs=[ANY] * nw,
        scratch_shapes=[pltpu.SemaphoreType.DMA((nw,))] * 3,
        compiler_params=_params(),
    )(*halves)


TILE_BYTES = 2 * 1024 * 1024


def _row_tile(rows, cols, mult=8):
    best = None
    for tr in range(mult, rows + 1, mult):
        if rows % tr == 0 and tr * cols * 4 <= TILE_BYTES:
            best = tr
    return best if best is not None else rows


def _pair_add(name, grad, other, core):
    _, half, cols = other.shape
    tr = _row_tile(half, cols)
    nb = half // tr

    def body(core_ref, g_ref, o_ref, out_ref):
        out_ref[...] = g_ref[...] + o_ref[...]

    return pl.pallas_call(
        body, name=name, out_shape=jax.ShapeDtypeStruct(other.shape, F32),
        grid_spec=pltpu.PrefetchScalarGridSpec(
            num_scalar_prefetch=1, grid=(N_CHIPS, nb),
            in_specs=[pl.BlockSpec((None, tr, cols), lambda s, i, core_ref: (s, core_ref[0] * nb + i, 0)),
                      pl.BlockSpec((None, tr, cols), lambda s, i, core_ref: (s, i, 0))],
            out_specs=pl.BlockSpec((None, tr, cols), lambda s, i, core_ref: (s, i, 0))),
        compiler_params=_params(("parallel", "parallel")),
    )(core, grad, other)


def _sum_chips(name, parts):
    _, half, cols = parts.shape
    tr = _row_tile(half, cols)

    def body(p_ref, out_ref):
        out_ref[...] = ((p_ref[0] + p_ref[1]) + p_ref[2]) + p_ref[3]

    return pl.pallas_call(
        body, name=name, out_shape=jax.ShapeDtypeStruct((half, cols), F32), grid=(half // tr,),
        in_specs=[pl.BlockSpec((N_CHIPS, tr, cols), lambda i: (0, i, 0))],
        out_specs=pl.BlockSpec((tr, cols), lambda i: (i, 0)),
        compiler_params=_params(("parallel",)),
    )(parts)


def _adamw_math(w, g, m, v):
    m2 = ADAM_B1 * m + (1.0 - ADAM_B1) * g
    v2 = ADAM_B2 * v + (1.0 - ADAM_B2) * (g * g)
    m_hat = m2 / (1.0 - ADAM_B1 ** ADAM_STEP)
    v_hat = v2 / (1.0 - ADAM_B2 ** ADAM_STEP)
    delta = -ADAM_LR * (m_hat / (jnp.sqrt(v_hat) + ADAM_EPS) + ADAM_WD * w)
    return delta, m2, v2


def _adamw(name, w, g, m, v):
    rows, cols = w.shape
    tr = _row_tile(rows, cols)

    def body(w_ref, g_ref, m_ref, v_ref, d_ref, m2_ref, v2_ref):
        d_ref[...], m2_ref[...], v2_ref[...] = _adamw_math(w_ref[...], g_ref[...], m_ref[...], v_ref[...])

    blk = pl.BlockSpec((tr, cols), lambda i: (i, 0))
    return pl.pallas_call(
        body, name=name, out_shape=[jax.ShapeDtypeStruct(w.shape, F32)] * 3, grid=(rows // tr,),
        in_specs=[blk] * 4, out_specs=[blk] * 3,
        compiler_params=_params(("parallel",)),
    )(w, g, m, v)


def _small_update(gathered, w, m, v):
    def body(gs_ref, w_ref, m_ref, v_ref, g_ref, d_ref, m2_ref, v2_ref):
        g = gs_ref[0]
        for dev in range(1, N_DEV):
            g = g + gs_ref[dev]
        g_ref[...] = g
        d_ref[...], m2_ref[...], v2_ref[...] = _adamw_math(w_ref[...], g, m_ref[...], v_ref[...])

    return pl.pallas_call(
        body, name="small_update", out_shape=[jax.ShapeDtypeStruct(w.shape, F32)] * 4,
        compiler_params=_params(),
    )(gathered, w, m, v)


def _ada_mod(c_all, w_shard, b_shard):
    d, n = w_shard.shape
    tn = _pick(n, (512, 256, 128))

    def body(c_ref, w_ref, b_ref, o_ref):
        act = _silu(c_ref[...]).astype(BF16)
        o_ref[...] = jnp.dot(act, w_ref[...].astype(BF16), preferred_element_type=F32) + b_ref[...]

    return pl.pallas_call(
        body, name="ada_mod", out_shape=jax.ShapeDtypeStruct((c_all.shape[0], n), F32), grid=(n // tn,),
        in_specs=[pl.BlockSpec(c_all.shape, lambda j: (0, 0)), pl.BlockSpec((d, tn), lambda j: (0, j)),
                  pl.BlockSpec((1, tn), lambda j: (0, j))],
        out_specs=pl.BlockSpec((c_all.shape[0], tn), lambda j: (0, j)),
        compiler_params=_params(("parallel",)),
    )(c_all, w_shard, b_shard)


def _ada_grad(c_pad, dmod_pad):
    rows, d = c_pad.shape
    n = dmod_pad.shape[1]
    tn = _pick(n, (512, 256, 128))

    def body(c_ref, g_ref, o_ref):
        act = _silu(c_ref[...]).astype(BF16)
        o_ref[...] = lax.dot_general(act, g_ref[...].astype(BF16), TN_DIMS, preferred_element_type=F32)

    return pl.pallas_call(
        body, name="ada_grad", out_shape=jax.ShapeDtypeStruct((d, n), F32), grid=(n // tn,),
        in_specs=[pl.BlockSpec((rows, d), lambda j: (0, 0)), pl.BlockSpec((rows, tn), lambda j: (0, j))],
        out_specs=pl.BlockSpec((d, tn), lambda j: (0, j)),
        compiler_params=_params(("parallel",)),
    )(c_pad, dmod_pad)


WEIGHTS = ['w_ada', 'b_ada', 'norm1_gain', 'norm2_gain', 'w_in', 'mu_rkv', 'mu_w', 'mu_a', 'mu_g', 'w0', 'w1',
           'w2', 'a0', 'a1', 'a2', 'g1', 'g2', 'k_k', 'k_a', 'r_k', 'ln_x_gain', 'ln_x_bias', 'q_norm_gain',
           'k_norm_gain', 'w_out', 'w_gate_up', 'w_down']
SMALL = ['b_ada', 'norm1_gain', 'norm2_gain', 'mu_rkv', 'mu_w', 'mu_a', 'mu_g', 'w0', 'a0', 'k_k', 'k_a', 'r_k',
         'ln_x_gain', 'ln_x_bias', 'q_norm_gain', 'k_norm_gain']
PACK_ROWS = 8
LOSS_SLOT = LANES


def _shift_down(a):
    return jnp.pad(a[:-1], ((1, 0), (0, 0)))


def _shift_up(a):
    return jnp.pad(a[1:], ((0, 1), (0, 0)))


def _pack_small(vals):
    flat = jnp.concatenate([v.reshape(1, -1) for v in vals], axis=1)
    unit = PACK_ROWS * LANES
    total = -(-flat.shape[1] // unit) * unit
    flat = jnp.pad(flat, ((0, 0), (0, total - flat.shape[1])))
    return flat.reshape(PACK_ROWS, total // PACK_ROWS)


def kernel(x, c, w_ada, b_ada, norm1_gain, norm2_gain, w_in, mu_rkv, mu_w, mu_a, mu_g, w0, w1, w2, a0, a1, a2, g1, g2, k_k, k_a, r_k, ln_x_gain, ln_x_bias, q_norm_gain, k_norm_gain, w_out, w_gate_up, w_down, loss_target, m_w_ada, m_b_ada, m_norm1_gain, m_norm2_gain, m_w_in, m_mu_rkv, m_mu_w, m_mu_a, m_mu_g, m_w0, m_w1, m_w2, m_a0, m_a1, m_a2, m_g1, m_g2, m_k_k, m_k_a, m_r_k, m_ln_x_gain, m_ln_x_bias, m_q_norm_gain, m_k_norm_gain, m_w_out, m_w_gate_up, m_w_down, v_w_ada, v_b_ada, v_norm1_gain, v_norm2_gain, v_w_in, v_mu_rkv, v_mu_w, v_mu_a, v_mu_g, v_w0, v_w1, v_w2, v_a0, v_a1, v_a2, v_g1, v_g2, v_k_k, v_k_a, v_r_k, v_ln_x_gain, v_ln_x_bias, v_q_norm_gain, v_k_norm_gain, v_w_out, v_w_gate_up, v_w_down):
    given = dict(locals())
    wt = {n: given[n][0] for n in WEIGHTS}
    mom = {n: given["m_" + n][0] for n in WEIGHTS}
    var = {n: given["v_" + n][0] for n in WEIGHTS}
    for tree in (wt, mom, var):
        tree["b_ada"] = tree["b_ada"].reshape(1, -1)
        for n in SMALL[1:]:
            tree[n] = tree[n].reshape(1, -1)

    ax, ay, ac = _coords()
    chip = 2 * ax + ay
    dev = 4 * ax + 2 * ay + ac
    xs, target = x[0], loss_target[0]
    t, d = xs.shape
    dr = wt["w0"].shape[1]
    ds = d - dr
    nh = ds // HEAD_DIM
    dff = wt["w_down"].shape[0] * N_CHIPS
    n_ada = wt["w_ada"].shape[1]
    lw, la, lg = wt["w1"].shape[1], wt["a1"].shape[1], wt["g1"].shape[1]

    def lora_a(tree):
        return jnp.concatenate([tree["w1"], tree["a1"], tree["g1"]], axis=1)

    def lora_b(tree):
        return jnp.concatenate([tree["w2"], tree["a2"], tree["g2"]], axis=0)

    shards = [wt["w_in"], wt["w_out"], wt["w_gate_up"], wt["w_down"], lora_a(wt), lora_b(wt)]
    full_in, full_out, full_gu, full_down, full_la, full_lb = _gather_weights([s.astype(BF16) for s in shards])
    full_out = full_out.reshape(d, d)
    full_down = full_down.reshape(dff, d)
    full_la = full_la.reshape(d, lw + la + lg).astype(F32)
    full_lb = full_lb.transpose(1, 0, 2).reshape(lw + la + lg, dr).astype(F32)
    w1f, a1f, g1f = full_la[:, :lw], full_la[:, lw:lw + la], full_la[:, lw + la:]
    w2f, a2f, g2f = full_lb[:lw], full_lb[lw:lw + la], full_lb[lw + la:]

    c_all = _all_gather8("gather_c", c.reshape(PACK_ROWS, d // PACK_ROWS)).reshape(N_DEV, d)
    b_shard = lax.dynamic_slice(wt["b_ada"], (0, chip * n_ada), (1, n_ada))
    mod_part = _ada_mod(c_all, wt["w_ada"], b_shard)
    mod_all = _all_gather8("gather_mod", mod_part)[::2]
    mod = lax.dynamic_slice(mod_all, (0, dev, 0), (N_CHIPS, 1, n_ada)).reshape(1, N_CHIPS * n_ada)
    sh1, sc1, gt1, sh2, sc2, gt2 = [mod[:, i * d:(i + 1) * d] for i in range(6)]

    h, h_bf = _rowwise("norm1", _fn_norm1, [xs], [wt["norm1_gain"], sc1, sh1], [(d, F32), (d, BF16)], 256)
    hp = _shift_down(h)
    p = _matmul("mm_in", h_bf, full_in, b_shards=True, tm=512, tn=1536, tk=2048)
    p_rkv, p_sb = (p, 3 * dr, 0), (p, 3 * ds, 1)
    pp = _shift_down(p[:, :3 * dr])
    pre_rows = [h, hp, p_rkv, pp]
    pre_params = [wt["mu_rkv"], wt["mu_w"], wt["mu_a"], wt["mu_g"], wt["w0"], wt["a0"], wt["k_k"], wt["k_a"],
                  w1f, w2f, a1f, a2f, g1f, g2f]
    pre = _rowwise("rwkv_pre", _fn_rwkv_pre, pre_rows, pre_params, [(dr, F32)] * 7, 64)
    r_, w_, k2, v_, rem, wr, g_ = pre
    npair = dr // LANES
    scan_cols = _pack_cols((rem, w_, wr, k2, r_), t, npair)
    v3 = v_.reshape(t, npair, LANES)
    y3, hist = _scan_fwd(scan_cols, v3)
    y_raw = y3.reshape(t, dr)
    post_rows = [y_raw, r_, k2, v_, g_]
    post_params = [wt["ln_x_gain"], wt["ln_x_bias"], wt["r_k"]]
    (y_rwkv,) = _rowwise("rwkv_post", _fn_rwkv_post, post_rows, post_params, [(dr, BF16)], 256)

    qg = jnp.tile(wt["q_norm_gain"], (1, nh))
    kg = jnp.tile(wt["k_norm_gain"], (1, nh))
    qn, kn, vs = _rowwise("qk_norm", _fn_qk_norm, [p_sb], [qg, kg], [(ds, BF16)] * 3, 256)

    def to_heads(a):
        return a.reshape(t, nh, HEAD_DIM).transpose(1, 0, 2)

    def from_heads(a):
        return a.transpose(1, 0, 2).reshape(t, ds)

    qh, kh, vh = to_heads(qn), to_heads(kn), to_heads(vs)
    o_h, lsum = _sb_fwd(qh, kh, vh)
    ycat = jnp.concatenate([y_rwkv, from_heads(o_h).astype(BF16)], axis=1)
    mix = _matmul("mm_out", ycat, full_out, tm=512, tn=1024, tk=2048)
    norm2_params = [gt1, wt["norm2_gain"], sc2, sh2]
    x1, h2 = _rowwise("mix_norm2", _fn_mix_norm2, [xs, mix], norm2_params, [(d, F32), (d, BF16)], 256)
    gu = _matmul("mm_gate_up", h2, full_gu, b_shards=True, tm=512, tn=1408, tk=2048)
    gate_up = [(gu, dff, 0), (gu, dff, 1)]
    (act,) = _rowwise("swiglu", _fn_swiglu, gate_up, [], [(dff, BF16)], 256)
    dn = _matmul("mm_down", act, full_down, tm=512, tn=1024, tk=1408)
    loss_vec, dout, ddn, dgt2 = _loss_head("loss_head", x1, dn, target, gt2)

    dact = _matmul("mm_down_dx", ddn, full_down, tb=True, tm=512, tn=1408, tk=2048)
    gw_down = _matmul("mm_down_dw", act, ddn, ta=True, tm=1408, tn=1024, tk=512)
    (dgate, dup), _ = _rowwise_bwd("swiglu_bwd", _fn_swiglu, gate_up, [], [[dact]], [BF16, BF16], [], 256)
    dgu = jnp.concatenate([dgate, dup], axis=1)
    dh2 = _matmul("mm_gate_up_dx", dgu, full_gu, tb=True, b_shards=True, tm=512, tn=1024, tk=1408)
    gw_gu = _matmul("mm_gate_up_dw", h2, dgu, ta=True, out_shards=True, tm=1024, tn=1408, tk=512)
    (dx_a, dmix), (dgt1, dgain2, dsc2, dsh2) = _rowwise_bwd(
        "mix_norm2_bwd", _fn_mix_norm2, [xs, mix], norm2_params, [[dout], [dh2]], [F32, BF16], [True] * 4, 128)
    dycat = _matmul("mm_out_dx", dmix, full_out, tb=True, tm=512, tn=1024, tk=2048)
    gw_out = _matmul("mm_out_dw", ycat, dmix, ta=True, tm=1024, tn=1024, tk=512)
    (dy_raw, dr_f, dk_f, dv_f, dg), (dlng, dlnb, drk) = _rowwise_bwd(
        "rwkv_post_bwd", _fn_rwkv_post, post_rows, post_params, [[(dycat, dr, 0)]], [F32] * 5, [True] * 3, 128)
    dcols, dv3 = _scan_bwd(scan_cols, v3, hist, dy_raw.reshape(t, npair, LANES))
    drem_s, dw_s, dwr_s, dk_s, dr_s = _unpack_cols(dcols, t, npair)
    dv_s = dv3.reshape(t, dr)
    do_h = to_heads(dycat[:, dr:])
    dqh, dkh, dvh = _sb_bwd(qh, kh, vh, lsum, do_h)
    (dp_sb,), (dqg, dkg) = _rowwise_bwd(
        "qk_norm_bwd", _fn_qk_norm, [p_sb], [qg, kg], [[from_heads(dqh)], [from_heads(dkh)], [from_heads(dvh)]],
        [F32], [True, True], 128)
    pre_cts = [[dr_s, dr_f], [dw_s], [dk_s, dk_f], [dv_s, dv_f], [drem_s], [dwr_s], [dg]]
    (dh_a, dhp, dp_rkv, dpp), pre_g = _rowwise_bwd(
        "rwkv_pre_bwd", _fn_rwkv_pre, pre_rows, pre_params, pre_cts, [F32] * 4, [True] * 14, 64)
    dp = jnp.concatenate([dp_rkv + _shift_up(dpp), dp_sb], axis=1).astype(BF16)
    dh_mm = _matmul("mm_in_dx", dp, full_in, tb=True, b_shards=True, tm=512, tn=1024, tk=1536)
    gw_in = _matmul("mm_in_dw", h_bf, dp, ta=True, out_shards=True, tm=1024, tn=1536, tk=512)
    (grad_x,), (dgain1, dsc1, dsh1) = _rowwise_bwd(
        "norm1_bwd", _fn_norm1, [xs], [wt["norm1_gain"], sc1, sh1], [[dh_a, dh_mm, _shift_up(dhp)], []],
        [F32], [True] * 3, 128, add_to_first=[dx_a])

    g_mu_rkv, g_mu_w, g_mu_a, g_mu_g, g_w0, g_a0, g_kk, g_ka, gw1, gw2, ga1, ga2, gg1, gg2 = pre_g
    g_la = jnp.concatenate([gw1, ga1, gg1], axis=1).reshape(N_CHIPS, d // N_CHIPS, lw + la + lg)
    g_lb = jnp.concatenate([gw2, ga2, gg2], axis=0)
    g_lb = g_lb.reshape(lw + la + lg, N_CHIPS, dr // N_CHIPS).transpose(1, 0, 2)
    local = [gw_in, gw_out.reshape(N_CHIPS, d // N_CHIPS, d), gw_gu, gw_down.reshape(N_CHIPS, dff // N_CHIPS, d),
             g_la, g_lb]
    names = ["w_in", "w_out", "w_gate_up", "w_down", "lora_a", "lora_b"]
    core = ac.reshape(1).astype(jnp.int32)
    from_sibling = _pair_exchange(local)
    pair_sums = [_pair_add("pair_add_" + n, g, o, core) for n, g, o in zip(names, local, from_sibling)]
    from_chips = _chip_exchange(pair_sums)
    halves = [_sum_chips("chip_sum_" + n, q) for n, q in zip(names, from_chips)]
    r_in, r_out, r_gu, r_down, r_la, r_lb = _pair_share(halves)

    dmod = jnp.concatenate([dsh1, dsc1, dgt1, dsh2, dsc2, dgt2], axis=1)
    dqg = dqg.reshape(nh, HEAD_DIM).sum(axis=0, keepdims=True)
    dkg = dkg.reshape(nh, HEAD_DIM).sum(axis=0, keepdims=True)
    small_g = [dmod, dgain1, dgain2, g_mu_rkv, g_mu_w, g_mu_a, g_mu_g, g_w0, g_a0, g_kk, g_ka, drk, dlng, dlnb,
               dqg, dkg]
    lead = jnp.zeros((1, LOSS_SLOT), F32)
    packed = _pack_small([loss_vec] + small_g)
    gathered = _all_gather8("gather_small", packed)
    sm_g, sm_d, sm_m, sm_v = _small_update(gathered, _pack_small([lead] + [wt[n] for n in SMALL]),
                                           _pack_small([lead] + [mom[n] for n in SMALL]),
                                           _pack_small([lead] + [var[n] for n in SMALL]))
    loss = sm_g.reshape(-1)[0]

    def unpack(packed_arr):
        flat, out, pos = packed_arr.reshape(-1), {}, LOSS_SLOT
        for n in SMALL:
            size = wt[n].size
            out[n] = flat[pos:pos + size]
            pos += size
        return out

    res = {"grad": unpack(sm_g), "delta": unpack(sm_d), "m": unpack(sm_m), "v": unpack(sm_v)}

    dmod_all = gathered.reshape(N_DEV, -1)[:, LOSS_SLOT:LOSS_SLOT + N_CHIPS * n_ada]
    dmod_cols = lax.dynamic_slice(dmod_all, (0, chip * n_ada), (N_DEV, n_ada))
    pad8 = ((0, N_DEV), (0, 0))
    res["grad"]["w_ada"] = _ada_grad(jnp.pad(c_all, pad8), jnp.pad(dmod_cols, pad8))

    res["grad"].update(w_in=r_in, w_out=r_out, w_gate_up=r_gu, w_down=r_down)
    for n in ("w_ada", "w_in", "w_out", "w_gate_up", "w_down"):
        res["delta"][n], res["m"][n], res["v"][n] = _adamw("adamw_" + n, wt[n], res["grad"][n], mom[n], var[n])
    la_d, la_m, la_v = _adamw("adamw_lora_a", lora_a(wt), r_la, lora_a(mom), lora_a(var))
    lb_d, lb_m, lb_v = _adamw("adamw_lora_b", lora_b(wt), r_lb, lora_b(mom), lora_b(var))
    for key, pa, pb in (("grad", r_la, r_lb), ("delta", la_d, lb_d), ("m", la_m, lb_m), ("v", la_v, lb_v)):
        res[key].update(w1=pa[:, :lw], a1=pa[:, lw:lw + la], g1=pa[:, lw + la:],
                        w2=pb[:lw], a2=pb[lw:lw + la], g2=pb[lw + la:])

    outs = [loss, grad_x[None]]
    for key in ("grad", "delta", "m", "v"):
        outs += [res[key][n].reshape(given[n].shape) for n in WEIGHTS]
    return tuple(outs)
```

```python
import functools
import math

import jax
import jax.numpy as jnp
from jax import lax
from jax.experimental import pallas as pl
from jax.experimental.pallas import tpu as pltpu

F32 = jnp.float32
BF16 = jnp.bfloat16
HEAD_DIM = 64
LANES = 128
RMS_EPS = 1e-6
GN_EPS = 64e-5
L2_EPS = 1e-12
ADAM_LR, ADAM_B1, ADAM_B2, ADAM_EPS, ADAM_WD, ADAM_STEP = 0.001, 0.9, 0.999, 1e-08, 0.01, 10
VMEM_LIMIT = 56 * 1024 * 1024
MESH = pl.DeviceIdType.MESH
HI = lax.Precision.HIGHEST
N_CHIPS = 4
N_DEV = 8
XY_MASKS = ((1, 0), (0, 1), (1, 1))


def _pick(dim, prefs):
    for p in prefs:
        if dim % p == 0:
            return p
    return dim


def _params(sem=None, vmem=VMEM_LIMIT):
    return pltpu.CompilerParams(dimension_semantics=sem, vmem_limit_bytes=vmem)


def _sigmoid(x):
    return 1.0 / (1.0 + jnp.exp(-x))


@jax.custom_vjp
def _softplus(x):
    return jnp.maximum(x, 0.0) + jnp.log(1.0 + jnp.exp(-jnp.abs(x)))


_softplus.defvjp(lambda x: (_softplus(x), x), lambda x, g: (g * _sigmoid(x),))


def _silu(x):
    return x * _sigmoid(x)


@jax.custom_vjp
def _bdot(a, b):
    return jnp.dot(a.astype(BF16), b.astype(BF16), preferred_element_type=F32)


def _bdot_bwd(res, g):
    a, b = res
    gb = g.astype(BF16)
    da = lax.dot_general(gb, b.astype(BF16), (((1,), (1,)), ((), ())), preferred_element_type=F32)
    db = lax.dot_general(a.astype(BF16), gb, (((0,), (0,)), ((), ())), preferred_element_type=F32)
    return da.astype(a.dtype), db.astype(b.dtype)


_bdot.defvjp(lambda a, b: (_bdot(a, b), (a, b)), _bdot_bwd)


def _head_ones():
    i = lax.broadcasted_iota(jnp.int32, (LANES, LANES), 0) // HEAD_DIM
    j = lax.broadcasted_iota(jnp.int32, (LANES, LANES), 1) // HEAD_DIM
    return (i == j).astype(F32)


def _hdot(x, ones):
    return jnp.dot(x, ones, precision=HI, preferred_element_type=F32)


@jax.custom_vjp
def _segsum(x):
    ones = _head_ones()
    parts = [_hdot(x[:, LANES * j:LANES * (j + 1)], ones) for j in range(x.shape[1] // LANES)]
    return parts[0] if len(parts) == 1 else jnp.concatenate(parts, axis=1)


_segsum.defvjp(lambda x: (_segsum(x), None), lambda _, g: (_segsum(g),))


def _rms(x, gain):
    return x * lax.rsqrt(jnp.mean(x * x, axis=-1, keepdims=True) + RMS_EPS) * gain


def _matmul(name, a, b, *, ta=False, tb=False, b_shards=False, out_shards=False, out_dtype=F32,
            tm=512, tn=512, tk=512):
    if ta:
        kdim, m = a.shape
    else:
        m, kdim = a.shape
    if b_shards:
        if tb:
            n, ks = b.shape[1], b.shape[2]
            assert ks * N_CHIPS == kdim
        else:
            ns = b.shape[2]
            n = ns * N_CHIPS
            assert b.shape[1] == kdim
    else:
        n = b.shape[0] if tb else b.shape[1]
    tm = _pick(m, (tm, 512, 256, 128))
    n_part = n // N_CHIPS if (out_shards or (b_shards and not tb)) else n
    tn = _pick(n_part, (tn, 512, 256, 128))
    k_part = kdim // N_CHIPS if (b_shards and tb) else kdim
    tk = _pick(k_part, (tk, 512, 256, 128))
    nb = n_part // tn
    kb = k_part // tk
    nk = kdim // tk
    grid = (m // tm, n // tn, nk)

    if ta:
        a_spec = pl.BlockSpec((tk, tm), lambda i, j, k: (k, i))
    else:
        a_spec = pl.BlockSpec((tm, tk), lambda i, j, k: (i, k))
    if b_shards and tb:
        b_spec = pl.BlockSpec((None, tn, tk), lambda i, j, k: (k // kb, j, k % kb))
    elif b_shards:
        b_spec = pl.BlockSpec((None, tk, tn), lambda i, j, k: (j // nb, k, j % nb))
    elif tb:
        b_spec = pl.BlockSpec((tn, tk), lambda i, j, k: (j, k))
    else:
        b_spec = pl.BlockSpec((tk, tn), lambda i, j, k: (k, j))
    if out_shards:
        o_spec = pl.BlockSpec((None, tm, tn), lambda i, j, k: (j // nb, i, j % nb))
        o_shape = jax.ShapeDtypeStruct((N_CHIPS, m, n_part), out_dtype)
    else:
        o_spec = pl.BlockSpec((tm, tn), lambda i, j, k: (i, j))
        o_shape = jax.ShapeDtypeStruct((m, n), out_dtype)
    dims = (((0 if ta else 1,), (1 if tb else 0,)), ((), ()))

    def body(a_ref, b_ref, o_ref, acc_ref):
        k = pl.program_id(2)

        @pl.when(k == 0)
        def _():
            acc_ref[...] = jnp.zeros_like(acc_ref)

        acc_ref[...] += lax.dot_general(a_ref[...].astype(BF16), b_ref[...].astype(BF16), dims,
                                        preferred_element_type=F32)

        @pl.when(k == nk - 1)
        def _():
            o_ref[...] = acc_ref[...].astype(o_ref.dtype)

    return pl.pallas_call(
        body, name=name, grid=grid, in_specs=[a_spec, b_spec], out_specs=o_spec, out_shape=o_shape,
        scratch_shapes=[pltpu.VMEM((tm, tn), F32)],
        compiler_params=_params(("parallel", "parallel", "arbitrary")),
    )(a, b)


def _row_in(spec, tile):
    if isinstance(spec, tuple):
        arr, width, cb = spec
    else:
        arr, width, cb = spec, spec.shape[1], 0
    return arr, pl.BlockSpec((tile, width), lambda i, cb=cb: (i, cb))


def _full_spec(arr):
    nd = arr.ndim
    return pl.BlockSpec(arr.shape, lambda i, nd=nd: (0,) * nd)


def _rowwise(name, fn, rows, params, outs, tile):
    t = (rows[0][0] if isinstance(rows[0], tuple) else rows[0]).shape[0]
    tile = _pick(t, (tile,))
    arrs, specs = zip(*[_row_in(s, tile) for s in rows])
    nr, npar = len(rows), len(params)

    def body(*refs):
        rv = [r[...].astype(F32) for r in refs[:nr]]
        pv = [p[...] for p in refs[nr:nr + npar]]
        res = fn(*rv, *pv)
        for o_ref, val in zip(refs[nr + npar:], res):
            o_ref[...] = val.astype(o_ref.dtype)

    return pl.pallas_call(
        body, name=name, grid=(t // tile,),
        in_specs=list(specs) + [_full_spec(p) for p in params],
        out_specs=[pl.BlockSpec((tile, w), lambda i: (i, 0)) for w, _ in outs],
        out_shape=[jax.ShapeDtypeStruct((t, w), d) for w, d in outs],
        compiler_params=_params(("parallel",)),
    )(*arrs, *params)


def _rowwise_bwd(name, fn, rows, params, cts, row_grads, param_grads, tile, add_to_first=()):
    t = (rows[0][0] if isinstance(rows[0], tuple) else rows[0]).shape[0]
    tile = _pick(t, (tile,))
    arrs, specs = zip(*[_row_in(s, tile) for s in rows])
    n_add = len(add_to_first)
    flat_cts = [c for group in cts for c in group] + list(add_to_first)
    c_arrs, c_specs = zip(*[_row_in(s, tile) for s in flat_cts])
    nr, npar, nc = len(rows), len(params), len(flat_cts)
    rg_idx = [i for i, d in enumerate(row_grads) if d is not None]
    pg_idx = [i for i, d in enumerate(param_grads) if d]

    def body(*refs):
        rv = [r[...].astype(F32) for r in refs[:nr]]
        pv = [p[...] for p in refs[nr:nr + npar]]
        cv = [c[...].astype(F32) for c in refs[nr + npar:nr + npar + nc]]
        o_refs = refs[nr + npar + nc:]
        outs, vjp = jax.vjp(fn, *rv, *pv)
        ct, pos = [], 0
        for group, o in zip(cts, outs):
            if group:
                acc = cv[pos]
                for extra in cv[pos + 1:pos + len(group)]:
                    acc = acc + extra
                pos += len(group)
            else:
                acc = jnp.zeros_like(o)
            ct.append(acc)
        grads = list(vjp(tuple(ct)))
        for extra in cv[nc - n_add:]:
            grads[rg_idx[0]] = grads[rg_idx[0]] + extra
        for o_ref, i in zip(o_refs[:len(rg_idx)], rg_idx):
            o_ref[...] = grads[i].astype(o_ref.dtype)
        first = pl.program_id(0) == 0
        for o_ref, i in zip(o_refs[len(rg_idx):], pg_idx):
            g = grads[nr + i].astype(F32)

            @pl.when(first)
            def _(o_ref=o_ref, g=g):
                o_ref[...] = g

            @pl.when(jnp.logical_not(first))
            def _(o_ref=o_ref, g=g):
                o_ref[...] += g

    def width(i):
        s = rows[i]
        return s[1] if isinstance(s, tuple) else s.shape[1]

    out_specs = [pl.BlockSpec((tile, width(i)), lambda i_: (i_, 0)) for i in rg_idx]
    out_shape = [jax.ShapeDtypeStruct((t, width(i)), row_grads[i]) for i in rg_idx]
    out_specs += [_full_spec(params[i]) for i in pg_idx]
    out_shape += [jax.ShapeDtypeStruct(params[i].shape, F32) for i in pg_idx]
    res = pl.pallas_call(
        body, name=name, grid=(t // tile,),
        in_specs=list(specs) + [_full_spec(p) for p in params] + list(c_specs),
        out_specs=out_specs, out_shape=out_shape,
        compiler_params=_params(("arbitrary",)),
    )(*arrs, *params, *c_arrs)
    return res[:len(rg_idx)], res[len(rg_idx):]


def _fn_norm1(x, gain, sc, sh):
    h = _rms(x, gain) * (1.0 + sc) + sh
    return h, h


def _fn_rwkv_pre(h, hp, p, pp, mu_rkv, mu_w, mu_a, mu_g, w0, a0, k_k, k_a, w1, w2, a1, a2, g1, g2):
    d = p.shape[1] // 3
    dh = hp - h
    xw = h + dh * mu_w
    xa = h + dh * mu_a
    xg = h + dh * mu_g
    pr = p + (pp - p) * mu_rkv
    r, k, v = pr[:, :d], pr[:, d:2 * d], pr[:, 2 * d:]
    w_log = -_softplus(-(w0 + _bdot(jnp.tanh(_bdot(xw, w1)), w2))) - 0.5
    decay = jnp.exp(-jnp.exp(w_log))
    a = _sigmoid(a0 + _bdot(_bdot(xa, a1), a2))
    g = _bdot(_sigmoid(_bdot(xg, g1)), g2)
    kk = k * k_k
    kk = kk * lax.rsqrt(_segsum(kk * kk) + L2_EPS)
    k2 = k * (1.0 + (a - 1.0) * k_a)
    return r, decay, k2, v, -kk, kk * a, g


def _fn_rwkv_post(y, r, k2, v, g, ln_g, ln_b, r_k):
    inv = 1.0 / HEAD_DIM
    mean = _segsum(y) * inv
    yc = y - mean
    var = _segsum(yc * yc) * inv
    yn = yc * lax.rsqrt(var + GN_EPS) * ln_g + ln_b
    bonus = _segsum(r * k2 * r_k) * v
    return ((yn + bonus) * g,)


def _fn_rwkv_post_cat(y, r, k2, v, g, o_sb, ln_g, ln_b, r_k):
    return (jnp.concatenate([_fn_rwkv_post(y, r, k2, v, g, ln_g, ln_b, r_k)[0], o_sb], axis=1),)


def _fn_qk_norm(p, qg, kg):
    d = p.shape[1] // 3
    q, k, v = p[:, :d], p[:, d:2 * d], p[:, 2 * d:]
    inv = 1.0 / HEAD_DIM
    qn = q * lax.rsqrt(_segsum(q * q) * inv + RMS_EPS) * qg
    kn = k * lax.rsqrt(_segsum(k * k) * inv + RMS_EPS) * kg
    return qn, kn, v


def _fn_mix_norm2(x, mix, gt1, gain, sc, sh):
    x1 = x + gt1 * mix
    h2 = _rms(x1, gain) * (1.0 + sc) + sh
    return x1, h2


def _fn_swiglu(gate, up):
    return (_silu(gate) * up,)


def _swiglu_bwd(gu, dact, tile=128):
    t, dff = dact.shape
    tile = _pick(t, (tile,))

    def body(gate_ref, up_ref, d_ref, o_ref):
        _, vjp = jax.vjp(_fn_swiglu, gate_ref[...], up_ref[...])
        dgate, dup = vjp((d_ref[...],))
        o_ref[:, :dff] = dgate.astype(o_ref.dtype)
        o_ref[:, dff:] = dup.astype(o_ref.dtype)

    return pl.pallas_call(
        body, name="swiglu_bwd", grid=(t // tile,),
        in_specs=[pl.BlockSpec((tile, dff), lambda i: (i, 0)), pl.BlockSpec((tile, dff), lambda i: (i, 1)),
                  pl.BlockSpec((tile, dff), lambda i: (i, 0))],
        out_specs=pl.BlockSpec((tile, 2 * dff), lambda i: (i, 0)),
        out_shape=jax.ShapeDtypeStruct((t, 2 * dff), BF16),
        compiler_params=_params(("parallel",)),
    )(gu, gu, dact)


def _loss_head(name, x1, dn, target, gt2, tile=256):
    t, d = x1.shape
    tile = _pick(t, (tile,))

    def body(x1_ref, dn_ref, tg_ref, gt_ref, loss_ref, dout_ref, ddn_ref, dgt_ref):
        dnv = dn_ref[...]
        gt = gt_ref[...]
        err = x1_ref[...] + gt * dnv - tg_ref[...]
        dout = err * (1.0 / d)
        dout_ref[...] = dout
        ddn_ref[...] = (dout * gt).astype(ddn_ref.dtype)
        part = 0.5 * jnp.sum(jnp.sum(err * dout, axis=-1, keepdims=True), axis=0, keepdims=True)
        dgt = jnp.sum(dout * dnv, axis=0, keepdims=True)
        first = pl.program_id(0) == 0

        @pl.when(first)
        def _():
            loss_ref[...] = jnp.broadcast_to(part, loss_ref.shape)
            dgt_ref[...] = dgt

        @pl.when(jnp.logical_not(first))
        def _():
            loss_ref[...] += jnp.broadcast_to(part, loss_ref.shape)
            dgt_ref[...] += dgt

    row = pl.BlockSpec((tile, d), lambda i: (i, 0))
    vec = pl.BlockSpec((1, d), lambda i: (0, 0))
    return pl.pallas_call(
        body, name=name, grid=(t // tile,),
        in_specs=[row, row, row, vec],
        out_specs=[pl.BlockSpec((1, LANES), lambda i: (0, 0)), row, row, vec],
        out_shape=[jax.ShapeDtypeStruct((1, LANES), F32), jax.ShapeDtypeStruct((t, d), F32),
                   jax.ShapeDtypeStruct((t, d), BF16), jax.ShapeDtypeStruct((1, d), F32)],
        compiler_params=_params(("arbitrary",)),
    )(x1, dn, target, gt2)


SCAN_BLOCK = 32
N_COL = 5
WIDE = 2 * LANES


def _wide_eye():
    i = lax.broadcasted_iota(jnp.int32, (HEAD_DIM, WIDE), 0)
    j = lax.broadcasted_iota(jnp.int32, (HEAD_DIM, WIDE), 1) % HEAD_DIM
    return (i == j).astype(BF16)


def _wide_ones():
    i = lax.broadcasted_iota(jnp.int32, (WIDE, WIDE), 0) // HEAD_DIM
    j = lax.broadcasted_iota(jnp.int32, (WIDE, WIDE), 1) // HEAD_DIM
    return (i == j).astype(BF16)


COL_PIECES = (1, 3, 3, 3, 1)


def _col_tiles(refs, i, nq, eye, ones_bf):
    pieces = [[], [], []]
    for ref, n_pieces in zip(refs, COL_PIECES):
        full = ref[pl.ds(i, 1), :]
        for q in range(nq):
            rest = full[:, q * WIDE:(q + 1) * WIDE]
            for level in range(n_pieces):
                part = rest.astype(BF16)
                pieces[level].append(part * eye)
                rest = rest - part.astype(F32)
    count = [len(level) * HEAD_DIM for level in pieces]
    out = jnp.dot(jnp.concatenate(pieces[0] + pieces[1] + pieces[2], axis=0), ones_bf, preferred_element_type=F32)
    first, second, third = out[:count[0]], out[count[0]:count[0] + count[1]], out[count[0] + count[1]:]
    lo, hi = nq * HEAD_DIM, count[0] - nq * HEAD_DIM
    exact = first[lo:hi] + second + third
    return jnp.concatenate([first[:lo], exact, first[hi:]], axis=0).reshape(N_COL * nq, HEAD_DIM, WIDE)


def _bf16_round(x):
    bits = lax.bitcast_convert_type(x, jnp.uint32)
    bits = (bits + jnp.uint32(0x7FFF) + ((bits >> 16) & jnp.uint32(1))) & jnp.uint32(0xFFFF0000)
    return lax.bitcast_convert_type(bits, F32)


def _pair_tile(tiles_ref, n, p, nq):
    return tiles_ref[n * nq + p // 2, :, (p % 2) * LANES:(p % 2 + 1) * LANES]


def _scan_fwd(rem, w, wr, k, r, v):
    t, dr = v.shape
    npair, nq = dr // LANES, dr // WIDE
    tb = _pick(t, (SCAN_BLOCK,))

    def body(rem_ref, w_ref, wr_ref, k_ref, r_ref, v_ref, y_ref, hist_ref, s_ref, tiles_a, tiles_b):
        @pl.when(pl.program_id(0) == 0)
        def _():
            s_ref[...] = jnp.zeros_like(s_ref)

        eye, ones_bf = _wide_eye(), _wide_ones()
        col_refs = (rem_ref, w_ref, wr_ref, k_ref, r_ref)

        def step(i, tiles_ref):
            v_full, y_rows = v_ref[pl.ds(i, 1), :], []
            for p in range(npair):
                s = s_ref[p]
                hist_ref[i, p] = s
                c_rem, c_w, c_wr, c_k, c_r = [_pair_tile(tiles_ref, n, p, nq) for n in range(N_COL)]
                sa = jnp.sum(_bf16_round(s) * c_rem, axis=0, keepdims=True)
                s2 = s * c_w + c_wr * sa + c_k * v_full[:, p * LANES:(p + 1) * LANES]
                y_rows.append(jnp.sum(_bf16_round(s2) * c_r, axis=0, keepdims=True))
                s_ref[p] = s2
            y_ref[pl.ds(i, 1), :] = jnp.concatenate(y_rows, axis=1)

        tiles_a[...] = _col_tiles(col_refs, 0, nq, eye, ones_bf)

        def two_steps(m, carry):
            i = 2 * m
            tiles_b[...] = _col_tiles(col_refs, i + 1, nq, eye, ones_bf)
            step(i, tiles_a)
            tiles_a[...] = _col_tiles(col_refs, jnp.minimum(i + 2, tb - 1), nq, eye, ones_bf)
            step(i + 1, tiles_b)
            return carry

        lax.fori_loop(0, tb // 2, two_steps, 0)

    blk = pl.BlockSpec((tb, dr), lambda i: (i, 0))
    tiles = pltpu.VMEM((N_COL * nq, HEAD_DIM, WIDE), F32)
    return pl.pallas_call(
        body, name="rwkv_scan_fwd", grid=(t // tb,),
        in_specs=[blk] * 6,
        out_specs=[blk, pl.BlockSpec((tb, npair, HEAD_DIM, LANES), lambda i: (i, 0, 0, 0))],
        out_shape=[jax.ShapeDtypeStruct((t, dr), F32), jax.ShapeDtypeStruct((t, npair, HEAD_DIM, LANES), F32)],
        scratch_shapes=[pltpu.VMEM((npair, HEAD_DIM, LANES), F32), tiles, tiles],
        compiler_params=_params(("arbitrary",)),
    )(rem, w, wr, k, r, v)


def _scan_bwd(rem, w, wr, k, r, v, hist, dy):
    t, dr = v.shape
    npair, nq = dr // LANES, dr // WIDE
    tb = _pick(t, (SCAN_BLOCK,))
    nblk = t // tb

    def body(rem_ref, w_ref, wr_ref, k_ref, r_ref, v_ref, hist_ref, dy_ref,
             drem_ref, dw_ref, dwr_ref, dk_ref, dr_ref, dv_ref, ds_ref, tiles_a, tiles_b):
        @pl.when(pl.program_id(0) == 0)
        def _():
            ds_ref[...] = jnp.zeros_like(ds_ref)

        eye, ones_bf = _wide_eye(), _wide_ones()
        col_refs = (rem_ref, w_ref, wr_ref, k_ref, r_ref)
        out_refs = (drem_ref, dw_ref, dwr_ref, dk_ref, dr_ref)

        def step(i, tiles_ref):
            grads = [[None] * npair for _ in range(N_COL)]
            v_full, dy_full, dv_rows = v_ref[pl.ds(i, 1), :], dy_ref[pl.ds(i, 1), :], []
            for p in range(npair):
                lanes = slice(p * LANES, (p + 1) * LANES)
                s = hist_ref[i, p]
                c_rem, c_w, c_wr, c_k, c_r = [_pair_tile(tiles_ref, n, p, nq) for n in range(N_COL)]
                v_row, dy_row = v_full[:, lanes], dy_full[:, lanes]
                s_b = _bf16_round(s)
                sa = jnp.sum(s_b * c_rem, axis=0, keepdims=True)
                s2 = s * c_w + c_wr * sa + c_k * v_row
                d2 = ds_ref[p] + c_r * dy_row
                dsa = jnp.sum(d2 * c_wr, axis=0, keepdims=True)
                dv_rows.append(jnp.sum(d2 * c_k, axis=0, keepdims=True))
                ds_ref[p] = d2 * c_w + c_rem * dsa
                for n, tile in enumerate((s_b * dsa, d2 * s, d2 * sa, d2 * v_row, _bf16_round(s2) * dy_row)):
                    grads[n][p] = tile.astype(BF16)
            wide = [jnp.concatenate(grads[n][2 * q:2 * q + 2], axis=1) for n in range(N_COL) for q in range(nq)]
            sums = jnp.dot(jnp.concatenate(wide, axis=0), ones_bf, preferred_element_type=F32)
            sums = sums.reshape(N_COL * nq, HEAD_DIM, WIDE)
            rows = jnp.sum(sums * eye.astype(F32)[None], axis=1)
            dv_ref[pl.ds(i, 1), :] = jnp.concatenate(dv_rows, axis=1)
            for n, o_ref in enumerate(out_refs):
                o_ref[pl.ds(i, 1), :] = jnp.concatenate([rows[n * nq + q:n * nq + q + 1] for q in range(nq)], axis=1)

        tiles_a[...] = _col_tiles(col_refs, tb - 1, nq, eye, ones_bf)

        def two_steps(m, carry):
            i = tb - 1 - 2 * m
            tiles_b[...] = _col_tiles(col_refs, i - 1, nq, eye, ones_bf)
            step(i, tiles_a)
            tiles_a[...] = _col_tiles(col_refs, jnp.maximum(i - 2, 0), nq, eye, ones_bf)
            step(i - 1, tiles_b)
            return carry

        lax.fori_loop(0, tb // 2, two_steps, 0)

    blk = pl.BlockSpec((tb, dr), lambda i: (nblk - 1 - i, 0))
    tiles = pltpu.VMEM((N_COL * nq, HEAD_DIM, WIDE), F32)
    return pl.pallas_call(
        body, name="rwkv_scan_bwd", grid=(nblk,),
        in_specs=[blk] * 6 + [pl.BlockSpec((tb, npair, HEAD_DIM, LANES), lambda i: (nblk - 1 - i, 0, 0, 0)), blk],
        out_specs=[blk] * 6,
        out_shape=[jax.ShapeDtypeStruct((t, dr), F32)] * 6,
        scratch_shapes=[pltpu.VMEM((npair, HEAD_DIM, LANES), F32), tiles, tiles],
        compiler_params=_params(("arbitrary",)),
    )(rem, w, wr, k, r, v, hist, dy)


SB_BLOCK = 256
SB_HEADS = 2
NT_DIMS = (((1,), (1,)), ((), ()))
TN_DIMS = (((0,), (0,)), ((), ()))
SB_SCALE = 1.0 / math.sqrt(HEAD_DIM)


def _dot2(x, tri):
    hi = x.astype(BF16)
    mid = (x - hi.astype(F32)).astype(BF16)
    return jnp.dot(hi, tri, preferred_element_type=F32) + jnp.dot(mid, tri, preferred_element_type=F32)


def _sb_block_iotas(bs):
    return lax.broadcasted_iota(jnp.int32, (bs, bs), 0), lax.broadcasted_iota(jnp.int32, (bs, bs), 1)


def _sb_fwd(q, k, v):
    h, t, d = q.shape
    bs = _pick(t, (SB_BLOCK,))
    nh = _pick(h, (SB_HEADS,))

    def body(q_ref, k_ref, v_ref, o_ref, l_ref):
        qi = pl.program_id(1)
        ri, ci = _sb_block_iotas(bs)
        tri_ge = (ri >= ci).astype(BF16)
        causal = ci < ri
        qv = [q_ref[hh] for hh in range(nh)]

        def block(hh, j, masked, acc, tailc):
            rows = pl.ds(pl.multiple_of(j * bs, bs), bs)
            z = lax.dot_general(qv[hh], k_ref[hh, rows, :], NT_DIMS, preferred_element_type=F32) * SB_SCALE
            log1m = -_softplus(z)
            if masked:
                log1m = jnp.where(causal, log1m, 0.0)
            cs = _dot2(log1m, tri_ge)
            a = jnp.exp(z + cs + tailc)
            if masked:
                a = jnp.where(causal, a, 0.0)
            acc = acc + jnp.dot(a.astype(BF16), v_ref[hh, rows, :], preferred_element_type=F32)
            return acc, tailc + cs[:, 0:1]

        carry = []
        for hh in range(nh):
            carry += block(hh, qi, True, jnp.zeros((bs, d), F32), jnp.zeros((bs, 1), F32))

        def kstep(n, carry):
            out = []
            for hh in range(nh):
                out += block(hh, qi - 1 - n, False, carry[2 * hh], carry[2 * hh + 1])
            return tuple(out)

        carry = lax.fori_loop(0, qi, kstep, tuple(carry))
        for hh in range(nh):
            o_ref[hh] = carry[2 * hh]
            l_ref[hh] = jnp.broadcast_to(carry[2 * hh + 1], (bs, d))

    qs = pl.BlockSpec((nh, bs, d), lambda hh, i: (hh, i, 0))
    ks = pl.BlockSpec((nh, t, d), lambda hh, i: (hh, 0, 0))
    return pl.pallas_call(
        body, name="sb_attn_fwd", grid=(h // nh, t // bs),
        in_specs=[qs, ks, ks], out_specs=[qs, qs],
        out_shape=[jax.ShapeDtypeStruct((h, t, d), F32)] * 2,
        compiler_params=_params(("parallel", "arbitrary")),
    )(q, k, v)


def _sb_bwd(q, k, v, lsum, do):
    h, t, d = q.shape
    bs = _pick(t, (SB_BLOCK,))
    nh = _pick(h, (SB_HEADS,))

    def body(q_ref, k_ref, v_ref, l_ref, do_ref, dq_ref, dk_ref, dv_ref):
        qi = pl.program_id(1)

        @pl.when(qi == 0)
        def _():
            dk_ref[...] = jnp.zeros_like(dk_ref)
            dv_ref[...] = jnp.zeros_like(dv_ref)

        ri, ci = _sb_block_iotas(bs)
        tri_lt = (ri < ci).astype(BF16)
        causal = ci < ri
        qv = [q_ref[hh] for hh in range(nh)]
        dob = [do_ref[hh].astype(BF16) for hh in range(nh)]
        ltot = [l_ref[hh][:, 0:1] for hh in range(nh)]

        def block(hh, j, masked, dq, pc, ec):
            rows = pl.ds(pl.multiple_of(j * bs, bs), bs)
            kb, vb = k_ref[hh, rows, :], v_ref[hh, rows, :]
            z = lax.dot_general(qv[hh], kb, NT_DIMS, preferred_element_type=F32) * SB_SCALE
            nsp = -_softplus(z)
            log1m = jnp.where(causal, nsp, 0.0) if masked else nsp
            below = _dot2(log1m, tri_lt) + pc
            a = jnp.exp(z + (ltot[hh] - below))
            if masked:
                a = jnp.where(causal, a, 0.0)
            da = lax.dot_general(dob[hh], vb, NT_DIMS, preferred_element_type=F32)
            e = a * da
            ebelow = _dot2(e, tri_lt) + ec
            dz = e * jnp.exp(nsp) - jnp.exp(z + nsp) * ebelow
            if masked:
                dz = jnp.where(causal, dz, 0.0)
            dzb = (dz * SB_SCALE).astype(BF16)
            dq = dq + jnp.dot(dzb, kb, preferred_element_type=F32)
            dk_ref[hh, rows, :] += lax.dot_general(dzb, qv[hh], TN_DIMS, preferred_element_type=F32)
            dv_ref[hh, rows, :] += lax.dot_general(a.astype(BF16), dob[hh], TN_DIMS, preferred_element_type=F32)
            pc = pc + jnp.sum(log1m, axis=1, keepdims=True)
            ec = ec + jnp.sum(e, axis=1, keepdims=True)
            return [dq, pc, ec]

        def kstep(j, carry):
            out = []
            for hh in range(nh):
                out += block(hh, j, False, *carry[3 * hh:3 * hh + 3])
            return tuple(out)

        zcol = jnp.zeros((bs, 1), F32)
        carry = lax.fori_loop(0, qi, kstep, (jnp.zeros((bs, d), F32), zcol, zcol) * nh)
        for hh in range(nh):
            dq_ref[hh] = block(hh, qi, True, *carry[3 * hh:3 * hh + 3])[0]

    qs = pl.BlockSpec((nh, bs, d), lambda hh, i: (hh, i, 0))
    ks = pl.BlockSpec((nh, t, d), lambda hh, i: (hh, 0, 0))
    return pl.pallas_call(
        body, name="sb_attn_bwd", grid=(h // nh, t // bs),
        in_specs=[qs, ks, ks, qs, qs], out_specs=[qs, ks, ks],
        out_shape=[jax.ShapeDtypeStruct((h, t, d), F32)] * 3,
        compiler_params=_params(("parallel", "arbitrary")),
    )(q, k, v, lsum, do)


ANY = pl.BlockSpec(memory_space=pl.ANY)
IN_VMEM = pl.BlockSpec(memory_space=pltpu.VMEM)


def _coords():
    return lax.axis_index("x"), lax.axis_index("y"), lax.axis_index("c")


def _flip(v, bit):
    return 1 - v if bit else v


def _remote(src, dst, send_sem, recv_sem, device):
    return pltpu.make_async_remote_copy(src_ref=src, dst_ref=dst, send_sem=send_sem, recv_sem=recv_sem,
                                        device_id=device, device_id_type=MESH)


def _all_gather8(name, blk):
    m, n = blk.shape

    def body(x_ref, o_ref, send_sems, recv_sems, local_sem):
        x, y, c = _coords()
        own = pltpu.make_async_copy(x_ref, o_ref.at[4 * x + 2 * y + c], local_sem)
        own.start()
        peers = []
        for bits in range(1, N_DEV):
            px, py, pc = _flip(x, (bits >> 2) & 1), _flip(y, (bits >> 1) & 1), _flip(c, bits & 1)
            peers.append((px, py, pc))
        sends = []
        for k, peer in enumerate(peers):
            cp = _remote(x_ref, o_ref.at[4 * x + 2 * y + c], send_sems.at[k], recv_sems.at[k], peer)
            cp.start()
            sends.append(cp)
        for k, (px, py, pc) in enumerate(peers):
            slot = o_ref.at[4 * px + 2 * py + pc]
            _remote(slot, slot, send_sems.at[k], recv_sems.at[k], (px, py, pc)).wait_recv()
        for cp in sends:
            cp.wait_send()
        own.wait()

    return pl.pallas_call(
        body, name=name, out_shape=jax.ShapeDtypeStruct((N_DEV, m, n), blk.dtype),
        in_specs=[IN_VMEM], out_specs=IN_VMEM,
        scratch_shapes=[pltpu.SemaphoreType.DMA((N_DEV - 1,)), pltpu.SemaphoreType.DMA((N_DEV - 1,)),
                        pltpu.SemaphoreType.DMA],
        compiler_params=_params(),
    )(blk)


def _gather_weights(shards):
    nw = len(shards)

    def body(*refs):
        ins, outs = refs[:nw], refs[nw:2 * nw]
        ici_send, ici_recv, d2d_send, d2d_recv = refs[2 * nw:]
        x, y, c = _coords()
        chip = 2 * x + y
        sibling = (x, y, 1 - c)
        started = []
        for w in range(nw):
            half = shards[w].shape[0] // 2
            mine = pl.ds(c * half, half)
            for j, (a, b) in enumerate(XY_MASKS):
                cp = _remote(ins[w].at[mine], outs[w].at[chip, mine], ici_send.at[3 * w + j],
                             ici_recv.at[3 * w + j], (_flip(x, a), _flip(y, b), c))
                cp.start()
                started.append(cp)
        for w in range(nw):
            half = shards[w].shape[0] // 2
            mine = pl.ds(c * half, half)
            for j, (a, b) in enumerate(XY_MASKS):
                px, py = _flip(x, a), _flip(y, b)
                landed = outs[w].at[2 * px + py, mine]
                _remote(landed, landed, ici_send.at[3 * w + j], ici_recv.at[3 * w + j], (px, py, c)).wait_recv()
                cp = _remote(landed, landed, d2d_send.at[3 * w + j], d2d_recv.at[3 * w + j], sibling)
                cp.start()
                started.append(cp)
        for w in range(nw):
            half = shards[w].shape[0] // 2
            theirs = pl.ds((1 - c) * half, half)
            for j, (a, b) in enumerate(XY_MASKS):
                slot = outs[w].at[2 * _flip(x, a) + _flip(y, b), theirs]
                _remote(slot, slot, d2d_send.at[3 * w + j], d2d_recv.at[3 * w + j], sibling).wait_recv()
        for cp in started:
            cp.wait_send()

    return pl.pallas_call(
        body, name="gather_weights",
        out_shape=[jax.ShapeDtypeStruct((N_CHIPS,) + s.shape, s.dtype) for s in shards],
        in_specs=[ANY] * nw, out_specs=[ANY] * nw,
        scratch_shapes=[pltpu.SemaphoreType.DMA((3 * nw,))] * 4,
        compiler_params=_params(),
    )(*shards)


def _pair_exchange(grads):
    nw = len(grads)

    def body(*refs):
        ins, outs = refs[:nw], refs[nw:2 * nw]
        send_sems, recv_sems = refs[2 * nw:]
        x, y, c = _coords()
        sibling = (x, y, 1 - c)
        sends = []
        for w in range(nw):
            half = grads[w].shape[1] // 2
            cp = _remote(ins[w].at[:, pl.ds((1 - c) * half, half)], outs[w], send_sems.at[w], recv_sems.at[w], sibling)
            cp.start()
            sends.append(cp)
        for w in range(nw):
            _remote(outs[w], outs[w], send_sems.at[w], recv_sems.at[w], sibling).wait_recv()
        for cp in sends:
            cp.wait_send()

    return pl.pallas_call(
        body, name="grad_pair_exchange",
        out_shape=[jax.ShapeDtypeStruct((N_CHIPS, g.shape[1] // 2, g.shape[2]), g.dtype) for g in grads],
        in_specs=[ANY] * nw, out_specs=[ANY] * nw,
        scratch_shapes=[pltpu.SemaphoreType.DMA((nw,))] * 2,
        compiler_params=_params(),
    )(*grads)


def _chip_exchange(parts):
    nw = len(parts)

    def body(*refs):
        ins, outs = refs[:nw], refs[nw:2 * nw]
        send_sems, recv_sems = refs[2 * nw:]
        x, y, c = _coords()
        chip = 2 * x + y
        sends = []
        for w in range(nw):
            for j, (a, b) in enumerate(XY_MASKS):
                px, py = _flip(x, a), _flip(y, b)
                cp = _remote(ins[w].at[2 * px + py], outs[w].at[chip], send_sems.at[3 * w + j],
                             recv_sems.at[3 * w + j], (px, py, c))
                cp.start()
                sends.append(cp)
        for w in range(nw):
            for j, (a, b) in enumerate(XY_MASKS):
                px, py = _flip(x, a), _flip(y, b)
                slot = outs[w].at[2 * px + py]
                _remote(slot, slot, send_sems.at[3 * w + j], recv_sems.at[3 * w + j], (px, py, c)).wait_recv()
        for cp in sends:
            cp.wait_send()

    return pl.pallas_call(
        body, name="grad_chip_exchange",
        out_shape=[jax.ShapeDtypeStruct(p.shape, p.dtype) for p in parts],
        in_specs=[ANY] * nw, out_specs=[ANY] * nw,
        scratch_shapes=[pltpu.SemaphoreType.DMA((3 * nw,))] * 2,
        compiler_params=_params(),
    )(*parts)


def _pair_share(shards):
    nw = len(shards)

    def body(*refs):
        bufs = refs[nw:2 * nw]
        send_sems, recv_sems = refs[2 * nw:]
        x, y, c = _coords()
        sibling = (x, y, 1 - c)
        sends = []
        for w in range(nw):
            half = shards[w].shape[0] // 2
            mine = bufs[w].at[pl.ds(c * half, half)]
            cp = _remote(mine, mine, send_sems.at[w], recv_sems.at[w], sibling)
            cp.start()
            sends.append(cp)
        for w in range(nw):
            half = shards[w].shape[0] // 2
            theirs = bufs[w].at[pl.ds((1 - c) * half, half)]
            _remote(theirs, theirs, send_sems.at[w], recv_sems.at[w], sibling).wait_recv()
        for cp in sends:
            cp.wait_send()

    return pl.pallas_call(
        body, name="grad_pair_share",
        out_shape=[jax.ShapeDtypeStruct(s.shape, s.dtype) for s in shards],
        in_specs=[ANY] * nw, out_specs=[ANY] * nw,
        input_output_aliases={w: w for w in range(nw)},
        scratch_shapes=[pltpu.SemaphoreType.DMA((nw,))] * 2,
        compiler_params=_params(),
    )(*shards)


TILE_BYTES = 2 * 1024 * 1024


def _row_tile(rows, cols, mult=8):
    best = None
    for tr in range(mult, rows + 1, mult):
        if rows % tr == 0 and tr * cols * 4 <= TILE_BYTES:
            best = tr
    return best if best is not None else rows


def _pair_add(name, grad, other, place):
    _, half, cols = other.shape
    tr = _row_tile(half, cols, mult=16)
    nb = half // tr

    def body(place_ref, g_ref, o_ref, out_ref):
        out_ref[...] = (g_ref[...] + o_ref[...]).astype(out_ref.dtype)

    return pl.pallas_call(
        body, name=name, out_shape=jax.ShapeDtypeStruct(other.shape, BF16),
        grid_spec=pltpu.PrefetchScalarGridSpec(
            num_scalar_prefetch=1, grid=(N_CHIPS, nb),
            in_specs=[pl.BlockSpec((None, tr, cols), lambda s, i, place_ref: (s, place_ref[0] * nb + i, 0)),
                      pl.BlockSpec((None, tr, cols), lambda s, i, place_ref: (s, i, 0))],
            out_specs=pl.BlockSpec((None, tr, cols), lambda s, i, place_ref: (s, i, 0))),
        compiler_params=_params(("parallel", "parallel")),
    )(place, grad, other)


def _sum_chips(name, own, others, place):
    _, half, cols = own.shape
    tr = _row_tile(half, cols, mult=16)
    nb = half // tr

    def body(place_ref, own_ref, a_ref, b_ref, c_ref, out_ref):
        total = own_ref[...].astype(F32) + a_ref[...].astype(F32)
        out_ref[...] = (total + b_ref[...].astype(F32)) + c_ref[...].astype(F32)

    def peer(mask):
        return pl.BlockSpec((None, tr, cols), lambda i, place_ref: (place_ref[1] ^ mask, i, 0))

    return pl.pallas_call(
        body, name=name, out_shape=jax.ShapeDtypeStruct((2 * half, cols), F32),
        grid_spec=pltpu.PrefetchScalarGridSpec(
            num_scalar_prefetch=1, grid=(nb,),
            in_specs=[peer(0), peer(1), peer(2), peer(3)],
            out_specs=pl.BlockSpec((tr, cols), lambda i, place_ref: (place_ref[0] * nb + i, 0))),
        compiler_params=_params(("parallel",)),
    )(place, own, others, others, others)


def _adamw_math(w, g, m, v):
    m2 = ADAM_B1 * m + (1.0 - ADAM_B1) * g
    v2 = ADAM_B2 * v + (1.0 - ADAM_B2) * (g * g)
    m_hat = m2 / (1.0 - ADAM_B1 ** ADAM_STEP)
    v_hat = v2 / (1.0 - ADAM_B2 ** ADAM_STEP)
    delta = -ADAM_LR * (m_hat / (jnp.sqrt(v_hat) + ADAM_EPS) + ADAM_WD * w)
    return delta, m2, v2


def _adamw(name, w, g, m, v):
    rows, cols = w.shape
    tr = _row_tile(rows, cols)

    def body(w_ref, g_ref, m_ref, v_ref, d_ref, m2_ref, v2_ref):
        d_ref[...], m2_ref[...], v2_ref[...] = _adamw_math(w_ref[...], g_ref[...], m_ref[...], v_ref[...])

    blk = pl.BlockSpec((tr, cols), lambda i: (i, 0))
    return pl.pallas_call(
        body, name=name, out_shape=[jax.ShapeDtypeStruct(w.shape, F32)] * 3, grid=(rows // tr,),
        in_specs=[blk] * 4, out_specs=[blk] * 3,
        compiler_params=_params(("parallel",)),
    )(w, g, m, v)


def _small_update(gathered, w, m, v):
    def body(gs_ref, w_ref, m_ref, v_ref, g_ref, d_ref, m2_ref, v2_ref):
        g = gs_ref[0]
        for dev in range(1, N_DEV):
            g = g + gs_ref[dev]
        g_ref[...] = g
        d_ref[...], m2_ref[...], v2_ref[...] = _adamw_math(w_ref[...], g, m_ref[...], v_ref[...])

    return pl.pallas_call(
        body, name="small_update", out_shape=[jax.ShapeDtypeStruct(w.shape, F32)] * 4,
        compiler_params=_params(),
    )(gathered, w, m, v)


def _ada_mod(c_all, w_shard, b_shard):
    d, n = w_shard.shape
    tn = _pick(n, (512, 256, 128))

    def body(c_ref, w_ref, b_ref, o_ref):
        act = _silu(c_ref[...]).astype(BF16)
        o_ref[...] = jnp.dot(act, w_ref[...].astype(BF16), preferred_element_type=F32) + b_ref[...]

    return pl.pallas_call(
        body, name="ada_mod", out_shape=jax.ShapeDtypeStruct((c_all.shape[0], n), F32), grid=(n // tn,),
        in_specs=[pl.BlockSpec(c_all.shape, lambda j: (0, 0)), pl.BlockSpec((d, tn), lambda j: (0, j)),
                  pl.BlockSpec((1, tn), lambda j: (0, j))],
        out_specs=pl.BlockSpec((c_all.shape[0], tn), lambda j: (0, j)),
        compiler_params=_params(("parallel",)),
    )(c_all, w_shard, b_shard)


def _ada_grad(c_pad, dmod_pad):
    rows, d = c_pad.shape
    n = dmod_pad.shape[1]
    tn = _pick(n, (512, 256, 128))

    def body(c_ref, g_ref, o_ref):
        act = _silu(c_ref[...]).astype(BF16)
        o_ref[...] = lax.dot_general(act, g_ref[...].astype(BF16), TN_DIMS, preferred_element_type=F32)

    return pl.pallas_call(
        body, name="ada_grad", out_shape=jax.ShapeDtypeStruct((d, n), F32), grid=(n // tn,),
        in_specs=[pl.BlockSpec((rows, d), lambda j: (0, 0)), pl.BlockSpec((rows, tn), lambda j: (0, j))],
        out_specs=pl.BlockSpec((d, tn), lambda j: (0, j)),
        compiler_params=_params(("parallel",)),
    )(c_pad, dmod_pad)


WEIGHTS = ['w_ada', 'b_ada', 'norm1_gain', 'norm2_gain', 'w_in', 'mu_rkv', 'mu_w', 'mu_a', 'mu_g', 'w0', 'w1',
           'w2', 'a0', 'a1', 'a2', 'g1', 'g2', 'k_k', 'k_a', 'r_k', 'ln_x_gain', 'ln_x_bias', 'q_norm_gain',
           'k_norm_gain', 'w_out', 'w_gate_up', 'w_down']
SMALL = ['b_ada', 'norm1_gain', 'norm2_gain', 'mu_rkv', 'mu_w', 'mu_a', 'mu_g', 'w0', 'a0', 'k_k', 'k_a', 'r_k',
         'ln_x_gain', 'ln_x_bias', 'q_norm_gain', 'k_norm_gain']
PACK_ROWS = 8
LOSS_SLOT = LANES


def _shift_down(a):
    return jnp.pad(a[:-1], ((1, 0), (0, 0)))


def _shift_up(a):
    return jnp.pad(a[1:], ((0, 1), (0, 0)))


def _pack_small(vals):
    flat = jnp.concatenate([v.reshape(1, -1) for v in vals], axis=1)
    unit = PACK_ROWS * LANES
    total = -(-flat.shape[1] // unit) * unit
    flat = jnp.pad(flat, ((0, 0), (0, total - flat.shape[1])))
    return flat.reshape(PACK_ROWS, total // PACK_ROWS)


def kernel(x, c, w_ada, b_ada, norm1_gain, norm2_gain, w_in, mu_rkv, mu_w, mu_a, mu_g, w0, w1, w2, a0, a1, a2, g1, g2, k_k, k_a, r_k, ln_x_gain, ln_x_bias, q_norm_gain, k_norm_gain, w_out, w_gate_up, w_down, loss_target, m_w_ada, m_b_ada, m_norm1_gain, m_norm2_gain, m_w_in, m_mu_rkv, m_mu_w, m_mu_a, m_mu_g, m_w0, m_w1, m_w2, m_a0, m_a1, m_a2, m_g1, m_g2, m_k_k, m_k_a, m_r_k, m_ln_x_gain, m_ln_x_bias, m_q_norm_gain, m_k_norm_gain, m_w_out, m_w_gate_up, m_w_down, v_w_ada, v_b_ada, v_norm1_gain, v_norm2_gain, v_w_in, v_mu_rkv, v_mu_w, v_mu_a, v_mu_g, v_w0, v_w1, v_w2, v_a0, v_a1, v_a2, v_g1, v_g2, v_k_k, v_k_a, v_r_k, v_ln_x_gain, v_ln_x_bias, v_q_norm_gain, v_k_norm_gain, v_w_out, v_w_gate_up, v_w_down):
    given = dict(locals())
    wt = {n: given[n][0] for n in WEIGHTS}
    mom = {n: given["m_" + n][0] for n in WEIGHTS}
    var = {n: given["v_" + n][0] for n in WEIGHTS}
    for tree in (wt, mom, var):
        tree["b_ada"] = tree["b_ada"].reshape(1, -1)
        for n in SMALL[1:]:
            tree[n] = tree[n].reshape(1, -1)

    ax, ay, ac = _coords()
    chip = 2 * ax + ay
    dev = 4 * ax + 2 * ay + ac
    xs, target = x[0], loss_target[0]
    t, d = xs.shape
    dr = wt["w0"].shape[1]
    ds = d - dr
    nh = ds // HEAD_DIM
    dff = wt["w_down"].shape[0] * N_CHIPS
    n_ada = wt["w_ada"].shape[1]
    lw, la, lg = wt["w1"].shape[1], wt["a1"].shape[1], wt["g1"].shape[1]

    def lora_a(tree):
        return jnp.concatenate([tree["w1"], tree["a1"], tree["g1"]], axis=1)

    def lora_b(tree):
        return jnp.concatenate([tree["w2"], tree["a2"], tree["g2"]], axis=0)

    shards = [wt["w_in"], wt["w_out"], wt["w_gate_up"], wt["w_down"], lora_a(wt), lora_b(wt)]
    shards = [s.astype(BF16) for s in shards]
    gathered_w = [lax.dynamic_update_slice(full, own[None], (chip, 0, 0))
                  for full, own in zip(_gather_weights(shards), shards)]
    full_in, full_out, full_gu, full_down, full_la, full_lb = gathered_w
    full_out = full_out.reshape(d, d)
    full_down = full_down.reshape(dff, d)
    full_la = full_la.reshape(d, lw + la + lg).astype(F32)
    full_lb = full_lb.transpose(1, 0, 2).reshape(lw + la + lg, dr).astype(F32)
    w1f, a1f, g1f = full_la[:, :lw], full_la[:, lw:lw + la], full_la[:, lw + la:]
    w2f, a2f, g2f = full_lb[:lw], full_lb[lw:lw + la], full_lb[lw + la:]

    c_all = _all_gather8("gather_c", c.reshape(PACK_ROWS, d // PACK_ROWS)).reshape(N_DEV, d)
    b_shard = lax.dynamic_slice(wt["b_ada"], (0, chip * n_ada), (1, n_ada))
    mod_part = _ada_mod(c_all, wt["w_ada"], b_shard)
    mod_all = _all_gather8("gather_mod", mod_part)[::2]
    mod = lax.dynamic_slice(mod_all, (0, dev, 0), (N_CHIPS, 1, n_ada)).reshape(1, N_CHIPS * n_ada)
    sh1, sc1, gt1, sh2, sc2, gt2 = [mod[:, i * d:(i + 1) * d] for i in range(6)]

    h, h_bf = _rowwise("norm1", _fn_norm1, [xs], [wt["norm1_gain"], sc1, sh1], [(d, F32), (d, BF16)], 256)
    hp = _shift_down(h)
    p = _matmul("mm_in", h_bf, full_in, b_shards=True, tm=512, tn=1536, tk=2048)
    p_rkv, p_sb = (p, 3 * dr, 0), (p, 3 * ds, 1)
    pp = _shift_down(p[:, :3 * dr])
    pre_rows = [h, hp, p_rkv, pp]
    pre_params = [wt["mu_rkv"], wt["mu_w"], wt["mu_a"], wt["mu_g"], wt["w0"], wt["a0"], wt["k_k"], wt["k_a"],
                  w1f, w2f, a1f, a2f, g1f, g2f]
    pre = _rowwise("rwkv_pre", _fn_rwkv_pre, pre_rows, pre_params, [(dr, F32)] * 7, 64)
    r_, w_, k2, v_, rem, wr, g_ = pre
    y_raw, hist = _scan_fwd(rem, w_, wr, k2, r_, v_)
    post_rows = [y_raw, r_, k2, v_, g_]
    post_params = [wt["ln_x_gain"], wt["ln_x_bias"], wt["r_k"]]

    qg = jnp.tile(wt["q_norm_gain"], (1, nh))
    kg = jnp.tile(wt["k_norm_gain"], (1, nh))
    qn, kn, vs = _rowwise("qk_norm", _fn_qk_norm, [p_sb], [qg, kg], [(ds, BF16)] * 3, 256)

    def to_heads(a):
        return a.reshape(t, nh, HEAD_DIM).transpose(1, 0, 2)

    def from_heads(a):
        return a.transpose(1, 0, 2).reshape(t, ds)

    qh, kh, vh = to_heads(qn), to_heads(kn), to_heads(vs)
    o_h, lsum = _sb_fwd(qh, kh, vh)
    (ycat,) = _rowwise("rwkv_post", _fn_rwkv_post_cat, post_rows + [from_heads(o_h)], post_params, [(d, BF16)], 256)
    mix = _matmul("mm_out", ycat, full_out, tm=512, tn=1024, tk=2048)
    norm2_params = [gt1, wt["norm2_gain"], sc2, sh2]
    x1, h2 = _rowwise("mix_norm2", _fn_mix_norm2, [xs, mix], norm2_params, [(d, F32), (d, BF16)], 256)
    gu = _matmul("mm_gate_up", h2, full_gu, b_shards=True, tm=512, tn=1408, tk=2048)
    gate_up = [(gu, dff, 0), (gu, dff, 1)]
    (act,) = _rowwise("swiglu", _fn_swiglu, gate_up, [], [(dff, BF16)], 256)
    dn = _matmul("mm_down", act, full_down, tm=512, tn=1024, tk=1408)
    loss_vec, dout, ddn, dgt2 = _loss_head("loss_head", x1, dn, target, gt2)

    dact = _matmul("mm_down_dx", ddn, full_down, tb=True, tm=512, tn=1408, tk=2048)
    gw_down = _matmul("mm_down_dw", act, ddn, ta=True, tm=1408, tn=1024, tk=512)
    dgu = _swiglu_bwd(gu, dact)
    dh2 = _matmul("mm_gate_up_dx", dgu, full_gu, tb=True, b_shards=True, tm=512, tn=1024, tk=1408)
    gw_gu = _matmul("mm_gate_up_dw", h2, dgu, ta=True, out_shards=True, tm=1024, tn=1408, tk=512)
    (dx_a, dmix), (dgt1, dgain2, dsc2, dsh2) = _rowwise_bwd(
        "mix_norm2_bwd", _fn_mix_norm2, [xs, mix], norm2_params, [[dout], [dh2]], [F32, BF16], [True] * 4, 128)
    dycat = _matmul("mm_out_dx", dmix, full_out, tb=True, tm=512, tn=1024, tk=2048)
    gw_out = _matmul("mm_out_dw", ycat, dmix, ta=True, tm=1024, tn=1024, tk=512)
    (dy_raw, dr_f, dk_f, dv_f, dg), (dlng, dlnb, drk) = _rowwise_bwd(
        "rwkv_post_bwd", _fn_rwkv_post, post_rows, post_params, [[(dycat, dr, 0)]], [F32] * 5, [True] * 3, 128)
    drem_s, dw_s, dwr_s, dk_s, dr_s, dv_s = _scan_bwd(rem, w_, wr, k2, r_, v_, hist, dy_raw)
    do_h = to_heads(dycat[:, dr:])
    dqh, dkh, dvh = _sb_bwd(qh, kh, vh, lsum, do_h)
    (dp_sb,), (dqg, dkg) = _rowwise_bwd(
        "qk_norm_bwd", _fn_qk_norm, [p_sb], [qg, kg], [[from_heads(dqh)], [from_heads(dkh)], [from_heads(dvh)]],
        [F32], [True, True], 128)
    pre_cts = [[dr_s, dr_f], [dw_s], [dk_s, dk_f], [dv_s, dv_f], [drem_s], [dwr_s], [dg]]
    (dh_a, dhp, dp_rkv, dpp), pre_g = _rowwise_bwd(
        "rwkv_pre_bwd", _fn_rwkv_pre, pre_rows, pre_params, pre_cts, [F32] * 4, [True] * 14, 64)
    dp = jnp.concatenate([dp_rkv + _shift_up(dpp), dp_sb], axis=1).astype(BF16)
    dh_mm = _matmul("mm_in_dx", dp, full_in, tb=True, b_shards=True, tm=512, tn=1024, tk=1536)
    gw_in = _matmul("mm_in_dw", h_bf, dp, ta=True, out_shards=True, tm=1024, tn=1536, tk=512)
    (grad_x,), (dgain1, dsc1, dsh1) = _rowwise_bwd(
        "norm1_bwd", _fn_norm1, [xs], [wt["norm1_gain"], sc1, sh1], [[dh_a, dh_mm, _shift_up(dhp)], []],
        [F32], [True] * 3, 128, add_to_first=[dx_a])

    g_mu_rkv, g_mu_w, g_mu_a, g_mu_g, g_w0, g_a0, g_kk, g_ka, gw1, gw2, ga1, ga2, gg1, gg2 = pre_g
    g_la = jnp.concatenate([gw1, ga1, gg1], axis=1).reshape(N_CHIPS, d // N_CHIPS, lw + la + lg)
    g_lb = jnp.concatenate([gw2, ga2, gg2], axis=0)
    g_lb = g_lb.reshape(lw + la + lg, N_CHIPS, dr // N_CHIPS).transpose(1, 0, 2)
    local = [gw_in, gw_out.reshape(N_CHIPS, d // N_CHIPS, d), gw_gu, gw_down.reshape(N_CHIPS, dff // N_CHIPS, d),
             g_la, g_lb]
    names = ["w_in", "w_out", "w_gate_up", "w_down", "lora_a", "lora_b"]
    place = jnp.stack([ac, chip]).astype(jnp.int32)
    from_sibling = _pair_exchange(local)
    pair_sums = [_pair_add("pair_add_" + n, g, o, place) for n, g, o in zip(names, local, from_sibling)]
    from_chips = _chip_exchange(pair_sums)
    halves = [_sum_chips("chip_sum_" + n, p, q, place) for n, p, q in zip(names, pair_sums, from_chips)]
    r_in, r_out, r_gu, r_down, r_la, r_lb = _pair_share(halves)

    dmod = jnp.concatenate([dsh1, dsc1, dgt1, dsh2, dsc2, dgt2], axis=1)
    dqg = dqg.reshape(nh, HEAD_DIM).sum(axis=0, keepdims=True)
    dkg = dkg.reshape(nh, HEAD_DIM).sum(axis=0, keepdims=True)
    small_g = [dmod, dgain1, dgain2, g_mu_rkv, g_mu_w, g_mu_a, g_mu_g, g_w0, g_a0, g_kk, g_ka, drk, dlng, dlnb,
               dqg, dkg]
    lead = jnp.zeros((1, LOSS_SLOT), F32)
    packed = _pack_small([loss_vec] + small_g)
    gathered = _all_gather8("gather_small", packed)
    sm_g, sm_d, sm_m, sm_v = _small_update(gathered, _pack_small([lead] + [wt[n] for n in SMALL]),
                                           _pack_small([lead] + [mom[n] for n in SMALL]),
                                           _pack_small([lead] + [var[n] for n in SMALL]))
    loss = sm_g.reshape(-1)[0]

    def unpack(packed_arr):
        flat, out, pos = packed_arr.reshape(-1), {}, LOSS_SLOT
        for n in SMALL:
            size = wt[n].size
            out[n] = flat[pos:pos + size]
            pos += size
        return out

    res = {"grad": unpack(sm_g), "delta": unpack(sm_d), "m": unpack(sm_m), "v": unpack(sm_v)}

    dmod_all = gathered.reshape(N_DEV, -1)[:, LOSS_SLOT:LOSS_SLOT + N_CHIPS * n_ada]
    dmod_cols = lax.dynamic_slice(dmod_all, (0, chip * n_ada), (N_DEV, n_ada))
    pad8 = ((0, N_DEV), (0, 0))
    res["grad"]["w_ada"] = _ada_grad(jnp.pad(c_all, pad8), jnp.pad(dmod_cols, pad8))

    res["grad"].update(w_in=r_in, w_out=r_out, w_gate_up=r_gu, w_down=r_down)
    for n in ("w_ada", "w_in", "w_out", "w_gate_up", "w_down"):
        res["delta"][n], res["m"][n], res["v"][n] = _adamw("adamw_" + n, wt[n], res["grad"][n], mom[n], var[n])
    la_d, la_m, la_v = _adamw("adamw_lora_a", lora_a(wt), r_la, lora_a(mom), lora_a(var))
    lb_d, lb_m, lb_v = _adamw("adamw_lora_b", lora_b(wt), r_lb, lora_b(mom), lora_b(var))
    for key, pa, pb in (("grad", r_la, r_lb), ("delta", la_d, lb_d), ("m", la_m, lb_m), ("v", la_v, lb_v)):
        res[key].update(w1=pa[:, :lw], a1=pa[:, lw:lw + la], g1=pa[:, lw + la:],
                        w2=pb[:lw], a2=pb[lw:lw + la], g2=pb[lw + la:])

    outs = [loss, grad_x[None]]
    for key in ("grad", "delta", "m", "v"):
        outs += [res[key][n].reshape(given[n].shape) for n in WEIGHTS]
    return tuple(outs)
```

```python
import functools
import math

import jax
import jax.numpy as jnp
from jax import lax
from jax.experimental import pallas as pl
from jax.experimental.pallas import tpu as pltpu

F32 = jnp.float32
BF16 = jnp.bfloat16
HEAD_DIM = 64
LANES = 128
RMS_EPS = 1e-6
GN_EPS = 64e-5
L2_EPS = 1e-12
ADAM_LR, ADAM_B1, ADAM_B2, ADAM_EPS, ADAM_WD, ADAM_STEP = 0.001, 0.9, 0.999, 1e-08, 0.01, 10
VMEM_LIMIT = 56 * 1024 * 1024
MESH = pl.DeviceIdType.MESH
HI = lax.Precision.HIGHEST
N_CHIPS = 4
N_DEV = 8
XY_MASKS = ((1, 0), (0, 1), (1, 1))


def _pick(dim, prefs):
    for p in prefs:
        if dim % p == 0:
            return p
    return dim


def _params(sem=None, vmem=VMEM_LIMIT):
    return pltpu.CompilerParams(dimension_semantics=sem, vmem_limit_bytes=vmem)


def _sigmoid(x):
    return 1.0 / (1.0 + jnp.exp(-x))


@jax.custom_vjp
def _softplus(x):
    return jnp.maximum(x, 0.0) + jnp.log(1.0 + jnp.exp(-jnp.abs(x)))


_softplus.defvjp(lambda x: (_softplus(x), x), lambda x, g: (g * _sigmoid(x),))


def _silu(x):
    return x * _sigmoid(x)


@jax.custom_vjp
def _bdot(a, b):
    return jnp.dot(a.astype(BF16), b.astype(BF16), preferred_element_type=F32)


def _bdot_bwd(res, g):
    a, b = res
    gb = g.astype(BF16)
    da = lax.dot_general(gb, b.astype(BF16), (((1,), (1,)), ((), ())), preferred_element_type=F32)
    db = lax.dot_general(a.astype(BF16), gb, (((0,), (0,)), ((), ())), preferred_element_type=F32)
    return da.astype(a.dtype), db.astype(b.dtype)


_bdot.defvjp(lambda a, b: (_bdot(a, b), (a, b)), _bdot_bwd)


def _head_ones():
    i = lax.broadcasted_iota(jnp.int32, (LANES, LANES), 0) // HEAD_DIM
    j = lax.broadcasted_iota(jnp.int32, (LANES, LANES), 1) // HEAD_DIM
    return (i == j).astype(F32)


def _hdot(x, ones):
    return jnp.dot(x, ones, precision=HI, preferred_element_type=F32)


@jax.custom_vjp
def _segsum(x):
    ones = _head_ones()
    parts = [_hdot(x[:, LANES * j:LANES * (j + 1)], ones) for j in range(x.shape[1] // LANES)]
    return parts[0] if len(parts) == 1 else jnp.concatenate(parts, axis=1)


_segsum.defvjp(lambda x: (_segsum(x), None), lambda _, g: (_segsum(g),))


def _rms(x, gain):
    return x * lax.rsqrt(jnp.mean(x * x, axis=-1, keepdims=True) + RMS_EPS) * gain


def _matmul(name, a, b, *, ta=False, tb=False, b_shards=False, out_shards=False, out_dtype=F32,
            tm=512, tn=512, tk=512, n_outer=False):
    if ta:
        kdim, m = a.shape
    else:
        m, kdim = a.shape
    if b_shards:
        if tb:
            n, ks = b.shape[1], b.shape[2]
            assert ks * N_CHIPS == kdim
        else:
            ns = b.shape[2]
            n = ns * N_CHIPS
            assert b.shape[1] == kdim
    else:
        n = b.shape[0] if tb else b.shape[1]
    tm = _pick(m, (tm, 512, 256, 128))
    n_part = n // N_CHIPS if (out_shards or (b_shards and not tb)) else n
    tn = _pick(n_part, (tn, 512, 256, 128))
    k_part = kdim // N_CHIPS if (b_shards and tb) else kdim
    tk = _pick(k_part, (tk, 512, 256, 128))
    nb = n_part // tn
    kb = k_part // tk
    nk = kdim // tk
    grid = (n // tn, m // tm, nk) if n_outer else (m // tm, n // tn, nk)

    def spec(shape, index):
        if n_outer:
            return pl.BlockSpec(shape, lambda j, i, k: index(i, j, k))
        return pl.BlockSpec(shape, index)

    if ta:
        a_spec = spec((tk, tm), lambda i, j, k: (k, i))
    else:
        a_spec = spec((tm, tk), lambda i, j, k: (i, k))
    if b_shards and tb:
        b_spec = spec((None, tn, tk), lambda i, j, k: (k // kb, j, k % kb))
    elif b_shards:
        b_spec = spec((None, tk, tn), lambda i, j, k: (j // nb, k, j % nb))
    elif tb:
        b_spec = spec((tn, tk), lambda i, j, k: (j, k))
    else:
        b_spec = spec((tk, tn), lambda i, j, k: (k, j))
    if out_shards:
        o_spec = spec((None, tm, tn), lambda i, j, k: (j // nb, i, j % nb))
        o_shape = jax.ShapeDtypeStruct((N_CHIPS, m, n_part), out_dtype)
    else:
        o_spec = spec((tm, tn), lambda i, j, k: (i, j))
        o_shape = jax.ShapeDtypeStruct((m, n), out_dtype)
    dims = (((0 if ta else 1,), (1 if tb else 0,)), ((), ()))

    def body(a_ref, b_ref, o_ref, acc_ref):
        k = pl.program_id(2)

        @pl.when(k == 0)
        def _():
            acc_ref[...] = jnp.zeros_like(acc_ref)

        acc_ref[...] += lax.dot_general(a_ref[...].astype(BF16), b_ref[...].astype(BF16), dims,
                                        preferred_element_type=F32)

        @pl.when(k == nk - 1)
        def _():
            o_ref[...] = acc_ref[...].astype(o_ref.dtype)

    return pl.pallas_call(
        body, name=name, grid=grid, in_specs=[a_spec, b_spec], out_specs=o_spec, out_shape=o_shape,
        scratch_shapes=[pltpu.VMEM((tm, tn), F32)],
        compiler_params=_params(("parallel", "parallel", "arbitrary")),
    )(a, b)


def _row_in(spec, tile):
    if isinstance(spec, tuple):
        arr, width, cb = spec
    else:
        arr, width, cb = spec, spec.shape[1], 0
    return arr, pl.BlockSpec((tile, width), lambda i, cb=cb: (i, cb))


def _full_spec(arr):
    nd = arr.ndim
    return pl.BlockSpec(arr.shape, lambda i, nd=nd: (0,) * nd)


def _rowwise(name, fn, rows, params, outs, tile):
    t = (rows[0][0] if isinstance(rows[0], tuple) else rows[0]).shape[0]
    tile = _pick(t, (tile,))
    arrs, specs = zip(*[_row_in(s, tile) for s in rows])
    nr, npar = len(rows), len(params)

    def body(*refs):
        rv = [r[...].astype(F32) for r in refs[:nr]]
        pv = [p[...] for p in refs[nr:nr + npar]]
        res = fn(*rv, *pv)
        for o_ref, val in zip(refs[nr + npar:], res):
            o_ref[...] = val.astype(o_ref.dtype)

    return pl.pallas_call(
        body, name=name, grid=(t // tile,),
        in_specs=list(specs) + [_full_spec(p) for p in params],
        out_specs=[pl.BlockSpec((tile, w), lambda i: (i, 0)) for w, _ in outs],
        out_shape=[jax.ShapeDtypeStruct((t, w), d) for w, d in outs],
        compiler_params=_params(("parallel",)),
    )(*arrs, *params)


def _rowwise_bwd(name, fn, rows, params, cts, row_grads, param_grads, tile, add_to_first=()):
    t = (rows[0][0] if isinstance(rows[0], tuple) else rows[0]).shape[0]
    tile = _pick(t, (tile,))
    arrs, specs = zip(*[_row_in(s, tile) for s in rows])
    n_add = len(add_to_first)
    flat_cts = [c for group in cts for c in group] + list(add_to_first)
    c_arrs, c_specs = zip(*[_row_in(s, tile) for s in flat_cts])
    nr, npar, nc = len(rows), len(params), len(flat_cts)
    rg_idx = [i for i, d in enumerate(row_grads) if d is not None]
    pg_idx = [i for i, d in enumerate(param_grads) if d]

    def body(*refs):
        rv = [r[...].astype(F32) for r in refs[:nr]]
        pv = [p[...] for p in refs[nr:nr + npar]]
        cv = [c[...].astype(F32) for c in refs[nr + npar:nr + npar + nc]]
        o_refs = refs[nr + npar + nc:]
        outs, vjp = jax.vjp(fn, *rv, *pv)
        ct, pos = [], 0
        for group, o in zip(cts, outs):
            if group:
                acc = cv[pos]
                for extra in cv[pos + 1:pos + len(group)]:
                    acc = acc + extra
                pos += len(group)
            else:
                acc = jnp.zeros_like(o)
            ct.append(acc)
        grads = list(vjp(tuple(ct)))
        for extra in cv[nc - n_add:]:
            grads[rg_idx[0]] = grads[rg_idx[0]] + extra
        for o_ref, i in zip(o_refs[:len(rg_idx)], rg_idx):
            o_ref[...] = grads[i].astype(o_ref.dtype)
        first = pl.program_id(0) == 0
        for o_ref, i in zip(o_refs[len(rg_idx):], pg_idx):
            g = grads[nr + i].astype(F32)

            @pl.when(first)
            def _(o_ref=o_ref, g=g):
                o_ref[...] = g

            @pl.when(jnp.logical_not(first))
            def _(o_ref=o_ref, g=g):
                o_ref[...] += g

    def width(i):
        s = rows[i]
        return s[1] if isinstance(s, tuple) else s.shape[1]

    out_specs = [pl.BlockSpec((tile, width(i)), lambda i_: (i_, 0)) for i in rg_idx]
    out_shape = [jax.ShapeDtypeStruct((t, width(i)), row_grads[i]) for i in rg_idx]
    out_specs += [_full_spec(params[i]) for i in pg_idx]
    out_shape += [jax.ShapeDtypeStruct(params[i].shape, F32) for i in pg_idx]
    res = pl.pallas_call(
        body, name=name, grid=(t // tile,),
        in_specs=list(specs) + [_full_spec(p) for p in params] + list(c_specs),
        out_specs=out_specs, out_shape=out_shape,
        compiler_params=_params(("arbitrary",)),
    )(*arrs, *params, *c_arrs)
    return res[:len(rg_idx)], res[len(rg_idx):]


def _fn_norm1(x, gain, sc, sh):
    h = _rms(x, gain) * (1.0 + sc) + sh
    return h, h


def _fn_rwkv_pre(h, hp, p, pp, mu_rkv, mu_w, mu_a, mu_g, w0, a0, k_k, k_a, w1, w2, a1, a2, g1, g2):
    d = p.shape[1] // 3
    dh = hp - h
    xw = h + dh * mu_w
    xa = h + dh * mu_a
    xg = h + dh * mu_g
    pr = p + (pp - p) * mu_rkv
    r, k, v = pr[:, :d], pr[:, d:2 * d], pr[:, 2 * d:]
    w_log = -_softplus(-(w0 + _bdot(jnp.tanh(_bdot(xw, w1)), w2))) - 0.5
    decay = jnp.exp(-jnp.exp(w_log))
    a = _sigmoid(a0 + _bdot(_bdot(xa, a1), a2))
    g = _bdot(_sigmoid(_bdot(xg, g1)), g2)
    kk = k * k_k
    kk = kk * lax.rsqrt(_segsum(kk * kk) + L2_EPS)
    k2 = k * (1.0 + (a - 1.0) * k_a)
    return r, decay, k2, v, -kk, kk * a, g


def _fn_rwkv_post(y, r, k2, v, g, ln_g, ln_b, r_k):
    inv = 1.0 / HEAD_DIM
    mean = _segsum(y) * inv
    yc = y - mean
    var = _segsum(yc * yc) * inv
    yn = yc * lax.rsqrt(var + GN_EPS) * ln_g + ln_b
    bonus = _segsum(r * k2 * r_k) * v
    return ((yn + bonus) * g,)


def _fn_rwkv_post_cat(y, r, k2, v, g, o_sb, ln_g, ln_b, r_k):
    return (jnp.concatenate([_fn_rwkv_post(y, r, k2, v, g, ln_g, ln_b, r_k)[0], o_sb], axis=1),)


def _fn_qk_norm(p, qg, kg):
    d = p.shape[1] // 3
    q, k, v = p[:, :d], p[:, d:2 * d], p[:, 2 * d:]
    inv = 1.0 / HEAD_DIM
    qn = q * lax.rsqrt(_segsum(q * q) * inv + RMS_EPS) * qg
    kn = k * lax.rsqrt(_segsum(k * k) * inv + RMS_EPS) * kg
    return qn, kn, v


def _fn_mix_norm2(x, mix, gt1, gain, sc, sh):
    x1 = x + gt1 * mix
    h2 = _rms(x1, gain) * (1.0 + sc) + sh
    return x1, h2


def _fn_swiglu(gate, up):
    return (_silu(gate) * up,)


def _swiglu_bwd(gu, dact, tile=128):
    t, dff = dact.shape
    tile = _pick(t, (tile,))

    def body(gate_ref, up_ref, d_ref, o_ref):
        _, vjp = jax.vjp(_fn_swiglu, gate_ref[...], up_ref[...])
        dgate, dup = vjp((d_ref[...],))
        o_ref[:, :dff] = dgate.astype(o_ref.dtype)
        o_ref[:, dff:] = dup.astype(o_ref.dtype)

    return pl.pallas_call(
        body, name="swiglu_bwd", grid=(t // tile,),
        in_specs=[pl.BlockSpec((tile, dff), lambda i: (i, 0)), pl.BlockSpec((tile, dff), lambda i: (i, 1)),
                  pl.BlockSpec((tile, dff), lambda i: (i, 0))],
        out_specs=pl.BlockSpec((tile, 2 * dff), lambda i: (i, 0)),
        out_shape=jax.ShapeDtypeStruct((t, 2 * dff), BF16),
        compiler_params=_params(("parallel",)),
    )(gu, gu, dact)


def _loss_head(name, x1, dn, target, gt2, tile=256):
    t, d = x1.shape
    tile = _pick(t, (tile,))

    def body(x1_ref, dn_ref, tg_ref, gt_ref, loss_ref, dout_ref, ddn_ref, dgt_ref):
        dnv = dn_ref[...]
        gt = gt_ref[...]
        err = x1_ref[...] + gt * dnv - tg_ref[...]
        dout = err * (1.0 / d)
        dout_ref[...] = dout
        ddn_ref[...] = (dout * gt).astype(ddn_ref.dtype)
        part = 0.5 * jnp.sum(jnp.sum(err * dout, axis=-1, keepdims=True), axis=0, keepdims=True)
        dgt = jnp.sum(dout * dnv, axis=0, keepdims=True)
        first = pl.program_id(0) == 0

        @pl.when(first)
        def _():
            loss_ref[...] = jnp.broadcast_to(part, loss_ref.shape)
            dgt_ref[...] = dgt

        @pl.when(jnp.logical_not(first))
        def _():
            loss_ref[...] += jnp.broadcast_to(part, loss_ref.shape)
            dgt_ref[...] += dgt

    row = pl.BlockSpec((tile, d), lambda i: (i, 0))
    vec = pl.BlockSpec((1, d), lambda i: (0, 0))
    return pl.pallas_call(
        body, name=name, grid=(t // tile,),
        in_specs=[row, row, row, vec],
        out_specs=[pl.BlockSpec((1, LANES), lambda i: (0, 0)), row, row, vec],
        out_shape=[jax.ShapeDtypeStruct((1, LANES), F32), jax.ShapeDtypeStruct((t, d), F32),
                   jax.ShapeDtypeStruct((t, d), BF16), jax.ShapeDtypeStruct((1, d), F32)],
        compiler_params=_params(("arbitrary",)),
    )(x1, dn, target, gt2)


SCAN_BLOCK = 32
N_COL = 5
WIDE = 2 * LANES


def _wide_eye():
    i = lax.broadcasted_iota(jnp.int32, (HEAD_DIM, WIDE), 0)
    j = lax.broadcasted_iota(jnp.int32, (HEAD_DIM, WIDE), 1) % HEAD_DIM
    return (i == j).astype(BF16)


def _wide_ones():
    i = lax.broadcasted_iota(jnp.int32, (WIDE, WIDE), 0) // HEAD_DIM
    j = lax.broadcasted_iota(jnp.int32, (WIDE, WIDE), 1) // HEAD_DIM
    return (i == j).astype(BF16)


COL_PIECES = (1, 3, 3, 3, 1)


def _col_tiles(refs, i, nq, eye, ones_bf):
    pieces = [[], [], []]
    for ref, n_pieces in zip(refs, COL_PIECES):
        full = ref[pl.ds(i, 1), :]
        for q in range(nq):
            rest = full[:, q * WIDE:(q + 1) * WIDE]
            for level in range(n_pieces):
                part = rest.astype(BF16)
                pieces[level].append(part * eye)
                rest = rest - part.astype(F32)
    count = [len(level) * HEAD_DIM for level in pieces]
    out = jnp.dot(jnp.concatenate(pieces[0] + pieces[1] + pieces[2], axis=0), ones_bf, preferred_element_type=F32)
    first, second, third = out[:count[0]], out[count[0]:count[0] + count[1]], out[count[0] + count[1]:]
    lo, hi = nq * HEAD_DIM, count[0] - nq * HEAD_DIM
    exact = first[lo:hi] + second + third
    return jnp.concatenate([first[:lo], exact, first[hi:]], axis=0).reshape(N_COL * nq, HEAD_DIM, WIDE)


def _bf16_round(x):
    bits = lax.bitcast_convert_type(x, jnp.uint32)
    bits = (bits + jnp.uint32(0x7FFF) + ((bits >> 16) & jnp.uint32(1))) & jnp.uint32(0xFFFF0000)
    return lax.bitcast_convert_type(bits, F32)


def _pair_tile(tiles_ref, n, p, nq):
    return tiles_ref[n * nq + p // 2, :, (p % 2) * LANES:(p % 2 + 1) * LANES]


def _scan_fwd(rem, w, wr, k, r, v):
    t, dr = v.shape
    npair, nq = dr // LANES, dr // WIDE
    tb = _pick(t, (SCAN_BLOCK,))

    def body(rem_ref, w_ref, wr_ref, k_ref, r_ref, v_ref, y_ref, hist_ref, s_ref, tiles_a, tiles_b):
        @pl.when(pl.program_id(0) == 0)
        def _():
            s_ref[...] = jnp.zeros_like(s_ref)

        eye, ones_bf = _wide_eye(), _wide_ones()
        col_refs = (rem_ref, w_ref, wr_ref, k_ref, r_ref)

        def step(i, tiles_ref):
            v_full, y_rows = v_ref[pl.ds(i, 1), :], []
            for p in range(npair):
                s = s_ref[p]
                hist_ref[i, p] = s
                c_rem, c_w, c_wr, c_k, c_r = [_pair_tile(tiles_ref, n, p, nq) for n in range(N_COL)]
                sa = jnp.sum(_bf16_round(s) * c_rem, axis=0, keepdims=True)
                s2 = s * c_w + c_wr * sa + c_k * v_full[:, p * LANES:(p + 1) * LANES]
                y_rows.append(jnp.sum(_bf16_round(s2) * c_r, axis=0, keepdims=True))
                s_ref[p] = s2
            y_ref[pl.ds(i, 1), :] = jnp.concatenate(y_rows, axis=1)

        tiles_a[...] = _col_tiles(col_refs, 0, nq, eye, ones_bf)

        def two_steps(m, carry):
            i = 2 * m
            tiles_b[...] = _col_tiles(col_refs, i + 1, nq, eye, ones_bf)
            step(i, tiles_a)
            tiles_a[...] = _col_tiles(col_refs, jnp.minimum(i + 2, tb - 1), nq, eye, ones_bf)
            step(i + 1, tiles_b)
            return carry

        lax.fori_loop(0, tb // 2, two_steps, 0)

    blk = pl.BlockSpec((tb, dr), lambda i: (i, 0))
    tiles = pltpu.VMEM((N_COL * nq, HEAD_DIM, WIDE), F32)
    return pl.pallas_call(
        body, name="rwkv_scan_fwd", grid=(t // tb,),
        in_specs=[blk] * 6,
        out_specs=[blk, pl.BlockSpec((tb, npair, HEAD_DIM, LANES), lambda i: (i, 0, 0, 0))],
        out_shape=[jax.ShapeDtypeStruct((t, dr), F32), jax.ShapeDtypeStruct((t, npair, HEAD_DIM, LANES), F32)],
        scratch_shapes=[pltpu.VMEM((npair, HEAD_DIM, LANES), F32), tiles, tiles],
        compiler_params=_params(("arbitrary",)),
    )(rem, w, wr, k, r, v)


def _scan_bwd(rem, w, wr, k, r, v, hist, dy):
    t, dr = v.shape
    npair, nq = dr // LANES, dr // WIDE
    tb = _pick(t, (SCAN_BLOCK,))
    nblk = t // tb

    def body(rem_ref, w_ref, wr_ref, k_ref, r_ref, v_ref, hist_ref, dy_ref,
             drem_ref, dw_ref, dwr_ref, dk_ref, dr_ref, dv_ref, ds_ref, tiles_a, tiles_b):
        @pl.when(pl.program_id(0) == 0)
        def _():
            ds_ref[...] = jnp.zeros_like(ds_ref)

        eye, ones_bf = _wide_eye(), _wide_ones()
        col_refs = (rem_ref, w_ref, wr_ref, k_ref, r_ref)
        out_refs = (drem_ref, dw_ref, dwr_ref, dk_ref, dr_ref)

        def step(i, tiles_ref):
            grads = [[None] * npair for _ in range(N_COL)]
            v_full, dy_full, dv_rows = v_ref[pl.ds(i, 1), :], dy_ref[pl.ds(i, 1), :], []
            for p in range(npair):
                lanes = slice(p * LANES, (p + 1) * LANES)
                s = hist_ref[i, p]
                c_rem, c_w, c_wr, c_k, c_r = [_pair_tile(tiles_ref, n, p, nq) for n in range(N_COL)]
                v_row, dy_row = v_full[:, lanes], dy_full[:, lanes]
                s_b = _bf16_round(s)
                sa = jnp.sum(s_b * c_rem, axis=0, keepdims=True)
                s2 = s * c_w + c_wr * sa + c_k * v_row
                d2 = ds_ref[p] + c_r * dy_row
                dsa = jnp.sum(d2 * c_wr, axis=0, keepdims=True)
                dv_rows.append(jnp.sum(d2 * c_k, axis=0, keepdims=True))
                ds_ref[p] = d2 * c_w + c_rem * dsa
                for n, tile in enumerate((s_b * dsa, d2 * s, d2 * sa, d2 * v_row, _bf16_round(s2) * dy_row)):
                    grads[n][p] = tile.astype(BF16)
            wide = [jnp.concatenate(grads[n][2 * q:2 * q + 2], axis=1) for n in range(N_COL) for q in range(nq)]
            sums = jnp.dot(jnp.concatenate(wide, axis=0), ones_bf, preferred_element_type=F32)
            sums = sums.reshape(N_COL * nq, HEAD_DIM, WIDE)
            rows = jnp.sum(sums * eye.astype(F32)[None], axis=1)
            dv_ref[pl.ds(i, 1), :] = jnp.concatenate(dv_rows, axis=1)
            for n, o_ref in enumerate(out_refs):
                o_ref[pl.ds(i, 1), :] = jnp.concatenate([rows[n * nq + q:n * nq + q + 1] for q in range(nq)], axis=1)

        tiles_a[...] = _col_tiles(col_refs, tb - 1, nq, eye, ones_bf)

        def two_steps(m, carry):
            i = tb - 1 - 2 * m
            tiles_b[...] = _col_tiles(col_refs, i - 1, nq, eye, ones_bf)
            step(i, tiles_a)
            tiles_a[...] = _col_tiles(col_refs, jnp.maximum(i - 2, 0), nq, eye, ones_bf)
            step(i - 1, tiles_b)
            return carry

        lax.fori_loop(0, tb // 2, two_steps, 0)

    blk = pl.BlockSpec((tb, dr), lambda i: (nblk - 1 - i, 0))
    tiles = pltpu.VMEM((N_COL * nq, HEAD_DIM, WIDE), F32)
    return pl.pallas_call(
        body, name="rwkv_scan_bwd", grid=(nblk,),
        in_specs=[blk] * 6 + [pl.BlockSpec((tb, npair, HEAD_DIM, LANES), lambda i: (nblk - 1 - i, 0, 0, 0)), blk],
        out_specs=[blk] * 6,
        out_shape=[jax.ShapeDtypeStruct((t, dr), F32)] * 6,
        scratch_shapes=[pltpu.VMEM((npair, HEAD_DIM, LANES), F32), tiles, tiles],
        compiler_params=_params(("arbitrary",)),
    )(rem, w, wr, k, r, v, hist, dy)


SB_BLOCK = 256
SB_HEADS = 4
SB_HEADS_BWD = 2
NT_DIMS = (((1,), (1,)), ((), ()))
TN_DIMS = (((0,), (0,)), ((), ()))
SB_SCALE = 1.0 / math.sqrt(HEAD_DIM)


def _dot2(x, tri):
    hi = x.astype(BF16)
    mid = (x - hi.astype(F32)).astype(BF16)
    return jnp.dot(hi, tri, preferred_element_type=F32) + jnp.dot(mid, tri, preferred_element_type=F32)


def _sb_block_iotas(bs):
    return lax.broadcasted_iota(jnp.int32, (bs, bs), 0), lax.broadcasted_iota(jnp.int32, (bs, bs), 1)


def _sb_fwd(q, k, v):
    h, t, d = q.shape
    bs = _pick(t, (SB_BLOCK,))
    nh = _pick(h, (SB_HEADS,))

    def body(q_ref, k_ref, v_ref, o_ref, l_ref):
        qi = pl.program_id(1)
        ri, ci = _sb_block_iotas(bs)
        tri_ge = (ri >= ci).astype(BF16)
        causal = ci < ri
        qv = [q_ref[hh] for hh in range(nh)]

        def blocks(j, masked, carry):
            accs, tails = carry[:nh], carry[nh:]
            heads = range(nh)
            rows = pl.ds(pl.multiple_of(j * bs, bs), bs)
            z = [lax.dot_general(qv[hh], k_ref[hh, rows, :], NT_DIMS, preferred_element_type=F32) * SB_SCALE
                 for hh in heads]
            log1m = [-_softplus(z[hh]) for hh in heads]
            if masked:
                log1m = [jnp.where(causal, x, 0.0) for x in log1m]
            cs = [_dot2(log1m[hh], tri_ge) for hh in heads]
            a = [jnp.exp(z[hh] + cs[hh] + tails[hh]) for hh in heads]
            if masked:
                a = [jnp.where(causal, x, 0.0) for x in a]
            accs = [accs[hh] + jnp.dot(a[hh].astype(BF16), v_ref[hh, rows, :], preferred_element_type=F32)
                    for hh in heads]
            return tuple(accs) + tuple(tails[hh] + cs[hh][:, 0:1] for hh in heads)

        carry = blocks(qi, True, (jnp.zeros((bs, d), F32),) * nh + (jnp.zeros((bs, 1), F32),) * nh)
        carry = lax.fori_loop(0, qi, lambda n, c: blocks(qi - 1 - n, False, c), carry)
        for hh in range(nh):
            o_ref[hh] = carry[hh]
            l_ref[hh] = jnp.broadcast_to(carry[nh + hh], (bs, d))

    qs = pl.BlockSpec((nh, bs, d), lambda hh, i: (hh, i, 0))
    ks = pl.BlockSpec((nh, t, d), lambda hh, i: (hh, 0, 0))
    return pl.pallas_call(
        body, name="sb_attn_fwd", grid=(h // nh, t // bs),
        in_specs=[qs, ks, ks], out_specs=[qs, qs],
        out_shape=[jax.ShapeDtypeStruct((h, t, d), F32)] * 2,
        compiler_params=_params(("parallel", "arbitrary")),
    )(q, k, v)


def _sb_bwd(q, k, v, lsum, do):
    h, t, d = q.shape
    bs = _pick(t, (SB_BLOCK,))
    nh = _pick(h, (SB_HEADS_BWD,))

    def body(q_ref, k_ref, v_ref, l_ref, do_ref, dq_ref, dk_ref, dv_ref):
        qi = pl.program_id(1)

        @pl.when(qi == 0)
        def _():
            dk_ref[...] = jnp.zeros_like(dk_ref)
            dv_ref[...] = jnp.zeros_like(dv_ref)

        ri, ci = _sb_block_iotas(bs)
        tri_lt = (ri < ci).astype(BF16)
        causal = ci < ri
        qv = [q_ref[hh] for hh in range(nh)]
        dob = [do_ref[hh].astype(BF16) for hh in range(nh)]
        ltot = [l_ref[hh][:, 0:1] for hh in range(nh)]

        def blocks(j, masked, carry):
            dq, pc, ec = carry[:nh], carry[nh:2 * nh], carry[2 * nh:]
            heads = range(nh)
            rows = pl.ds(pl.multiple_of(j * bs, bs), bs)
            z = [lax.dot_general(qv[hh], k_ref[hh, rows, :], NT_DIMS, preferred_element_type=F32) * SB_SCALE
                 for hh in heads]
            da = [lax.dot_general(dob[hh], v_ref[hh, rows, :], NT_DIMS, preferred_element_type=F32) for hh in heads]
            nsp = [-_softplus(z[hh]) for hh in heads]
            log1m = [jnp.where(causal, x, 0.0) for x in nsp] if masked else nsp
            below = [_dot2(log1m[hh], tri_lt) + pc[hh] for hh in heads]
            a = [jnp.exp(z[hh] + (ltot[hh] - below[hh])) for hh in heads]
            if masked:
                a = [jnp.where(causal, x, 0.0) for x in a]
            e = [a[hh] * da[hh] for hh in heads]
            ebelow = [_dot2(e[hh], tri_lt) + ec[hh] for hh in heads]
            dz = [e[hh] * jnp.exp(nsp[hh]) - jnp.exp(z[hh] + nsp[hh]) * ebelow[hh] for hh in heads]
            if masked:
                dz = [jnp.where(causal, x, 0.0) for x in dz]
            dzb = [(x * SB_SCALE).astype(BF16) for x in dz]
            for hh in heads:
                dv_ref[hh, rows, :] += lax.dot_general(a[hh].astype(BF16), dob[hh], TN_DIMS,
                                                       preferred_element_type=F32)
            for hh in heads:
                dk_ref[hh, rows, :] += lax.dot_general(dzb[hh], qv[hh], TN_DIMS, preferred_element_type=F32)
            dq = [dq[hh] + jnp.dot(dzb[hh], k_ref[hh, rows, :], preferred_element_type=F32) for hh in heads]
            pc = [pc[hh] + jnp.sum(log1m[hh], axis=1, keepdims=True) for hh in heads]
            ec = [ec[hh] + jnp.sum(e[hh], axis=1, keepdims=True) for hh in heads]
            return tuple(dq) + tuple(pc) + tuple(ec)

        zcol = jnp.zeros((bs, 1), F32)
        carry = lax.fori_loop(0, qi, lambda j, c: blocks(j, False, c),
                              (jnp.zeros((bs, d), F32),) * nh + (zcol,) * (2 * nh))
        carry = blocks(qi, True, carry)
        for hh in range(nh):
            dq_ref[hh] = carry[hh]

    qs = pl.BlockSpec((nh, bs, d), lambda hh, i: (hh, i, 0))
    ks = pl.BlockSpec((nh, t, d), lambda hh, i: (hh, 0, 0))
    return pl.pallas_call(
        body, name="sb_attn_bwd", grid=(h // nh, t // bs),
        in_specs=[qs, ks, ks, qs, qs], out_specs=[qs, ks, ks],
        out_shape=[jax.ShapeDtypeStruct((h, t, d), F32)] * 3,
        compiler_params=_params(("parallel", "arbitrary")),
    )(q, k, v, lsum, do)


ANY = pl.BlockSpec(memory_space=pl.ANY)
IN_VMEM = pl.BlockSpec(memory_space=pltpu.VMEM)


def _coords():
    return lax.axis_index("x"), lax.axis_index("y"), lax.axis_index("c")


def _flip(v, bit):
    return 1 - v if bit else v


def _remote(src, dst, send_sem, recv_sem, device):
    return pltpu.make_async_remote_copy(src_ref=src, dst_ref=dst, send_sem=send_sem, recv_sem=recv_sem,
                                        device_id=device, device_id_type=MESH)


def _all_gather8(name, blk):
    m, n = blk.shape

    def body(x_ref, o_ref, send_sems, recv_sems, local_sem):
        x, y, c = _coords()
        own = pltpu.make_async_copy(x_ref, o_ref.at[4 * x + 2 * y + c], local_sem)
        own.start()
        peers = []
        for bits in range(1, N_DEV):
            px, py, pc = _flip(x, (bits >> 2) & 1), _flip(y, (bits >> 1) & 1), _flip(c, bits & 1)
            peers.append((px, py, pc))
        sends = []
        for k, peer in enumerate(peers):
            cp = _remote(x_ref, o_ref.at[4 * x + 2 * y + c], send_sems.at[k], recv_sems.at[k], peer)
            cp.start()
            sends.append(cp)
        for k, (px, py, pc) in enumerate(peers):
            slot = o_ref.at[4 * px + 2 * py + pc]
            _remote(slot, slot, send_sems.at[k], recv_sems.at[k], (px, py, pc)).wait_recv()
        for cp in sends:
            cp.wait_send()
        own.wait()

    return pl.pallas_call(
        body, name=name, out_shape=jax.ShapeDtypeStruct((N_DEV, m, n), blk.dtype),
        in_specs=[IN_VMEM], out_specs=IN_VMEM,
        scratch_shapes=[pltpu.SemaphoreType.DMA((N_DEV - 1,)), pltpu.SemaphoreType.DMA((N_DEV - 1,)),
                        pltpu.SemaphoreType.DMA],
        compiler_params=_params(),
    )(blk)


def _gather_weights(shards):
    nw = len(shards)

    def body(*refs):
        ins, outs = refs[:nw], refs[nw:2 * nw]
        ici_send, ici_recv, d2d_send, d2d_recv = refs[2 * nw:]
        x, y, c = _coords()
        chip = 2 * x + y
        sibling = (x, y, 1 - c)
        started = []
        for w in range(nw):
            half = shards[w].shape[0] // 2
            mine = pl.ds(c * half, half)
            for j, (a, b) in enumerate(XY_MASKS):
                cp = _remote(ins[w].at[mine], outs[w].at[chip, mine], ici_send.at[3 * w + j],
                             ici_recv.at[3 * w + j], (_flip(x, a), _flip(y, b), c))
                cp.start()
                started.append(cp)
        for w in range(nw):
            half = shards[w].shape[0] // 2
            mine = pl.ds(c * half, half)
            for j, (a, b) in enumerate(XY_MASKS):
                px, py = _flip(x, a), _flip(y, b)
                landed = outs[w].at[2 * px + py, mine]
                _remote(landed, landed, ici_send.at[3 * w + j], ici_recv.at[3 * w + j], (px, py, c)).wait_recv()
                cp = _remote(landed, landed, d2d_send.at[3 * w + j], d2d_recv.at[3 * w + j], sibling)
                cp.start()
                started.append(cp)
        for w in range(nw):
            half = shards[w].shape[0] // 2
            theirs = pl.ds((1 - c) * half, half)
            for j, (a, b) in enumerate(XY_MASKS):
                slot = outs[w].at[2 * _flip(x, a) + _flip(y, b), theirs]
                _remote(slot, slot, d2d_send.at[3 * w + j], d2d_recv.at[3 * w + j], sibling).wait_recv()
        for cp in started:
            cp.wait_send()

    return pl.pallas_call(
        body, name="gather_weights",
        out_shape=[jax.ShapeDtypeStruct((N_CHIPS,) + s.shape, s.dtype) for s in shards],
        in_specs=[ANY] * nw, out_specs=[ANY] * nw,
        scratch_shapes=[pltpu.SemaphoreType.DMA((3 * nw,))] * 4,
        compiler_params=_params(),
    )(*shards)


def _pair_exchange(grads):
    nw = len(grads)

    def body(*refs):
        ins, outs = refs[:nw], refs[nw:2 * nw]
        send_sems, recv_sems = refs[2 * nw:]
        x, y, c = _coords()
        sibling = (x, y, 1 - c)
        sends = []
        for w in range(nw):
            half = grads[w].shape[1] // 2
            cp = _remote(ins[w].at[:, pl.ds((1 - c) * half, half)], outs[w], send_sems.at[w], recv_sems.at[w], sibling)
            cp.start()
            sends.append(cp)
        for w in range(nw):
            _remote(outs[w], outs[w], send_sems.at[w], recv_sems.at[w], sibling).wait_recv()
        for cp in sends:
            cp.wait_send()

    return pl.pallas_call(
        body, name="grad_pair_exchange",
        out_shape=[jax.ShapeDtypeStruct((N_CHIPS, g.shape[1] // 2, g.shape[2]), g.dtype) for g in grads],
        in_specs=[ANY] * nw, out_specs=[ANY] * nw,
        scratch_shapes=[pltpu.SemaphoreType.DMA((nw,))] * 2,
        compiler_params=_params(),
    )(*grads)


def _chip_exchange(parts):
    nw = len(parts)

    def body(*refs):
        ins, outs = refs[:nw], refs[nw:2 * nw]
        send_sems, recv_sems = refs[2 * nw:]
        x, y, c = _coords()
        chip = 2 * x + y
        sends = []
        for w in range(nw):
            for j, (a, b) in enumerate(XY_MASKS):
                px, py = _flip(x, a), _flip(y, b)
                cp = _remote(ins[w].at[2 * px + py], outs[w].at[chip], send_sems.at[3 * w + j],
                             recv_sems.at[3 * w + j], (px, py, c))
                cp.start()
                sends.append(cp)
        for w in range(nw):
            for j, (a, b) in enumerate(XY_MASKS):
                px, py = _flip(x, a), _flip(y, b)
                slot = outs[w].at[2 * px + py]
                _remote(slot, slot, send_sems.at[3 * w + j], recv_sems.at[3 * w + j], (px, py, c)).wait_recv()
        for cp in sends:
            cp.wait_send()

    return pl.pallas_call(
        body, name="grad_chip_exchange",
        out_shape=[jax.ShapeDtypeStruct(p.shape, p.dtype) for p in parts],
        in_specs=[ANY] * nw, out_specs=[ANY] * nw,
        scratch_shapes=[pltpu.SemaphoreType.DMA((3 * nw,))] * 2,
        compiler_params=_params(),
    )(*parts)


def _pair_share(shards):
    nw = len(shards)

    def body(*refs):
        bufs = refs[nw:2 * nw]
        send_sems, recv_sems = refs[2 * nw:]
        x, y, c = _coords()
        sibling = (x, y, 1 - c)
        sends = []
        for w in range(nw):
            half = shards[w].shape[0] // 2
            mine = bufs[w].at[pl.ds(c * half, half)]
            cp = _remote(mine, mine, send_sems.at[w], recv_sems.at[w], sibling)
            cp.start()
            sends.append(cp)
        for w in range(nw):
            half = shards[w].shape[0] // 2
            theirs = bufs[w].at[pl.ds((1 - c) * half, half)]
            _remote(theirs, theirs, send_sems.at[w], recv_sems.at[w], sibling).wait_recv()
        for cp in sends:
            cp.wait_send()

    return pl.pallas_call(
        body, name="grad_pair_share",
        out_shape=[jax.ShapeDtypeStruct(s.shape, s.dtype) for s in shards],
        in_specs=[ANY] * nw, out_specs=[ANY] * nw,
        input_output_aliases={w: w for w in range(nw)},
        scratch_shapes=[pltpu.SemaphoreType.DMA((nw,))] * 2,
        compiler_params=_params(),
    )(*shards)


TILE_BYTES = 2 * 1024 * 1024


def _row_tile(rows, cols, mult=8):
    best = None
    for tr in range(mult, rows + 1, mult):
        if rows % tr == 0 and tr * cols * 4 <= TILE_BYTES:
            best = tr
    return best if best is not None else rows


def _pair_add(name, grad, other, place):
    _, half, cols = other.shape
    tr = _row_tile(half, cols, mult=16)
    nb = half // tr

    def body(place_ref, g_ref, o_ref, out_ref):
        out_ref[...] = (g_ref[...] + o_ref[...]).astype(out_ref.dtype)

    return pl.pallas_call(
        body, name=name, out_shape=jax.ShapeDtypeStruct(other.shape, BF16),
        grid_spec=pltpu.PrefetchScalarGridSpec(
            num_scalar_prefetch=1, grid=(N_CHIPS, nb),
            in_specs=[pl.BlockSpec((None, tr, cols), lambda s, i, place_ref: (s, place_ref[0] * nb + i, 0)),
                      pl.BlockSpec((None, tr, cols), lambda s, i, place_ref: (s, i, 0))],
            out_specs=pl.BlockSpec((None, tr, cols), lambda s, i, place_ref: (s, i, 0))),
        compiler_params=_params(("parallel", "parallel")),
    )(place, grad, other)


def _sum_chips(name, own, others, place):
    _, half, cols = own.shape
    tr = _row_tile(half, cols, mult=16)
    nb = half // tr

    def body(place_ref, own_ref, a_ref, b_ref, c_ref, out_ref):
        total = own_ref[...].astype(F32) + a_ref[...].astype(F32)
        out_ref[...] = (total + b_ref[...].astype(F32)) + c_ref[...].astype(F32)

    def peer(mask):
        return pl.BlockSpec((None, tr, cols), lambda i, place_ref: (place_ref[1] ^ mask, i, 0))

    return pl.pallas_call(
        body, name=name, out_shape=jax.ShapeDtypeStruct((2 * half, cols), F32),
        grid_spec=pltpu.PrefetchScalarGridSpec(
            num_scalar_prefetch=1, grid=(nb,),
            in_specs=[peer(0), peer(1), peer(2), peer(3)],
            out_specs=pl.BlockSpec((tr, cols), lambda i, place_ref: (place_ref[0] * nb + i, 0))),
        compiler_params=_params(("parallel",)),
    )(place, own, others, others, others)


def _adamw_math(w, g, m, v):
    m2 = ADAM_B1 * m + (1.0 - ADAM_B1) * g
    v2 = ADAM_B2 * v + (1.0 - ADAM_B2) * (g * g)
    m_hat = m2 / (1.0 - ADAM_B1 ** ADAM_STEP)
    v_hat = v2 / (1.0 - ADAM_B2 ** ADAM_STEP)
    delta = -ADAM_LR * (m_hat / (jnp.sqrt(v_hat) + ADAM_EPS) + ADAM_WD * w)
    return delta, m2, v2


def _adamw(name, w, g, m, v):
    rows, cols = w.shape
    tr = _row_tile(rows, cols)

    def body(w_ref, g_ref, m_ref, v_ref, d_ref, m2_ref, v2_ref):
        d_ref[...], m2_ref[...], v2_ref[...] = _adamw_math(w_ref[...], g_ref[...], m_ref[...], v_ref[...])

    blk = pl.BlockSpec((tr, cols), lambda i: (i, 0))
    return pl.pallas_call(
        body, name=name, out_shape=[jax.ShapeDtypeStruct(w.shape, F32)] * 3, grid=(rows // tr,),
        in_specs=[blk] * 4, out_specs=[blk] * 3,
        compiler_params=_params(("parallel",)),
    )(w, g, m, v)


def _small_update(gathered, w, m, v):
    def body(gs_ref, w_ref, m_ref, v_ref, g_ref, d_ref, m2_ref, v2_ref):
        g = gs_ref[0]
        for dev in range(1, N_DEV):
            g = g + gs_ref[dev]
        g_ref[...] = g
        d_ref[...], m2_ref[...], v2_ref[...] = _adamw_math(w_ref[...], g, m_ref[...], v_ref[...])

    return pl.pallas_call(
        body, name="small_update", out_shape=[jax.ShapeDtypeStruct(w.shape, F32)] * 4,
        compiler_params=_params(),
    )(gathered, w, m, v)


def _ada_mod(c_all, w_shard, b_shard):
    d, n = w_shard.shape
    tn = _pick(n, (512, 256, 128))

    def body(c_ref, w_ref, b_ref, o_ref):
        act = _silu(c_ref[...]).astype(BF16)
        o_ref[...] = jnp.dot(act, w_ref[...].astype(BF16), preferred_element_type=F32) + b_ref[...]

    return pl.pallas_call(
        body, name="ada_mod", out_shape=jax.ShapeDtypeStruct((c_all.shape[0], n), F32), grid=(n // tn,),
        in_specs=[pl.BlockSpec(c_all.shape, lambda j: (0, 0)), pl.BlockSpec((d, tn), lambda j: (0, j)),
                  pl.BlockSpec((1, tn), lambda j: (0, j))],
        out_specs=pl.BlockSpec((c_all.shape[0], tn), lambda j: (0, j)),
        compiler_params=_params(("parallel",)),
    )(c_all, w_shard, b_shard)


def _ada_grad(c_pad, dmod_pad):
    rows, d = c_pad.shape
    n = dmod_pad.shape[1]
    tn = _pick(n, (512, 256, 128))

    def body(c_ref, g_ref, o_ref):
        act = _silu(c_ref[...]).astype(BF16)
        o_ref[...] = lax.dot_general(act, g_ref[...].astype(BF16), TN_DIMS, preferred_element_type=F32)

    return pl.pallas_call(
        body, name="ada_grad", out_shape=jax.ShapeDtypeStruct((d, n), F32), grid=(n // tn,),
        in_specs=[pl.BlockSpec((rows, d), lambda j: (0, 0)), pl.BlockSpec((rows, tn), lambda j: (0, j))],
        out_specs=pl.BlockSpec((d, tn), lambda j: (0, j)),
        compiler_params=_params(("parallel",)),
    )(c_pad, dmod_pad)


WEIGHTS = ['w_ada', 'b_ada', 'norm1_gain', 'norm2_gain', 'w_in', 'mu_rkv', 'mu_w', 'mu_a', 'mu_g', 'w0', 'w1',
           'w2', 'a0', 'a1', 'a2', 'g1', 'g2', 'k_k', 'k_a', 'r_k', 'ln_x_gain', 'ln_x_bias', 'q_norm_gain',
           'k_norm_gain', 'w_out', 'w_gate_up', 'w_down']
SMALL = ['b_ada', 'norm1_gain', 'norm2_gain', 'mu_rkv', 'mu_w', 'mu_a', 'mu_g', 'w0', 'a0', 'k_k', 'k_a', 'r_k',
         'ln_x_gain', 'ln_x_bias', 'q_norm_gain', 'k_norm_gain']
PACK_ROWS = 8
LOSS_SLOT = LANES


def _shift_down(a):
    return jnp.pad(a[:-1], ((1, 0), (0, 0)))


def _shift_up(a):
    return jnp.pad(a[1:], ((0, 1), (0, 0)))


def _pack_small(vals):
    flat = jnp.concatenate([v.reshape(1, -1) for v in vals], axis=1)
    unit = PACK_ROWS * LANES
    total = -(-flat.shape[1] // unit) * unit
    flat = jnp.pad(flat, ((0, 0), (0, total - flat.shape[1])))
    return flat.reshape(PACK_ROWS, total // PACK_ROWS)


def kernel(x, c, w_ada, b_ada, norm1_gain, norm2_gain, w_in, mu_rkv, mu_w, mu_a, mu_g, w0, w1, w2, a0, a1, a2, g1, g2, k_k, k_a, r_k, ln_x_gain, ln_x_bias, q_norm_gain, k_norm_gain, w_out, w_gate_up, w_down, loss_target, m_w_ada, m_b_ada, m_norm1_gain, m_norm2_gain, m_w_in, m_mu_rkv, m_mu_w, m_mu_a, m_mu_g, m_w0, m_w1, m_w2, m_a0, m_a1, m_a2, m_g1, m_g2, m_k_k, m_k_a, m_r_k, m_ln_x_gain, m_ln_x_bias, m_q_norm_gain, m_k_norm_gain, m_w_out, m_w_gate_up, m_w_down, v_w_ada, v_b_ada, v_norm1_gain, v_norm2_gain, v_w_in, v_mu_rkv, v_mu_w, v_mu_a, v_mu_g, v_w0, v_w1, v_w2, v_a0, v_a1, v_a2, v_g1, v_g2, v_k_k, v_k_a, v_r_k, v_ln_x_gain, v_ln_x_bias, v_q_norm_gain, v_k_norm_gain, v_w_out, v_w_gate_up, v_w_down):
    given = dict(locals())
    wt = {n: given[n][0] for n in WEIGHTS}
    mom = {n: given["m_" + n][0] for n in WEIGHTS}
    var = {n: given["v_" + n][0] for n in WEIGHTS}
    for tree in (wt, mom, var):
        tree["b_ada"] = tree["b_ada"].reshape(1, -1)
        for n in SMALL[1:]:
            tree[n] = tree[n].reshape(1, -1)

    ax, ay, ac = _coords()
    chip = 2 * ax + ay
    dev = 4 * ax + 2 * ay + ac
    xs, target = x[0], loss_target[0]
    t, d = xs.shape
    dr = wt["w0"].shape[1]
    ds = d - dr
    nh = ds // HEAD_DIM
    dff = wt["w_down"].shape[0] * N_CHIPS
    n_ada = wt["w_ada"].shape[1]
    lw, la, lg = wt["w1"].shape[1], wt["a1"].shape[1], wt["g1"].shape[1]

    def lora_a(tree):
        return jnp.concatenate([tree["w1"], tree["a1"], tree["g1"]], axis=1)

    def lora_b(tree):
        return jnp.concatenate([tree["w2"], tree["a2"], tree["g2"]], axis=0)

    shards = [wt["w_in"], wt["w_out"], wt["w_gate_up"], wt["w_down"], lora_a(wt), lora_b(wt)]
    shards = [s.astype(BF16) for s in shards]
    gathered_w = [lax.dynamic_update_slice(full, own[None], (chip, 0, 0))
                  for full, own in zip(_gather_weights(shards), shards)]
    full_in, full_out, full_gu, full_down, full_la, full_lb = gathered_w
    full_out = full_out.reshape(d, d)
    full_down = full_down.reshape(dff, d)
    full_la = full_la.reshape(d, lw + la + lg).astype(F32)
    full_lb = full_lb.transpose(1, 0, 2).reshape(lw + la + lg, dr).astype(F32)
    w1f, a1f, g1f = full_la[:, :lw], full_la[:, lw:lw + la], full_la[:, lw + la:]
    w2f, a2f, g2f = full_lb[:lw], full_lb[lw:lw + la], full_lb[lw + la:]

    c_all = _all_gather8("gather_c", c.reshape(PACK_ROWS, d // PACK_ROWS)).reshape(N_DEV, d)
    b_shard = lax.dynamic_slice(wt["b_ada"], (0, chip * n_ada), (1, n_ada))
    mod_part = _ada_mod(c_all, wt["w_ada"], b_shard)
    mod_all = _all_gather8("gather_mod", mod_part)[::2]
    mod = lax.dynamic_slice(mod_all, (0, dev, 0), (N_CHIPS, 1, n_ada)).reshape(1, N_CHIPS * n_ada)
    sh1, sc1, gt1, sh2, sc2, gt2 = [mod[:, i * d:(i + 1) * d] for i in range(6)]

    h, h_bf = _rowwise("norm1", _fn_norm1, [xs], [wt["norm1_gain"], sc1, sh1], [(d, F32), (d, BF16)], 256)
    hp = _shift_down(h)
    p = _matmul("mm_in", h_bf, full_in, b_shards=True, tm=512, tn=1536, tk=2048, n_outer=True)
    p_rkv, p_sb = (p, 3 * dr, 0), (p, 3 * ds, 1)
    pp = _shift_down(p[:, :3 * dr])
    pre_rows = [h, hp, p_rkv, pp]
    pre_params = [wt["mu_rkv"], wt["mu_w"], wt["mu_a"], wt["mu_g"], wt["w0"], wt["a0"], wt["k_k"], wt["k_a"],
                  w1f, w2f, a1f, a2f, g1f, g2f]
    pre = _rowwise("rwkv_pre", _fn_rwkv_pre, pre_rows, pre_params, [(dr, F32)] * 7, 64)
    r_, w_, k2, v_, rem, wr, g_ = pre
    y_raw, hist = _scan_fwd(rem, w_, wr, k2, r_, v_)
    post_rows = [y_raw, r_, k2, v_, g_]
    post_params = [wt["ln_x_gain"], wt["ln_x_bias"], wt["r_k"]]

    qg = jnp.tile(wt["q_norm_gain"], (1, nh))
    kg = jnp.tile(wt["k_norm_gain"], (1, nh))
    qn, kn, vs = _rowwise("qk_norm", _fn_qk_norm, [p_sb], [qg, kg], [(ds, BF16)] * 3, 256)

    def to_heads(a):
        return a.reshape(t, nh, HEAD_DIM).transpose(1, 0, 2)

    def from_heads(a):
        return a.transpose(1, 0, 2).reshape(t, ds)

    qh, kh, vh = to_heads(qn), to_heads(kn), to_heads(vs)
    o_h, lsum = _sb_fwd(qh, kh, vh)
    (ycat,) = _rowwise("rwkv_post", _fn_rwkv_post_cat, post_rows + [from_heads(o_h)], post_params, [(d, BF16)], 256)
    mix = _matmul("mm_out", ycat, full_out, tm=512, tn=1024, tk=2048, n_outer=True)
    norm2_params = [gt1, wt["norm2_gain"], sc2, sh2]
    x1, h2 = _rowwise("mix_norm2", _fn_mix_norm2, [xs, mix], norm2_params, [(d, F32), (d, BF16)], 256)
    gu = _matmul("mm_gate_up", h2, full_gu, b_shards=True, tm=512, tn=1408, tk=2048, n_outer=True)
    gate_up = [(gu, dff, 0), (gu, dff, 1)]
    (act,) = _rowwise("swiglu", _fn_swiglu, gate_up, [], [(dff, BF16)], 256)
    dn = _matmul("mm_down", act, full_down, tm=1024, tn=1024, tk=1408)
    loss_vec, dout, ddn, dgt2 = _loss_head("loss_head", x1, dn, target, gt2)

    dact = _matmul("mm_down_dx", ddn, full_down, tb=True, tm=512, tn=1408, tk=2048, n_outer=True)
    gw_down = _matmul("mm_down_dw", act, ddn, ta=True, tm=1408, tn=1024, tk=512)
    dgu = _swiglu_bwd(gu, dact)
    dh2 = _matmul("mm_gate_up_dx", dgu, full_gu, tb=True, b_shards=True, tm=1024, tn=1024, tk=1408)
    gw_gu = _matmul("mm_gate_up_dw", h2, dgu, ta=True, out_shards=True, tm=1024, tn=1408, tk=512)
    (dx_a, dmix), (dgt1, dgain2, dsc2, dsh2) = _rowwise_bwd(
        "mix_norm2_bwd", _fn_mix_norm2, [xs, mix], norm2_params, [[dout], [dh2]], [F32, BF16], [True] * 4, 128)
    dycat = _matmul("mm_out_dx", dmix, full_out, tb=True, tm=512, tn=1024, tk=2048, n_outer=True)
    gw_out = _matmul("mm_out_dw", ycat, dmix, ta=True, tm=1024, tn=1024, tk=512)
    (dy_raw, dr_f, dk_f, dv_f, dg), (dlng, dlnb, drk) = _rowwise_bwd(
        "rwkv_post_bwd", _fn_rwkv_post, post_rows, post_params, [[(dycat, dr, 0)]], [F32] * 5, [True] * 3, 128)
    drem_s, dw_s, dwr_s, dk_s, dr_s, dv_s = _scan_bwd(rem, w_, wr, k2, r_, v_, hist, dy_raw)
    do_h = to_heads(dycat[:, dr:])
    dqh, dkh, dvh = _sb_bwd(qh, kh, vh, lsum, do_h)
    (dp_sb,), (dqg, dkg) = _rowwise_bwd(
        "qk_norm_bwd", _fn_qk_norm, [p_sb], [qg, kg], [[from_heads(dqh)], [from_heads(dkh)], [from_heads(dvh)]],
        [F32], [True, True], 128)
    pre_cts = [[dr_s, dr_f], [dw_s], [dk_s, dk_f], [dv_s, dv_f], [drem_s], [dwr_s], [dg]]
    (dh_a, dhp, dp_rkv, dpp), pre_g = _rowwise_bwd(
        "rwkv_pre_bwd", _fn_rwkv_pre, pre_rows, pre_params, pre_cts, [F32] * 4, [True] * 14, 64)
    dp = jnp.concatenate([dp_rkv + _shift_up(dpp), dp_sb], axis=1).astype(BF16)
    dh_mm = _matmul("mm_in_dx", dp, full_in, tb=True, b_shards=True, tm=1024, tn=1024, tk=1536)
    gw_in = _matmul("mm_in_dw", h_bf, dp, ta=True, out_shards=True, tm=1024, tn=1536, tk=512)
    (grad_x,), (dgain1, dsc1, dsh1) = _rowwise_bwd(
        "norm1_bwd", _fn_norm1, [xs], [wt["norm1_gain"], sc1, sh1], [[dh_a, dh_mm, _shift_up(dhp)], []],
        [F32], [True] * 3, 128, add_to_first=[dx_a])

    g_mu_rkv, g_mu_w, g_mu_a, g_mu_g, g_w0, g_a0, g_kk, g_ka, gw1, gw2, ga1, ga2, gg1, gg2 = pre_g
    g_la = jnp.concatenate([gw1, ga1, gg1], axis=1).reshape(N_CHIPS, d // N_CHIPS, lw + la + lg)
    g_lb = jnp.concatenate([gw2, ga2, gg2], axis=0)
    g_lb = g_lb.reshape(lw + la + lg, N_CHIPS, dr // N_CHIPS).transpose(1, 0, 2)
    local = [gw_in, gw_out.reshape(N_CHIPS, d // N_CHIPS, d), gw_gu, gw_down.reshape(N_CHIPS, dff // N_CHIPS, d),
             g_la, g_lb]
    names = ["w_in", "w_out", "w_gate_up", "w_down", "lora_a", "lora_b"]
    place = jnp.stack([ac, chip]).astype(jnp.int32)
    from_sibling = _pair_exchange(local)
    pair_sums = [_pair_add("pair_add_" + n, g, o, place) for n, g, o in zip(names, local, from_sibling)]
    from_chips = _chip_exchange(pair_sums)
    halves = [_sum_chips("chip_sum_" + n, p, q, place) for n, p, q in zip(names, pair_sums, from_chips)]
    r_in, r_out, r_gu, r_down, r_la, r_lb = _pair_share(halves)

    dmod = jnp.concatenate([dsh1, dsc1, dgt1, dsh2, dsc2, dgt2], axis=1)
    dqg = dqg.reshape(nh, HEAD_DIM).sum(axis=0, keepdims=True)
    dkg = dkg.reshape(nh, HEAD_DIM).sum(axis=0, keepdims=True)
    small_g = [dmod, dgain1, dgain2, g_mu_rkv, g_mu_w, g_mu_a, g_mu_g, g_w0, g_a0, g_kk, g_ka, drk, dlng, dlnb,
               dqg, dkg]
    lead = jnp.zeros((1, LOSS_SLOT), F32)
    packed = _pack_small([loss_vec] + small_g)
    gathered = _all_gather8("gather_small", packed)
    sm_g, sm_d, sm_m, sm_v = _small_update(gathered, _pack_small([lead] + [wt[n] for n in SMALL]),
                                           _pack_small([lead] + [mom[n] for n in SMALL]),
                                           _pack_small([lead] + [var[n] for n in SMALL]))
    loss = sm_g.reshape(-1)[0]

    def unpack(packed_arr):
        flat, out, pos = packed_arr.reshape(-1), {}, LOSS_SLOT
        for n in SMALL:
            size = wt[n].size
            out[n] = flat[pos:pos + size]
            pos += size
        return out

    res = {"grad": unpack(sm_g), "delta": unpack(sm_d), "m": unpack(sm_m), "v": unpack(sm_v)}

    dmod_all = gathered.reshape(N_DEV, -1)[:, LOSS_SLOT:LOSS_SLOT + N_CHIPS * n_ada]
    dmod_cols = lax.dynamic_slice(dmod_all, (0, chip * n_ada), (N_DEV, n_ada))
    pad8 = ((0, N_DEV), (0, 0))
    res["grad"]["w_ada"] = _ada_grad(jnp.pad(c_all, pad8), jnp.pad(dmod_cols, pad8))

    res["grad"].update(w_in=r_in, w_out=r_out, w_gate_up=r_gu, w_down=r_down)
    for n in ("w_ada", "w_in", "w_out", "w_gate_up", "w_down"):
        res["delta"][n], res["m"][n], res["v"][n] = _adamw("adamw_" + n, wt[n], res["grad"][n], mom[n], var[n])
    la_d, la_m, la_v = _adamw("adamw_lora_a", lora_a(wt), r_la, lora_a(mom), lora_a(var))
    lb_d, lb_m, lb_v = _adamw("adamw_lora_b", lora_b(wt), r_lb, lora_b(mom), lora_b(var))
    for key, pa, pb in (("grad", r_la, r_lb), ("delta", la_d, lb_d), ("m", la_m, lb_m), ("v", la_v, lb_v)):
        res[key].update(w1=pa[:, :lw], a1=pa[:, lw:lw + la], g1=pa[:, lw + la:],
                        w2=pb[:lw], a2=pb[lw:lw + la], g2=pb[lw + la:])

    outs = [loss, grad_x[None]]
    for key in ("grad", "delta", "m", "v"):
        outs += [res[key][n].reshape(given[n].shape) for n in WEIGHTS]
    return tuple(outs)
```

```python
import functools
import math

import jax
import jax.numpy as jnp
from jax import lax
from jax.experimental import pallas as pl
from jax.experimental.pallas import tpu as pltpu

F32 = jnp.float32
BF16 = jnp.bfloat16
HEAD_DIM = 64
LANES = 128
RMS_EPS = 1e-6
GN_EPS = 64e-5
L2_EPS = 1e-12
ADAM_LR, ADAM_B1, ADAM_B2, ADAM_EPS, ADAM_WD, ADAM_STEP = 0.001, 0.9, 0.999, 1e-08, 0.01, 10
VMEM_LIMIT = 56 * 1024 * 1024
MESH = pl.DeviceIdType.MESH
HI = lax.Precision.HIGHEST
N_CHIPS = 4
N_DEV = 8
XY_MASKS = ((1, 0), (0, 1), (1, 1))


def _pick(dim, prefs):
    for p in prefs:
        if dim % p == 0:
            return p
    return dim


def _params(sem=None, vmem=VMEM_LIMIT):
    return pltpu.CompilerParams(dimension_semantics=sem, vmem_limit_bytes=vmem)


def _sigmoid(x):
    return 1.0 / (1.0 + jnp.exp(-x))


@jax.custom_vjp
def _softplus(x):
    return jnp.maximum(x, 0.0) + jnp.log(1.0 + jnp.exp(-jnp.abs(x)))


_softplus.defvjp(lambda x: (_softplus(x), x), lambda x, g: (g * _sigmoid(x),))


def _silu(x):
    return x * _sigmoid(x)


@jax.custom_vjp
def _bdot(a, b):
    return jnp.dot(a.astype(BF16), b.astype(BF16), preferred_element_type=F32)


def _bdot_bwd(res, g):
    a, b = res
    gb = g.astype(BF16)
    da = lax.dot_general(gb, b.astype(BF16), (((1,), (1,)), ((), ())), preferred_element_type=F32)
    db = lax.dot_general(a.astype(BF16), gb, (((0,), (0,)), ((), ())), preferred_element_type=F32)
    return da.astype(a.dtype), db.astype(b.dtype)


_bdot.defvjp(lambda a, b: (_bdot(a, b), (a, b)), _bdot_bwd)


def _head_ones():
    i = lax.broadcasted_iota(jnp.int32, (LANES, LANES), 0) // HEAD_DIM
    j = lax.broadcasted_iota(jnp.int32, (LANES, LANES), 1) // HEAD_DIM
    return (i == j).astype(F32)


def _hdot(x, ones):
    return jnp.dot(x, ones, precision=HI, preferred_element_type=F32)


@jax.custom_vjp
def _segsum(x):
    ones = _head_ones()
    parts = [_hdot(x[:, LANES * j:LANES * (j + 1)], ones) for j in range(x.shape[1] // LANES)]
    return parts[0] if len(parts) == 1 else jnp.concatenate(parts, axis=1)


_segsum.defvjp(lambda x: (_segsum(x), None), lambda _, g: (_segsum(g),))


def _rms(x, gain):
    return x * lax.rsqrt(jnp.mean(x * x, axis=-1, keepdims=True) + RMS_EPS) * gain


def _matmul(name, a, b, *, ta=False, tb=False, b_shards=False, out_shards=False, out_dtype=F32,
            tm=512, tn=512, tk=512, n_outer=False):
    if ta:
        kdim, m = a.shape
    else:
        m, kdim = a.shape
    if b_shards:
        if tb:
            n, ks = b.shape[1], b.shape[2]
            assert ks * N_CHIPS == kdim
        else:
            ns = b.shape[2]
            n = ns * N_CHIPS
            assert b.shape[1] == kdim
    else:
        n = b.shape[0] if tb else b.shape[1]
    tm = _pick(m, (tm, 512, 256, 128))
    n_part = n // N_CHIPS if (out_shards or (b_shards and not tb)) else n
    tn = _pick(n_part, (tn, 512, 256, 128))
    k_part = kdim // N_CHIPS if (b_shards and tb) else kdim
    tk = _pick(k_part, (tk, 512, 256, 128))
    nb = n_part // tn
    kb = k_part // tk
    nk = kdim // tk
    grid = (n // tn, m // tm, nk) if n_outer else (m // tm, n // tn, nk)

    def spec(shape, index):
        if n_outer:
            return pl.BlockSpec(shape, lambda j, i, k: index(i, j, k))
        return pl.BlockSpec(shape, index)

    if ta:
        a_spec = spec((tk, tm), lambda i, j, k: (k, i))
    else:
        a_spec = spec((tm, tk), lambda i, j, k: (i, k))
    if b_shards and tb:
        b_spec = spec((None, tn, tk), lambda i, j, k: (k // kb, j, k % kb))
    elif b_shards:
        b_spec = spec((None, tk, tn), lambda i, j, k: (j // nb, k, j % nb))
    elif tb:
        b_spec = spec((tn, tk), lambda i, j, k: (j, k))
    else:
        b_spec = spec((tk, tn), lambda i, j, k: (k, j))
    if out_shards:
        o_spec = spec((None, tm, tn), lambda i, j, k: (j // nb, i, j % nb))
        o_shape = jax.ShapeDtypeStruct((N_CHIPS, m, n_part), out_dtype)
    else:
        o_spec = spec((tm, tn), lambda i, j, k: (i, j))
        o_shape = jax.ShapeDtypeStruct((m, n), out_dtype)
    dims = (((0 if ta else 1,), (1 if tb else 0,)), ((), ()))

    def body(a_ref, b_ref, o_ref, acc_ref):
        k = pl.program_id(2)

        @pl.when(k == 0)
        def _():
            acc_ref[...] = jnp.zeros_like(acc_ref)

        acc_ref[...] += lax.dot_general(a_ref[...].astype(BF16), b_ref[...].astype(BF16), dims,
                                        preferred_element_type=F32)

        @pl.when(k == nk - 1)
        def _():
            o_ref[...] = acc_ref[...].astype(o_ref.dtype)

    return pl.pallas_call(
        body, name=name, grid=grid, in_specs=[a_spec, b_spec], out_specs=o_spec, out_shape=o_shape,
        scratch_shapes=[pltpu.VMEM((tm, tn), F32)],
        compiler_params=_params(("parallel", "parallel", "arbitrary")),
    )(a, b)


def _row_in(spec, tile):
    if isinstance(spec, tuple):
        arr, width, cb = spec
    else:
        arr, width, cb = spec, spec.shape[1], 0
    return arr, pl.BlockSpec((tile, width), lambda i, cb=cb: (i, cb))


def _full_spec(arr):
    nd = arr.ndim
    return pl.BlockSpec(arr.shape, lambda i, nd=nd: (0,) * nd)


def _rowwise(name, fn, rows, params, outs, tile):
    t = (rows[0][0] if isinstance(rows[0], tuple) else rows[0]).shape[0]
    tile = _pick(t, (tile,))
    arrs, specs = zip(*[_row_in(s, tile) for s in rows])
    nr, npar = len(rows), len(params)

    def body(*refs):
        rv = [r[...].astype(F32) for r in refs[:nr]]
        pv = [p[...] for p in refs[nr:nr + npar]]
        res = fn(*rv, *pv)
        for o_ref, val in zip(refs[nr + npar:], res):
            o_ref[...] = val.astype(o_ref.dtype)

    return pl.pallas_call(
        body, name=name, grid=(t // tile,),
        in_specs=list(specs) + [_full_spec(p) for p in params],
        out_specs=[pl.BlockSpec((tile, w), lambda i: (i, 0)) for w, _ in outs],
        out_shape=[jax.ShapeDtypeStruct((t, w), d) for w, d in outs],
        compiler_params=_params(("parallel",)),
    )(*arrs, *params)


def _rowwise_bwd(name, fn, rows, params, cts, row_grads, param_grads, tile, add_to_first=()):
    t = (rows[0][0] if isinstance(rows[0], tuple) else rows[0]).shape[0]
    tile = _pick(t, (tile,))
    arrs, specs = zip(*[_row_in(s, tile) for s in rows])
    n_add = len(add_to_first)
    flat_cts = [c for group in cts for c in group] + list(add_to_first)
    c_arrs, c_specs = zip(*[_row_in(s, tile) for s in flat_cts])
    nr, npar, nc = len(rows), len(params), len(flat_cts)
    rg_idx = [i for i, d in enumerate(row_grads) if d is not None]
    pg_idx = [i for i, d in enumerate(param_grads) if d]

    def body(*refs):
        rv = [r[...].astype(F32) for r in refs[:nr]]
        pv = [p[...] for p in refs[nr:nr + npar]]
        cv = [c[...].astype(F32) for c in refs[nr + npar:nr + npar + nc]]
        o_refs = refs[nr + npar + nc:]
        outs, vjp = jax.vjp(fn, *rv, *pv)
        ct, pos = [], 0
        for group, o in zip(cts, outs):
            if group:
                acc = cv[pos]
                for extra in cv[pos + 1:pos + len(group)]:
                    acc = acc + extra
                pos += len(group)
            else:
                acc = jnp.zeros_like(o)
            ct.append(acc)
        grads = list(vjp(tuple(ct)))
        for extra in cv[nc - n_add:]:
            grads[rg_idx[0]] = grads[rg_idx[0]] + extra
        for o_ref, i in zip(o_refs[:len(rg_idx)], rg_idx):
            o_ref[...] = grads[i].astype(o_ref.dtype)
        first = pl.program_id(0) == 0
        for o_ref, i in zip(o_refs[len(rg_idx):], pg_idx):
            g = grads[nr + i].astype(F32)

            @pl.when(first)
            def _(o_ref=o_ref, g=g):
                o_ref[...] = g

            @pl.when(jnp.logical_not(first))
            def _(o_ref=o_ref, g=g):
                o_ref[...] += g

    def width(i):
        s = rows[i]
        return s[1] if isinstance(s, tuple) else s.shape[1]

    out_specs = [pl.BlockSpec((tile, width(i)), lambda i_: (i_, 0)) for i in rg_idx]
    out_shape = [jax.ShapeDtypeStruct((t, width(i)), row_grads[i]) for i in rg_idx]
    out_specs += [_full_spec(params[i]) for i in pg_idx]
    out_shape += [jax.ShapeDtypeStruct(params[i].shape, F32) for i in pg_idx]
    res = pl.pallas_call(
        body, name=name, grid=(t // tile,),
        in_specs=list(specs) + [_full_spec(p) for p in params] + list(c_specs),
        out_specs=out_specs, out_shape=out_shape,
        compiler_params=_params(("arbitrary",)),
    )(*arrs, *params, *c_arrs)
    return res[:len(rg_idx)], res[len(rg_idx):]


def _fn_norm1(x, gain, sc, sh):
    h = _rms(x, gain) * (1.0 + sc) + sh
    return h, h


def _fn_rwkv_pre(h, hp, p, pp, mu_rkv, mu_w, mu_a, mu_g, w0, a0, k_k, k_a, w1, w2, a1, a2, g1, g2):
    d = p.shape[1] // 3
    dh = hp - h
    xw = h + dh * mu_w
    xa = h + dh * mu_a
    xg = h + dh * mu_g
    pr = p + (pp - p) * mu_rkv
    r, k, v = pr[:, :d], pr[:, d:2 * d], pr[:, 2 * d:]
    w_log = -_softplus(-(w0 + _bdot(jnp.tanh(_bdot(xw, w1)), w2))) - 0.5
    decay = jnp.exp(-jnp.exp(w_log))
    a = _sigmoid(a0 + _bdot(_bdot(xa, a1), a2))
    g = _bdot(_sigmoid(_bdot(xg, g1)), g2)
    kk = k * k_k
    kk = kk * lax.rsqrt(_segsum(kk * kk) + L2_EPS)
    k2 = k * (1.0 + (a - 1.0) * k_a)
    return r, decay, k2, v, -kk, kk * a, g


def _fn_rwkv_post(y, r, k2, v, g, ln_g, ln_b, r_k):
    inv = 1.0 / HEAD_DIM
    mean = _segsum(y) * inv
    yc = y - mean
    var = _segsum(yc * yc) * inv
    yn = yc * lax.rsqrt(var + GN_EPS) * ln_g + ln_b
    bonus = _segsum(r * k2 * r_k) * v
    return ((yn + bonus) * g,)


def _fn_rwkv_post_cat(y, r, k2, v, g, o_sb, ln_g, ln_b, r_k):
    return (jnp.concatenate([_fn_rwkv_post(y, r, k2, v, g, ln_g, ln_b, r_k)[0], o_sb], axis=1),)


def _fn_qk_norm(p, qg, kg):
    d = p.shape[1] // 3
    q, k, v = p[:, :d], p[:, d:2 * d], p[:, 2 * d:]
    inv = 1.0 / HEAD_DIM
    qn = q * lax.rsqrt(_segsum(q * q) * inv + RMS_EPS) * qg
    kn = k * lax.rsqrt(_segsum(k * k) * inv + RMS_EPS) * kg
    return qn, kn, v


def _fn_mix_norm2(x, mix, gt1, gain, sc, sh):
    x1 = x + gt1 * mix
    h2 = _rms(x1, gain) * (1.0 + sc) + sh
    return x1, h2


def _fn_swiglu(gate, up):
    return (_silu(gate) * up,)


def _swiglu_bwd(gu, dact, tile=128):
    t, dff = dact.shape
    tile = _pick(t, (tile,))

    def body(gate_ref, up_ref, d_ref, o_ref):
        _, vjp = jax.vjp(_fn_swiglu, gate_ref[...], up_ref[...])
        dgate, dup = vjp((d_ref[...],))
        o_ref[:, :dff] = dgate.astype(o_ref.dtype)
        o_ref[:, dff:] = dup.astype(o_ref.dtype)

    return pl.pallas_call(
        body, name="swiglu_bwd", grid=(t // tile,),
        in_specs=[pl.BlockSpec((tile, dff), lambda i: (i, 0)), pl.BlockSpec((tile, dff), lambda i: (i, 1)),
                  pl.BlockSpec((tile, dff), lambda i: (i, 0))],
        out_specs=pl.BlockSpec((tile, 2 * dff), lambda i: (i, 0)),
        out_shape=jax.ShapeDtypeStruct((t, 2 * dff), BF16),
        compiler_params=_params(("parallel",)),
    )(gu, gu, dact)


def _loss_head(name, x1, dn, target, gt2, tile=256):
    t, d = x1.shape
    tile = _pick(t, (tile,))

    def body(x1_ref, dn_ref, tg_ref, gt_ref, loss_ref, dout_ref, ddn_ref, dgt_ref):
        dnv = dn_ref[...]
        gt = gt_ref[...]
        err = x1_ref[...] + gt * dnv - tg_ref[...]
        dout = err * (1.0 / d)
        dout_ref[...] = dout
        ddn_ref[...] = (dout * gt).astype(ddn_ref.dtype)
        part = 0.5 * jnp.sum(jnp.sum(err * dout, axis=-1, keepdims=True), axis=0, keepdims=True)
        dgt = jnp.sum(dout * dnv, axis=0, keepdims=True)
        first = pl.program_id(0) == 0

        @pl.when(first)
        def _():
            loss_ref[...] = jnp.broadcast_to(part, loss_ref.shape)
            dgt_ref[...] = dgt

        @pl.when(jnp.logical_not(first))
        def _():
            loss_ref[...] += jnp.broadcast_to(part, loss_ref.shape)
            dgt_ref[...] += dgt

    row = pl.BlockSpec((tile, d), lambda i: (i, 0))
    vec = pl.BlockSpec((1, d), lambda i: (0, 0))
    return pl.pallas_call(
        body, name=name, grid=(t // tile,),
        in_specs=[row, row, row, vec],
        out_specs=[pl.BlockSpec((1, LANES), lambda i: (0, 0)), row, row, vec],
        out_shape=[jax.ShapeDtypeStruct((1, LANES), F32), jax.ShapeDtypeStruct((t, d), F32),
                   jax.ShapeDtypeStruct((t, d), BF16), jax.ShapeDtypeStruct((1, d), F32)],
        compiler_params=_params(("arbitrary",)),
    )(x1, dn, target, gt2)


SCAN_BLOCK = 32
N_COL = 5
WIDE = 2 * LANES


def _wide_eye():
    i = lax.broadcasted_iota(jnp.int32, (HEAD_DIM, WIDE), 0)
    j = lax.broadcasted_iota(jnp.int32, (HEAD_DIM, WIDE), 1) % HEAD_DIM
    return (i == j).astype(BF16)


def _wide_ones():
    i = lax.broadcasted_iota(jnp.int32, (WIDE, WIDE), 0) // HEAD_DIM
    j = lax.broadcasted_iota(jnp.int32, (WIDE, WIDE), 1) // HEAD_DIM
    return (i == j).astype(BF16)


COL_PIECES = (1, 3, 2, 2, 1)


def _col_tiles(refs, i, nq, eye, ones_bf):
    pieces = [[], [], []]
    for ref, n_pieces in zip(refs, COL_PIECES):
        full = ref[pl.ds(i, 1), :]
        for q in range(nq):
            rest = full[:, q * WIDE:(q + 1) * WIDE]
            for level in range(n_pieces):
                part = rest.astype(BF16)
                pieces[level].append(part * eye)
                rest = rest - part.astype(F32)
    tile = nq * HEAD_DIM
    out = jnp.dot(jnp.concatenate(pieces[0] + pieces[1] + pieces[2], axis=0), ones_bf, preferred_element_type=F32)
    first, second, third = out[:5 * tile], out[5 * tile:8 * tile], out[8 * tile:]
    refined = first[tile:4 * tile] + second
    parts = [first[:tile], refined[:tile] + third, refined[tile:], first[4 * tile:]]
    return jnp.concatenate(parts, axis=0).reshape(N_COL * nq, HEAD_DIM, WIDE)


def _bf16_round(x):
    bits = lax.bitcast_convert_type(x, jnp.uint32)
    bits = (bits + jnp.uint32(0x7FFF) + ((bits >> 16) & jnp.uint32(1))) & jnp.uint32(0xFFFF0000)
    return lax.bitcast_convert_type(bits, F32)


def _pair_tile(tiles_ref, n, p, nq):
    return tiles_ref[n * nq + p // 2, :, (p % 2) * LANES:(p % 2 + 1) * LANES]


def _scan_fwd(rem, w, wr, k, r, v):
    t, dr = v.shape
    npair, nq = dr // LANES, dr // WIDE
    tb = _pick(t, (SCAN_BLOCK,))

    def body(rem_ref, w_ref, wr_ref, k_ref, r_ref, v_ref, y_ref, hist_ref, last_ref, s_ref, sb_ref,
             tiles_a, tiles_b):
        @pl.when(pl.program_id(0) == 0)
        def _():
            s_ref[...] = jnp.zeros_like(s_ref)
            sb_ref[...] = jnp.zeros_like(sb_ref)

        eye, ones_bf = _wide_eye(), _wide_ones()
        col_refs = (rem_ref, w_ref, wr_ref, k_ref, r_ref)

        def step(i, tiles_ref):
            v_full, y_rows = v_ref[pl.ds(i, 1), :], []
            for p in range(npair):
                s = s_ref[p]
                hist_ref[i, p] = s
                c_rem, c_w, c_wr, c_k, c_r = [_pair_tile(tiles_ref, n, p, nq) for n in range(N_COL)]
                sa = jnp.sum(sb_ref[p] * c_rem, axis=0, keepdims=True)
                s2 = s * c_w + c_wr * sa + c_k * v_full[:, p * LANES:(p + 1) * LANES]
                s2_b = _bf16_round(s2)
                y_rows.append(jnp.sum(s2_b * c_r, axis=0, keepdims=True))
                s_ref[p] = s2
                sb_ref[p] = s2_b
            y_ref[pl.ds(i, 1), :] = jnp.concatenate(y_rows, axis=1)

        tiles_a[...] = _col_tiles(col_refs, 0, nq, eye, ones_bf)

        def two_steps(m, carry):
            i = 2 * m
            tiles_b[...] = _col_tiles(col_refs, i + 1, nq, eye, ones_bf)
            step(i, tiles_a)
            tiles_a[...] = _col_tiles(col_refs, jnp.minimum(i + 2, tb - 1), nq, eye, ones_bf)
            step(i + 1, tiles_b)
            return carry

        lax.fori_loop(0, tb // 2, two_steps, 0)
        last_ref[...] = sb_ref[...]

    blk = pl.BlockSpec((tb, dr), lambda i: (i, 0))
    tiles = pltpu.VMEM((N_COL * nq, HEAD_DIM, WIDE), F32)
    state = pltpu.VMEM((npair, HEAD_DIM, LANES), F32)
    return pl.pallas_call(
        body, name="rwkv_scan_fwd", grid=(t // tb,),
        in_specs=[blk] * 6,
        out_specs=[blk, pl.BlockSpec((tb, npair, HEAD_DIM, LANES), lambda i: (i, 0, 0, 0)),
                   pl.BlockSpec((npair, HEAD_DIM, LANES), lambda i: (0, 0, 0))],
        out_shape=[jax.ShapeDtypeStruct((t, dr), F32), jax.ShapeDtypeStruct((t, npair, HEAD_DIM, LANES), F32),
                   jax.ShapeDtypeStruct((npair, HEAD_DIM, LANES), F32)],
        scratch_shapes=[state, state, tiles, tiles],
        compiler_params=_params(("arbitrary",)),
    )(rem, w, wr, k, r, v)


def _scan_bwd(rem, w, wr, k, r, v, hist, last, dy):
    t, dr = v.shape
    npair, nq = dr // LANES, dr // WIDE
    tb = _pick(t, (SCAN_BLOCK,))
    nblk = t // tb

    def body(rem_ref, w_ref, wr_ref, k_ref, r_ref, v_ref, hist_ref, last_ref, dy_ref,
             drem_ref, dw_ref, dwr_ref, dk_ref, dr_ref, dv_ref, ds_ref, next_ref, tiles_a, tiles_b):
        @pl.when(pl.program_id(0) == 0)
        def _():
            ds_ref[...] = jnp.zeros_like(ds_ref)
            next_ref[...] = last_ref[...]

        eye, ones_bf = _wide_eye(), _wide_ones()
        col_refs = (rem_ref, w_ref, wr_ref, k_ref, r_ref)
        out_refs = (drem_ref, dw_ref, dwr_ref, dk_ref, dr_ref)

        def step(i, tiles_ref):
            grads = [[None] * npair for _ in range(N_COL)]
            v_full, dy_full, dv_rows = v_ref[pl.ds(i, 1), :], dy_ref[pl.ds(i, 1), :], []
            for p in range(npair):
                lanes = slice(p * LANES, (p + 1) * LANES)
                s = hist_ref[i, p]
                c_rem, c_w, c_wr, c_k, c_r = [_pair_tile(tiles_ref, n, p, nq) for n in range(N_COL)]
                v_row, dy_row = v_full[:, lanes], dy_full[:, lanes]
                s_b = _bf16_round(s)
                s2_b = next_ref[p]
                next_ref[p] = s_b
                sa = jnp.sum(s_b * c_rem, axis=0, keepdims=True)
                d2 = ds_ref[p] + c_r * dy_row
                dsa = jnp.sum(d2 * c_wr, axis=0, keepdims=True)
                dv_rows.append(jnp.sum(d2 * c_k, axis=0, keepdims=True))
                ds_ref[p] = d2 * c_w + c_rem * dsa
                for n, tile in enumerate((s_b * dsa, d2 * s, d2 * sa, d2 * v_row, s2_b * dy_row)):
                    grads[n][p] = tile.astype(BF16)
            wide = [jnp.concatenate(grads[n][2 * q:2 * q + 2], axis=1) for n in range(N_COL) for q in range(nq)]
            sums = jnp.dot(jnp.concatenate(wide, axis=0), ones_bf, preferred_element_type=F32)
            sums = sums.reshape(N_COL * nq, HEAD_DIM, WIDE)
            rows = jnp.sum(sums * eye.astype(F32)[None], axis=1)
            dv_ref[pl.ds(i, 1), :] = jnp.concatenate(dv_rows, axis=1)
            for n, o_ref in enumerate(out_refs):
                o_ref[pl.ds(i, 1), :] = jnp.concatenate([rows[n * nq + q:n * nq + q + 1] for q in range(nq)], axis=1)

        tiles_a[...] = _col_tiles(col_refs, tb - 1, nq, eye, ones_bf)

        def two_steps(m, carry):
            i = tb - 1 - 2 * m
            tiles_b[...] = _col_tiles(col_refs, i - 1, nq, eye, ones_bf)
            step(i, tiles_a)
            tiles_a[...] = _col_tiles(col_refs, jnp.maximum(i - 2, 0), nq, eye, ones_bf)
            step(i - 1, tiles_b)
            return carry

        lax.fori_loop(0, tb // 2, two_steps, 0)

    blk = pl.BlockSpec((tb, dr), lambda i: (nblk - 1 - i, 0))
    tiles = pltpu.VMEM((N_COL * nq, HEAD_DIM, WIDE), F32)
    state = pltpu.VMEM((npair, HEAD_DIM, LANES), F32)
    return pl.pallas_call(
        body, name="rwkv_scan_bwd", grid=(nblk,),
        in_specs=[blk] * 6 + [pl.BlockSpec((tb, npair, HEAD_DIM, LANES), lambda i: (nblk - 1 - i, 0, 0, 0)),
                              pl.BlockSpec((npair, HEAD_DIM, LANES), lambda i: (0, 0, 0)), blk],
        out_specs=[blk] * 6,
        out_shape=[jax.ShapeDtypeStruct((t, dr), F32)] * 6,
        scratch_shapes=[state, state, tiles, tiles],
        compiler_params=_params(("arbitrary",)),
    )(rem, w, wr, k, r, v, hist, last, dy)


SB_BLOCK = 256
SB_HEADS = 8
SB_HEADS_BWD = 4
NT_DIMS = (((1,), (1,)), ((), ()))
TN_DIMS = (((0,), (0,)), ((), ()))
SB_SCALE = 1.0 / math.sqrt(HEAD_DIM)


def _dot2(x, tri):
    hi = x.astype(BF16)
    mid = (x - hi.astype(F32)).astype(BF16)
    return jnp.dot(hi, tri, preferred_element_type=F32) + jnp.dot(mid, tri, preferred_element_type=F32)


def _sb_block_iotas(bs):
    return lax.broadcasted_iota(jnp.int32, (bs, bs), 0), lax.broadcasted_iota(jnp.int32, (bs, bs), 1)


def _sb_fwd(q, k, v):
    h, t, d = q.shape
    bs = _pick(t, (SB_BLOCK,))
    nh = _pick(h, (SB_HEADS,))

    def body(q_ref, k_ref, v_ref, o_ref, l_ref):
        qi = pl.program_id(1)
        ri, ci = _sb_block_iotas(bs)
        tri_ge = (ri >= ci).astype(BF16)
        causal = ci < ri
        qv = [q_ref[hh] for hh in range(nh)]

        def blocks(j, masked, carry):
            accs, tails = carry[:nh], carry[nh:]
            heads = range(nh)
            rows = pl.ds(pl.multiple_of(j * bs, bs), bs)
            z = [lax.dot_general(qv[hh], k_ref[hh, rows, :], NT_DIMS, preferred_element_type=F32) * SB_SCALE
                 for hh in heads]
            log1m = [-_softplus(z[hh]) for hh in heads]
            if masked:
                log1m = [jnp.where(causal, x, 0.0) for x in log1m]
            cs = [_dot2(log1m[hh], tri_ge) for hh in heads]
            a = [jnp.exp(z[hh] + cs[hh] + tails[hh]) for hh in heads]
            if masked:
                a = [jnp.where(causal, x, 0.0) for x in a]
            accs = [accs[hh] + jnp.dot(a[hh].astype(BF16), v_ref[hh, rows, :], preferred_element_type=F32)
                    for hh in heads]
            return tuple(accs) + tuple(tails[hh] + cs[hh][:, 0:1] for hh in heads)

        carry = blocks(qi, True, (jnp.zeros((bs, d), F32),) * nh + (jnp.zeros((bs, 1), F32),) * nh)
        carry = lax.fori_loop(0, qi, lambda n, c: blocks(qi - 1 - n, False, c), carry)
        for hh in range(nh):
            o_ref[hh] = carry[hh]
            l_ref[hh] = jnp.broadcast_to(carry[nh + hh], (bs, d))

    qs = pl.BlockSpec((nh, bs, d), lambda hh, i: (hh, i, 0))
    ks = pl.BlockSpec((nh, t, d), lambda hh, i: (hh, 0, 0), pipeline_mode=pl.Buffered(1))
    return pl.pallas_call(
        body, name="sb_attn_fwd", grid=(h // nh, t // bs),
        in_specs=[qs, ks, ks], out_specs=[qs, qs],
        out_shape=[jax.ShapeDtypeStruct((h, t, d), F32)] * 2,
        compiler_params=_params(("parallel", "arbitrary")),
    )(q, k, v)


def _sb_bwd(q, k, v, lsum, do):
    h, t, d = q.shape
    bs = _pick(t, (SB_BLOCK,))
    nh = _pick(h, (SB_HEADS_BWD,))

    def body(q_ref, k_ref, v_ref, l_ref, do_ref, dq_ref, dk_ref, dv_ref):
        qi = pl.program_id(1)

        @pl.when(qi == 0)
        def _():
            dk_ref[...] = jnp.zeros_like(dk_ref)
            dv_ref[...] = jnp.zeros_like(dv_ref)

        ri, ci = _sb_block_iotas(bs)
        tri_lt = (ri < ci).astype(BF16)
        causal = ci < ri
        qv = [q_ref[hh] for hh in range(nh)]
        dob = [do_ref[hh].astype(BF16) for hh in range(nh)]
        ltot = [l_ref[hh][:, 0:1] for hh in range(nh)]

        def blocks(j, masked, carry):
            dq, pc, ec = carry[:nh], carry[nh:2 * nh], carry[2 * nh:]
            heads = range(nh)
            rows = pl.ds(pl.multiple_of(j * bs, bs), bs)
            z = [lax.dot_general(qv[hh], k_ref[hh, rows, :], NT_DIMS, preferred_element_type=F32) * SB_SCALE
                 for hh in heads]
            da = [lax.dot_general(dob[hh], v_ref[hh, rows, :], NT_DIMS, preferred_element_type=F32) for hh in heads]
            nsp = [-_softplus(z[hh]) for hh in heads]
            log1m = [jnp.where(causal, x, 0.0) for x in nsp] if masked else nsp
            below = [_dot2(log1m[hh], tri_lt) + pc[hh] for hh in heads]
            a = [jnp.exp(z[hh] + (ltot[hh] - below[hh])) for hh in heads]
            if masked:
                a = [jnp.where(causal, x, 0.0) for x in a]
            e = [a[hh] * da[hh] for hh in heads]
            ebelow = [_dot2(e[hh], tri_lt) + ec[hh] for hh in heads]
            dz = [e[hh] * jnp.exp(nsp[hh]) - jnp.exp(z[hh] + nsp[hh]) * ebelow[hh] for hh in heads]
            if masked:
                dz = [jnp.where(causal, x, 0.0) for x in dz]
            dzb = [(x * SB_SCALE).astype(BF16) for x in dz]
            for hh in heads:
                dv_ref[hh, rows, :] += lax.dot_general(a[hh].astype(BF16), dob[hh], TN_DIMS,
                                                       preferred_element_type=F32)
            for hh in heads:
                dk_ref[hh, rows, :] += lax.dot_general(dzb[hh], qv[hh], TN_DIMS, preferred_element_type=F32)
            dq = [dq[hh] + jnp.dot(dzb[hh], k_ref[hh, rows, :], preferred_element_type=F32) for hh in heads]
            pc = [pc[hh] + jnp.sum(log1m[hh], axis=1, keepdims=True) for hh in heads]
            ec = [ec[hh] + jnp.sum(e[hh], axis=1, keepdims=True) for hh in heads]
            return tuple(dq) + tuple(pc) + tuple(ec)

        zcol = jnp.zeros((bs, 1), F32)
        carry = lax.fori_loop(0, qi, lambda j, c: blocks(j, False, c),
                              (jnp.zeros((bs, d), F32),) * nh + (zcol,) * (2 * nh))
        carry = blocks(qi, True, carry)
        for hh in range(nh):
            dq_ref[hh] = carry[hh]

    qs = pl.BlockSpec((nh, bs, d), lambda hh, i: (hh, i, 0))
    ks = pl.BlockSpec((nh, t, d), lambda hh, i: (hh, 0, 0), pipeline_mode=pl.Buffered(1))
    return pl.pallas_call(
        body, name="sb_attn_bwd", grid=(h // nh, t // bs),
        in_specs=[qs, ks, ks, qs, qs], out_specs=[qs, ks, ks],
        out_shape=[jax.ShapeDtypeStruct((h, t, d), F32)] * 3,
        compiler_params=_params(("parallel", "arbitrary")),
    )(q, k, v, lsum, do)


ANY = pl.BlockSpec(memory_space=pl.ANY)
IN_VMEM = pl.BlockSpec(memory_space=pltpu.VMEM)


def _coords():
    return lax.axis_index("x"), lax.axis_index("y"), lax.axis_index("c")


def _flip(v, bit):
    return 1 - v if bit else v


def _remote(src, dst, send_sem, recv_sem, device):
    return pltpu.make_async_remote_copy(src_ref=src, dst_ref=dst, send_sem=send_sem, recv_sem=recv_sem,
                                        device_id=device, device_id_type=MESH)


def _all_gather8(name, blk):
    m, n = blk.shape

    def body(x_ref, o_ref, send_sems, recv_sems, local_sem):
        x, y, c = _coords()
        own = pltpu.make_async_copy(x_ref, o_ref.at[4 * x + 2 * y + c], local_sem)
        own.start()
        peers = []
        for bits in range(1, N_DEV):
            px, py, pc = _flip(x, (bits >> 2) & 1), _flip(y, (bits >> 1) & 1), _flip(c, bits & 1)
            peers.append((px, py, pc))
        sends = []
        for k, peer in enumerate(peers):
            cp = _remote(x_ref, o_ref.at[4 * x + 2 * y + c], send_sems.at[k], recv_sems.at[k], peer)
            cp.start()
            sends.append(cp)
        for k, (px, py, pc) in enumerate(peers):
            slot = o_ref.at[4 * px + 2 * py + pc]
            _remote(slot, slot, send_sems.at[k], recv_sems.at[k], (px, py, pc)).wait_recv()
        for cp in sends:
            cp.wait_send()
        own.wait()

    return pl.pallas_call(
        body, name=name, out_shape=jax.ShapeDtypeStruct((N_DEV, m, n), blk.dtype),
        in_specs=[IN_VMEM], out_specs=IN_VMEM,
        scratch_shapes=[pltpu.SemaphoreType.DMA((N_DEV - 1,)), pltpu.SemaphoreType.DMA((N_DEV - 1,)),
                        pltpu.SemaphoreType.DMA],
        compiler_params=_params(),
    )(blk)


def _gather_weights(shards):
    nw = len(shards)

    def body(*refs):
        ins, outs = refs[:nw], refs[nw:2 * nw]
        ici_send, ici_recv, d2d_send, d2d_recv = refs[2 * nw:]
        x, y, c = _coords()
        chip = 2 * x + y
        sibling = (x, y, 1 - c)
        started = []
        for w in range(nw):
            half = shards[w].shape[0] // 2
            mine = pl.ds(c * half, half)
            for j, (a, b) in enumerate(XY_MASKS):
                cp = _remote(ins[w].at[mine], outs[w].at[chip, mine], ici_send.at[3 * w + j],
                             ici_recv.at[3 * w + j], (_flip(x, a), _flip(y, b), c))
                cp.start()
                started.append(cp)
        for w in range(nw):
            half = shards[w].shape[0] // 2
            mine = pl.ds(c * half, half)
            for j, (a, b) in enumerate(XY_MASKS):
                px, py = _flip(x, a), _flip(y, b)
                landed = outs[w].at[2 * px + py, mine]
                _remote(landed, landed, ici_send.at[3 * w + j], ici_recv.at[3 * w + j], (px, py, c)).wait_recv()
                cp = _remote(landed, landed, d2d_send.at[3 * w + j], d2d_recv.at[3 * w + j], sibling)
                cp.start()
                started.append(cp)
        for w in range(nw):
            half = shards[w].shape[0] // 2
            theirs = pl.ds((1 - c) * half, half)
            for j, (a, b) in enumerate(XY_MASKS):
                slot = outs[w].at[2 * _flip(x, a) + _flip(y, b), theirs]
                _remote(slot, slot, d2d_send.at[3 * w + j], d2d_recv.at[3 * w + j], sibling).wait_recv()
        for cp in started:
            cp.wait_send()

    return pl.pallas_call(
        body, name="gather_weights",
        out_shape=[jax.ShapeDtypeStruct((N_CHIPS,) + s.shape, s.dtype) for s in shards],
        in_specs=[ANY] * nw, out_specs=[ANY] * nw,
        scratch_shapes=[pltpu.SemaphoreType.DMA((3 * nw,))] * 4,
        compiler_params=_params(),
    )(*shards)


def _pair_exchange(grads):
    nw = len(grads)

    def body(*refs):
        ins, outs = refs[:nw], refs[nw:2 * nw]
        send_sems, recv_sems = refs[2 * nw:]
        x, y, c = _coords()
        sibling = (x, y, 1 - c)
        sends = []
        for w in range(nw):
            half = grads[w].shape[1] // 2
            cp = _remote(ins[w].at[:, pl.ds((1 - c) * half, half)], outs[w], send_sems.at[w], recv_sems.at[w], sibling)
            cp.start()
            sends.append(cp)
        for w in range(nw):
            _remote(outs[w], outs[w], send_sems.at[w], recv_sems.at[w], sibling).wait_recv()
        for cp in sends:
            cp.wait_send()

    return pl.pallas_call(
        body, name="grad_pair_exchange",
        out_shape=[jax.ShapeDtypeStruct((N_CHIPS, g.shape[1] // 2, g.shape[2]), g.dtype) for g in grads],
        in_specs=[ANY] * nw, out_specs=[ANY] * nw,
        scratch_shapes=[pltpu.SemaphoreType.DMA((nw,))] * 2,
        compiler_params=_params(),
    )(*grads)


def _chip_exchange(parts):
    nw = len(parts)

    def body(*refs):
        ins, outs = refs[:nw], refs[nw:2 * nw]
        send_sems, recv_sems = refs[2 * nw:]
        x, y, c = _coords()
        chip = 2 * x + y
        sends = []
        for w in range(nw):
            for j, (a, b) in enumerate(XY_MASKS):
                px, py = _flip(x, a), _flip(y, b)
                cp = _remote(ins[w].at[2 * px + py], outs[w].at[chip], send_sems.at[3 * w + j],
                             recv_sems.at[3 * w + j], (px, py, c))
                cp.start()
                sends.append(cp)
        for w in range(nw):
            for j, (a, b) in enumerate(XY_MASKS):
                px, py = _flip(x, a), _flip(y, b)
                slot = outs[w].at[2 * px + py]
                _remote(slot, slot, send_sems.at[3 * w + j], recv_sems.at[3 * w + j], (px, py, c)).wait_recv()
        for cp in sends:
            cp.wait_send()

    return pl.pallas_call(
        body, name="grad_chip_exchange",
        out_shape=[jax.ShapeDtypeStruct(p.shape, p.dtype) for p in parts],
        in_specs=[ANY] * nw, out_specs=[ANY] * nw,
        scratch_shapes=[pltpu.SemaphoreType.DMA((3 * nw,))] * 2,
        compiler_params=_params(),
    )(*parts)


def _pair_share(shards):
    nw = len(shards)

    def body(*refs):
        bufs = refs[nw:2 * nw]
        send_sems, recv_sems = refs[2 * nw:]
        x, y, c = _coords()
        sibling = (x, y, 1 - c)
        sends = []
        for w in range(nw):
            half = shards[w].shape[0] // 2
            mine = bufs[w].at[pl.ds(c * half, half)]
            cp = _remote(mine, mine, send_sems.at[w], recv_sems.at[w], sibling)
            cp.start()
            sends.append(cp)
        for w in range(nw):
            half = shards[w].shape[0] // 2
            theirs = bufs[w].at[pl.ds((1 - c) * half, half)]
            _remote(theirs, theirs, send_sems.at[w], recv_sems.at[w], sibling).wait_recv()
        for cp in sends:
            cp.wait_send()

    return pl.pallas_call(
        body, name="grad_pair_share",
        out_shape=[jax.ShapeDtypeStruct(s.shape, s.dtype) for s in shards],
        in_specs=[ANY] * nw, out_specs=[ANY] * nw,
        input_output_aliases={w: w for w in range(nw)},
        scratch_shapes=[pltpu.SemaphoreType.DMA((nw,))] * 2,
        compiler_params=_params(),
    )(*shards)


TILE_BYTES = 2 * 1024 * 1024


def _row_tile(rows, cols, mult=8):
    best = None
    for tr in range(mult, rows + 1, mult):
        if rows % tr == 0 and tr * cols * 4 <= TILE_BYTES:
            best = tr
    return best if best is not None else rows


def _pair_add(name, grad, other, place):
    _, half, cols = other.shape
    tr = _row_tile(half, cols, mult=16)
    nb = half // tr

    def body(place_ref, g_ref, o_ref, out_ref):
        out_ref[...] = (g_ref[...] + o_ref[...]).astype(out_ref.dtype)

    return pl.pallas_call(
        body, name=name, out_shape=jax.ShapeDtypeStruct(other.shape, BF16),
        grid_spec=pltpu.PrefetchScalarGridSpec(
            num_scalar_prefetch=1, grid=(N_CHIPS, nb),
            in_specs=[pl.BlockSpec((None, tr, cols), lambda s, i, place_ref: (s, place_ref[0] * nb + i, 0)),
                      pl.BlockSpec((None, tr, cols), lambda s, i, place_ref: (s, i, 0))],
            out_specs=pl.BlockSpec((None, tr, cols), lambda s, i, place_ref: (s, i, 0))),
        compiler_params=_params(("parallel", "parallel")),
    )(place, grad, other)


def _sum_chips(name, own, others, place):
    _, half, cols = own.shape
    tr = _row_tile(half, cols, mult=16)
    nb = half // tr

    def body(place_ref, own_ref, a_ref, b_ref, c_ref, out_ref):
        total = own_ref[...].astype(F32) + a_ref[...].astype(F32)
        out_ref[...] = (total + b_ref[...].astype(F32)) + c_ref[...].astype(F32)

    def peer(mask):
        return pl.BlockSpec((None, tr, cols), lambda i, place_ref: (place_ref[1] ^ mask, i, 0))

    return pl.pallas_call(
        body, name=name, out_shape=jax.ShapeDtypeStruct((2 * half, cols), F32),
        grid_spec=pltpu.PrefetchScalarGridSpec(
            num_scalar_prefetch=1, grid=(nb,),
            in_specs=[peer(0), peer(1), peer(2), peer(3)],
            out_specs=pl.BlockSpec((tr, cols), lambda i, place_ref: (place_ref[0] * nb + i, 0))),
        compiler_params=_params(("parallel",)),
    )(place, own, others, others, others)


def _adamw_math(w, g, m, v):
    m2 = ADAM_B1 * m + (1.0 - ADAM_B1) * g
    v2 = ADAM_B2 * v + (1.0 - ADAM_B2) * (g * g)
    m_hat = m2 / (1.0 - ADAM_B1 ** ADAM_STEP)
    v_hat = v2 / (1.0 - ADAM_B2 ** ADAM_STEP)
    delta = -ADAM_LR * (m_hat / (jnp.sqrt(v_hat) + ADAM_EPS) + ADAM_WD * w)
    return delta, m2, v2


def _adamw(name, w, g, m, v):
    rows, cols = w.shape
    tr = _row_tile(rows, cols)

    def body(w_ref, g_ref, m_ref, v_ref, d_ref, m2_ref, v2_ref):
        d_ref[...], m2_ref[...], v2_ref[...] = _adamw_math(w_ref[...], g_ref[...], m_ref[...], v_ref[...])

    blk = pl.BlockSpec((tr, cols), lambda i: (i, 0))
    return pl.pallas_call(
        body, name=name, out_shape=[jax.ShapeDtypeStruct(w.shape, F32)] * 3, grid=(rows // tr,),
        in_specs=[blk] * 4, out_specs=[blk] * 3,
        compiler_params=_params(("parallel",)),
    )(w, g, m, v)


def _small_update(gathered, w, m, v):
    def body(gs_ref, w_ref, m_ref, v_ref, g_ref, d_ref, m2_ref, v2_ref):
        g = gs_ref[0]
        for dev in range(1, N_DEV):
            g = g + gs_ref[dev]
        g_ref[...] = g
        d_ref[...], m2_ref[...], v2_ref[...] = _adamw_math(w_ref[...], g, m_ref[...], v_ref[...])

    return pl.pallas_call(
        body, name="small_update", out_shape=[jax.ShapeDtypeStruct(w.shape, F32)] * 4,
        compiler_params=_params(),
    )(gathered, w, m, v)


def _ada_mod(c_all, w_shard, b_shard):
    d, n = w_shard.shape
    tn = _pick(n, (512, 256, 128))

    def body(c_ref, w_ref, b_ref, o_ref):
        act = _silu(c_ref[...]).astype(BF16)
        o_ref[...] = jnp.dot(act, w_ref[...].astype(BF16), preferred_element_type=F32) + b_ref[...]

    return pl.pallas_call(
        body, name="ada_mod", out_shape=jax.ShapeDtypeStruct((c_all.shape[0], n), F32), grid=(n // tn,),
        in_specs=[pl.BlockSpec(c_all.shape, lambda j: (0, 0)), pl.BlockSpec((d, tn), lambda j: (0, j)),
                  pl.BlockSpec((1, tn), lambda j: (0, j))],
        out_specs=pl.BlockSpec((c_all.shape[0], tn), lambda j: (0, j)),
        compiler_params=_params(("parallel",)),
    )(c_all, w_shard, b_shard)


def _ada_grad(c_pad, dmod_pad):
    rows, d = c_pad.shape
    n = dmod_pad.shape[1]
    tn = _pick(n, (512, 256, 128))

    def body(c_ref, g_ref, o_ref):
        act = _silu(c_ref[...]).astype(BF16)
        o_ref[...] = lax.dot_general(act, g_ref[...].astype(BF16), TN_DIMS, preferred_element_type=F32)

    return pl.pallas_call(
        body, name="ada_grad", out_shape=jax.ShapeDtypeStruct((d, n), F32), grid=(n // tn,),
        in_specs=[pl.BlockSpec((rows, d), lambda j: (0, 0)), pl.BlockSpec((rows, tn), lambda j: (0, j))],
        out_specs=pl.BlockSpec((d, tn), lambda j: (0, j)),
        compiler_params=_params(("parallel",)),
    )(c_pad, dmod_pad)


WEIGHTS = ['w_ada', 'b_ada', 'norm1_gain', 'norm2_gain', 'w_in', 'mu_rkv', 'mu_w', 'mu_a', 'mu_g', 'w0', 'w1',
           'w2', 'a0', 'a1', 'a2', 'g1', 'g2', 'k_k', 'k_a', 'r_k', 'ln_x_gain', 'ln_x_bias', 'q_norm_gain',
           'k_norm_gain', 'w_out', 'w_gate_up', 'w_down']
SMALL = ['b_ada', 'norm1_gain', 'norm2_gain', 'mu_rkv', 'mu_w', 'mu_a', 'mu_g', 'w0', 'a0', 'k_k', 'k_a', 'r_k',
         'ln_x_gain', 'ln_x_bias', 'q_norm_gain', 'k_norm_gain']
PACK_ROWS = 8
LOSS_SLOT = LANES


def _shift_down(a):
    return jnp.pad(a[:-1], ((1, 0), (0, 0)))


def _shift_up(a):
    return jnp.pad(a[1:], ((0, 1), (0, 0)))


def _pack_small(vals):
    flat = jnp.concatenate([v.reshape(1, -1) for v in vals], axis=1)
    unit = PACK_ROWS * LANES
    total = -(-flat.shape[1] // unit) * unit
    flat = jnp.pad(flat, ((0, 0), (0, total - flat.shape[1])))
    return flat.reshape(PACK_ROWS, total // PACK_ROWS)


def kernel(x, c, w_ada, b_ada, norm1_gain, norm2_gain, w_in, mu_rkv, mu_w, mu_a, mu_g, w0, w1, w2, a0, a1, a2, g1, g2, k_k, k_a, r_k, ln_x_gain, ln_x_bias, q_norm_gain, k_norm_gain, w_out, w_gate_up, w_down, loss_target, m_w_ada, m_b_ada, m_norm1_gain, m_norm2_gain, m_w_in, m_mu_rkv, m_mu_w, m_mu_a, m_mu_g, m_w0, m_w1, m_w2, m_a0, m_a1, m_a2, m_g1, m_g2, m_k_k, m_k_a, m_r_k, m_ln_x_gain, m_ln_x_bias, m_q_norm_gain, m_k_norm_gain, m_w_out, m_w_gate_up, m_w_down, v_w_ada, v_b_ada, v_norm1_gain, v_norm2_gain, v_w_in, v_mu_rkv, v_mu_w, v_mu_a, v_mu_g, v_w0, v_w1, v_w2, v_a0, v_a1, v_a2, v_g1, v_g2, v_k_k, v_k_a, v_r_k, v_ln_x_gain, v_ln_x_bias, v_q_norm_gain, v_k_norm_gain, v_w_out, v_w_gate_up, v_w_down):
    given = dict(locals())
    wt = {n: given[n][0] for n in WEIGHTS}
    mom = {n: given["m_" + n][0] for n in WEIGHTS}
    var = {n: given["v_" + n][0] for n in WEIGHTS}
    for tree in (wt, mom, var):
        tree["b_ada"] = tree["b_ada"].reshape(1, -1)
        for n in SMALL[1:]:
            tree[n] = tree[n].reshape(1, -1)

    ax, ay, ac = _coords()
    chip = 2 * ax + ay
    dev = 4 * ax + 2 * ay + ac
    xs, target = x[0], loss_target[0]
    t, d = xs.shape
    dr = wt["w0"].shape[1]
    ds = d - dr
    nh = ds // HEAD_DIM
    dff = wt["w_down"].shape[0] * N_CHIPS
    n_ada = wt["w_ada"].shape[1]
    lw, la, lg = wt["w1"].shape[1], wt["a1"].shape[1], wt["g1"].shape[1]

    def lora_a(tree):
        return jnp.concatenate([tree["w1"], tree["a1"], tree["g1"]], axis=1)

    def lora_b(tree):
        return jnp.concatenate([tree["w2"], tree["a2"], tree["g2"]], axis=0)

    shards = [wt["w_in"], wt["w_out"], wt["w_gate_up"], wt["w_down"], lora_a(wt), lora_b(wt)]
    shards = [s.astype(BF16) for s in shards]
    gathered_w = [lax.dynamic_update_slice(full, own[None], (chip, 0, 0))
                  for full, own in zip(_gather_weights(shards), shards)]
    full_in, full_out, full_gu, full_down, full_la, full_lb = gathered_w
    full_out = full_out.reshape(d, d)
    full_down = full_down.reshape(dff, d)
    full_la = full_la.reshape(d, lw + la + lg).astype(F32)
    full_lb = full_lb.transpose(1, 0, 2).reshape(lw + la + lg, dr).astype(F32)
    w1f, a1f, g1f = full_la[:, :lw], full_la[:, lw:lw + la], full_la[:, lw + la:]
    w2f, a2f, g2f = full_lb[:lw], full_lb[lw:lw + la], full_lb[lw + la:]

    c_all = _all_gather8("gather_c", c.reshape(PACK_ROWS, d // PACK_ROWS)).reshape(N_DEV, d)
    b_shard = lax.dynamic_slice(wt["b_ada"], (0, chip * n_ada), (1, n_ada))
    mod_part = _ada_mod(c_all, wt["w_ada"], b_shard)
    mod_all = _all_gather8("gather_mod", mod_part)[::2]
    mod = lax.dynamic_slice(mod_all, (0, dev, 0), (N_CHIPS, 1, n_ada)).reshape(1, N_CHIPS * n_ada)
    sh1, sc1, gt1, sh2, sc2, gt2 = [mod[:, i * d:(i + 1) * d] for i in range(6)]

    h, h_bf = _rowwise("norm1", _fn_norm1, [xs], [wt["norm1_gain"], sc1, sh1], [(d, F32), (d, BF16)], 256)
    hp = _shift_down(h)
    p = _matmul("mm_in", h_bf, full_in, b_shards=True, tm=512, tn=1536, tk=2048, n_outer=True)
    p_rkv, p_sb = (p, 3 * dr, 0), (p, 3 * ds, 1)
    pp = _shift_down(p[:, :3 * dr])
    pre_rows = [h, hp, p_rkv, pp]
    pre_params = [wt["mu_rkv"], wt["mu_w"], wt["mu_a"], wt["mu_g"], wt["w0"], wt["a0"], wt["k_k"], wt["k_a"],
                  w1f, w2f, a1f, a2f, g1f, g2f]
    pre = _rowwise("rwkv_pre", _fn_rwkv_pre, pre_rows, pre_params, [(dr, F32)] * 7, 64)
    r_, w_, k2, v_, rem, wr, g_ = pre
    y_raw, hist, s_last = _scan_fwd(rem, w_, wr, k2, r_, v_)
    post_rows = [y_raw, r_, k2, v_, g_]
    post_params = [wt["ln_x_gain"], wt["ln_x_bias"], wt["r_k"]]

    qg = jnp.tile(wt["q_norm_gain"], (1, nh))
    kg = jnp.tile(wt["k_norm_gain"], (1, nh))
    qn, kn, vs = _rowwise("qk_norm", _fn_qk_norm, [p_sb], [qg, kg], [(ds, BF16)] * 3, 256)

    def to_heads(a):
        return a.reshape(t, nh, HEAD_DIM).transpose(1, 0, 2)

    def from_heads(a):
        return a.transpose(1, 0, 2).reshape(t, ds)

    qh, kh, vh = to_heads(qn), to_heads(kn), to_heads(vs)
    o_h, lsum = _sb_fwd(qh, kh, vh)
    (ycat,) = _rowwise("rwkv_post", _fn_rwkv_post_cat, post_rows + [from_heads(o_h)], post_params, [(d, BF16)], 256)
    mix = _matmul("mm_out", ycat, full_out, tm=512, tn=1024, tk=2048, n_outer=True)
    norm2_params = [gt1, wt["norm2_gain"], sc2, sh2]
    x1, h2 = _rowwise("mix_norm2", _fn_mix_norm2, [xs, mix], norm2_params, [(d, F32), (d, BF16)], 256)
    gu = _matmul("mm_gate_up", h2, full_gu, b_shards=True, tm=512, tn=1408, tk=2048, n_outer=True)
    gate_up = [(gu, dff, 0), (gu, dff, 1)]
    (act,) = _rowwise("swiglu", _fn_swiglu, gate_up, [], [(dff, BF16)], 256)
    dn = _matmul("mm_down", act, full_down, tm=1024, tn=1024, tk=1408)
    loss_vec, dout, ddn, dgt2 = _loss_head("loss_head", x1, dn, target, gt2)

    dact = _matmul("mm_down_dx", ddn, full_down, tb=True, tm=512, tn=1408, tk=2048, n_outer=True)
    gw_down = _matmul("mm_down_dw", act, ddn, ta=True, tm=1408, tn=1024, tk=512)
    dgu = _swiglu_bwd(gu, dact)
    dh2 = _matmul("mm_gate_up_dx", dgu, full_gu, tb=True, b_shards=True, tm=1024, tn=1024, tk=1408)
    gw_gu = _matmul("mm_gate_up_dw", h2, dgu, ta=True, out_shards=True, tm=1024, tn=1408, tk=512)
    (dx_a, dmix), (dgt1, dgain2, dsc2, dsh2) = _rowwise_bwd(
        "mix_norm2_bwd", _fn_mix_norm2, [xs, mix], norm2_params, [[dout], [dh2]], [F32, BF16], [True] * 4, 128)
    dycat = _matmul("mm_out_dx", dmix, full_out, tb=True, tm=512, tn=1024, tk=2048, n_outer=True)
    gw_out = _matmul("mm_out_dw", ycat, dmix, ta=True, tm=1024, tn=1024, tk=512)
    (dy_raw, dr_f, dk_f, dv_f, dg), (dlng, dlnb, drk) = _rowwise_bwd(
        "rwkv_post_bwd", _fn_rwkv_post, post_rows, post_params, [[(dycat, dr, 0)]], [F32] * 5, [True] * 3, 128)
    drem_s, dw_s, dwr_s, dk_s, dr_s, dv_s = _scan_bwd(rem, w_, wr, k2, r_, v_, hist, s_last, dy_raw)
    do_h = to_heads(dycat[:, dr:])
    dqh, dkh, dvh = _sb_bwd(qh, kh, vh, lsum, do_h)
    (dp_sb,), (dqg, dkg) = _rowwise_bwd(
        "qk_norm_bwd", _fn_qk_norm, [p_sb], [qg, kg], [[from_heads(dqh)], [from_heads(dkh)], [from_heads(dvh)]],
        [F32], [True, True], 128)
    pre_cts = [[dr_s, dr_f], [dw_s], [dk_s, dk_f], [dv_s, dv_f], [drem_s], [dwr_s], [dg]]
    (dh_a, dhp, dp_rkv, dpp), pre_g = _rowwise_bwd(
        "rwkv_pre_bwd", _fn_rwkv_pre, pre_rows, pre_params, pre_cts, [F32] * 4, [True] * 14, 64)
    dp = jnp.concatenate([dp_rkv + _shift_up(dpp), dp_sb], axis=1).astype(BF16)
    dh_mm = _matmul("mm_in_dx", dp, full_in, tb=True, b_shards=True, tm=1024, tn=1024, tk=1536)
    gw_in = _matmul("mm_in_dw", h_bf, dp, ta=True, out_shards=True, tm=1024, tn=1536, tk=512)
    (grad_x,), (dgain1, dsc1, dsh1) = _rowwise_bwd(
        "norm1_bwd", _fn_norm1, [xs], [wt["norm1_gain"], sc1, sh1], [[dh_a, dh_mm, _shift_up(dhp)], []],
        [F32], [True] * 3, 128, add_to_first=[dx_a])

    g_mu_rkv, g_mu_w, g_mu_a, g_mu_g, g_w0, g_a0, g_kk, g_ka, gw1, gw2, ga1, ga2, gg1, gg2 = pre_g
    g_la = jnp.concatenate([gw1, ga1, gg1], axis=1).reshape(N_CHIPS, d // N_CHIPS, lw + la + lg)
    g_lb = jnp.concatenate([gw2, ga2, gg2], axis=0)
    g_lb = g_lb.reshape(lw + la + lg, N_CHIPS, dr // N_CHIPS).transpose(1, 0, 2)
    local = [gw_in, gw_out.reshape(N_CHIPS, d // N_CHIPS, d), gw_gu, gw_down.reshape(N_CHIPS, dff // N_CHIPS, d),
             g_la, g_lb]
    names = ["w_in", "w_out", "w_gate_up", "w_down", "lora_a", "lora_b"]
    place = jnp.stack([ac, chip]).astype(jnp.int32)
    from_sibling = _pair_exchange(local)
    pair_sums = [_pair_add("pair_add_" + n, g, o, place) for n, g, o in zip(names, local, from_sibling)]
    from_chips = _chip_exchange(pair_sums)
    halves = [_sum_chips("chip_sum_" + n, p, q, place) for n, p, q in zip(names, pair_sums, from_chips)]
    r_in, r_out, r_gu, r_down, r_la, r_lb = _pair_share(halves)

    dmod = jnp.concatenate([dsh1, dsc1, dgt1, dsh2, dsc2, dgt2], axis=1)
    dqg = dqg.reshape(nh, HEAD_DIM).sum(axis=0, keepdims=True)
    dkg = dkg.reshape(nh, HEAD_DIM).sum(axis=0, keepdims=True)
    small_g = [dmod, dgain1, dgain2, g_mu_rkv, g_mu_w, g_mu_a, g_mu_g, g_w0, g_a0, g_kk, g_ka, drk, dlng, dlnb,
               dqg, dkg]
    lead = jnp.zeros((1, LOSS_SLOT), F32)
    packed = _pack_small([loss_vec] + small_g)
    gathered = _all_gather8("gather_small", packed)
    sm_g, sm_d, sm_m, sm_v = _small_update(gathered, _pack_small([lead] + [wt[n] for n in SMALL]),
                                           _pack_small([lead] + [mom[n] for n in SMALL]),
                                           _pack_small([lead] + [var[n] for n in SMALL]))
    loss = sm_g.reshape(-1)[0]

    def unpack(packed_arr):
        flat, out, pos = packed_arr.reshape(-1), {}, LOSS_SLOT
        for n in SMALL:
            size = wt[n].size
            out[n] = flat[pos:pos + size]
            pos += size
        return out

    res = {"grad": unpack(sm_g), "delta": unpack(sm_d), "m": unpack(sm_m), "v": unpack(sm_v)}

    dmod_all = gathered.reshape(N_DEV, -1)[:, LOSS_SLOT:LOSS_SLOT + N_CHIPS * n_ada]
    dmod_cols = lax.dynamic_slice(dmod_all, (0, chip * n_ada), (N_DEV, n_ada))
    pad8 = ((0, N_DEV), (0, 0))
    res["grad"]["w_ada"] = _ada_grad(jnp.pad(c_all, pad8), jnp.pad(dmod_cols, pad8))

    res["grad"].update(w_in=r_in, w_out=r_out, w_gate_up=r_gu, w_down=r_down)
    for n in ("w_ada", "w_in", "w_out", "w_gate_up", "w_down"):
        res["delta"][n], res["m"][n], res["v"][n] = _adamw("adamw_" + n, wt[n], res["grad"][n], mom[n], var[n])
    la_d, la_m, la_v = _adamw("adamw_lora_a", lora_a(wt), r_la, lora_a(mom), lora_a(var))
    lb_d, lb_m, lb_v = _adamw("adamw_lora_b", lora_b(wt), r_lb, lora_b(mom), lora_b(var))
    for key, pa, pb in (("grad", r_la, r_lb), ("delta", la_d, lb_d), ("m", la_m, lb_m), ("v", la_v, lb_v)):
        res[key].update(w1=pa[:, :lw], a1=pa[:, lw:lw + la], g1=pa[:, lw + la:],
                        w2=pb[:lw], a2=pb[lw:lw + la], g2=pb[lw + la:])

    outs = [loss, grad_x[None]]
    for key in ("grad", "delta", "m", "v"):
        outs += [res[key][n].reshape(given[n].shape) for n in WEIGHTS]
    return tuple(outs)
```

```python
import functools
import math

import jax
import jax.numpy as jnp
from jax import lax
from jax.experimental import pallas as pl
from jax.experimental.pallas import tpu as pltpu

F32 = jnp.float32
BF16 = jnp.bfloat16
HEAD_DIM = 64
LANES = 128
RMS_EPS = 1e-6
GN_EPS = 64e-5
L2_EPS = 1e-12
ADAM_LR, ADAM_B1, ADAM_B2, ADAM_EPS, ADAM_WD, ADAM_STEP = 0.001, 0.9, 0.999, 1e-08, 0.01, 10
VMEM_LIMIT = 56 * 1024 * 1024
MESH = pl.DeviceIdType.MESH
HI = lax.Precision.HIGHEST
N_CHIPS = 4
N_DEV = 8
XY_MASKS = ((1, 0), (0, 1), (1, 1))


def _pick(dim, prefs):
    for p in prefs:
        if dim % p == 0:
            return p
    return dim


def _params(sem=None, vmem=VMEM_LIMIT):
    return pltpu.CompilerParams(dimension_semantics=sem, vmem_limit_bytes=vmem)


def _sigmoid(x):
    return 1.0 / (1.0 + jnp.exp(-x))


@jax.custom_vjp
def _softplus(x):
    return jnp.maximum(x, 0.0) + jnp.log(1.0 + jnp.exp(-jnp.abs(x)))


_softplus.defvjp(lambda x: (_softplus(x), x), lambda x, g: (g * _sigmoid(x),))


def _silu(x):
    return x * _sigmoid(x)


@jax.custom_vjp
def _bdot(a, b):
    return jnp.dot(a.astype(BF16), b.astype(BF16), preferred_element_type=F32)


def _bdot_bwd(res, g):
    a, b = res
    gb = g.astype(BF16)
    da = lax.dot_general(gb, b.astype(BF16), (((1,), (1,)), ((), ())), preferred_element_type=F32)
    db = lax.dot_general(a.astype(BF16), gb, (((0,), (0,)), ((), ())), preferred_element_type=F32)
    return da.astype(a.dtype), db.astype(b.dtype)


_bdot.defvjp(lambda a, b: (_bdot(a, b), (a, b)), _bdot_bwd)


def _head_ones():
    i = lax.broadcasted_iota(jnp.int32, (LANES, LANES), 0) // HEAD_DIM
    j = lax.broadcasted_iota(jnp.int32, (LANES, LANES), 1) // HEAD_DIM
    return (i == j).astype(F32)


def _hdot(x, ones):
    return jnp.dot(x, ones, precision=HI, preferred_element_type=F32)


@jax.custom_vjp
def _segsum(x):
    ones = _head_ones()
    parts = [_hdot(x[:, LANES * j:LANES * (j + 1)], ones) for j in range(x.shape[1] // LANES)]
    return parts[0] if len(parts) == 1 else jnp.concatenate(parts, axis=1)


_segsum.defvjp(lambda x: (_segsum(x), None), lambda _, g: (_segsum(g),))


def _rms(x, gain):
    return x * lax.rsqrt(jnp.mean(x * x, axis=-1, keepdims=True) + RMS_EPS) * gain


def _matmul(name, a, b, *, ta=False, tb=False, b_shards=False, out_shards=False, out_dtype=F32,
            tm=512, tn=512, tk=512, n_outer=False):
    if ta:
        kdim, m = a.shape
    else:
        m, kdim = a.shape
    if b_shards:
        if tb:
            n, ks = b.shape[1], b.shape[2]
            assert ks * N_CHIPS == kdim
        else:
            ns = b.shape[2]
            n = ns * N_CHIPS
            assert b.shape[1] == kdim
    else:
        n = b.shape[0] if tb else b.shape[1]
    tm = _pick(m, (tm, 512, 256, 128))
    n_part = n // N_CHIPS if (out_shards or (b_shards and not tb)) else n
    tn = _pick(n_part, (tn, 512, 256, 128))
    k_part = kdim // N_CHIPS if (b_shards and tb) else kdim
    tk = _pick(k_part, (tk, 512, 256, 128))
    nb = n_part // tn
    kb = k_part // tk
    nk = kdim // tk
    grid = (n // tn, m // tm, nk) if n_outer else (m // tm, n // tn, nk)

    def spec(shape, index):
        if n_outer:
            return pl.BlockSpec(shape, lambda j, i, k: index(i, j, k))
        return pl.BlockSpec(shape, index)

    if ta:
        a_spec = spec((tk, tm), lambda i, j, k: (k, i))
    else:
        a_spec = spec((tm, tk), lambda i, j, k: (i, k))
    if b_shards and tb:
        b_spec = spec((None, tn, tk), lambda i, j, k: (k // kb, j, k % kb))
    elif b_shards:
        b_spec = spec((None, tk, tn), lambda i, j, k: (j // nb, k, j % nb))
    elif tb:
        b_spec = spec((tn, tk), lambda i, j, k: (j, k))
    else:
        b_spec = spec((tk, tn), lambda i, j, k: (k, j))
    if out_shards:
        o_spec = spec((None, tm, tn), lambda i, j, k: (j // nb, i, j % nb))
        o_shape = jax.ShapeDtypeStruct((N_CHIPS, m, n_part), out_dtype)
    else:
        o_spec = spec((tm, tn), lambda i, j, k: (i, j))
        o_shape = jax.ShapeDtypeStruct((m, n), out_dtype)
    dims = (((0 if ta else 1,), (1 if tb else 0,)), ((), ()))

    def body(a_ref, b_ref, o_ref, acc_ref):
        k = pl.program_id(2)

        @pl.when(k == 0)
        def _():
            acc_ref[...] = jnp.zeros_like(acc_ref)

        acc_ref[...] += lax.dot_general(a_ref[...].astype(BF16), b_ref[...].astype(BF16), dims,
                                        preferred_element_type=F32)

        @pl.when(k == nk - 1)
        def _():
            o_ref[...] = acc_ref[...].astype(o_ref.dtype)

    return pl.pallas_call(
        body, name=name, grid=grid, in_specs=[a_spec, b_spec], out_specs=o_spec, out_shape=o_shape,
        scratch_shapes=[pltpu.VMEM((tm, tn), F32)],
        compiler_params=_params(("parallel", "parallel", "arbitrary")),
    )(a, b)


def _row_in(spec, tile):
    if isinstance(spec, tuple):
        arr, width, cb = spec
    else:
        arr, width, cb = spec, spec.shape[1], 0
    return arr, pl.BlockSpec((tile, width), lambda i, cb=cb: (i, cb))


def _full_spec(arr):
    nd = arr.ndim
    return pl.BlockSpec(arr.shape, lambda i, nd=nd: (0,) * nd)


def _rowwise(name, fn, rows, params, outs, tile):
    t = (rows[0][0] if isinstance(rows[0], tuple) else rows[0]).shape[0]
    tile = _pick(t, (tile,))
    arrs, specs = zip(*[_row_in(s, tile) for s in rows])
    nr, npar = len(rows), len(params)

    def body(*refs):
        rv = [r[...].astype(F32) for r in refs[:nr]]
        pv = [p[...] for p in refs[nr:nr + npar]]
        res = fn(*rv, *pv)
        for o_ref, val in zip(refs[nr + npar:], res):
            o_ref[...] = val.astype(o_ref.dtype)

    return pl.pallas_call(
        body, name=name, grid=(t // tile,),
        in_specs=list(specs) + [_full_spec(p) for p in params],
        out_specs=[pl.BlockSpec((tile, w), lambda i: (i, 0)) for w, _ in outs],
        out_shape=[jax.ShapeDtypeStruct((t, w), d) for w, d in outs],
        compiler_params=_params(("parallel",)),
    )(*arrs, *params)


def _rowwise_bwd(name, fn, rows, params, cts, row_grads, param_grads, tile, add_to_first=()):
    t = (rows[0][0] if isinstance(rows[0], tuple) else rows[0]).shape[0]
    tile = _pick(t, (tile,))
    arrs, specs = zip(*[_row_in(s, tile) for s in rows])
    n_add = len(add_to_first)
    flat_cts = [c for group in cts for c in group] + list(add_to_first)
    c_arrs, c_specs = zip(*[_row_in(s, tile) for s in flat_cts])
    nr, npar, nc = len(rows), len(params), len(flat_cts)
    rg_idx = [i for i, d in enumerate(row_grads) if d is not None]
    pg_idx = [i for i, d in enumerate(param_grads) if d]

    def body(*refs):
        rv = [r[...].astype(F32) for r in refs[:nr]]
        pv = [p[...] for p in refs[nr:nr + npar]]
        cv = [c[...].astype(F32) for c in refs[nr + npar:nr + npar + nc]]
        o_refs = refs[nr + npar + nc:]
        outs, vjp = jax.vjp(fn, *rv, *pv)
        ct, pos = [], 0
        for group, o in zip(cts, outs):
            if group:
                acc = cv[pos]
                for extra in cv[pos + 1:pos + len(group)]:
                    acc = acc + extra
                pos += len(group)
            else:
                acc = jnp.zeros_like(o)
            ct.append(acc)
        grads = list(vjp(tuple(ct)))
        for extra in cv[nc - n_add:]:
            grads[rg_idx[0]] = grads[rg_idx[0]] + extra
        for o_ref, i in zip(o_refs[:len(rg_idx)], rg_idx):
            o_ref[...] = grads[i].astype(o_ref.dtype)
        first = pl.program_id(0) == 0
        for o_ref, i in zip(o_refs[len(rg_idx):], pg_idx):
            g = grads[nr + i].astype(F32)

            @pl.when(first)
            def _(o_ref=o_ref, g=g):
                o_ref[...] = g

            @pl.when(jnp.logical_not(first))
            def _(o_ref=o_ref, g=g):
                o_ref[...] += g

    def width(i):
        s = rows[i]
        return s[1] if isinstance(s, tuple) else s.shape[1]

    out_specs = [pl.BlockSpec((tile, width(i)), lambda i_: (i_, 0)) for i in rg_idx]
    out_shape = [jax.ShapeDtypeStruct((t, width(i)), row_grads[i]) for i in rg_idx]
    out_specs += [_full_spec(params[i]) for i in pg_idx]
    out_shape += [jax.ShapeDtypeStruct(params[i].shape, F32) for i in pg_idx]
    res = pl.pallas_call(
        body, name=name, grid=(t // tile,),
        in_specs=list(specs) + [_full_spec(p) for p in params] + list(c_specs),
        out_specs=out_specs, out_shape=out_shape,
        compiler_params=_params(("arbitrary",)),
    )(*arrs, *params, *c_arrs)
    return res[:len(rg_idx)], res[len(rg_idx):]


def _fn_norm1(x, gain, sc, sh):
    h = _rms(x, gain) * (1.0 + sc) + sh
    return h, h


def _fn_rwkv_pre(h, hp, p, pp, mu_rkv, mu_w, mu_a, mu_g, w0, a0, k_k, k_a, w1, w2, a1, a2, g1, g2):
    d = p.shape[1] // 3
    dh = hp - h
    xw = h + dh * mu_w
    xa = h + dh * mu_a
    xg = h + dh * mu_g
    pr = p + (pp - p) * mu_rkv
    r, k, v = pr[:, :d], pr[:, d:2 * d], pr[:, 2 * d:]
    w_log = -_softplus(-(w0 + _bdot(jnp.tanh(_bdot(xw, w1)), w2))) - 0.5
    decay = jnp.exp(-jnp.exp(w_log))
    a = _sigmoid(a0 + _bdot(_bdot(xa, a1), a2))
    g = _bdot(_sigmoid(_bdot(xg, g1)), g2)
    kk = k * k_k
    kk = kk * lax.rsqrt(_segsum(kk * kk) + L2_EPS)
    k2 = k * (1.0 + (a - 1.0) * k_a)
    return r, decay, k2, v, -kk, kk * a, g


def _fn_rwkv_post(y, r, k2, v, g, ln_g, ln_b, r_k):
    inv = 1.0 / HEAD_DIM
    mean = _segsum(y) * inv
    yc = y - mean
    var = _segsum(yc * yc) * inv
    yn = yc * lax.rsqrt(var + GN_EPS) * ln_g + ln_b
    bonus = _segsum(r * k2 * r_k) * v
    return ((yn + bonus) * g,)


def _fn_rwkv_post_cat(y, r, k2, v, g, o_sb, ln_g, ln_b, r_k):
    return (jnp.concatenate([_fn_rwkv_post(y, r, k2, v, g, ln_g, ln_b, r_k)[0], o_sb], axis=1),)


def _fn_qk_norm(p, qg, kg):
    d = p.shape[1] // 3
    q, k, v = p[:, :d], p[:, d:2 * d], p[:, 2 * d:]
    inv = 1.0 / HEAD_DIM
    qn = q * lax.rsqrt(_segsum(q * q) * inv + RMS_EPS) * qg
    kn = k * lax.rsqrt(_segsum(k * k) * inv + RMS_EPS) * kg
    return qn, kn, v


def _fn_mix_norm2(x, mix, gt1, gain, sc, sh):
    x1 = x + gt1 * mix
    h2 = _rms(x1, gain) * (1.0 + sc) + sh
    return x1, h2


def _fn_swiglu(gate, up):
    return (_silu(gate) * up,)


def _swiglu_bwd(gu, dact, tile=128):
    t, dff = dact.shape
    tile = _pick(t, (tile,))

    def body(gate_ref, up_ref, d_ref, o_ref):
        _, vjp = jax.vjp(_fn_swiglu, gate_ref[...], up_ref[...])
        dgate, dup = vjp((d_ref[...],))
        o_ref[:, :dff] = dgate.astype(o_ref.dtype)
        o_ref[:, dff:] = dup.astype(o_ref.dtype)

    return pl.pallas_call(
        body, name="swiglu_bwd", grid=(t // tile,),
        in_specs=[pl.BlockSpec((tile, dff), lambda i: (i, 0)), pl.BlockSpec((tile, dff), lambda i: (i, 1)),
                  pl.BlockSpec((tile, dff), lambda i: (i, 0))],
        out_specs=pl.BlockSpec((tile, 2 * dff), lambda i: (i, 0)),
        out_shape=jax.ShapeDtypeStruct((t, 2 * dff), BF16),
        compiler_params=_params(("parallel",)),
    )(gu, gu, dact)


def _loss_head(name, x1, dn, target, gt2, tile=256):
    t, d = x1.shape
    tile = _pick(t, (tile,))

    def body(x1_ref, dn_ref, tg_ref, gt_ref, loss_ref, dout_ref, ddn_ref, dgt_ref):
        dnv = dn_ref[...]
        gt = gt_ref[...]
        err = x1_ref[...] + gt * dnv - tg_ref[...]
        dout = err * (1.0 / d)
        dout_ref[...] = dout
        ddn_ref[...] = (dout * gt).astype(ddn_ref.dtype)
        part = 0.5 * jnp.sum(jnp.sum(err * dout, axis=-1, keepdims=True), axis=0, keepdims=True)
        dgt = jnp.sum(dout * dnv, axis=0, keepdims=True)
        first = pl.program_id(0) == 0

        @pl.when(first)
        def _():
            loss_ref[...] = jnp.broadcast_to(part, loss_ref.shape)
            dgt_ref[...] = dgt

        @pl.when(jnp.logical_not(first))
        def _():
            loss_ref[...] += jnp.broadcast_to(part, loss_ref.shape)
            dgt_ref[...] += dgt

    row = pl.BlockSpec((tile, d), lambda i: (i, 0))
    vec = pl.BlockSpec((1, d), lambda i: (0, 0))
    return pl.pallas_call(
        body, name=name, grid=(t // tile,),
        in_specs=[row, row, row, vec],
        out_specs=[pl.BlockSpec((1, LANES), lambda i: (0, 0)), row, row, vec],
        out_shape=[jax.ShapeDtypeStruct((1, LANES), F32), jax.ShapeDtypeStruct((t, d), F32),
                   jax.ShapeDtypeStruct((t, d), BF16), jax.ShapeDtypeStruct((1, d), F32)],
        compiler_params=_params(("arbitrary",)),
    )(x1, dn, target, gt2)


SCAN_BLOCK = 32
N_COL = 5
WIDE = 2 * LANES


def _wide_eye():
    i = lax.broadcasted_iota(jnp.int32, (HEAD_DIM, WIDE), 0)
    j = lax.broadcasted_iota(jnp.int32, (HEAD_DIM, WIDE), 1) % HEAD_DIM
    return (i == j).astype(BF16)


def _wide_ones():
    i = lax.broadcasted_iota(jnp.int32, (WIDE, WIDE), 0) // HEAD_DIM
    j = lax.broadcasted_iota(jnp.int32, (WIDE, WIDE), 1) // HEAD_DIM
    return (i == j).astype(BF16)


COL_PIECES = (1, 3, 2, 2, 1)


def _col_tiles(refs, i, nq, eye, ones_bf):
    pieces = [[], [], []]
    for ref, n_pieces in zip(refs, COL_PIECES):
        full = ref[pl.ds(i, 1), :]
        for q in range(nq):
            rest = full[:, q * WIDE:(q + 1) * WIDE]
            for level in range(n_pieces):
                part = rest.astype(BF16)
                pieces[level].append(part * eye)
                rest = rest - part.astype(F32)
    tile = nq * HEAD_DIM
    out = jnp.dot(jnp.concatenate(pieces[0] + pieces[1] + pieces[2], axis=0), ones_bf, preferred_element_type=F32)
    first, second, third = out[:5 * tile], out[5 * tile:8 * tile], out[8 * tile:]
    refined = first[tile:4 * tile] + second
    parts = [first[:tile], refined[:tile] + third, refined[tile:], first[4 * tile:]]
    return jnp.concatenate(parts, axis=0).reshape(N_COL * nq, HEAD_DIM, WIDE)


def _bf16_round(x):
    bits = lax.bitcast_convert_type(x, jnp.uint32)
    bits = (bits + jnp.uint32(0x7FFF) + ((bits >> 16) & jnp.uint32(1))) & jnp.uint32(0xFFFF0000)
    return lax.bitcast_convert_type(bits, F32)


def _pair_tile(tiles_ref, n, p, nq):
    return tiles_ref[n * nq + p // 2, :, (p % 2) * LANES:(p % 2 + 1) * LANES]


def _scan_fwd(rem, w, wr, k, r, v):
    t, dr = v.shape
    npair, nq = dr // LANES, dr // WIDE
    tb = _pick(t, (SCAN_BLOCK,))

    def body(rem_ref, w_ref, wr_ref, k_ref, r_ref, v_ref, y_ref, hist_ref, last_ref, s_ref, sb_ref,
             tiles_a, tiles_b):
        @pl.when(pl.program_id(0) == 0)
        def _():
            s_ref[...] = jnp.zeros_like(s_ref)
            sb_ref[...] = jnp.zeros_like(sb_ref)

        eye, ones_bf = _wide_eye(), _wide_ones()
        col_refs = (rem_ref, w_ref, wr_ref, k_ref, r_ref)

        def step(i, tiles_ref):
            v_full, y_rows = v_ref[pl.ds(i, 1), :], []
            for p in range(npair):
                s = s_ref[p]
                hist_ref[i, p] = s
                c_rem, c_w, c_wr, c_k, c_r = [_pair_tile(tiles_ref, n, p, nq) for n in range(N_COL)]
                sa = jnp.sum(sb_ref[p] * c_rem, axis=0, keepdims=True)
                s2 = s * c_w + c_wr * sa + c_k * v_full[:, p * LANES:(p + 1) * LANES]
                s2_b = _bf16_round(s2)
                y_rows.append(jnp.sum(s2_b * c_r, axis=0, keepdims=True))
                s_ref[p] = s2
                sb_ref[p] = s2_b
            y_ref[pl.ds(i, 1), :] = jnp.concatenate(y_rows, axis=1)

        tiles_a[...] = _col_tiles(col_refs, 0, nq, eye, ones_bf)

        def two_steps(m, carry):
            i = 2 * m
            tiles_b[...] = _col_tiles(col_refs, i + 1, nq, eye, ones_bf)
            step(i, tiles_a)
            tiles_a[...] = _col_tiles(col_refs, jnp.minimum(i + 2, tb - 1), nq, eye, ones_bf)
            step(i + 1, tiles_b)
            return carry

        lax.fori_loop(0, tb // 2, two_steps, 0, unroll=2)
        last_ref[...] = sb_ref[...]

    blk = pl.BlockSpec((tb, dr), lambda i: (i, 0))
    tiles = pltpu.VMEM((N_COL * nq, HEAD_DIM, WIDE), F32)
    state = pltpu.VMEM((npair, HEAD_DIM, LANES), F32)
    return pl.pallas_call(
        body, name="rwkv_scan_fwd", grid=(t // tb,),
        in_specs=[blk] * 6,
        out_specs=[blk, pl.BlockSpec((tb, npair, HEAD_DIM, LANES), lambda i: (i, 0, 0, 0)),
                   pl.BlockSpec((npair, HEAD_DIM, LANES), lambda i: (0, 0, 0))],
        out_shape=[jax.ShapeDtypeStruct((t, dr), F32), jax.ShapeDtypeStruct((t, npair, HEAD_DIM, LANES), F32),
                   jax.ShapeDtypeStruct((npair, HEAD_DIM, LANES), F32)],
        scratch_shapes=[state, state, tiles, tiles],
        compiler_params=_params(("arbitrary",)),
    )(rem, w, wr, k, r, v)


def _scan_bwd(rem, w, wr, k, r, v, hist, last, dy):
    t, dr = v.shape
    npair, nq = dr // LANES, dr // WIDE
    tb = _pick(t, (SCAN_BLOCK,))
    nblk = t // tb

    def body(rem_ref, w_ref, wr_ref, k_ref, r_ref, v_ref, hist_ref, last_ref, dy_ref,
             drem_ref, dw_ref, dwr_ref, dk_ref, dr_ref, dv_ref, ds_ref, next_ref, tiles_a, tiles_b):
        @pl.when(pl.program_id(0) == 0)
        def _():
            ds_ref[...] = jnp.zeros_like(ds_ref)
            next_ref[...] = last_ref[...]

        eye, ones_bf = _wide_eye(), _wide_ones()
        col_refs = (rem_ref, w_ref, wr_ref, k_ref, r_ref)
        out_refs = (drem_ref, dw_ref, dwr_ref, dk_ref, dr_ref)

        def step(i, tiles_ref):
            grads = [[None] * npair for _ in range(N_COL)]
            v_full, dy_full, dv_rows = v_ref[pl.ds(i, 1), :], dy_ref[pl.ds(i, 1), :], []
            for p in range(npair):
                lanes = slice(p * LANES, (p + 1) * LANES)
                s = hist_ref[i, p]
                c_rem, c_w, c_wr, c_k, c_r = [_pair_tile(tiles_ref, n, p, nq) for n in range(N_COL)]
                v_row, dy_row = v_full[:, lanes], dy_full[:, lanes]
                s_b = _bf16_round(s)
                s2_b = next_ref[p]
                next_ref[p] = s_b
                sa = jnp.sum(s_b * c_rem, axis=0, keepdims=True)
                d2 = ds_ref[p] + c_r * dy_row
                dsa = jnp.sum(d2 * c_wr, axis=0, keepdims=True)
                dv_rows.append(jnp.sum(d2 * c_k, axis=0, keepdims=True))
                ds_ref[p] = d2 * c_w + c_rem * dsa
                for n, tile in enumerate((s_b * dsa, d2 * s, d2 * sa, d2 * v_row, s2_b * dy_row)):
                    grads[n][p] = tile.astype(BF16)
            wide = [jnp.concatenate(grads[n][2 * q:2 * q + 2], axis=1) for n in range(N_COL) for q in range(nq)]
            sums = jnp.dot(jnp.concatenate(wide, axis=0), ones_bf, preferred_element_type=F32)
            sums = sums.reshape(N_COL * nq, HEAD_DIM, WIDE)
            rows = jnp.sum(sums * eye.astype(F32)[None], axis=1)
            dv_ref[pl.ds(i, 1), :] = jnp.concatenate(dv_rows, axis=1)
            for n, o_ref in enumerate(out_refs):
                o_ref[pl.ds(i, 1), :] = jnp.concatenate([rows[n * nq + q:n * nq + q + 1] for q in range(nq)], axis=1)

        tiles_a[...] = _col_tiles(col_refs, tb - 1, nq, eye, ones_bf)

        def two_steps(m, carry):
            i = tb - 1 - 2 * m
            tiles_b[...] = _col_tiles(col_refs, i - 1, nq, eye, ones_bf)
            step(i, tiles_a)
            tiles_a[...] = _col_tiles(col_refs, jnp.maximum(i - 2, 0), nq, eye, ones_bf)
            step(i - 1, tiles_b)
            return carry

        lax.fori_loop(0, tb // 2, two_steps, 0, unroll=2)

    blk = pl.BlockSpec((tb, dr), lambda i: (nblk - 1 - i, 0))
    tiles = pltpu.VMEM((N_COL * nq, HEAD_DIM, WIDE), F32)
    state = pltpu.VMEM((npair, HEAD_DIM, LANES), F32)
    return pl.pallas_call(
        body, name="rwkv_scan_bwd", grid=(nblk,),
        in_specs=[blk] * 6 + [pl.BlockSpec((tb, npair, HEAD_DIM, LANES), lambda i: (nblk - 1 - i, 0, 0, 0)),
                              pl.BlockSpec((npair, HEAD_DIM, LANES), lambda i: (0, 0, 0)), blk],
        out_specs=[blk] * 6,
        out_shape=[jax.ShapeDtypeStruct((t, dr), F32)] * 6,
        scratch_shapes=[state, state, tiles, tiles],
        compiler_params=_params(("arbitrary",)),
    )(rem, w, wr, k, r, v, hist, last, dy)


SB_BLOCK = 256
SB_HEADS = 8
SB_HEADS_BWD = 4
NT_DIMS = (((1,), (1,)), ((), ()))
TN_DIMS = (((0,), (0,)), ((), ()))
SB_SCALE = 1.0 / math.sqrt(HEAD_DIM)


def _dot2(x, tri):
    hi = x.astype(BF16)
    mid = (x - hi.astype(F32)).astype(BF16)
    return jnp.dot(hi, tri, preferred_element_type=F32) + jnp.dot(mid, tri, preferred_element_type=F32)


def _sb_block_iotas(bs):
    return lax.broadcasted_iota(jnp.int32, (bs, bs), 0), lax.broadcasted_iota(jnp.int32, (bs, bs), 1)


def _when_step(group, block):
    return pl.when(jnp.logical_and(pl.program_id(0) == group, pl.program_id(1) == block))


def _sb_fwd(q, k, v, shards):
    h, t, d = q.shape
    bs = _pick(t, (SB_BLOCK,))
    nh = _pick(h, (SB_HEADS,))
    ngroup, nblock = h // nh, t // bs
    nw = len(shards)
    heights = [s.shape[0] for s in shards]

    def body(q_ref, k_ref, v_ref, *rest):
        w_in, (o_ref, l_ref), w_out, sems = rest[:nw], rest[nw:nw + 2], rest[nw + 2:2 * nw + 2], rest[2 * nw + 2:]

        @_when_step(0, 0)
        def _():
            _gather_phase("issue", w_in, w_out, sems, heights)

        @_when_step(ngroup - 1, (2 * nblock) // 3)
        def _():
            _gather_phase("forward", w_in, w_out, sems, heights)

        qi = pl.program_id(1)
        ri, ci = _sb_block_iotas(bs)
        tri_ge = (ri >= ci).astype(BF16)
        causal = ci < ri
        qv = [q_ref[hh] for hh in range(nh)]

        def blocks(j, masked, carry):
            accs, tails = carry[:nh], carry[nh:]
            heads = range(nh)
            rows = pl.ds(pl.multiple_of(j * bs, bs), bs)
            z = [lax.dot_general(qv[hh], k_ref[hh, rows, :], NT_DIMS, preferred_element_type=F32) * SB_SCALE
                 for hh in heads]
            log1m = [-_softplus(z[hh]) for hh in heads]
            if masked:
                log1m = [jnp.where(causal, x, 0.0) for x in log1m]
            cs = [_dot2(log1m[hh], tri_ge) for hh in heads]
            a = [jnp.exp(z[hh] + cs[hh] + tails[hh]) for hh in heads]
            if masked:
                a = [jnp.where(causal, x, 0.0) for x in a]
            accs = [accs[hh] + jnp.dot(a[hh].astype(BF16), v_ref[hh, rows, :], preferred_element_type=F32)
                    for hh in heads]
            return tuple(accs) + tuple(tails[hh] + cs[hh][:, 0:1] for hh in heads)

        carry = blocks(qi, True, (jnp.zeros((bs, d), F32),) * nh + (jnp.zeros((bs, 1), F32),) * nh)
        carry = lax.fori_loop(0, qi, lambda n, c: blocks(qi - 1 - n, False, c), carry)
        for hh in range(nh):
            o_ref[hh] = carry[hh]
            l_ref[hh] = jnp.broadcast_to(carry[nh + hh], (bs, d))

        @_when_step(ngroup - 1, nblock - 1)
        def _():
            _gather_phase("finish", w_in, w_out, sems, heights)

    qs = pl.BlockSpec((nh, bs, d), lambda hh, i: (hh, i, 0))
    ks = pl.BlockSpec((nh, t, d), lambda hh, i: (hh, 0, 0), pipeline_mode=pl.Buffered(1))
    res = pl.pallas_call(
        body, name="sb_attn_fwd", grid=(ngroup, nblock),
        in_specs=[qs, ks, ks] + [ANY] * nw, out_specs=[qs, qs] + [ANY] * nw,
        out_shape=[jax.ShapeDtypeStruct((h, t, d), F32)] * 2 + _gather_out_shapes(shards),
        scratch_shapes=_gather_sems(nw),
        compiler_params=_params(("arbitrary", "arbitrary")),
    )(q, k, v, *shards)
    return res[0], res[1], res[2:]


def _sb_bwd(q, k, v, lsum, do, parts):
    h, t, d = q.shape
    bs = _pick(t, (SB_BLOCK,))
    nh = _pick(h, (SB_HEADS_BWD,))
    ngroup, nblock = h // nh, t // bs
    nw = len(parts)

    def body(q_ref, k_ref, v_ref, l_ref, do_ref, *rest):
        p_in, (dq_ref, dk_ref, dv_ref), p_out, sems = rest[:nw], rest[nw:nw + 3], rest[nw + 3:2 * nw + 3], rest[2 * nw + 3:]

        @_when_step(0, 0)
        def _():
            _exchange_phase("issue", p_in, p_out, sems)

        qi = pl.program_id(1)

        @pl.when(qi == 0)
        def _():
            dk_ref[...] = jnp.zeros_like(dk_ref)
            dv_ref[...] = jnp.zeros_like(dv_ref)

        ri, ci = _sb_block_iotas(bs)
        tri_lt = (ri < ci).astype(BF16)
        causal = ci < ri
        qv = [q_ref[hh] for hh in range(nh)]
        dob = [do_ref[hh].astype(BF16) for hh in range(nh)]
        ltot = [l_ref[hh][:, 0:1] for hh in range(nh)]

        def blocks(j, masked, carry):
            dq, pc, ec = carry[:nh], carry[nh:2 * nh], carry[2 * nh:]
            heads = range(nh)
            rows = pl.ds(pl.multiple_of(j * bs, bs), bs)
            z = [lax.dot_general(qv[hh], k_ref[hh, rows, :], NT_DIMS, preferred_element_type=F32) * SB_SCALE
                 for hh in heads]
            da = [lax.dot_general(dob[hh], v_ref[hh, rows, :], NT_DIMS, preferred_element_type=F32) for hh in heads]
            nsp = [-_softplus(z[hh]) for hh in heads]
            log1m = [jnp.where(causal, x, 0.0) for x in nsp] if masked else nsp
            below = [_dot2(log1m[hh], tri_lt) + pc[hh] for hh in heads]
            a = [jnp.exp(z[hh] + (ltot[hh] - below[hh])) for hh in heads]
            if masked:
                a = [jnp.where(causal, x, 0.0) for x in a]
            e = [a[hh] * da[hh] for hh in heads]
            ebelow = [_dot2(e[hh], tri_lt) + ec[hh] for hh in heads]
            dz = [e[hh] * jnp.exp(nsp[hh]) - jnp.exp(z[hh] + nsp[hh]) * ebelow[hh] for hh in heads]
            if masked:
                dz = [jnp.where(causal, x, 0.0) for x in dz]
            dzb = [(x * SB_SCALE).astype(BF16) for x in dz]
            for hh in heads:
                dv_ref[hh, rows, :] += lax.dot_general(a[hh].astype(BF16), dob[hh], TN_DIMS,
                                                       preferred_element_type=F32)
            for hh in heads:
                dk_ref[hh, rows, :] += lax.dot_general(dzb[hh], qv[hh], TN_DIMS, preferred_element_type=F32)
            dq = [dq[hh] + jnp.dot(dzb[hh], k_ref[hh, rows, :], preferred_element_type=F32) for hh in heads]
            pc = [pc[hh] + jnp.sum(log1m[hh], axis=1, keepdims=True) for hh in heads]
            ec = [ec[hh] + jnp.sum(e[hh], axis=1, keepdims=True) for hh in heads]
            return tuple(dq) + tuple(pc) + tuple(ec)

        zcol = jnp.zeros((bs, 1), F32)
        carry = lax.fori_loop(0, qi, lambda j, c: blocks(j, False, c),
                              (jnp.zeros((bs, d), F32),) * nh + (zcol,) * (2 * nh))
        carry = blocks(qi, True, carry)
        for hh in range(nh):
            dq_ref[hh] = carry[hh]

        @_when_step(ngroup - 1, nblock - 1)
        def _():
            _exchange_phase("finish", p_in, p_out, sems)

    qs = pl.BlockSpec((nh, bs, d), lambda hh, i: (hh, i, 0))
    ks = pl.BlockSpec((nh, t, d), lambda hh, i: (hh, 0, 0), pipeline_mode=pl.Buffered(1))
    res = pl.pallas_call(
        body, name="sb_attn_bwd", grid=(ngroup, nblock),
        in_specs=[qs, ks, ks, qs, qs] + [ANY] * nw, out_specs=[qs, ks, ks] + [ANY] * nw,
        out_shape=[jax.ShapeDtypeStruct((h, t, d), F32)] * 3 + [jax.ShapeDtypeStruct(p.shape, p.dtype) for p in parts],
        scratch_shapes=_exchange_sems(nw),
        compiler_params=_params(("arbitrary", "arbitrary")),
    )(q, k, v, lsum, do, *parts)
    return res[0], res[1], res[2], res[3:]


ANY = pl.BlockSpec(memory_space=pl.ANY)
IN_VMEM = pl.BlockSpec(memory_space=pltpu.VMEM)


def _coords():
    return lax.axis_index("x"), lax.axis_index("y"), lax.axis_index("c")


def _flip(v, bit):
    return 1 - v if bit else v


def _remote(src, dst, send_sem, recv_sem, device):
    return pltpu.make_async_remote_copy(src_ref=src, dst_ref=dst, send_sem=send_sem, recv_sem=recv_sem,
                                        device_id=device, device_id_type=MESH)


def _all_gather8(name, blk):
    m, n = blk.shape

    def body(x_ref, o_ref, send_sems, recv_sems, local_sem):
        x, y, c = _coords()
        own = pltpu.make_async_copy(x_ref, o_ref.at[4 * x + 2 * y + c], local_sem)
        own.start()
        peers = []
        for bits in range(1, N_DEV):
            px, py, pc = _flip(x, (bits >> 2) & 1), _flip(y, (bits >> 1) & 1), _flip(c, bits & 1)
            peers.append((px, py, pc))
        sends = []
        for k, peer in enumerate(peers):
            cp = _remote(x_ref, o_ref.at[4 * x + 2 * y + c], send_sems.at[k], recv_sems.at[k], peer)
            cp.start()
            sends.append(cp)
        for k, (px, py, pc) in enumerate(peers):
            slot = o_ref.at[4 * px + 2 * py + pc]
            _remote(slot, slot, send_sems.at[k], recv_sems.at[k], (px, py, pc)).wait_recv()
        for cp in sends:
            cp.wait_send()
        own.wait()

    return pl.pallas_call(
        body, name=name, out_shape=jax.ShapeDtypeStruct((N_DEV, m, n), blk.dtype),
        in_specs=[IN_VMEM], out_specs=IN_VMEM,
        scratch_shapes=[pltpu.SemaphoreType.DMA((N_DEV - 1,)), pltpu.SemaphoreType.DMA((N_DEV - 1,)),
                        pltpu.SemaphoreType.DMA],
        compiler_params=_params(),
    )(blk)


def _gather_weights(shards):
    nw = len(shards)
    heights = [s.shape[0] for s in shards]

    def body(*refs):
        ins, outs, sems = refs[:nw], refs[nw:2 * nw], refs[2 * nw:]
        _gather_phase("issue", ins, outs, sems, heights)
        _gather_phase("forward", ins, outs, sems, heights)
        _gather_phase("finish", ins, outs, sems, heights)

    return pl.pallas_call(
        body, name="gather_weights",
        out_shape=_gather_out_shapes(shards),
        in_specs=[ANY] * nw, out_specs=[ANY] * nw,
        scratch_shapes=_gather_sems(nw),
        compiler_params=_params(),
    )(*shards)


def _gather_out_shapes(shards):
    return [jax.ShapeDtypeStruct((N_CHIPS,) + s.shape, s.dtype) for s in shards]


def _gather_sems(nw):
    return [pltpu.SemaphoreType.DMA((3 * nw,))] * 4


def _gather_phase(phase, ins, outs, sems, heights):
    ici_send, ici_recv, d2d_send, d2d_recv = sems
    x, y, c = _coords()
    chip = 2 * x + y
    sibling = (x, y, 1 - c)
    for w, height in enumerate(heights):
        half = height // 2
        mine, theirs = pl.ds(c * half, half), pl.ds((1 - c) * half, half)
        for j, (a, b) in enumerate(XY_MASKS):
            px, py = _flip(x, a), _flip(y, b)
            k = 3 * w + j
            over_ici = _remote(ins[w].at[mine], outs[w].at[chip, mine], ici_send.at[k], ici_recv.at[k], (px, py, c))
            landed = outs[w].at[2 * px + py, mine]
            onward = _remote(landed, landed, d2d_send.at[k], d2d_recv.at[k], sibling)
            if phase == "issue":
                over_ici.start()
            elif phase == "forward":
                _remote(landed, landed, ici_send.at[k], ici_recv.at[k], (px, py, c)).wait_recv()
                onward.start()
            else:
                slot = outs[w].at[2 * px + py, theirs]
                _remote(slot, slot, d2d_send.at[k], d2d_recv.at[k], sibling).wait_recv()
                over_ici.wait_send()
                onward.wait_send()


def _pair_exchange(name, grads):
    nw = len(grads)

    def body(*refs):
        ins, outs = refs[:nw], refs[nw:2 * nw]
        send_sems, recv_sems = refs[2 * nw:]
        x, y, c = _coords()
        sibling = (x, y, 1 - c)
        sends = []
        for w in range(nw):
            half = grads[w].shape[1] // 2
            cp = _remote(ins[w].at[:, pl.ds((1 - c) * half, half)], outs[w], send_sems.at[w], recv_sems.at[w], sibling)
            cp.start()
            sends.append(cp)
        for w in range(nw):
            _remote(outs[w], outs[w], send_sems.at[w], recv_sems.at[w], sibling).wait_recv()
        for cp in sends:
            cp.wait_send()

    return pl.pallas_call(
        body, name=name,
        out_shape=[jax.ShapeDtypeStruct((N_CHIPS, g.shape[1] // 2, g.shape[2]), g.dtype) for g in grads],
        in_specs=[ANY] * nw, out_specs=[ANY] * nw,
        scratch_shapes=[pltpu.SemaphoreType.DMA((nw,))] * 2,
        compiler_params=_params(),
    )(*grads)


def _chip_exchange(name, parts):
    nw = len(parts)

    def body(*refs):
        ins, outs, sems = refs[:nw], refs[nw:2 * nw], refs[2 * nw:]
        _exchange_phase("issue", ins, outs, sems)
        _exchange_phase("finish", ins, outs, sems)

    return pl.pallas_call(
        body, name=name,
        out_shape=[jax.ShapeDtypeStruct(p.shape, p.dtype) for p in parts],
        in_specs=[ANY] * nw, out_specs=[ANY] * nw,
        scratch_shapes=_exchange_sems(nw),
        compiler_params=_params(),
    )(*parts)


def _exchange_sems(nw):
    return [pltpu.SemaphoreType.DMA((3 * nw,))] * 2


def _exchange_phase(phase, ins, outs, sems):
    send_sems, recv_sems = sems
    x, y, c = _coords()
    chip = 2 * x + y
    for w in range(len(ins)):
        for j, (a, b) in enumerate(XY_MASKS):
            px, py = _flip(x, a), _flip(y, b)
            k = 3 * w + j
            send = _remote(ins[w].at[2 * px + py], outs[w].at[chip], send_sems.at[k], recv_sems.at[k], (px, py, c))
            if phase == "issue":
                send.start()
            else:
                slot = outs[w].at[2 * px + py]
                _remote(slot, slot, send_sems.at[k], recv_sems.at[k], (px, py, c)).wait_recv()
                send.wait_send()


def _pair_share(shards):
    nw = len(shards)

    def body(*refs):
        bufs = refs[nw:2 * nw]
        send_sems, recv_sems = refs[2 * nw:]
        x, y, c = _coords()
        sibling = (x, y, 1 - c)
        sends = []
        for w in range(nw):
            half = shards[w].shape[0] // 2
            mine = bufs[w].at[pl.ds(c * half, half)]
            cp = _remote(mine, mine, send_sems.at[w], recv_sems.at[w], sibling)
            cp.start()
            sends.append(cp)
        for w in range(nw):
            half = shards[w].shape[0] // 2
            theirs = bufs[w].at[pl.ds((1 - c) * half, half)]
            _remote(theirs, theirs, send_sems.at[w], recv_sems.at[w], sibling).wait_recv()
        for cp in sends:
            cp.wait_send()

    return pl.pallas_call(
        body, name="grad_pair_share",
        out_shape=[jax.ShapeDtypeStruct(s.shape, s.dtype) for s in shards],
        in_specs=[ANY] * nw, out_specs=[ANY] * nw,
        input_output_aliases={w: w for w in range(nw)},
        scratch_shapes=[pltpu.SemaphoreType.DMA((nw,))] * 2,
        compiler_params=_params(),
    )(*shards)


TILE_BYTES = 2 * 1024 * 1024


def _row_tile(rows, cols, mult=8):
    best = None
    for tr in range(mult, rows + 1, mult):
        if rows % tr == 0 and tr * cols * 4 <= TILE_BYTES:
            best = tr
    return best if best is not None else rows


def _pair_add(name, grad, other, place):
    _, half, cols = other.shape
    tr = _row_tile(half, cols, mult=16)
    nb = half // tr

    def body(place_ref, g_ref, o_ref, out_ref):
        out_ref[...] = (g_ref[...] + o_ref[...]).astype(out_ref.dtype)

    return pl.pallas_call(
        body, name=name, out_shape=jax.ShapeDtypeStruct(other.shape, BF16),
        grid_spec=pltpu.PrefetchScalarGridSpec(
            num_scalar_prefetch=1, grid=(N_CHIPS, nb),
            in_specs=[pl.BlockSpec((None, tr, cols), lambda s, i, place_ref: (s, place_ref[0] * nb + i, 0)),
                      pl.BlockSpec((None, tr, cols), lambda s, i, place_ref: (s, i, 0))],
            out_specs=pl.BlockSpec((None, tr, cols), lambda s, i, place_ref: (s, i, 0))),
        compiler_params=_params(("parallel", "parallel")),
    )(place, grad, other)


def _sum_chips(name, own, others, place):
    _, half, cols = own.shape
    tr = _row_tile(half, cols, mult=16)
    nb = half // tr

    def body(place_ref, own_ref, a_ref, b_ref, c_ref, out_ref):
        total = own_ref[...].astype(F32) + a_ref[...].astype(F32)
        out_ref[...] = (total + b_ref[...].astype(F32)) + c_ref[...].astype(F32)

    def peer(mask):
        return pl.BlockSpec((None, tr, cols), lambda i, place_ref: (place_ref[1] ^ mask, i, 0))

    return pl.pallas_call(
        body, name=name, out_shape=jax.ShapeDtypeStruct((2 * half, cols), F32),
        grid_spec=pltpu.PrefetchScalarGridSpec(
            num_scalar_prefetch=1, grid=(nb,),
            in_specs=[peer(0), peer(1), peer(2), peer(3)],
            out_specs=pl.BlockSpec((tr, cols), lambda i, place_ref: (place_ref[0] * nb + i, 0))),
        compiler_params=_params(("parallel",)),
    )(place, own, others, others, others)


def _adamw_math(w, g, m, v):
    m2 = ADAM_B1 * m + (1.0 - ADAM_B1) * g
    v2 = ADAM_B2 * v + (1.0 - ADAM_B2) * (g * g)
    m_hat = m2 / (1.0 - ADAM_B1 ** ADAM_STEP)
    v_hat = v2 / (1.0 - ADAM_B2 ** ADAM_STEP)
    delta = -ADAM_LR * (m_hat / (jnp.sqrt(v_hat) + ADAM_EPS) + ADAM_WD * w)
    return delta, m2, v2


def _adamw(name, w, g, m, v):
    rows, cols = w.shape
    tr = _row_tile(rows, cols)

    def body(w_ref, g_ref, m_ref, v_ref, d_ref, m2_ref, v2_ref):
        d_ref[...], m2_ref[...], v2_ref[...] = _adamw_math(w_ref[...], g_ref[...], m_ref[...], v_ref[...])

    blk = pl.BlockSpec((tr, cols), lambda i: (i, 0))
    return pl.pallas_call(
        body, name=name, out_shape=[jax.ShapeDtypeStruct(w.shape, F32)] * 3, grid=(rows // tr,),
        in_specs=[blk] * 4, out_specs=[blk] * 3,
        compiler_params=_params(("parallel",)),
    )(w, g, m, v)


def _small_update(gathered, w, m, v):
    def body(gs_ref, w_ref, m_ref, v_ref, g_ref, d_ref, m2_ref, v2_ref):
        g = gs_ref[0]
        for dev in range(1, N_DEV):
            g = g + gs_ref[dev]
        g_ref[...] = g
        d_ref[...], m2_ref[...], v2_ref[...] = _adamw_math(w_ref[...], g, m_ref[...], v_ref[...])

    return pl.pallas_call(
        body, name="small_update", out_shape=[jax.ShapeDtypeStruct(w.shape, F32)] * 4,
        compiler_params=_params(),
    )(gathered, w, m, v)


def _ada_mod(c_all, w_shard, b_shard):
    d, n = w_shard.shape
    tn = _pick(n, (512, 256, 128))

    def body(c_ref, w_ref, b_ref, o_ref):
        act = _silu(c_ref[...]).astype(BF16)
        o_ref[...] = jnp.dot(act, w_ref[...].astype(BF16), preferred_element_type=F32) + b_ref[...]

    return pl.pallas_call(
        body, name="ada_mod", out_shape=jax.ShapeDtypeStruct((c_all.shape[0], n), F32), grid=(n // tn,),
        in_specs=[pl.BlockSpec(c_all.shape, lambda j: (0, 0)), pl.BlockSpec((d, tn), lambda j: (0, j)),
                  pl.BlockSpec((1, tn), lambda j: (0, j))],
        out_specs=pl.BlockSpec((c_all.shape[0], tn), lambda j: (0, j)),
        compiler_params=_params(("parallel",)),
    )(c_all, w_shard, b_shard)


def _ada_grad(c_pad, dmod_pad):
    rows, d = c_pad.shape
    n = dmod_pad.shape[1]
    tn = _pick(n, (512, 256, 128))

    def body(c_ref, g_ref, o_ref):
        act = _silu(c_ref[...]).astype(BF16)
        o_ref[...] = lax.dot_general(act, g_ref[...].astype(BF16), TN_DIMS, preferred_element_type=F32)

    return pl.pallas_call(
        body, name="ada_grad", out_shape=jax.ShapeDtypeStruct((d, n), F32), grid=(n // tn,),
        in_specs=[pl.BlockSpec((rows, d), lambda j: (0, 0)), pl.BlockSpec((rows, tn), lambda j: (0, j))],
        out_specs=pl.BlockSpec((d, tn), lambda j: (0, j)),
        compiler_params=_params(("parallel",)),
    )(c_pad, dmod_pad)


WEIGHTS = ['w_ada', 'b_ada', 'norm1_gain', 'norm2_gain', 'w_in', 'mu_rkv', 'mu_w', 'mu_a', 'mu_g', 'w0', 'w1',
           'w2', 'a0', 'a1', 'a2', 'g1', 'g2', 'k_k', 'k_a', 'r_k', 'ln_x_gain', 'ln_x_bias', 'q_norm_gain',
           'k_norm_gain', 'w_out', 'w_gate_up', 'w_down']
SMALL = ['b_ada', 'norm1_gain', 'norm2_gain', 'mu_rkv', 'mu_w', 'mu_a', 'mu_g', 'w0', 'a0', 'k_k', 'k_a', 'r_k',
         'ln_x_gain', 'ln_x_bias', 'q_norm_gain', 'k_norm_gain']
PACK_ROWS = 8
LOSS_SLOT = LANES


def _shift_down(a):
    return jnp.pad(a[:-1], ((1, 0), (0, 0)))


def _shift_up(a):
    return jnp.pad(a[1:], ((0, 1), (0, 0)))


def _pack_small(vals):
    flat = jnp.concatenate([v.reshape(1, -1) for v in vals], axis=1)
    unit = PACK_ROWS * LANES
    total = -(-flat.shape[1] // unit) * unit
    flat = jnp.pad(flat, ((0, 0), (0, total - flat.shape[1])))
    return flat.reshape(PACK_ROWS, total // PACK_ROWS)


def kernel(x, c, w_ada, b_ada, norm1_gain, norm2_gain, w_in, mu_rkv, mu_w, mu_a, mu_g, w0, w1, w2, a0, a1, a2, g1, g2, k_k, k_a, r_k, ln_x_gain, ln_x_bias, q_norm_gain, k_norm_gain, w_out, w_gate_up, w_down, loss_target, m_w_ada, m_b_ada, m_norm1_gain, m_norm2_gain, m_w_in, m_mu_rkv, m_mu_w, m_mu_a, m_mu_g, m_w0, m_w1, m_w2, m_a0, m_a1, m_a2, m_g1, m_g2, m_k_k, m_k_a, m_r_k, m_ln_x_gain, m_ln_x_bias, m_q_norm_gain, m_k_norm_gain, m_w_out, m_w_gate_up, m_w_down, v_w_ada, v_b_ada, v_norm1_gain, v_norm2_gain, v_w_in, v_mu_rkv, v_mu_w, v_mu_a, v_mu_g, v_w0, v_w1, v_w2, v_a0, v_a1, v_a2, v_g1, v_g2, v_k_k, v_k_a, v_r_k, v_ln_x_gain, v_ln_x_bias, v_q_norm_gain, v_k_norm_gain, v_w_out, v_w_gate_up, v_w_down):
    given = dict(locals())
    wt = {n: given[n][0] for n in WEIGHTS}
    mom = {n: given["m_" + n][0] for n in WEIGHTS}
    var = {n: given["v_" + n][0] for n in WEIGHTS}
    for tree in (wt, mom, var):
        tree["b_ada"] = tree["b_ada"].reshape(1, -1)
        for n in SMALL[1:]:
            tree[n] = tree[n].reshape(1, -1)

    ax, ay, ac = _coords()
    chip = 2 * ax + ay
    dev = 4 * ax + 2 * ay + ac
    xs, target = x[0], loss_target[0]
    t, d = xs.shape
    dr = wt["w0"].shape[1]
    ds = d - dr
    nh = ds // HEAD_DIM
    dff = wt["w_down"].shape[0] * N_CHIPS
    n_ada = wt["w_ada"].shape[1]
    lw, la, lg = wt["w1"].shape[1], wt["a1"].shape[1], wt["g1"].shape[1]

    def lora_a(tree):
        return jnp.concatenate([tree["w1"], tree["a1"], tree["g1"]], axis=1)

    def lora_b(tree):
        return jnp.concatenate([tree["w2"], tree["a2"], tree["g2"]], axis=0)

    def with_own_slot(gathered, own):
        return [lax.dynamic_update_slice(full, shard[None], (chip, 0, 0)) for full, shard in zip(gathered, own)]

    first_shards = [s.astype(BF16) for s in (wt["w_in"], lora_a(wt), lora_b(wt))]
    later_shards = [s.astype(BF16) for s in (wt["w_out"], wt["w_gate_up"], wt["w_down"])]
    full_in, full_la, full_lb = with_own_slot(_gather_weights(first_shards), first_shards)
    full_la = full_la.reshape(d, lw + la + lg).astype(F32)
    full_lb = full_lb.transpose(1, 0, 2).reshape(lw + la + lg, dr).astype(F32)
    w1f, a1f, g1f = full_la[:, :lw], full_la[:, lw:lw + la], full_la[:, lw + la:]
    w2f, a2f, g2f = full_lb[:lw], full_lb[lw:lw + la], full_lb[lw + la:]

    c_all = _all_gather8("gather_c", c.reshape(PACK_ROWS, d // PACK_ROWS)).reshape(N_DEV, d)
    b_shard = lax.dynamic_slice(wt["b_ada"], (0, chip * n_ada), (1, n_ada))
    mod_part = _ada_mod(c_all, wt["w_ada"], b_shard)
    mod_all = _all_gather8("gather_mod", mod_part)[::2]
    mod = lax.dynamic_slice(mod_all, (0, dev, 0), (N_CHIPS, 1, n_ada)).reshape(1, N_CHIPS * n_ada)
    sh1, sc1, gt1, sh2, sc2, gt2 = [mod[:, i * d:(i + 1) * d] for i in range(6)]

    h, h_bf = _rowwise("norm1", _fn_norm1, [xs], [wt["norm1_gain"], sc1, sh1], [(d, F32), (d, BF16)], 256)
    hp = _shift_down(h)
    p = _matmul("mm_in", h_bf, full_in, b_shards=True, tm=512, tn=1536, tk=2048, n_outer=True)
    p_rkv, p_sb = (p, 3 * dr, 0), (p, 3 * ds, 1)
    pp = _shift_down(p[:, :3 * dr])
    pre_rows = [h, hp, p_rkv, pp]
    pre_params = [wt["mu_rkv"], wt["mu_w"], wt["mu_a"], wt["mu_g"], wt["w0"], wt["a0"], wt["k_k"], wt["k_a"],
                  w1f, w2f, a1f, a2f, g1f, g2f]
    pre = _rowwise("rwkv_pre", _fn_rwkv_pre, pre_rows, pre_params, [(dr, F32)] * 7, 64)
    r_, w_, k2, v_, rem, wr, g_ = pre
    y_raw, hist, s_last = _scan_fwd(rem, w_, wr, k2, r_, v_)
    post_rows = [y_raw, r_, k2, v_, g_]
    post_params = [wt["ln_x_gain"], wt["ln_x_bias"], wt["r_k"]]

    qg = jnp.tile(wt["q_norm_gain"], (1, nh))
    kg = jnp.tile(wt["k_norm_gain"], (1, nh))
    qn, kn, vs = _rowwise("qk_norm", _fn_qk_norm, [p_sb], [qg, kg], [(ds, BF16)] * 3, 256)

    def to_heads(a):
        return a.reshape(t, nh, HEAD_DIM).transpose(1, 0, 2)

    def from_heads(a):
        return a.transpose(1, 0, 2).reshape(t, ds)

    qh, kh, vh = to_heads(qn), to_heads(kn), to_heads(vs)
    o_h, lsum, later_full = _sb_fwd(qh, kh, vh, later_shards)
    full_out, full_gu, full_down = with_own_slot(later_full, later_shards)
    full_out = full_out.reshape(d, d)
    full_down = full_down.reshape(dff, d)
    (ycat,) = _rowwise("rwkv_post", _fn_rwkv_post_cat, post_rows + [from_heads(o_h)], post_params, [(d, BF16)], 256)
    mix = _matmul("mm_out", ycat, full_out, tm=512, tn=1024, tk=2048, n_outer=True)
    norm2_params = [gt1, wt["norm2_gain"], sc2, sh2]
    x1, h2 = _rowwise("mix_norm2", _fn_mix_norm2, [xs, mix], norm2_params, [(d, F32), (d, BF16)], 256)
    gu = _matmul("mm_gate_up", h2, full_gu, b_shards=True, tm=512, tn=1408, tk=2048, n_outer=True)
    gate_up = [(gu, dff, 0), (gu, dff, 1)]
    (act,) = _rowwise("swiglu", _fn_swiglu, gate_up, [], [(dff, BF16)], 256)
    dn = _matmul("mm_down", act, full_down, tm=1024, tn=1024, tk=1408)
    loss_vec, dout, ddn, dgt2 = _loss_head("loss_head", x1, dn, target, gt2)

    dact = _matmul("mm_down_dx", ddn, full_down, tb=True, tm=512, tn=1408, tk=2048, n_outer=True)
    gw_down = _matmul("mm_down_dw", act, ddn, ta=True, tm=1408, tn=1024, tk=512)
    dgu = _swiglu_bwd(gu, dact)
    dh2 = _matmul("mm_gate_up_dx", dgu, full_gu, tb=True, b_shards=True, tm=1024, tn=1024, tk=1408)
    gw_gu = _matmul("mm_gate_up_dw", h2, dgu, ta=True, out_shards=True, tm=1024, tn=1408, tk=512)
    (dx_a, dmix), (dgt1, dgain2, dsc2, dsh2) = _rowwise_bwd(
        "mix_norm2_bwd", _fn_mix_norm2, [xs, mix], norm2_params, [[dout], [dh2]], [F32, BF16], [True] * 4, 128)
    dycat = _matmul("mm_out_dx", dmix, full_out, tb=True, tm=512, tn=1024, tk=2048, n_outer=True)
    gw_out = _matmul("mm_out_dw", ycat, dmix, ta=True, tm=1024, tn=1024, tk=512)
    (dy_raw, dr_f, dk_f, dv_f, dg), (dlng, dlnb, drk) = _rowwise_bwd(
        "rwkv_post_bwd", _fn_rwkv_post, post_rows, post_params, [[(dycat, dr, 0)]], [F32] * 5, [True] * 3, 128)
    place = jnp.stack([ac, chip]).astype(jnp.int32)

    def pair_reduce(tag, names, grads):
        from_sibling = _pair_exchange("grad_pair_exchange_" + tag, grads)
        return [_pair_add("pair_add_" + n, g, o, place) for n, g, o in zip(names, grads, from_sibling)]

    early_names = ["w_out", "w_gate_up", "w_down"]
    early_sums = pair_reduce("early", early_names, [gw_out.reshape(N_CHIPS, d // N_CHIPS, d), gw_gu,
                                                    gw_down.reshape(N_CHIPS, dff // N_CHIPS, d)])
    do_h = to_heads(dycat[:, dr:])
    dqh, dkh, dvh, early_from_chips = _sb_bwd(qh, kh, vh, lsum, do_h, early_sums)
    drem_s, dw_s, dwr_s, dk_s, dr_s, dv_s = _scan_bwd(rem, w_, wr, k2, r_, v_, hist, s_last, dy_raw)
    (dp_sb,), (dqg, dkg) = _rowwise_bwd(
        "qk_norm_bwd", _fn_qk_norm, [p_sb], [qg, kg], [[from_heads(dqh)], [from_heads(dkh)], [from_heads(dvh)]],
        [F32], [True, True], 128)
    pre_cts = [[dr_s, dr_f], [dw_s], [dk_s, dk_f], [dv_s, dv_f], [drem_s], [dwr_s], [dg]]
    (dh_a, dhp, dp_rkv, dpp), pre_g = _rowwise_bwd(
        "rwkv_pre_bwd", _fn_rwkv_pre, pre_rows, pre_params, pre_cts, [F32] * 4, [True] * 14, 64)
    dp = jnp.concatenate([dp_rkv + _shift_up(dpp), dp_sb], axis=1).astype(BF16)
    dh_mm = _matmul("mm_in_dx", dp, full_in, tb=True, b_shards=True, tm=1024, tn=1024, tk=1536)
    gw_in = _matmul("mm_in_dw", h_bf, dp, ta=True, out_shards=True, tm=1024, tn=1536, tk=512)
    (grad_x,), (dgain1, dsc1, dsh1) = _rowwise_bwd(
        "norm1_bwd", _fn_norm1, [xs], [wt["norm1_gain"], sc1, sh1], [[dh_a, dh_mm, _shift_up(dhp)], []],
        [F32], [True] * 3, 128, add_to_first=[dx_a])

    g_mu_rkv, g_mu_w, g_mu_a, g_mu_g, g_w0, g_a0, g_kk, g_ka, gw1, gw2, ga1, ga2, gg1, gg2 = pre_g
    g_la = jnp.concatenate([gw1, ga1, gg1], axis=1).reshape(N_CHIPS, d // N_CHIPS, lw + la + lg)
    g_lb = jnp.concatenate([gw2, ga2, gg2], axis=0)
    g_lb = g_lb.reshape(lw + la + lg, N_CHIPS, dr // N_CHIPS).transpose(1, 0, 2)
    late_names = ["w_in", "lora_a", "lora_b"]
    late_sums = pair_reduce("late", late_names, [gw_in, g_la, g_lb])
    late_from_chips = _chip_exchange("grad_chip_exchange", late_sums)
    halves = [_sum_chips("chip_sum_" + n, p, q, place)
              for n, p, q in zip(early_names + late_names, early_sums + late_sums,
                                 list(early_from_chips) + list(late_from_chips))]
    r_out, r_gu, r_down, r_in, r_la, r_lb = _pair_share(halves)

    dmod = jnp.concatenate([dsh1, dsc1, dgt1, dsh2, dsc2, dgt2], axis=1)
    dqg = dqg.reshape(nh, HEAD_DIM).sum(axis=0, keepdims=True)
    dkg = dkg.reshape(nh, HEAD_DIM).sum(axis=0, keepdims=True)
    small_g = [dmod, dgain1, dgain2, g_mu_rkv, g_mu_w, g_mu_a, g_mu_g, g_w0, g_a0, g_kk, g_ka, drk, dlng, dlnb,
               dqg, dkg]
    lead = jnp.zeros((1, LOSS_SLOT), F32)
    packed = _pack_small([loss_vec] + small_g)
    gathered = _all_gather8("gather_small", packed)
    sm_g, sm_d, sm_m, sm_v = _small_update(gathered, _pack_small([lead] + [wt[n] for n in SMALL]),
                                           _pack_small([lead] + [mom[n] for n in SMALL]),
                                           _pack_small([lead] + [var[n] for n in SMALL]))
    loss = sm_g.reshape(-1)[0]

    def unpack(packed_arr):
        flat, out, pos = packed_arr.reshape(-1), {}, LOSS_SLOT
        for n in SMALL:
            size = wt[n].size
            out[n] = flat[pos:pos + size]
            pos += size
        return out

    res = {"grad": unpack(sm_g), "delta": unpack(sm_d), "m": unpack(sm_m), "v": unpack(sm_v)}

    dmod_all = gathered.reshape(N_DEV, -1)[:, LOSS_SLOT:LOSS_SLOT + N_CHIPS * n_ada]
    dmod_cols = lax.dynamic_slice(dmod_all, (0, chip * n_ada), (N_DEV, n_ada))
    pad8 = ((0, N_DEV), (0, 0))
    res["grad"]["w_ada"] = _ada_grad(jnp.pad(c_all, pad8), jnp.pad(dmod_cols, pad8))

    res["grad"].update(w_in=r_in, w_out=r_out, w_gate_up=r_gu, w_down=r_down)
    for n in ("w_ada", "w_in", "w_out", "w_gate_up", "w_down"):
        res["delta"][n], res["m"][n], res["v"][n] = _adamw("adamw_" + n, wt[n], res["grad"][n], mom[n], var[n])
    la_d, la_m, la_v = _adamw("adamw_lora_a", lora_a(wt), r_la, lora_a(mom), lora_a(var))
    lb_d, lb_m, lb_v = _adamw("adamw_lora_b", lora_b(wt), r_lb, lora_b(mom), lora_b(var))
    for key, pa, pb in (("grad", r_la, r_lb), ("delta", la_d, lb_d), ("m", la_m, lb_m), ("v", la_v, lb_v)):
        res[key].update(w1=pa[:, :lw], a1=pa[:, lw:lw + la], g1=pa[:, lw + la:],
                        w2=pb[:lw], a2=pb[lw:lw + la], g2=pb[lw + la:])

    outs = [loss, grad_x[None]]
    for key in ("grad", "delta", "m", "v"):
        outs += [res[key][n].reshape(given[n].shape) for n in WEIGHTS]
    return tuple(outs)
```

```python
import functools
import math

import jax
import jax.numpy as jnp
from jax import lax
from jax.experimental import pallas as pl
from jax.experimental.pallas import tpu as pltpu

F32 = jnp.float32
BF16 = jnp.bfloat16
HEAD_DIM = 64
LANES = 128
RMS_EPS = 1e-6
GN_EPS = 64e-5
L2_EPS = 1e-12
ADAM_LR, ADAM_B1, ADAM_B2, ADAM_EPS, ADAM_WD, ADAM_STEP = 0.001, 0.9, 0.999, 1e-08, 0.01, 10
VMEM_LIMIT = 56 * 1024 * 1024
MESH = pl.DeviceIdType.MESH
HI = lax.Precision.HIGHEST
N_CHIPS = 4
N_DEV = 8
XY_MASKS = ((1, 0), (0, 1), (1, 1))


def _pick(dim, prefs):
    for p in prefs:
        if dim % p == 0:
            return p
    return dim


def _params(sem=None, vmem=VMEM_LIMIT):
    return pltpu.CompilerParams(dimension_semantics=sem, vmem_limit_bytes=vmem)


def _sigmoid(x):
    return 1.0 / (1.0 + jnp.exp(-x))


@jax.custom_vjp
def _softplus(x):
    return jnp.maximum(x, 0.0) + jnp.log(1.0 + jnp.exp(-jnp.abs(x)))


_softplus.defvjp(lambda x: (_softplus(x), x), lambda x, g: (g * _sigmoid(x),))


def _silu(x):
    return x * _sigmoid(x)


@jax.custom_vjp
def _bdot(a, b):
    return jnp.dot(a.astype(BF16), b.astype(BF16), preferred_element_type=F32)


def _bdot_bwd(res, g):
    a, b = res
    gb = g.astype(BF16)
    da = lax.dot_general(gb, b.astype(BF16), (((1,), (1,)), ((), ())), preferred_element_type=F32)
    db = lax.dot_general(a.astype(BF16), gb, (((0,), (0,)), ((), ())), preferred_element_type=F32)
    return da.astype(a.dtype), db.astype(b.dtype)


_bdot.defvjp(lambda a, b: (_bdot(a, b), (a, b)), _bdot_bwd)


def _head_ones():
    i = lax.broadcasted_iota(jnp.int32, (LANES, LANES), 0) // HEAD_DIM
    j = lax.broadcasted_iota(jnp.int32, (LANES, LANES), 1) // HEAD_DIM
    return (i == j).astype(F32)


def _hdot(x, ones_bf):
    hi = x.astype(BF16)
    rest = x - hi.astype(F32)
    mid = rest.astype(BF16)
    lo = (rest - mid.astype(F32)).astype(BF16)
    out = jnp.dot(hi, ones_bf, preferred_element_type=F32)
    out += jnp.dot(mid, ones_bf, preferred_element_type=F32)
    return out + jnp.dot(lo, ones_bf, preferred_element_type=F32)


@jax.custom_vjp
def _segsum(x):
    ones = _head_ones().astype(BF16)
    parts = [_hdot(x[:, LANES * j:LANES * (j + 1)], ones) for j in range(x.shape[1] // LANES)]
    return parts[0] if len(parts) == 1 else jnp.concatenate(parts, axis=1)


_segsum.defvjp(lambda x: (_segsum(x), None), lambda _, g: (_segsum(g),))


def _rms(x, gain):
    return x * lax.rsqrt(jnp.mean(x * x, axis=-1, keepdims=True) + RMS_EPS) * gain


def _matmul(name, a, b, *, ta=False, tb=False, b_shards=False, out_shards=False, out_dtype=F32,
            tm=512, tn=512, tk=512, n_outer=False):
    if ta:
        kdim, m = a.shape
    else:
        m, kdim = a.shape
    if b_shards:
        if tb:
            n, ks = b.shape[1], b.shape[2]
            assert ks * N_CHIPS == kdim
        else:
            ns = b.shape[2]
            n = ns * N_CHIPS
            assert b.shape[1] == kdim
    else:
        n = b.shape[0] if tb else b.shape[1]
    tm = _pick(m, (tm, 512, 256, 128))
    n_part = n // N_CHIPS if (out_shards or (b_shards and not tb)) else n
    tn = _pick(n_part, (tn, 512, 256, 128))
    k_part = kdim // N_CHIPS if (b_shards and tb) else kdim
    tk = _pick(k_part, (tk, 512, 256, 128))
    nb = n_part // tn
    kb = k_part // tk
    nk = kdim // tk
    grid = (n // tn, m // tm, nk) if n_outer else (m // tm, n // tn, nk)

    def spec(shape, index):
        if n_outer:
            return pl.BlockSpec(shape, lambda j, i, k: index(i, j, k))
        return pl.BlockSpec(shape, index)

    if ta:
        a_spec = spec((tk, tm), lambda i, j, k: (k, i))
    else:
        a_spec = spec((tm, tk), lambda i, j, k: (i, k))
    if b_shards and tb:
        b_spec = spec((None, tn, tk), lambda i, j, k: (k // kb, j, k % kb))
    elif b_shards:
        b_spec = spec((None, tk, tn), lambda i, j, k: (j // nb, k, j % nb))
    elif tb:
        b_spec = spec((tn, tk), lambda i, j, k: (j, k))
    else:
        b_spec = spec((tk, tn), lambda i, j, k: (k, j))
    if out_shards:
        o_spec = spec((None, tm, tn), lambda i, j, k: (j // nb, i, j % nb))
        o_shape = jax.ShapeDtypeStruct((N_CHIPS, m, n_part), out_dtype)
    else:
        o_spec = spec((tm, tn), lambda i, j, k: (i, j))
        o_shape = jax.ShapeDtypeStruct((m, n), out_dtype)
    dims = (((0 if ta else 1,), (1 if tb else 0,)), ((), ()))

    def body(a_ref, b_ref, o_ref, acc_ref):
        k = pl.program_id(2)

        @pl.when(k == 0)
        def _():
            acc_ref[...] = jnp.zeros_like(acc_ref)

        acc_ref[...] += lax.dot_general(a_ref[...].astype(BF16), b_ref[...].astype(BF16), dims,
                                        preferred_element_type=F32)

        @pl.when(k == nk - 1)
        def _():
            o_ref[...] = acc_ref[...].astype(o_ref.dtype)

    return pl.pallas_call(
        body, name=name, grid=grid, in_specs=[a_spec, b_spec], out_specs=o_spec, out_shape=o_shape,
        scratch_shapes=[pltpu.VMEM((tm, tn), F32)],
        compiler_params=_params(("parallel", "parallel", "arbitrary")),
    )(a, b)


def _row_in(spec, tile):
    if isinstance(spec, tuple):
        arr, width, cb = spec
    else:
        arr, width, cb = spec, spec.shape[1], 0
    return arr, pl.BlockSpec((tile, width), lambda i, cb=cb: (i, cb))


def _full_spec(arr):
    nd = arr.ndim
    return pl.BlockSpec(arr.shape, lambda i, nd=nd: (0,) * nd, pipeline_mode=pl.Buffered(1))


def _rowwise(name, fn, rows, params, outs, tile):
    t = (rows[0][0] if isinstance(rows[0], tuple) else rows[0]).shape[0]
    tile = _pick(t, (tile,))
    arrs, specs = zip(*[_row_in(s, tile) for s in rows])
    nr, npar = len(rows), len(params)

    def body(*refs):
        rv = [r[...].astype(F32) for r in refs[:nr]]
        pv = [p[...] for p in refs[nr:nr + npar]]
        res = fn(*rv, *pv)
        for o_ref, val in zip(refs[nr + npar:], res):
            o_ref[...] = val.astype(o_ref.dtype)

    return pl.pallas_call(
        body, name=name, grid=(t // tile,),
        in_specs=list(specs) + [_full_spec(p) for p in params],
        out_specs=[pl.BlockSpec((tile, w), lambda i: (i, 0)) for w, _ in outs],
        out_shape=[jax.ShapeDtypeStruct((t, w), d) for w, d in outs],
        compiler_params=_params(("parallel",)),
    )(*arrs, *params)


def _rowwise_bwd(name, fn, rows, params, cts, row_grads, param_grads, tile, add_to_first=()):
    t = (rows[0][0] if isinstance(rows[0], tuple) else rows[0]).shape[0]
    tile = _pick(t, (tile,))
    arrs, specs = zip(*[_row_in(s, tile) for s in rows])
    n_add = len(add_to_first)
    flat_cts = [c for group in cts for c in group] + list(add_to_first)
    c_arrs, c_specs = zip(*[_row_in(s, tile) for s in flat_cts])
    nr, npar, nc = len(rows), len(params), len(flat_cts)
    rg_idx = [i for i, d in enumerate(row_grads) if d is not None]
    pg_idx = [i for i, d in enumerate(param_grads) if d]

    def body(*refs):
        rv = [r[...].astype(F32) for r in refs[:nr]]
        pv = [p[...] for p in refs[nr:nr + npar]]
        cv = [c[...].astype(F32) for c in refs[nr + npar:nr + npar + nc]]
        o_refs = refs[nr + npar + nc:]
        outs, vjp = jax.vjp(fn, *rv, *pv)
        ct, pos = [], 0
        for group, o in zip(cts, outs):
            if group:
                acc = cv[pos]
                for extra in cv[pos + 1:pos + len(group)]:
                    acc = acc + extra
                pos += len(group)
            else:
                acc = jnp.zeros_like(o)
            ct.append(acc)
        grads = list(vjp(tuple(ct)))
        for extra in cv[nc - n_add:]:
            grads[rg_idx[0]] = grads[rg_idx[0]] + extra
        for o_ref, i in zip(o_refs[:len(rg_idx)], rg_idx):
            o_ref[...] = grads[i].astype(o_ref.dtype)
        first = pl.program_id(0) == 0
        for o_ref, i in zip(o_refs[len(rg_idx):], pg_idx):
            g = grads[nr + i].astype(F32)

            @pl.when(first)
            def _(o_ref=o_ref, g=g):
                o_ref[...] = g

            @pl.when(jnp.logical_not(first))
            def _(o_ref=o_ref, g=g):
                o_ref[...] += g

    def width(i):
        s = rows[i]
        return s[1] if isinstance(s, tuple) else s.shape[1]

    out_specs = [pl.BlockSpec((tile, width(i)), lambda i_: (i_, 0)) for i in rg_idx]
    out_shape = [jax.ShapeDtypeStruct((t, width(i)), row_grads[i]) for i in rg_idx]
    out_specs += [_full_spec(params[i]) for i in pg_idx]
    out_shape += [jax.ShapeDtypeStruct(params[i].shape, F32) for i in pg_idx]
    res = pl.pallas_call(
        body, name=name, grid=(t // tile,),
        in_specs=list(specs) + [_full_spec(p) for p in params] + list(c_specs),
        out_specs=out_specs, out_shape=out_shape,
        compiler_params=_params(("arbitrary",)),
    )(*arrs, *params, *c_arrs)
    return res[:len(rg_idx)], res[len(rg_idx):]


def _fn_norm1(x, gain, sc, sh):
    h = _rms(x, gain) * (1.0 + sc) + sh
    return h, h


def _fn_rwkv_pre(h, hp, p, pp, mu_rkv, mu_w, mu_a, mu_g, w0, a0, k_k, k_a, w1, w2, a1, a2, g1, g2):
    d = p.shape[1] // 3
    dh = hp - h
    xw = h + dh * mu_w
    xa = h + dh * mu_a
    xg = h + dh * mu_g
    pr = p + (pp - p) * mu_rkv
    r, k, v = pr[:, :d], pr[:, d:2 * d], pr[:, 2 * d:]
    w_log = -_softplus(-(w0 + _bdot(jnp.tanh(_bdot(xw, w1)), w2))) - 0.5
    decay = jnp.exp(-jnp.exp(w_log))
    a = _sigmoid(a0 + _bdot(_bdot(xa, a1), a2))
    g = _bdot(_sigmoid(_bdot(xg, g1)), g2)
    kk = k * k_k
    kk = kk * lax.rsqrt(_segsum(kk * kk) + L2_EPS)
    k2 = k * (1.0 + (a - 1.0) * k_a)
    return r, decay, k2, v, -kk, kk * a, g


def _fn_rwkv_post(y, r, k2, v, g, ln_g, ln_b, r_k):
    inv = 1.0 / HEAD_DIM
    mean = _segsum(y) * inv
    yc = y - mean
    var = _segsum(yc * yc) * inv
    yn = yc * lax.rsqrt(var + GN_EPS) * ln_g + ln_b
    bonus = _segsum(r * k2 * r_k) * v
    return ((yn + bonus) * g,)


def _fn_rwkv_post_cat(y, r, k2, v, g, o_sb, ln_g, ln_b, r_k):
    return (jnp.concatenate([_fn_rwkv_post(y, r, k2, v, g, ln_g, ln_b, r_k)[0], o_sb], axis=1),)


def _fn_qk_norm(p, qg, kg):
    d = p.shape[1] // 3
    q, k, v = p[:, :d], p[:, d:2 * d], p[:, 2 * d:]
    inv = 1.0 / HEAD_DIM
    qn = q * lax.rsqrt(_segsum(q * q) * inv + RMS_EPS) * qg
    kn = k * lax.rsqrt(_segsum(k * k) * inv + RMS_EPS) * kg
    return qn, kn, v


def _fn_mix_norm2(x, mix, gt1, gain, sc, sh):
    x1 = x + gt1 * mix
    h2 = _rms(x1, gain) * (1.0 + sc) + sh
    return x1, h2


def _fn_swiglu(gate, up):
    return (_silu(gate) * up,)


def _swiglu_bwd(gu, dact, tile=128):
    t, dff = dact.shape
    tile = _pick(t, (tile,))

    def body(gate_ref, up_ref, d_ref, o_ref):
        _, vjp = jax.vjp(_fn_swiglu, gate_ref[...], up_ref[...])
        dgate, dup = vjp((d_ref[...],))
        o_ref[:, :dff] = dgate.astype(o_ref.dtype)
        o_ref[:, dff:] = dup.astype(o_ref.dtype)

    return pl.pallas_call(
        body, name="swiglu_bwd", grid=(t // tile,),
        in_specs=[pl.BlockSpec((tile, dff), lambda i: (i, 0)), pl.BlockSpec((tile, dff), lambda i: (i, 1)),
                  pl.BlockSpec((tile, dff), lambda i: (i, 0))],
        out_specs=pl.BlockSpec((tile, 2 * dff), lambda i: (i, 0)),
        out_shape=jax.ShapeDtypeStruct((t, 2 * dff), BF16),
        compiler_params=_params(("parallel",)),
    )(gu, gu, dact)


def _loss_head(name, x1, dn, target, gt2, tile=256):
    t, d = x1.shape
    tile = _pick(t, (tile,))

    def body(x1_ref, dn_ref, tg_ref, gt_ref, loss_ref, dout_ref, ddn_ref, dgt_ref):
        dnv = dn_ref[...]
        gt = gt_ref[...]
        err = x1_ref[...] + gt * dnv - tg_ref[...]
        dout = err * (1.0 / d)
        dout_ref[...] = dout
        ddn_ref[...] = (dout * gt).astype(ddn_ref.dtype)
        part = 0.5 * jnp.sum(jnp.sum(err * dout, axis=-1, keepdims=True), axis=0, keepdims=True)
        dgt = jnp.sum(dout * dnv, axis=0, keepdims=True)
        first = pl.program_id(0) == 0

        @pl.when(first)
        def _():
            loss_ref[...] = jnp.broadcast_to(part, loss_ref.shape)
            dgt_ref[...] = dgt

        @pl.when(jnp.logical_not(first))
        def _():
            loss_ref[...] += jnp.broadcast_to(part, loss_ref.shape)
            dgt_ref[...] += dgt

    row = pl.BlockSpec((tile, d), lambda i: (i, 0))
    vec = pl.BlockSpec((1, d), lambda i: (0, 0))
    return pl.pallas_call(
        body, name=name, grid=(t // tile,),
        in_specs=[row, row, row, vec],
        out_specs=[pl.BlockSpec((1, LANES), lambda i: (0, 0)), row, row, vec],
        out_shape=[jax.ShapeDtypeStruct((1, LANES), F32), jax.ShapeDtypeStruct((t, d), F32),
                   jax.ShapeDtypeStruct((t, d), BF16), jax.ShapeDtypeStruct((1, d), F32)],
        compiler_params=_params(("arbitrary",)),
    )(x1, dn, target, gt2)


SCAN_BLOCK = 32
N_COL = 5
WIDE = 2 * LANES


def _wide_eye():
    i = lax.broadcasted_iota(jnp.int32, (HEAD_DIM, WIDE), 0)
    j = lax.broadcasted_iota(jnp.int32, (HEAD_DIM, WIDE), 1) % HEAD_DIM
    return (i == j).astype(BF16)


def _wide_ones():
    i = lax.broadcasted_iota(jnp.int32, (WIDE, WIDE), 0) // HEAD_DIM
    j = lax.broadcasted_iota(jnp.int32, (WIDE, WIDE), 1) // HEAD_DIM
    return (i == j).astype(BF16)


COL_PIECES = (1, 3, 2, 2, 1)


def _col_tiles(refs, i, nq, eye, ones_bf):
    pieces = [[], [], []]
    for ref, n_pieces in zip(refs, COL_PIECES):
        full = ref[pl.ds(i, 1), :]
        for q in range(nq):
            rest = full[:, q * WIDE:(q + 1) * WIDE]
            for level in range(n_pieces):
                part = rest.astype(BF16)
                pieces[level].append(part * eye)
                rest = rest - part.astype(F32)
    tile = nq * HEAD_DIM
    out = jnp.dot(jnp.concatenate(pieces[0] + pieces[1] + pieces[2], axis=0), ones_bf, preferred_element_type=F32)
    first, second, third = out[:5 * tile], out[5 * tile:8 * tile], out[8 * tile:]
    refined = first[tile:4 * tile] + second
    parts = [first[:tile], refined[:tile] + third, refined[tile:], first[4 * tile:]]
    return jnp.concatenate(parts, axis=0).reshape(N_COL * nq, HEAD_DIM, WIDE)


def _bf16_round(x):
    bits = lax.bitcast_convert_type(x, jnp.uint32)
    bits = (bits + jnp.uint32(0x7FFF) + ((bits >> 16) & jnp.uint32(1))) & jnp.uint32(0xFFFF0000)
    return lax.bitcast_convert_type(bits, F32)


def _pair_tile(tiles_ref, n, p, nq):
    return tiles_ref[n * nq + p // 2, :, (p % 2) * LANES:(p % 2 + 1) * LANES]


def _scan_fwd(rem, w, wr, k, r, v):
    t, dr = v.shape
    npair, nq = dr // LANES, dr // WIDE
    tb = _pick(t, (SCAN_BLOCK,))

    def body(rem_ref, w_ref, wr_ref, k_ref, r_ref, v_ref, *rest):
        next_refs, (y_ref, hist_ref, last_ref, s_ref, sb_ref, tiles_a, tiles_b) = rest[:N_COL], rest[N_COL:]
        eye, ones_bf = _wide_eye(), _wide_ones()
        col_refs = (rem_ref, w_ref, wr_ref, k_ref, r_ref)

        @pl.when(pl.program_id(0) == 0)
        def _():
            s_ref[...] = jnp.zeros_like(s_ref)
            sb_ref[...] = jnp.zeros_like(sb_ref)
            tiles_a[...] = _col_tiles(col_refs, 0, nq, eye, ones_bf)

        def step(i, tiles_ref):
            v_full, y_rows = v_ref[pl.ds(i, 1), :], []
            for p in range(npair):
                s = s_ref[p]
                hist_ref[i, p] = s
                c_rem, c_w, c_wr, c_k, c_r = [_pair_tile(tiles_ref, n, p, nq) for n in range(N_COL)]
                sa = jnp.sum(sb_ref[p] * c_rem, axis=0, keepdims=True)
                s2 = s * c_w + c_wr * sa + c_k * v_full[:, p * LANES:(p + 1) * LANES]
                s2_b = _bf16_round(s2)
                y_rows.append(jnp.sum(s2_b * c_r, axis=0, keepdims=True))
                s_ref[p] = s2
                sb_ref[p] = s2_b
            y_ref[pl.ds(i, 1), :] = jnp.concatenate(y_rows, axis=1)

        def two_steps(i, ahead_refs, ahead_row):
            tiles_b[...] = _col_tiles(col_refs, i + 1, nq, eye, ones_bf)
            step(i, tiles_a)
            tiles_a[...] = _col_tiles(ahead_refs, ahead_row, nq, eye, ones_bf)
            step(i + 1, tiles_b)

        def loop_body(m, carry):
            two_steps(2 * m, col_refs, 2 * m + 2)
            return carry

        lax.fori_loop(0, tb // 2 - 1, loop_body, 0, unroll=2)
        two_steps(tb - 2, next_refs, 0)
        last_ref[...] = sb_ref[...]

    nblk = t // tb
    blk = pl.BlockSpec((tb, dr), lambda i: (i, 0))
    nxt = pl.BlockSpec((8, dr), lambda i: (jnp.minimum(i + 1, nblk - 1) * (tb // 8), 0))
    tiles = pltpu.VMEM((N_COL * nq, HEAD_DIM, WIDE), F32)
    state = pltpu.VMEM((npair, HEAD_DIM, LANES), F32)
    return pl.pallas_call(
        body, name="rwkv_scan_fwd", grid=(nblk,),
        in_specs=[blk] * 6 + [nxt] * N_COL,
        out_specs=[blk, pl.BlockSpec((tb, npair, HEAD_DIM, LANES), lambda i: (i, 0, 0, 0)),
                   pl.BlockSpec((npair, HEAD_DIM, LANES), lambda i: (0, 0, 0))],
        out_shape=[jax.ShapeDtypeStruct((t, dr), F32), jax.ShapeDtypeStruct((t, npair, HEAD_DIM, LANES), F32),
                   jax.ShapeDtypeStruct((npair, HEAD_DIM, LANES), F32)],
        scratch_shapes=[state, state, tiles, tiles],
        compiler_params=_params(("arbitrary",)),
    )(rem, w, wr, k, r, v, rem, w, wr, k, r)


def _scan_bwd(rem, w, wr, k, r, v, hist, last, dy):
    t, dr = v.shape
    npair, nq = dr // LANES, dr // WIDE
    tb = _pick(t, (SCAN_BLOCK,))
    nblk = t // tb

    def body(rem_ref, w_ref, wr_ref, k_ref, r_ref, v_ref, hist_ref, last_ref, dy_ref, *rest):
        prev_refs, out_refs = rest[:N_COL], rest[N_COL:2 * N_COL]
        dv_ref, ds_ref, next_ref, tiles_a, tiles_b = rest[2 * N_COL:]
        eye, ones_bf = _wide_eye(), _wide_ones()
        col_refs = (rem_ref, w_ref, wr_ref, k_ref, r_ref)

        @pl.when(pl.program_id(0) == 0)
        def _():
            ds_ref[...] = jnp.zeros_like(ds_ref)
            next_ref[...] = last_ref[...]
            tiles_a[...] = _col_tiles(col_refs, tb - 1, nq, eye, ones_bf)

        def step(i, tiles_ref):
            grads = [[None] * npair for _ in range(N_COL)]
            v_full, dy_full, dv_rows = v_ref[pl.ds(i, 1), :], dy_ref[pl.ds(i, 1), :], []
            for p in range(npair):
                lanes = slice(p * LANES, (p + 1) * LANES)
                s = hist_ref[i, p]
                c_rem, c_w, c_wr, c_k, c_r = [_pair_tile(tiles_ref, n, p, nq) for n in range(N_COL)]
                v_row, dy_row = v_full[:, lanes], dy_full[:, lanes]
                s_b = _bf16_round(s)
                s2_b = next_ref[p]
                next_ref[p] = s_b
                sa = jnp.sum(s_b * c_rem, axis=0, keepdims=True)
                d2 = ds_ref[p] + c_r * dy_row
                dsa = jnp.sum(d2 * c_wr, axis=0, keepdims=True)
                dv_rows.append(jnp.sum(d2 * c_k, axis=0, keepdims=True))
                ds_ref[p] = d2 * c_w + c_rem * dsa
                for n, tile in enumerate((s_b * dsa, d2 * s, d2 * sa, d2 * v_row, s2_b * dy_row)):
                    grads[n][p] = tile.astype(BF16)
            wide = [jnp.concatenate(grads[n][2 * q:2 * q + 2], axis=1) for n in range(N_COL) for q in range(nq)]
            sums = jnp.dot(jnp.concatenate(wide, axis=0), ones_bf, preferred_element_type=F32)
            sums = sums.reshape(N_COL * nq, HEAD_DIM, WIDE)
            rows = jnp.sum(sums * eye.astype(F32)[None], axis=1)
            dv_ref[pl.ds(i, 1), :] = jnp.concatenate(dv_rows, axis=1)
            for n, o_ref in enumerate(out_refs):
                o_ref[pl.ds(i, 1), :] = jnp.concatenate([rows[n * nq + q:n * nq + q + 1] for q in range(nq)], axis=1)

        def two_steps(i, ahead_refs, ahead_row):
            tiles_b[...] = _col_tiles(col_refs, i - 1, nq, eye, ones_bf)
            step(i, tiles_a)
            tiles_a[...] = _col_tiles(ahead_refs, ahead_row, nq, eye, ones_bf)
            step(i - 1, tiles_b)

        def loop_body(m, carry):
            i = tb - 1 - 2 * m
            two_steps(i, col_refs, i - 2)
            return carry

        lax.fori_loop(0, tb // 2 - 1, loop_body, 0, unroll=2)
        two_steps(1, prev_refs, 7)

    blk = pl.BlockSpec((tb, dr), lambda i: (nblk - 1 - i, 0))
    prv = pl.BlockSpec((8, dr), lambda i: (jnp.maximum((nblk - 1 - i) * (tb // 8) - 1, 0), 0))
    tiles = pltpu.VMEM((N_COL * nq, HEAD_DIM, WIDE), F32)
    state = pltpu.VMEM((npair, HEAD_DIM, LANES), F32)
    return pl.pallas_call(
        body, name="rwkv_scan_bwd", grid=(nblk,),
        in_specs=[blk] * 6 + [pl.BlockSpec((tb, npair, HEAD_DIM, LANES), lambda i: (nblk - 1 - i, 0, 0, 0)),
                              pl.BlockSpec((npair, HEAD_DIM, LANES), lambda i: (0, 0, 0)), blk] + [prv] * N_COL,
        out_specs=[blk] * 6,
        out_shape=[jax.ShapeDtypeStruct((t, dr), F32)] * 6,
        scratch_shapes=[state, state, tiles, tiles],
        compiler_params=_params(("arbitrary",)),
    )(rem, w, wr, k, r, v, hist, last, dy, rem, w, wr, k, r)


SB_BLOCK = 256
SB_HEADS = 8
SB_HEADS_BWD = 4
NT_DIMS = (((1,), (1,)), ((), ()))
TN_DIMS = (((0,), (0,)), ((), ()))
SB_SCALE = 1.0 / math.sqrt(HEAD_DIM)


def _dot2(x, tri):
    hi = x.astype(BF16)
    mid = (x - hi.astype(F32)).astype(BF16)
    return jnp.dot(hi, tri, preferred_element_type=F32) + jnp.dot(mid, tri, preferred_element_type=F32)


def _sb_block_iotas(bs):
    return lax.broadcasted_iota(jnp.int32, (bs, bs), 0), lax.broadcasted_iota(jnp.int32, (bs, bs), 1)


def _when_step(group, block):
    return pl.when(jnp.logical_and(pl.program_id(0) == group, pl.program_id(1) == block))


def _sb_fwd(q, k, v, shards):
    h, t, d = q.shape
    bs = _pick(t, (SB_BLOCK,))
    nh = _pick(h, (SB_HEADS,))
    ngroup, nblock = h // nh, t // bs
    nw = len(shards)
    heights = [s.shape[0] for s in shards]

    def body(q_ref, k_ref, v_ref, *rest):
        w_in, (o_ref, l_ref), w_out, sems = rest[:nw], rest[nw:nw + 2], rest[nw + 2:2 * nw + 2], rest[2 * nw + 2:]

        @_when_step(0, 0)
        def _():
            _gather_phase("issue", w_in, w_out, sems, heights)

        @_when_step(ngroup - 1, (2 * nblock) // 3)
        def _():
            _gather_phase("forward", w_in, w_out, sems, heights)

        qi = pl.program_id(1)
        ri, ci = _sb_block_iotas(bs)
        tri_ge = (ri >= ci).astype(BF16)
        causal = ci < ri
        qv = [q_ref[hh] for hh in range(nh)]

        def blocks(j, masked, carry):
            accs, tails = carry[:nh], carry[nh:]
            heads = range(nh)
            rows = pl.ds(pl.multiple_of(j * bs, bs), bs)
            z = [lax.dot_general(qv[hh], k_ref[hh, rows, :], NT_DIMS, preferred_element_type=F32) * SB_SCALE
                 for hh in heads]
            log1m = [-_softplus(z[hh]) for hh in heads]
            if masked:
                log1m = [jnp.where(causal, x, 0.0) for x in log1m]
            cs = [_dot2(log1m[hh], tri_ge) for hh in heads]
            a = [jnp.exp(z[hh] + cs[hh] + tails[hh]) for hh in heads]
            if masked:
                a = [jnp.where(causal, x, 0.0) for x in a]
            accs = [accs[hh] + jnp.dot(a[hh].astype(BF16), v_ref[hh, rows, :], preferred_element_type=F32)
                    for hh in heads]
            return tuple(accs) + tuple(tails[hh] + cs[hh][:, 0:1] for hh in heads)

        carry = blocks(qi, True, (jnp.zeros((bs, d), F32),) * nh + (jnp.zeros((bs, 1), F32),) * nh)
        carry = lax.fori_loop(0, qi, lambda n, c: blocks(qi - 1 - n, False, c), carry)
        for hh in range(nh):
            o_ref[hh] = carry[hh]
            l_ref[hh] = jnp.broadcast_to(carry[nh + hh], (bs, d))

        @_when_step(ngroup - 1, nblock - 1)
        def _():
            _gather_phase("finish", w_in, w_out, sems, heights)

    qs = pl.BlockSpec((nh, bs, d), lambda hh, i: (hh, i, 0))
    ks = pl.BlockSpec((nh, t, d), lambda hh, i: (hh, 0, 0), pipeline_mode=pl.Buffered(1))
    res = pl.pallas_call(
        body, name="sb_attn_fwd", grid=(ngroup, nblock),
        in_specs=[qs, ks, ks] + [ANY] * nw, out_specs=[qs, qs] + [ANY] * nw,
        out_shape=[jax.ShapeDtypeStruct((h, t, d), F32)] * 2 + _gather_out_shapes(shards),
        scratch_shapes=_gather_sems(nw),
        compiler_params=_params(("arbitrary", "arbitrary")),
    )(q, k, v, *shards)
    return res[0], res[1], res[2:]


def _sb_bwd(q, k, v, lsum, do, parts):
    h, t, d = q.shape
    bs = _pick(t, (SB_BLOCK,))
    nh = _pick(h, (SB_HEADS_BWD,))
    ngroup, nblock = h // nh, t // bs
    nw = len(parts)

    def body(q_ref, k_ref, v_ref, l_ref, do_ref, *rest):
        p_in, (dq_ref, dk_ref, dv_ref), p_out, sems = rest[:nw], rest[nw:nw + 3], rest[nw + 3:2 * nw + 3], rest[2 * nw + 3:]

        @_when_step(0, 0)
        def _():
            _exchange_phase("issue", p_in, p_out, sems)

        qi = pl.program_id(1)

        @pl.when(qi == 0)
        def _():
            dk_ref[...] = jnp.zeros_like(dk_ref)
            dv_ref[...] = jnp.zeros_like(dv_ref)

        ri, ci = _sb_block_iotas(bs)
        tri_lt = (ri < ci).astype(BF16)
        causal = ci < ri
        qv = [q_ref[hh] for hh in range(nh)]
        dob = [do_ref[hh].astype(BF16) for hh in range(nh)]
        ltot = [l_ref[hh][:, 0:1] for hh in range(nh)]

        def blocks(j, masked, carry):
            dq, pc, ec = carry[:nh], carry[nh:2 * nh], carry[2 * nh:]
            heads = range(nh)
            rows = pl.ds(pl.multiple_of(j * bs, bs), bs)
            z = [lax.dot_general(qv[hh], k_ref[hh, rows, :], NT_DIMS, preferred_element_type=F32) * SB_SCALE
                 for hh in heads]
            da = [lax.dot_general(dob[hh], v_ref[hh, rows, :], NT_DIMS, preferred_element_type=F32) for hh in heads]
            nsp = [-_softplus(z[hh]) for hh in heads]
            log1m = [jnp.where(causal, x, 0.0) for x in nsp] if masked else nsp
            below = [_dot2(log1m[hh], tri_lt) + pc[hh] for hh in heads]
            a = [jnp.exp(z[hh] + (ltot[hh] - below[hh])) for hh in heads]
            if masked:
                a = [jnp.where(causal, x, 0.0) for x in a]
            e = [a[hh] * da[hh] for hh in heads]
            ebelow = [_dot2(e[hh], tri_lt) + ec[hh] for hh in heads]
            dz = [e[hh] * jnp.exp(nsp[hh]) - jnp.exp(z[hh] + nsp[hh]) * ebelow[hh] for hh in heads]
            if masked:
                dz = [jnp.where(causal, x, 0.0) for x in dz]
            dzb = [(x * SB_SCALE).astype(BF16) for x in dz]
            for hh in heads:
                dv_ref[hh, rows, :] += lax.dot_general(a[hh].astype(BF16), dob[hh], TN_DIMS,
                                                       preferred_element_type=F32)
            for hh in heads:
                dk_ref[hh, rows, :] += lax.dot_general(dzb[hh], qv[hh], TN_DIMS, preferred_element_type=F32)
            dq = [dq[hh] + jnp.dot(dzb[hh], k_ref[hh, rows, :], preferred_element_type=F32) for hh in heads]
            pc = [pc[hh] + jnp.sum(log1m[hh], axis=1, keepdims=True) for hh in heads]
            ec = [ec[hh] + jnp.sum(e[hh], axis=1, keepdims=True) for hh in heads]
            return tuple(dq) + tuple(pc) + tuple(ec)

        zcol = jnp.zeros((bs, 1), F32)
        carry = lax.fori_loop(0, qi, lambda j, c: blocks(j, False, c),
                              (jnp.zeros((bs, d), F32),) * nh + (zcol,) * (2 * nh))
        carry = blocks(qi, True, carry)
        for hh in range(nh):
            dq_ref[hh] = carry[hh]

        @_when_step(ngroup - 1, nblock - 1)
        def _():
            _exchange_phase("finish", p_in, p_out, sems)

    qs = pl.BlockSpec((nh, bs, d), lambda hh, i: (hh, i, 0))
    ks = pl.BlockSpec((nh, t, d), lambda hh, i: (hh, 0, 0), pipeline_mode=pl.Buffered(1))
    res = pl.pallas_call(
        body, name="sb_attn_bwd", grid=(ngroup, nblock),
        in_specs=[qs, ks, ks, qs, qs] + [ANY] * nw, out_specs=[qs, ks, ks] + [ANY] * nw,
        out_shape=[jax.ShapeDtypeStruct((h, t, d), F32)] * 3 + [jax.ShapeDtypeStruct(p.shape, p.dtype) for p in parts],
        scratch_shapes=_exchange_sems(nw),
        compiler_params=_params(("arbitrary", "arbitrary")),
    )(q, k, v, lsum, do, *parts)
    return res[0], res[1], res[2], res[3:]


ANY = pl.BlockSpec(memory_space=pl.ANY)
IN_VMEM = pl.BlockSpec(memory_space=pltpu.VMEM)


def _coords():
    return lax.axis_index("x"), lax.axis_index("y"), lax.axis_index("c")


def _flip(v, bit):
    return 1 - v if bit else v


def _remote(src, dst, send_sem, recv_sem, device):
    return pltpu.make_async_remote_copy(src_ref=src, dst_ref=dst, send_sem=send_sem, recv_sem=recv_sem,
                                        device_id=device, device_id_type=MESH)


def _all_gather8(name, blk):
    m, n = blk.shape

    def body(x_ref, o_ref, send_sems, recv_sems, local_sem):
        x, y, c = _coords()
        own = pltpu.make_async_copy(x_ref, o_ref.at[4 * x + 2 * y + c], local_sem)
        own.start()
        peers = []
        for bits in range(1, N_DEV):
            px, py, pc = _flip(x, (bits >> 2) & 1), _flip(y, (bits >> 1) & 1), _flip(c, bits & 1)
            peers.append((px, py, pc))
        sends = []
        for k, peer in enumerate(peers):
            cp = _remote(x_ref, o_ref.at[4 * x + 2 * y + c], send_sems.at[k], recv_sems.at[k], peer)
            cp.start()
            sends.append(cp)
        for k, (px, py, pc) in enumerate(peers):
            slot = o_ref.at[4 * px + 2 * py + pc]
            _remote(slot, slot, send_sems.at[k], recv_sems.at[k], (px, py, pc)).wait_recv()
        for cp in sends:
            cp.wait_send()
        own.wait()

    return pl.pallas_call(
        body, name=name, out_shape=jax.ShapeDtypeStruct((N_DEV, m, n), blk.dtype),
        in_specs=[IN_VMEM], out_specs=IN_VMEM,
        scratch_shapes=[pltpu.SemaphoreType.DMA((N_DEV - 1,)), pltpu.SemaphoreType.DMA((N_DEV - 1,)),
                        pltpu.SemaphoreType.DMA],
        compiler_params=_params(),
    )(blk)


def _gather_weights(shards):
    nw = len(shards)
    heights = [s.shape[0] for s in shards]

    def body(*refs):
        ins, outs, sems = refs[:nw], refs[nw:2 * nw], refs[2 * nw:]
        _gather_phase("issue", ins, outs, sems, heights)
        _gather_phase("forward", ins, outs, sems, heights)
        _gather_phase("finish", ins, outs, sems, heights)

    return pl.pallas_call(
        body, name="gather_weights",
        out_shape=_gather_out_shapes(shards),
        in_specs=[ANY] * nw, out_specs=[ANY] * nw,
        scratch_shapes=_gather_sems(nw),
        compiler_params=_params(),
    )(*shards)


def _gather_out_shapes(shards):
    return [jax.ShapeDtypeStruct((N_CHIPS,) + s.shape, s.dtype) for s in shards]


def _gather_sems(nw):
    return [pltpu.SemaphoreType.DMA((3 * nw,))] * 4


def _gather_phase(phase, ins, outs, sems, heights):
    ici_send, ici_recv, d2d_send, d2d_recv = sems
    x, y, c = _coords()
    chip = 2 * x + y
    sibling = (x, y, 1 - c)
    for w, height in enumerate(heights):
        half = height // 2
        mine, theirs = pl.ds(c * half, half), pl.ds((1 - c) * half, half)
        for j, (a, b) in enumerate(XY_MASKS):
            px, py = _flip(x, a), _flip(y, b)
            k = 3 * w + j
            over_ici = _remote(ins[w].at[mine], outs[w].at[chip, mine], ici_send.at[k], ici_recv.at[k], (px, py, c))
            landed = outs[w].at[2 * px + py, mine]
            onward = _remote(landed, landed, d2d_send.at[k], d2d_recv.at[k], sibling)
            if phase == "issue":
                over_ici.start()
            elif phase == "forward":
                _remote(landed, landed, ici_send.at[k], ici_recv.at[k], (px, py, c)).wait_recv()
                onward.start()
            else:
                slot = outs[w].at[2 * px + py, theirs]
                _remote(slot, slot, d2d_send.at[k], d2d_recv.at[k], sibling).wait_recv()
                over_ici.wait_send()
                onward.wait_send()


def _pair_exchange(name, grads):
    nw = len(grads)

    def body(*refs):
        ins, outs = refs[:nw], refs[nw:2 * nw]
        send_sems, recv_sems = refs[2 * nw:]
        x, y, c = _coords()
        sibling = (x, y, 1 - c)
        sends = []
        for w in range(nw):
            half = grads[w].shape[1] // 2
            cp = _remote(ins[w].at[:, pl.ds((1 - c) * half, half)], outs[w], send_sems.at[w], recv_sems.at[w], sibling)
            cp.start()
            sends.append(cp)
        for w in range(nw):
            _remote(outs[w], outs[w], send_sems.at[w], recv_sems.at[w], sibling).wait_recv()
        for cp in sends:
            cp.wait_send()

    return pl.pallas_call(
        body, name=name,
        out_shape=[jax.ShapeDtypeStruct((N_CHIPS, g.shape[1] // 2, g.shape[2]), g.dtype) for g in grads],
        in_specs=[ANY] * nw, out_specs=[ANY] * nw,
        scratch_shapes=[pltpu.SemaphoreType.DMA((nw,))] * 2,
        compiler_params=_params(),
    )(*grads)


def _chip_exchange(name, parts):
    nw = len(parts)

    def body(*refs):
        ins, outs, sems = refs[:nw], refs[nw:2 * nw], refs[2 * nw:]
        _exchange_phase("issue", ins, outs, sems)
        _exchange_phase("finish", ins, outs, sems)

    return pl.pallas_call(
        body, name=name,
        out_shape=[jax.ShapeDtypeStruct(p.shape, p.dtype) for p in parts],
        in_specs=[ANY] * nw, out_specs=[ANY] * nw,
        scratch_shapes=_exchange_sems(nw),
        compiler_params=_params(),
    )(*parts)


def _exchange_sems(nw):
    return [pltpu.SemaphoreType.DMA((3 * nw,))] * 2


def _exchange_phase(phase, ins, outs, sems):
    send_sems, recv_sems = sems
    x, y, c = _coords()
    chip = 2 * x + y
    for w in range(len(ins)):
        for j, (a, b) in enumerate(XY_MASKS):
            px, py = _flip(x, a), _flip(y, b)
            k = 3 * w + j
            send = _remote(ins[w].at[2 * px + py], outs[w].at[chip], send_sems.at[k], recv_sems.at[k], (px, py, c))
            if phase == "issue":
                send.start()
            else:
                slot = outs[w].at[2 * px + py]
                _remote(slot, slot, send_sems.at[k], recv_sems.at[k], (px, py, c)).wait_recv()
                send.wait_send()


def _pair_share(shards):
    nw = len(shards)

    def body(*refs):
        bufs = refs[nw:2 * nw]
        send_sems, recv_sems = refs[2 * nw:]
        x, y, c = _coords()
        sibling = (x, y, 1 - c)
        sends = []
        for w in range(nw):
            half = shards[w].shape[0] // 2
            mine = bufs[w].at[pl.ds(c * half, half)]
            cp = _remote(mine, mine, send_sems.at[w], recv_sems.at[w], sibling)
            cp.start()
            sends.append(cp)
        for w in range(nw):
            half = shards[w].shape[0] // 2
            theirs = bufs[w].at[pl.ds((1 - c) * half, half)]
            _remote(theirs, theirs, send_sems.at[w], recv_sems.at[w], sibling).wait_recv()
        for cp in sends:
            cp.wait_send()

    return pl.pallas_call(
        body, name="grad_pair_share",
        out_shape=[jax.ShapeDtypeStruct(s.shape, s.dtype) for s in shards],
        in_specs=[ANY] * nw, out_specs=[ANY] * nw,
        input_output_aliases={w: w for w in range(nw)},
        scratch_shapes=[pltpu.SemaphoreType.DMA((nw,))] * 2,
        compiler_params=_params(),
    )(*shards)


TILE_BYTES = 2 * 1024 * 1024


def _row_tile(rows, cols, mult=8):
    best = None
    for tr in range(mult, rows + 1, mult):
        if rows % tr == 0 and tr * cols * 4 <= TILE_BYTES:
            best = tr
    return best if best is not None else rows


def _pair_add(name, grad, other, place):
    _, half, cols = other.shape
    tr = _row_tile(half, cols, mult=16)
    nb = half // tr

    def body(place_ref, g_ref, o_ref, out_ref):
        out_ref[...] = (g_ref[...] + o_ref[...]).astype(out_ref.dtype)

    return pl.pallas_call(
        body, name=name, out_shape=jax.ShapeDtypeStruct(other.shape, BF16),
        grid_spec=pltpu.PrefetchScalarGridSpec(
            num_scalar_prefetch=1, grid=(N_CHIPS, nb),
            in_specs=[pl.BlockSpec((None, tr, cols), lambda s, i, place_ref: (s, place_ref[0] * nb + i, 0)),
                      pl.BlockSpec((None, tr, cols), lambda s, i, place_ref: (s, i, 0))],
            out_specs=pl.BlockSpec((None, tr, cols), lambda s, i, place_ref: (s, i, 0))),
        compiler_params=_params(("parallel", "parallel")),
    )(place, grad, other)


def _sum_chips(name, own, others, place):
    _, half, cols = own.shape
    tr = _row_tile(half, cols, mult=16)
    nb = half // tr

    def body(place_ref, own_ref, a_ref, b_ref, c_ref, out_ref):
        total = own_ref[...].astype(F32) + a_ref[...].astype(F32)
        out_ref[...] = (total + b_ref[...].astype(F32)) + c_ref[...].astype(F32)

    def peer(mask):
        return pl.BlockSpec((None, tr, cols), lambda i, place_ref: (place_ref[1] ^ mask, i, 0))

    return pl.pallas_call(
        body, name=name, out_shape=jax.ShapeDtypeStruct((2 * half, cols), F32),
        grid_spec=pltpu.PrefetchScalarGridSpec(
            num_scalar_prefetch=1, grid=(nb,),
            in_specs=[peer(0), peer(1), peer(2), peer(3)],
            out_specs=pl.BlockSpec((tr, cols), lambda i, place_ref: (place_ref[0] * nb + i, 0))),
        compiler_params=_params(("parallel",)),
    )(place, own, others, others, others)


def _adamw_math(w, g, m, v):
    m2 = ADAM_B1 * m + (1.0 - ADAM_B1) * g
    v2 = ADAM_B2 * v + (1.0 - ADAM_B2) * (g * g)
    m_hat = m2 / (1.0 - ADAM_B1 ** ADAM_STEP)
    v_hat = v2 / (1.0 - ADAM_B2 ** ADAM_STEP)
    delta = -ADAM_LR * (m_hat / (jnp.sqrt(v_hat) + ADAM_EPS) + ADAM_WD * w)
    return delta, m2, v2


def _adamw(name, w, g, m, v):
    rows, cols = w.shape
    tr = _row_tile(rows, cols)

    def body(w_ref, g_ref, m_ref, v_ref, d_ref, m2_ref, v2_ref):
        d_ref[...], m2_ref[...], v2_ref[...] = _adamw_math(w_ref[...], g_ref[...], m_ref[...], v_ref[...])

    blk = pl.BlockSpec((tr, cols), lambda i: (i, 0))
    return pl.pallas_call(
        body, name=name, out_shape=[jax.ShapeDtypeStruct(w.shape, F32)] * 3, grid=(rows // tr,),
        in_specs=[blk] * 4, out_specs=[blk] * 3,
        compiler_params=_params(("parallel",)),
    )(w, g, m, v)


def _small_update(gathered, w, m, v):
    def body(gs_ref, w_ref, m_ref, v_ref, g_ref, d_ref, m2_ref, v2_ref):
        g = gs_ref[0]
        for dev in range(1, N_DEV):
            g = g + gs_ref[dev]
        g_ref[...] = g
        d_ref[...], m2_ref[...], v2_ref[...] = _adamw_math(w_ref[...], g, m_ref[...], v_ref[...])

    return pl.pallas_call(
        body, name="small_update", out_shape=[jax.ShapeDtypeStruct(w.shape, F32)] * 4,
        compiler_params=_params(),
    )(gathered, w, m, v)


def _ada_mod(c_all, w_shard, b_shard):
    d, n = w_shard.shape
    tn = _pick(n, (512, 256, 128))

    def body(c_ref, w_ref, b_ref, o_ref):
        act = _silu(c_ref[...]).astype(BF16)
        o_ref[...] = jnp.dot(act, w_ref[...].astype(BF16), preferred_element_type=F32) + b_ref[...]

    return pl.pallas_call(
        body, name="ada_mod", out_shape=jax.ShapeDtypeStruct((c_all.shape[0], n), F32), grid=(n // tn,),
        in_specs=[pl.BlockSpec(c_all.shape, lambda j: (0, 0)), pl.BlockSpec((d, tn), lambda j: (0, j)),
                  pl.BlockSpec((1, tn), lambda j: (0, j))],
        out_specs=pl.BlockSpec((c_all.shape[0], tn), lambda j: (0, j)),
        compiler_params=_params(("parallel",)),
    )(c_all, w_shard, b_shard)


def _ada_grad(c_pad, dmod_pad):
    rows, d = c_pad.shape
    n = dmod_pad.shape[1]
    tn = _pick(n, (512, 256, 128))

    def body(c_ref, g_ref, o_ref):
        act = _silu(c_ref[...]).astype(BF16)
        o_ref[...] = lax.dot_general(act, g_ref[...].astype(BF16), TN_DIMS, preferred_element_type=F32)

    return pl.pallas_call(
        body, name="ada_grad", out_shape=jax.ShapeDtypeStruct((d, n), F32), grid=(n // tn,),
        in_specs=[pl.BlockSpec((rows, d), lambda j: (0, 0)), pl.BlockSpec((rows, tn), lambda j: (0, j))],
        out_specs=pl.BlockSpec((d, tn), lambda j: (0, j)),
        compiler_params=_params(("parallel",)),
    )(c_pad, dmod_pad)


WEIGHTS = ['w_ada', 'b_ada', 'norm1_gain', 'norm2_gain', 'w_in', 'mu_rkv', 'mu_w', 'mu_a', 'mu_g', 'w0', 'w1',
           'w2', 'a0', 'a1', 'a2', 'g1', 'g2', 'k_k', 'k_a', 'r_k', 'ln_x_gain', 'ln_x_bias', 'q_norm_gain',
           'k_norm_gain', 'w_out', 'w_gate_up', 'w_down']
SMALL = ['b_ada', 'norm1_gain', 'norm2_gain', 'mu_rkv', 'mu_w', 'mu_a', 'mu_g', 'w0', 'a0', 'k_k', 'k_a', 'r_k',
         'ln_x_gain', 'ln_x_bias', 'q_norm_gain', 'k_norm_gain']
PACK_ROWS = 8
LOSS_SLOT = LANES


def _shift_down(a):
    return jnp.pad(a[:-1], ((1, 0), (0, 0)))


def _shift_up(a):
    return jnp.pad(a[1:], ((0, 1), (0, 0)))


def _pack_small(vals):
    flat = jnp.concatenate([v.reshape(1, -1) for v in vals], axis=1)
    unit = PACK_ROWS * LANES
    total = -(-flat.shape[1] // unit) * unit
    flat = jnp.pad(flat, ((0, 0), (0, total - flat.shape[1])))
    return flat.reshape(PACK_ROWS, total // PACK_ROWS)


def kernel(x, c, w_ada, b_ada, norm1_gain, norm2_gain, w_in, mu_rkv, mu_w, mu_a, mu_g, w0, w1, w2, a0, a1, a2, g1, g2, k_k, k_a, r_k, ln_x_gain, ln_x_bias, q_norm_gain, k_norm_gain, w_out, w_gate_up, w_down, loss_target, m_w_ada, m_b_ada, m_norm1_gain, m_norm2_gain, m_w_in, m_mu_rkv, m_mu_w, m_mu_a, m_mu_g, m_w0, m_w1, m_w2, m_a0, m_a1, m_a2, m_g1, m_g2, m_k_k, m_k_a, m_r_k, m_ln_x_gain, m_ln_x_bias, m_q_norm_gain, m_k_norm_gain, m_w_out, m_w_gate_up, m_w_down, v_w_ada, v_b_ada, v_norm1_gain, v_norm2_gain, v_w_in, v_mu_rkv, v_mu_w, v_mu_a, v_mu_g, v_w0, v_w1, v_w2, v_a0, v_a1, v_a2, v_g1, v_g2, v_k_k, v_k_a, v_r_k, v_ln_x_gain, v_ln_x_bias, v_q_norm_gain, v_k_norm_gain, v_w_out, v_w_gate_up, v_w_down):
    given = dict(locals())
    wt = {n: given[n][0] for n in WEIGHTS}
    mom = {n: given["m_" + n][0] for n in WEIGHTS}
    var = {n: given["v_" + n][0] for n in WEIGHTS}
    for tree in (wt, mom, var):
        tree["b_ada"] = tree["b_ada"].reshape(1, -1)
        for n in SMALL[1:]:
            tree[n] = tree[n].reshape(1, -1)

    ax, ay, ac = _coords()
    chip = 2 * ax + ay
    dev = 4 * ax + 2 * ay + ac
    xs, target = x[0], loss_target[0]
    t, d = xs.shape
    dr = wt["w0"].shape[1]
    ds = d - dr
    nh = ds // HEAD_DIM
    dff = wt["w_down"].shape[0] * N_CHIPS
    n_ada = wt["w_ada"].shape[1]
    lw, la, lg = wt["w1"].shape[1], wt["a1"].shape[1], wt["g1"].shape[1]

    def lora_a(tree):
        return jnp.concatenate([tree["w1"], tree["a1"], tree["g1"]], axis=1)

    def lora_b(tree):
        return jnp.concatenate([tree["w2"], tree["a2"], tree["g2"]], axis=0)

    def with_own_slot(gathered, own):
        return [lax.dynamic_update_slice(full, shard[None], (chip, 0, 0)) for full, shard in zip(gathered, own)]

    first_shards = [s.astype(BF16) for s in (wt["w_in"], lora_a(wt), lora_b(wt))]
    later_shards = [s.astype(BF16) for s in (wt["w_out"], wt["w_gate_up"], wt["w_down"])]
    full_in, full_la, full_lb = with_own_slot(_gather_weights(first_shards), first_shards)
    full_la = full_la.reshape(d, lw + la + lg).astype(F32)
    full_lb = full_lb.transpose(1, 0, 2).reshape(lw + la + lg, dr).astype(F32)
    w1f, a1f, g1f = full_la[:, :lw], full_la[:, lw:lw + la], full_la[:, lw + la:]
    w2f, a2f, g2f = full_lb[:lw], full_lb[lw:lw + la], full_lb[lw + la:]

    c_all = _all_gather8("gather_c", c.reshape(PACK_ROWS, d // PACK_ROWS)).reshape(N_DEV, d)
    b_shard = lax.dynamic_slice(wt["b_ada"], (0, chip * n_ada), (1, n_ada))
    mod_part = _ada_mod(c_all, wt["w_ada"], b_shard)
    mod_all = _all_gather8("gather_mod", mod_part)[::2]
    mod = lax.dynamic_slice(mod_all, (0, dev, 0), (N_CHIPS, 1, n_ada)).reshape(1, N_CHIPS * n_ada)
    sh1, sc1, gt1, sh2, sc2, gt2 = [mod[:, i * d:(i + 1) * d] for i in range(6)]

    h, h_bf = _rowwise("norm1", _fn_norm1, [xs], [wt["norm1_gain"], sc1, sh1], [(d, F32), (d, BF16)], 256)
    hp = _shift_down(h)
    p = _matmul("mm_in", h_bf, full_in, b_shards=True, tm=512, tn=1536, tk=2048, n_outer=True)
    p_rkv, p_sb = (p, 3 * dr, 0), (p, 3 * ds, 1)
    pp = _shift_down(p[:, :3 * dr])
    pre_rows = [h, hp, p_rkv, pp]
    pre_params = [wt["mu_rkv"], wt["mu_w"], wt["mu_a"], wt["mu_g"], wt["w0"], wt["a0"], wt["k_k"], wt["k_a"],
                  w1f, w2f, a1f, a2f, g1f, g2f]
    pre = _rowwise("rwkv_pre", _fn_rwkv_pre, pre_rows, pre_params, [(dr, F32)] * 7, 128)
    r_, w_, k2, v_, rem, wr, g_ = pre
    y_raw, hist, s_last = _scan_fwd(rem, w_, wr, k2, r_, v_)
    post_rows = [y_raw, r_, k2, v_, g_]
    post_params = [wt["ln_x_gain"], wt["ln_x_bias"], wt["r_k"]]

    qg = jnp.tile(wt["q_norm_gain"], (1, nh))
    kg = jnp.tile(wt["k_norm_gain"], (1, nh))
    qn, kn, vs = _rowwise("qk_norm", _fn_qk_norm, [p_sb], [qg, kg], [(ds, BF16)] * 3, 256)

    def to_heads(a):
        return a.reshape(t, nh, HEAD_DIM).transpose(1, 0, 2)

    def from_heads(a):
        return a.transpose(1, 0, 2).reshape(t, ds)

    qh, kh, vh = to_heads(qn), to_heads(kn), to_heads(vs)
    o_h, lsum, later_full = _sb_fwd(qh, kh, vh, later_shards)
    full_out, full_gu, full_down = with_own_slot(later_full, later_shards)
    full_out = full_out.reshape(d, d)
    full_down = full_down.reshape(dff, d)
    (ycat,) = _rowwise("rwkv_post", _fn_rwkv_post_cat, post_rows + [from_heads(o_h)], post_params, [(d, BF16)], 256)
    mix = _matmul("mm_out", ycat, full_out, tm=512, tn=1024, tk=2048, n_outer=True)
    norm2_params = [gt1, wt["norm2_gain"], sc2, sh2]
    x1, h2 = _rowwise("mix_norm2", _fn_mix_norm2, [xs, mix], norm2_params, [(d, F32), (d, BF16)], 256)
    gu = _matmul("mm_gate_up", h2, full_gu, b_shards=True, tm=512, tn=1408, tk=2048, n_outer=True)
    gate_up = [(gu, dff, 0), (gu, dff, 1)]
    (act,) = _rowwise("swiglu", _fn_swiglu, gate_up, [], [(dff, BF16)], 256)
    dn = _matmul("mm_down", act, full_down, tm=1024, tn=1024, tk=1408)
    loss_vec, dout, ddn, dgt2 = _loss_head("loss_head", x1, dn, target, gt2)

    dact = _matmul("mm_down_dx", ddn, full_down, tb=True, tm=512, tn=1408, tk=2048, n_outer=True)
    gw_down = _matmul("mm_down_dw", act, ddn, ta=True, tm=1408, tn=1024, tk=512)
    dgu = _swiglu_bwd(gu, dact)
    dh2 = _matmul("mm_gate_up_dx", dgu, full_gu, tb=True, b_shards=True, tm=1024, tn=1024, tk=1408)
    gw_gu = _matmul("mm_gate_up_dw", h2, dgu, ta=True, out_shards=True, tm=1024, tn=1408, tk=512)
    (dx_a, dmix), (dgt1, dgain2, dsc2, dsh2) = _rowwise_bwd(
        "mix_norm2_bwd", _fn_mix_norm2, [xs, mix], norm2_params, [[dout], [dh2]], [F32, BF16], [True] * 4, 128)
    dycat = _matmul("mm_out_dx", dmix, full_out, tb=True, tm=512, tn=1024, tk=2048, n_outer=True)
    gw_out = _matmul("mm_out_dw", ycat, dmix, ta=True, tm=1024, tn=1024, tk=512)
    (dy_raw, dr_f, dk_f, dv_f, dg), (dlng, dlnb, drk) = _rowwise_bwd(
        "rwkv_post_bwd", _fn_rwkv_post, post_rows, post_params, [[(dycat, dr, 0)]], [F32] * 5, [True] * 3, 128)
    place = jnp.stack([ac, chip]).astype(jnp.int32)

    def pair_reduce(tag, names, grads):
        from_sibling = _pair_exchange("grad_pair_exchange_" + tag, grads)
        return [_pair_add("pair_add_" + n, g, o, place) for n, g, o in zip(names, grads, from_sibling)]

    early_names = ["w_out", "w_gate_up", "w_down"]
    early_sums = pair_reduce("early", early_names, [gw_out.reshape(N_CHIPS, d // N_CHIPS, d), gw_gu,
                                                    gw_down.reshape(N_CHIPS, dff // N_CHIPS, d)])
    do_h = to_heads(dycat[:, dr:])
    dqh, dkh, dvh, early_from_chips = _sb_bwd(qh, kh, vh, lsum, do_h, early_sums)
    drem_s, dw_s, dwr_s, dk_s, dr_s, dv_s = _scan_bwd(rem, w_, wr, k2, r_, v_, hist, s_last, dy_raw)
    (dp_sb,), (dqg, dkg) = _rowwise_bwd(
        "qk_norm_bwd", _fn_qk_norm, [p_sb], [qg, kg], [[from_heads(dqh)], [from_heads(dkh)], [from_heads(dvh)]],
        [F32], [True, True], 128)
    pre_cts = [[dr_s, dr_f], [dw_s], [dk_s, dk_f], [dv_s, dv_f], [drem_s], [dwr_s], [dg]]
    (dh_a, dhp, dp_rkv, dpp), pre_g = _rowwise_bwd(
        "rwkv_pre_bwd", _fn_rwkv_pre, pre_rows, pre_params, pre_cts, [F32] * 4, [True] * 14, 128)
    dp = jnp.concatenate([dp_rkv + _shift_up(dpp), dp_sb], axis=1).astype(BF16)
    dh_mm = _matmul("mm_in_dx", dp, full_in, tb=True, b_shards=True, tm=1024, tn=1024, tk=1536)
    gw_in = _matmul("mm_in_dw", h_bf, dp, ta=True, out_shards=True, tm=1024, tn=1536, tk=512)
    (grad_x,), (dgain1, dsc1, dsh1) = _rowwise_bwd(
        "norm1_bwd", _fn_norm1, [xs], [wt["norm1_gain"], sc1, sh1], [[dh_a, dh_mm, _shift_up(dhp)], []],
        [F32], [True] * 3, 128, add_to_first=[dx_a])

    g_mu_rkv, g_mu_w, g_mu_a, g_mu_g, g_w0, g_a0, g_kk, g_ka, gw1, gw2, ga1, ga2, gg1, gg2 = pre_g
    g_la = jnp.concatenate([gw1, ga1, gg1], axis=1).reshape(N_CHIPS, d // N_CHIPS, lw + la + lg)
    g_lb = jnp.concatenate([gw2, ga2, gg2], axis=0)
    g_lb = g_lb.reshape(lw + la + lg, N_CHIPS, dr // N_CHIPS).transpose(1, 0, 2)
    late_names = ["w_in", "lora_a", "lora_b"]
    late_sums = pair_reduce("late", late_names, [gw_in, g_la, g_lb])
    late_from_chips = _chip_exchange("grad_chip_exchange", late_sums)
    halves = [_sum_chips("chip_sum_" + n, p, q, place)
              for n, p, q in zip(early_names + late_names, early_sums + late_sums,
                                 list(early_from_chips) + list(late_from_chips))]
    r_out, r_gu, r_down, r_in, r_la, r_lb = _pair_share(halves)

    dmod = jnp.concatenate([dsh1, dsc1, dgt1, dsh2, dsc2, dgt2], axis=1)
    dqg = dqg.reshape(nh, HEAD_DIM).sum(axis=0, keepdims=True)
    dkg = dkg.reshape(nh, HEAD_DIM).sum(axis=0, keepdims=True)
    small_g = [dmod, dgain1, dgain2, g_mu_rkv, g_mu_w, g_mu_a, g_mu_g, g_w0, g_a0, g_kk, g_ka, drk, dlng, dlnb,
               dqg, dkg]
    lead = jnp.zeros((1, LOSS_SLOT), F32)
    packed = _pack_small([loss_vec] + small_g)
    gathered = _all_gather8("gather_small", packed)
    sm_g, sm_d, sm_m, sm_v = _small_update(gathered, _pack_small([lead] + [wt[n] for n in SMALL]),
                                           _pack_small([lead] + [mom[n] for n in SMALL]),
                                           _pack_small([lead] + [var[n] for n in SMALL]))
    loss = sm_g.reshape(-1)[0]

    def unpack(packed_arr):
        flat, out, pos = packed_arr.reshape(-1), {}, LOSS_SLOT
        for n in SMALL:
            size = wt[n].size
            out[n] = flat[pos:pos + size]
            pos += size
        return out

    res = {"grad": unpack(sm_g), "delta": unpack(sm_d), "m": unpack(sm_m), "v": unpack(sm_v)}

    dmod_all = gathered.reshape(N_DEV, -1)[:, LOSS_SLOT:LOSS_SLOT + N_CHIPS * n_ada]
    dmod_cols = lax.dynamic_slice(dmod_all, (0, chip * n_ada), (N_DEV, n_ada))
    pad8 = ((0, N_DEV), (0, 0))
    res["grad"]["w_ada"] = _ada_grad(jnp.pad(c_all, pad8), jnp.pad(dmod_cols, pad8))

    res["grad"].update(w_in=r_in, w_out=r_out, w_gate_up=r_gu, w_down=r_down)
    for n in ("w_ada", "w_in", "w_out", "w_gate_up", "w_down"):
        res["delta"][n], res["m"][n], res["v"][n] = _adamw("adamw_" + n, wt[n], res["grad"][n], mom[n], var[n])
    la_d, la_m, la_v = _adamw("adamw_lora_a", lora_a(wt), r_la, lora_a(mom), lora_a(var))
    lb_d, lb_m, lb_v = _adamw("adamw_lora_b", lora_b(wt), r_lb, lora_b(mom), lora_b(var))
    for key, pa, pb in (("grad", r_la, r_lb), ("delta", la_d, lb_d), ("m", la_m, lb_m), ("v", la_v, lb_v)):
        res[key].update(w1=pa[:, :lw], a1=pa[:, lw:lw + la], g1=pa[:, lw + la:],
                        w2=pb[:lw], a2=pb[lw:lw + la], g2=pb[lw + la:])

    outs = [loss, grad_x[None]]
    for key in ("grad", "delta", "m", "v"):
        outs += [res[key][n].reshape(given[n].shape) for n in WEIGHTS]
    return tuple(outs)
```

```python
import functools
import math

import jax
import jax.numpy as jnp
from jax import lax
from jax.experimental import pallas as pl
from jax.experimental.pallas import tpu as pltpu

F32 = jnp.float32
BF16 = jnp.bfloat16
HEAD_DIM = 64
LANES = 128
RMS_EPS = 1e-6
GN_EPS = 64e-5
L2_EPS = 1e-12
ADAM_LR, ADAM_B1, ADAM_B2, ADAM_EPS, ADAM_WD, ADAM_STEP = 0.001, 0.9, 0.999, 1e-08, 0.01, 10
VMEM_LIMIT = 56 * 1024 * 1024
MESH = pl.DeviceIdType.MESH
HI = lax.Precision.HIGHEST
N_CHIPS = 4
N_DEV = 8
XY_MASKS = ((1, 0), (0, 1), (1, 1))


def _pick(dim, prefs):
    for p in prefs:
        if dim % p == 0:
            return p
    return dim


def _params(sem=None, vmem=VMEM_LIMIT):
    return pltpu.CompilerParams(dimension_semantics=sem, vmem_limit_bytes=vmem)


def _sigmoid(x):
    return 1.0 / (1.0 + jnp.exp(-x))


@jax.custom_vjp
def _softplus(x):
    return jnp.maximum(x, 0.0) + jnp.log(1.0 + jnp.exp(-jnp.abs(x)))


_softplus.defvjp(lambda x: (_softplus(x), x), lambda x, g: (g * _sigmoid(x),))


def _silu(x):
    return x * _sigmoid(x)


@jax.custom_vjp
def _bdot(a, b):
    return jnp.dot(a.astype(BF16), b.astype(BF16), preferred_element_type=F32)


def _bdot_bwd(res, g):
    a, b = res
    gb = g.astype(BF16)
    da = lax.dot_general(gb, b.astype(BF16), (((1,), (1,)), ((), ())), preferred_element_type=F32)
    db = lax.dot_general(a.astype(BF16), gb, (((0,), (0,)), ((), ())), preferred_element_type=F32)
    return da.astype(a.dtype), db.astype(b.dtype)


_bdot.defvjp(lambda a, b: (_bdot(a, b), (a, b)), _bdot_bwd)


def _head_ones():
    i = lax.broadcasted_iota(jnp.int32, (LANES, LANES), 0) // HEAD_DIM
    j = lax.broadcasted_iota(jnp.int32, (LANES, LANES), 1) // HEAD_DIM
    return (i == j).astype(F32)


def _hdot(x, ones_bf):
    hi = x.astype(BF16)
    rest = x - hi.astype(F32)
    mid = rest.astype(BF16)
    lo = (rest - mid.astype(F32)).astype(BF16)
    out = jnp.dot(hi, ones_bf, preferred_element_type=F32)
    out += jnp.dot(mid, ones_bf, preferred_element_type=F32)
    return out + jnp.dot(lo, ones_bf, preferred_element_type=F32)


@jax.custom_vjp
def _segsum(x):
    ones = _head_ones().astype(BF16)
    parts = [_hdot(x[:, LANES * j:LANES * (j + 1)], ones) for j in range(x.shape[1] // LANES)]
    return parts[0] if len(parts) == 1 else jnp.concatenate(parts, axis=1)


_segsum.defvjp(lambda x: (_segsum(x), None), lambda _, g: (_segsum(g),))


def _rms(x, gain):
    return x * lax.rsqrt(jnp.mean(x * x, axis=-1, keepdims=True) + RMS_EPS) * gain


def _matmul(name, a, b, *, ta=False, tb=False, b_shards=False, out_shards=False, out_dtype=F32,
            tm=512, tn=512, tk=512, n_outer=False):
    if ta:
        kdim, m = a.shape
    else:
        m, kdim = a.shape
    if b_shards:
        if tb:
            n, ks = b.shape[1], b.shape[2]
            assert ks * N_CHIPS == kdim
        else:
            ns = b.shape[2]
            n = ns * N_CHIPS
            assert b.shape[1] == kdim
    else:
        n = b.shape[0] if tb else b.shape[1]
    tm = _pick(m, (tm, 512, 256, 128))
    n_part = n // N_CHIPS if (out_shards or (b_shards and not tb)) else n
    tn = _pick(n_part, (tn, 512, 256, 128))
    k_part = kdim // N_CHIPS if (b_shards and tb) else kdim
    tk = _pick(k_part, (tk, 512, 256, 128))
    nb = n_part // tn
    kb = k_part // tk
    nk = kdim // tk
    grid = (n // tn, m // tm, nk) if n_outer else (m // tm, n // tn, nk)

    def spec(shape, index):
        if n_outer:
            return pl.BlockSpec(shape, lambda j, i, k: index(i, j, k))
        return pl.BlockSpec(shape, index)

    if ta:
        a_spec = spec((tk, tm), lambda i, j, k: (k, i))
    else:
        a_spec = spec((tm, tk), lambda i, j, k: (i, k))
    if b_shards and tb:
        b_spec = spec((None, tn, tk), lambda i, j, k: (k // kb, j, k % kb))
    elif b_shards:
        b_spec = spec((None, tk, tn), lambda i, j, k: (j // nb, k, j % nb))
    elif tb:
        b_spec = spec((tn, tk), lambda i, j, k: (j, k))
    else:
        b_spec = spec((tk, tn), lambda i, j, k: (k, j))
    if out_shards:
        o_spec = spec((None, tm, tn), lambda i, j, k: (j // nb, i, j % nb))
        o_shape = jax.ShapeDtypeStruct((N_CHIPS, m, n_part), out_dtype)
    else:
        o_spec = spec((tm, tn), lambda i, j, k: (i, j))
        o_shape = jax.ShapeDtypeStruct((m, n), out_dtype)
    dims = (((0 if ta else 1,), (1 if tb else 0,)), ((), ()))

    def product(a_ref, b_ref):
        return lax.dot_general(a_ref[...].astype(BF16), b_ref[...].astype(BF16), dims, preferred_element_type=F32)

    def body_one_step(a_ref, b_ref, o_ref):
        o_ref[...] = product(a_ref, b_ref).astype(o_ref.dtype)

    def body(a_ref, b_ref, o_ref, acc_ref):
        k = pl.program_id(2)

        @pl.when(k == 0)
        def _():
            acc_ref[...] = product(a_ref, b_ref)

        @pl.when(jnp.logical_and(k > 0, k < nk - 1))
        def _():
            acc_ref[...] += product(a_ref, b_ref)

        @pl.when(k == nk - 1)
        def _():
            o_ref[...] = (acc_ref[...] + product(a_ref, b_ref)).astype(o_ref.dtype)

    return pl.pallas_call(
        body_one_step if nk == 1 else body, name=name, grid=grid, in_specs=[a_spec, b_spec], out_specs=o_spec,
        out_shape=o_shape, scratch_shapes=[] if nk == 1 else [pltpu.VMEM((tm, tn), F32)],
        compiler_params=_params(("parallel", "parallel", "arbitrary")),
    )(a, b)


def _row_in(spec, tile):
    if isinstance(spec, tuple):
        arr, width, cb = spec
    else:
        arr, width, cb = spec, spec.shape[1], 0
    return arr, pl.BlockSpec((tile, width), lambda i, cb=cb: (i, cb))


def _full_spec(arr):
    nd = arr.ndim
    return pl.BlockSpec(arr.shape, lambda i, nd=nd: (0,) * nd, pipeline_mode=pl.Buffered(1))


def _rowwise(name, fn, rows, params, outs, tile):
    t = (rows[0][0] if isinstance(rows[0], tuple) else rows[0]).shape[0]
    tile = _pick(t, (tile,))
    arrs, specs = zip(*[_row_in(s, tile) for s in rows])
    nr, npar = len(rows), len(params)

    def body(*refs):
        rv = [r[...].astype(F32) for r in refs[:nr]]
        pv = [p[...] for p in refs[nr:nr + npar]]
        res = fn(*rv, *pv)
        for o_ref, val in zip(refs[nr + npar:], res):
            o_ref[...] = val.astype(o_ref.dtype)

    return pl.pallas_call(
        body, name=name, grid=(t // tile,),
        in_specs=list(specs) + [_full_spec(p) for p in params],
        out_specs=[pl.BlockSpec((tile, w), lambda i: (i, 0)) for w, _ in outs],
        out_shape=[jax.ShapeDtypeStruct((t, w), d) for w, d in outs],
        compiler_params=_params(("parallel",)),
    )(*arrs, *params)


def _rowwise_bwd(name, fn, rows, params, cts, row_grads, param_grads, tile, add_to_first=()):
    t = (rows[0][0] if isinstance(rows[0], tuple) else rows[0]).shape[0]
    tile = _pick(t, (tile,))
    arrs, specs = zip(*[_row_in(s, tile) for s in rows])
    n_add = len(add_to_first)
    flat_cts = [c for group in cts for c in group] + list(add_to_first)
    c_arrs, c_specs = zip(*[_row_in(s, tile) for s in flat_cts])
    nr, npar, nc = len(rows), len(params), len(flat_cts)
    rg_idx = [i for i, d in enumerate(row_grads) if d is not None]
    pg_idx = [i for i, d in enumerate(param_grads) if d]

    def body(*refs):
        rv = [r[...].astype(F32) for r in refs[:nr]]
        pv = [p[...] for p in refs[nr:nr + npar]]
        cv = [c[...].astype(F32) for c in refs[nr + npar:nr + npar + nc]]
        o_refs = refs[nr + npar + nc:]
        outs, vjp = jax.vjp(fn, *rv, *pv)
        ct, pos = [], 0
        for group, o in zip(cts, outs):
            if group:
                acc = cv[pos]
                for extra in cv[pos + 1:pos + len(group)]:
                    acc = acc + extra
                pos += len(group)
            else:
                acc = jnp.zeros_like(o)
            ct.append(acc)
        grads = list(vjp(tuple(ct)))
        for extra in cv[nc - n_add:]:
            grads[rg_idx[0]] = grads[rg_idx[0]] + extra
        for o_ref, i in zip(o_refs[:len(rg_idx)], rg_idx):
            o_ref[...] = grads[i].astype(o_ref.dtype)
        first = pl.program_id(0) == 0
        for o_ref, i in zip(o_refs[len(rg_idx):], pg_idx):
            g = grads[nr + i].astype(F32)

            @pl.when(first)
            def _(o_ref=o_ref, g=g):
                o_ref[...] = g

            @pl.when(jnp.logical_not(first))
            def _(o_ref=o_ref, g=g):
                o_ref[...] += g

    def width(i):
        s = rows[i]
        return s[1] if isinstance(s, tuple) else s.shape[1]

    out_specs = [pl.BlockSpec((tile, width(i)), lambda i_: (i_, 0)) for i in rg_idx]
    out_shape = [jax.ShapeDtypeStruct((t, width(i)), row_grads[i]) for i in rg_idx]
    out_specs += [_full_spec(params[i]) for i in pg_idx]
    out_shape += [jax.ShapeDtypeStruct(params[i].shape, F32) for i in pg_idx]
    res = pl.pallas_call(
        body, name=name, grid=(t // tile,),
        in_specs=list(specs) + [_full_spec(p) for p in params] + list(c_specs),
        out_specs=out_specs, out_shape=out_shape,
        compiler_params=_params(("arbitrary",)),
    )(*arrs, *params, *c_arrs)
    return res[:len(rg_idx)], res[len(rg_idx):]


def _fn_norm1(x, gain, sc, sh):
    h = _rms(x, gain) * (1.0 + sc) + sh
    return h, h


def _fn_rwkv_pre(h, hp, p, pp, mu_rkv, mu_w, mu_a, mu_g, w0, a0, k_k, k_a, w1, w2, a1, a2, g1, g2):
    d = p.shape[1] // 3
    dh = hp - h
    xw = h + dh * mu_w
    xa = h + dh * mu_a
    xg = h + dh * mu_g
    pr = p + (pp - p) * mu_rkv
    r, k, v = pr[:, :d], pr[:, d:2 * d], pr[:, 2 * d:]
    w_log = -_softplus(-(w0 + _bdot(jnp.tanh(_bdot(xw, w1)), w2))) - 0.5
    decay = jnp.exp(-jnp.exp(w_log))
    a = _sigmoid(a0 + _bdot(_bdot(xa, a1), a2))
    g = _bdot(_sigmoid(_bdot(xg, g1)), g2)
    kk = k * k_k
    kk = kk * lax.rsqrt(_segsum(kk * kk) + L2_EPS)
    k2 = k * (1.0 + (a - 1.0) * k_a)
    return r, decay, k2, v, -kk, kk * a, g


def _fn_rwkv_post(y, r, k2, v, g, ln_g, ln_b, r_k):
    inv = 1.0 / HEAD_DIM
    mean = _segsum(y) * inv
    yc = y - mean
    var = _segsum(yc * yc) * inv
    yn = yc * lax.rsqrt(var + GN_EPS) * ln_g + ln_b
    bonus = _segsum(r * k2 * r_k) * v
    return ((yn + bonus) * g,)


def _fn_rwkv_post_cat(y, r, k2, v, g, o_sb, ln_g, ln_b, r_k):
    return (jnp.concatenate([_fn_rwkv_post(y, r, k2, v, g, ln_g, ln_b, r_k)[0], o_sb], axis=1),)


def _fn_qk_norm(p, qg, kg):
    d = p.shape[1] // 3
    q, k, v = p[:, :d], p[:, d:2 * d], p[:, 2 * d:]
    inv = 1.0 / HEAD_DIM
    qn = q * lax.rsqrt(_segsum(q * q) * inv + RMS_EPS) * qg
    kn = k * lax.rsqrt(_segsum(k * k) * inv + RMS_EPS) * kg
    return qn, kn, v


def _fn_mix_norm2(x, mix, gt1, gain, sc, sh):
    x1 = x + gt1 * mix
    h2 = _rms(x1, gain) * (1.0 + sc) + sh
    return x1, h2


def _fn_swiglu(gate, up):
    return (_silu(gate) * up,)


def _swiglu_bwd(gu, dact, tile=128):
    t, dff = dact.shape
    tile = _pick(t, (tile,))

    def body(gate_ref, up_ref, d_ref, o_ref):
        _, vjp = jax.vjp(_fn_swiglu, gate_ref[...], up_ref[...])
        dgate, dup = vjp((d_ref[...],))
        o_ref[:, :dff] = dgate.astype(o_ref.dtype)
        o_ref[:, dff:] = dup.astype(o_ref.dtype)

    return pl.pallas_call(
        body, name="swiglu_bwd", grid=(t // tile,),
        in_specs=[pl.BlockSpec((tile, dff), lambda i: (i, 0)), pl.BlockSpec((tile, dff), lambda i: (i, 1)),
                  pl.BlockSpec((tile, dff), lambda i: (i, 0))],
        out_specs=pl.BlockSpec((tile, 2 * dff), lambda i: (i, 0)),
        out_shape=jax.ShapeDtypeStruct((t, 2 * dff), BF16),
        compiler_params=_params(("parallel",)),
    )(gu, gu, dact)


def _loss_head(name, x1, dn, target, gt2, tile=256):
    t, d = x1.shape
    tile = _pick(t, (tile,))

    def body(x1_ref, dn_ref, tg_ref, gt_ref, loss_ref, dout_ref, ddn_ref, dgt_ref):
        dnv = dn_ref[...]
        gt = gt_ref[...]
        err = x1_ref[...] + gt * dnv - tg_ref[...]
        dout = err * (1.0 / d)
        dout_ref[...] = dout
        ddn_ref[...] = (dout * gt).astype(ddn_ref.dtype)
        part = 0.5 * jnp.sum(jnp.sum(err * dout, axis=-1, keepdims=True), axis=0, keepdims=True)
        dgt = jnp.sum(dout * dnv, axis=0, keepdims=True)
        first = pl.program_id(0) == 0

        @pl.when(first)
        def _():
            loss_ref[...] = jnp.broadcast_to(part, loss_ref.shape)
            dgt_ref[...] = dgt

        @pl.when(jnp.logical_not(first))
        def _():
            loss_ref[...] += jnp.broadcast_to(part, loss_ref.shape)
            dgt_ref[...] += dgt

    row = pl.BlockSpec((tile, d), lambda i: (i, 0))
    vec = pl.BlockSpec((1, d), lambda i: (0, 0))
    return pl.pallas_call(
        body, name=name, grid=(t // tile,),
        in_specs=[row, row, row, vec],
        out_specs=[pl.BlockSpec((1, LANES), lambda i: (0, 0)), row, row, vec],
        out_shape=[jax.ShapeDtypeStruct((1, LANES), F32), jax.ShapeDtypeStruct((t, d), F32),
                   jax.ShapeDtypeStruct((t, d), BF16), jax.ShapeDtypeStruct((1, d), F32)],
        compiler_params=_params(("arbitrary",)),
    )(x1, dn, target, gt2)


SCAN_BLOCK = 32
N_COL = 5
WIDE = 2 * LANES


def _wide_eye():
    i = lax.broadcasted_iota(jnp.int32, (HEAD_DIM, WIDE), 0)
    j = lax.broadcasted_iota(jnp.int32, (HEAD_DIM, WIDE), 1) % HEAD_DIM
    return (i == j).astype(BF16)


def _wide_ones():
    i = lax.broadcasted_iota(jnp.int32, (WIDE, WIDE), 0) // HEAD_DIM
    j = lax.broadcasted_iota(jnp.int32, (WIDE, WIDE), 1) // HEAD_DIM
    return (i == j).astype(BF16)


COL_PIECES = (1, 2, 2, 2, 1)


def _col_tiles(refs, i, nq, eye, ones_bf):
    levels = max(COL_PIECES)
    pieces = [[] for _ in range(levels)]
    for ref, n_pieces in zip(refs, COL_PIECES):
        full = ref[pl.ds(i, 1), :]
        for q in range(nq):
            rest = full[:, q * WIDE:(q + 1) * WIDE]
            for level in range(n_pieces):
                part = rest.astype(BF16)
                pieces[level].append(part * eye)
                rest = rest - part.astype(F32)
    rows = nq * HEAD_DIM
    out = jnp.dot(jnp.concatenate(sum(pieces, []), axis=0), ones_bf, preferred_element_type=F32)
    start, place = 0, {}
    for level in range(levels):
        for n, n_pieces in enumerate(COL_PIECES):
            if n_pieces > level:
                place[level, n] = start
                start += rows
    tiles = []
    for n, n_pieces in enumerate(COL_PIECES):
        acc = out[place[0, n]:place[0, n] + rows]
        for level in range(1, n_pieces):
            acc = acc + out[place[level, n]:place[level, n] + rows]
        tiles.append(acc)
    return jnp.concatenate(tiles, axis=0).reshape(N_COL * nq, HEAD_DIM, WIDE)


def _bf16_round(x):
    bits = lax.bitcast_convert_type(x, jnp.uint32)
    bits = (bits + jnp.uint32(0x7FFF) + ((bits >> 16) & jnp.uint32(1))) & jnp.uint32(0xFFFF0000)
    return lax.bitcast_convert_type(bits, F32)


def _pair_tile(tiles_ref, n, p, nq):
    return tiles_ref[n * nq + p // 2, :, (p % 2) * LANES:(p % 2 + 1) * LANES]


def _scan_fwd(rem, w, wr, k, r, v):
    t, dr = v.shape
    npair, nq = dr // LANES, dr // WIDE
    tb = _pick(t, (SCAN_BLOCK,))

    def body(rem_ref, w_ref, wr_ref, k_ref, r_ref, v_ref, *rest):
        next_refs, (y_ref, hist_ref, last_ref, s_ref, sb_ref, tiles_a, tiles_b) = rest[:N_COL], rest[N_COL:]
        eye, ones_bf = _wide_eye(), _wide_ones()
        col_refs = (rem_ref, w_ref, wr_ref, k_ref, r_ref)

        @pl.when(pl.program_id(0) == 0)
        def _():
            s_ref[...] = jnp.zeros_like(s_ref)
            sb_ref[...] = jnp.zeros_like(sb_ref)
            tiles_a[...] = _col_tiles(col_refs, 0, nq, eye, ones_bf)

        def step(i, tiles_ref):
            v_full, y_rows = v_ref[pl.ds(i, 1), :], []
            for p in range(npair):
                s = s_ref[p]
                hist_ref[i, p] = s
                c_rem, c_w, c_wr, c_k, c_r = [_pair_tile(tiles_ref, n, p, nq) for n in range(N_COL)]
                sa = jnp.sum(sb_ref[p] * c_rem, axis=0, keepdims=True)
                s2 = s * c_w + c_wr * sa + c_k * v_full[:, p * LANES:(p + 1) * LANES]
                s2_b = _bf16_round(s2)
                y_rows.append(jnp.sum(s2_b * c_r, axis=0, keepdims=True))
                s_ref[p] = s2
                sb_ref[p] = s2_b
            y_ref[pl.ds(i, 1), :] = jnp.concatenate(y_rows, axis=1)

        def two_steps(i, ahead_refs, ahead_row):
            tiles_b[...] = _col_tiles(col_refs, i + 1, nq, eye, ones_bf)
            step(i, tiles_a)
            tiles_a[...] = _col_tiles(ahead_refs, ahead_row, nq, eye, ones_bf)
            step(i + 1, tiles_b)

        def loop_body(m, carry):
            two_steps(2 * m, col_refs, 2 * m + 2)
            return carry

        lax.fori_loop(0, tb // 2 - 1, loop_body, 0, unroll=2)
        two_steps(tb - 2, next_refs, 0)
        last_ref[...] = sb_ref[...]

    nblk = t // tb
    blk = pl.BlockSpec((tb, dr), lambda i: (i, 0))
    nxt = pl.BlockSpec((8, dr), lambda i: (jnp.minimum(i + 1, nblk - 1) * (tb // 8), 0))
    tiles = pltpu.VMEM((N_COL * nq, HEAD_DIM, WIDE), F32)
    state = pltpu.VMEM((npair, HEAD_DIM, LANES), F32)
    return pl.pallas_call(
        body, name="rwkv_scan_fwd", grid=(nblk,),
        in_specs=[blk] * 6 + [nxt] * N_COL,
        out_specs=[blk, pl.BlockSpec((tb, npair, HEAD_DIM, LANES), lambda i: (i, 0, 0, 0)),
                   pl.BlockSpec((npair, HEAD_DIM, LANES), lambda i: (0, 0, 0))],
        out_shape=[jax.ShapeDtypeStruct((t, dr), F32), jax.ShapeDtypeStruct((t, npair, HEAD_DIM, LANES), F32),
                   jax.ShapeDtypeStruct((npair, HEAD_DIM, LANES), F32)],
        scratch_shapes=[state, state, tiles, tiles],
        compiler_params=_params(("arbitrary",)),
    )(rem, w, wr, k, r, v, rem, w, wr, k, r)


def _scan_bwd(rem, w, wr, k, r, v, hist, last, dy):
    t, dr = v.shape
    npair, nq = dr // LANES, dr // WIDE
    tb = _pick(t, (SCAN_BLOCK,))
    nblk = t // tb

    def body(rem_ref, w_ref, wr_ref, k_ref, r_ref, v_ref, hist_ref, last_ref, dy_ref, *rest):
        prev_refs, out_refs = rest[:N_COL], rest[N_COL:2 * N_COL]
        dv_ref, ds_ref, next_ref, tiles_a, tiles_b = rest[2 * N_COL:]
        eye, ones_bf = _wide_eye(), _wide_ones()
        col_refs = (rem_ref, w_ref, wr_ref, k_ref, r_ref)

        @pl.when(pl.program_id(0) == 0)
        def _():
            ds_ref[...] = jnp.zeros_like(ds_ref)
            next_ref[...] = last_ref[...]
            tiles_a[...] = _col_tiles(col_refs, tb - 1, nq, eye, ones_bf)

        def step(i, tiles_ref):
            grads = [[None] * npair for _ in range(N_COL)]
            v_full, dy_full, dv_rows = v_ref[pl.ds(i, 1), :], dy_ref[pl.ds(i, 1), :], []
            for p in range(npair):
                lanes = slice(p * LANES, (p + 1) * LANES)
                s = hist_ref[i, p]
                c_rem, c_w, c_wr, c_k, c_r = [_pair_tile(tiles_ref, n, p, nq) for n in range(N_COL)]
                v_row, dy_row = v_full[:, lanes], dy_full[:, lanes]
                s_b = _bf16_round(s)
                s2_b = next_ref[p]
                next_ref[p] = s_b
                sa = jnp.sum(s_b * c_rem, axis=0, keepdims=True)
                d2 = ds_ref[p] + c_r * dy_row
                dsa = jnp.sum(d2 * c_wr, axis=0, keepdims=True)
                dv_rows.append(jnp.sum(d2 * c_k, axis=0, keepdims=True))
                ds_ref[p] = d2 * c_w + c_rem * dsa
                for n, tile in enumerate((s_b * dsa, d2 * s, d2 * sa, d2 * v_row, s2_b * dy_row)):
                    grads[n][p] = tile.astype(BF16)
            wide = [jnp.concatenate(grads[n][2 * q:2 * q + 2], axis=1) for n in range(N_COL) for q in range(nq)]
            sums = jnp.dot(jnp.concatenate(wide, axis=0), ones_bf, preferred_element_type=F32)
            sums = sums.reshape(N_COL * nq, HEAD_DIM, WIDE)
            rows = jnp.sum(sums * eye.astype(F32)[None], axis=1)
            dv_ref[pl.ds(i, 1), :] = jnp.concatenate(dv_rows, axis=1)
            for n, o_ref in enumerate(out_refs):
                o_ref[pl.ds(i, 1), :] = jnp.concatenate([rows[n * nq + q:n * nq + q + 1] for q in range(nq)], axis=1)

        def two_steps(i, ahead_refs, ahead_row):
            tiles_b[...] = _col_tiles(col_refs, i - 1, nq, eye, ones_bf)
            step(i, tiles_a)
            tiles_a[...] = _col_tiles(ahead_refs, ahead_row, nq, eye, ones_bf)
            step(i - 1, tiles_b)

        def loop_body(m, carry):
            i = tb - 1 - 2 * m
            two_steps(i, col_refs, i - 2)
            return carry

        lax.fori_loop(0, tb // 2 - 1, loop_body, 0, unroll=2)
        two_steps(1, prev_refs, 7)

    blk = pl.BlockSpec((tb, dr), lambda i: (nblk - 1 - i, 0))
    prv = pl.BlockSpec((8, dr), lambda i: (jnp.maximum((nblk - 1 - i) * (tb // 8) - 1, 0), 0))
    tiles = pltpu.VMEM((N_COL * nq, HEAD_DIM, WIDE), F32)
    state = pltpu.VMEM((npair, HEAD_DIM, LANES), F32)
    return pl.pallas_call(
        body, name="rwkv_scan_bwd", grid=(nblk,),
        in_specs=[blk] * 6 + [pl.BlockSpec((tb, npair, HEAD_DIM, LANES), lambda i: (nblk - 1 - i, 0, 0, 0)),
                              pl.BlockSpec((npair, HEAD_DIM, LANES), lambda i: (0, 0, 0)), blk] + [prv] * N_COL,
        out_specs=[blk] * 6,
        out_shape=[jax.ShapeDtypeStruct((t, dr), F32)] * 6,
        scratch_shapes=[state, state, tiles, tiles],
        compiler_params=_params(("arbitrary",)),
    )(rem, w, wr, k, r, v, hist, last, dy, rem, w, wr, k, r)


SB_BLOCK = 256
SB_HEADS = 8
SB_HEADS_BWD = 4
NT_DIMS = (((1,), (1,)), ((), ()))
TN_DIMS = (((0,), (0,)), ((), ()))
SB_SCALE = 1.0 / math.sqrt(HEAD_DIM)


def _dot2(x, tri):
    hi = x.astype(BF16)
    mid = (x - hi.astype(F32)).astype(BF16)
    return jnp.dot(hi, tri, preferred_element_type=F32) + jnp.dot(mid, tri, preferred_element_type=F32)


def _sb_block_iotas(bs):
    return lax.broadcasted_iota(jnp.int32, (bs, bs), 0), lax.broadcasted_iota(jnp.int32, (bs, bs), 1)


def _when_step(group, block):
    return pl.when(jnp.logical_and(pl.program_id(0) == group, pl.program_id(1) == block))


def _sb_fwd(q, k, v, shards):
    h, t, d = q.shape
    bs = _pick(t, (SB_BLOCK,))
    nh = _pick(h, (SB_HEADS,))
    ngroup, nblock = h // nh, t // bs
    nw = len(shards)
    heights = [s.shape[0] for s in shards]

    def body(q_ref, k_ref, v_ref, *rest):
        w_in, (o_ref, l_ref), w_out, sems = rest[:nw], rest[nw:nw + 2], rest[nw + 2:2 * nw + 2], rest[2 * nw + 2:]

        @_when_step(0, 0)
        def _():
            _gather_phase("issue", w_in, w_out, sems, heights)

        @_when_step(ngroup - 1, (2 * nblock) // 3)
        def _():
            _gather_phase("forward", w_in, w_out, sems, heights)

        qi = pl.program_id(1)
        ri, ci = _sb_block_iotas(bs)
        tri_ge = (ri >= ci).astype(BF16)
        causal = ci < ri
        qv = [q_ref[hh] for hh in range(nh)]

        def blocks(j, masked, carry):
            accs, tails = carry[:nh], carry[nh:]
            heads = range(nh)
            rows = pl.ds(pl.multiple_of(j * bs, bs), bs)
            z = [lax.dot_general(qv[hh], k_ref[hh, rows, :], NT_DIMS, preferred_element_type=F32) * SB_SCALE
                 for hh in heads]
            log1m = [-_softplus(z[hh]) for hh in heads]
            if masked:
                log1m = [jnp.where(causal, x, 0.0) for x in log1m]
            cs = [_dot2(log1m[hh], tri_ge) for hh in heads]
            a = [jnp.exp(z[hh] + cs[hh] + tails[hh]) for hh in heads]
            if masked:
                a = [jnp.where(causal, x, 0.0) for x in a]
            accs = [accs[hh] + jnp.dot(a[hh].astype(BF16), v_ref[hh, rows, :], preferred_element_type=F32)
                    for hh in heads]
            return tuple(accs) + tuple(tails[hh] + cs[hh][:, 0:1] for hh in heads)

        carry = blocks(qi, True, (jnp.zeros((bs, d), F32),) * nh + (jnp.zeros((bs, 1), F32),) * nh)
        carry = lax.fori_loop(0, qi, lambda n, c: blocks(qi - 1 - n, False, c), carry)
        for hh in range(nh):
            o_ref[hh] = carry[hh]
            l_ref[hh] = jnp.broadcast_to(carry[nh + hh], (bs, d))

        @_when_step(ngroup - 1, nblock - 1)
        def _():
            _gather_phase("finish", w_in, w_out, sems, heights)

    qs = pl.BlockSpec((nh, bs, d), lambda hh, i: (hh, i, 0))
    ks = pl.BlockSpec((nh, t, d), lambda hh, i: (hh, 0, 0), pipeline_mode=pl.Buffered(1))
    res = pl.pallas_call(
        body, name="sb_attn_fwd", grid=(ngroup, nblock),
        in_specs=[qs, ks, ks] + [ANY] * nw, out_specs=[qs, qs] + [ANY] * nw,
        out_shape=[jax.ShapeDtypeStruct((h, t, d), F32)] * 2 + _gather_out_shapes(shards),
        scratch_shapes=_gather_sems(nw),
        compiler_params=_params(("arbitrary", "arbitrary")),
    )(q, k, v, *shards)
    return res[0], res[1], res[2:]


def _sb_bwd(q, k, v, lsum, do, parts):
    h, t, d = q.shape
    bs = _pick(t, (SB_BLOCK,))
    nh = _pick(h, (SB_HEADS_BWD,))
    ngroup, nblock = h // nh, t // bs
    nw = len(parts)

    def body(q_ref, k_ref, v_ref, l_ref, do_ref, *rest):
        p_in, (dq_ref, dk_ref, dv_ref), p_out, sems = rest[:nw], rest[nw:nw + 3], rest[nw + 3:2 * nw + 3], rest[2 * nw + 3:]

        @_when_step(0, 0)
        def _():
            _exchange_phase("issue", p_in, p_out, sems)

        qi = pl.program_id(1)

        @pl.when(qi == 0)
        def _():
            dk_ref[...] = jnp.zeros_like(dk_ref)
            dv_ref[...] = jnp.zeros_like(dv_ref)

        ri, ci = _sb_block_iotas(bs)
        tri_lt = (ri < ci).astype(BF16)
        causal = ci < ri
        qv = [q_ref[hh] for hh in range(nh)]
        dob = [do_ref[hh].astype(BF16) for hh in range(nh)]
        ltot = [l_ref[hh][:, 0:1] for hh in range(nh)]

        def blocks(j, masked, carry):
            dq, pc, ec = carry[:nh], carry[nh:2 * nh], carry[2 * nh:]
            heads = range(nh)
            rows = pl.ds(pl.multiple_of(j * bs, bs), bs)
            z = [lax.dot_general(qv[hh], k_ref[hh, rows, :], NT_DIMS, preferred_element_type=F32) * SB_SCALE
                 for hh in heads]
            da = [lax.dot_general(dob[hh], v_ref[hh, rows, :], NT_DIMS, preferred_element_type=F32) for hh in heads]
            nsp = [-_softplus(z[hh]) for hh in heads]
            log1m = [jnp.where(causal, x, 0.0) for x in nsp] if masked else nsp
            below = [_dot2(log1m[hh], tri_lt) + pc[hh] for hh in heads]
            a = [jnp.exp(z[hh] + (ltot[hh] - below[hh])) for hh in heads]
            if masked:
                a = [jnp.where(causal, x, 0.0) for x in a]
            e = [a[hh] * da[hh] for hh in heads]
            ebelow = [_dot2(e[hh], tri_lt) + ec[hh] for hh in heads]
            dz = [e[hh] * jnp.exp(nsp[hh]) - jnp.exp(z[hh] + nsp[hh]) * ebelow[hh] for hh in heads]
            if masked:
                dz = [jnp.where(causal, x, 0.0) for x in dz]
            dzb = [(x * SB_SCALE).astype(BF16) for x in dz]
            for hh in heads:
                dv_ref[hh, rows, :] += lax.dot_general(a[hh].astype(BF16), dob[hh], TN_DIMS,
                                                       preferred_element_type=F32)
            for hh in heads:
                dk_ref[hh, rows, :] += lax.dot_general(dzb[hh], qv[hh], TN_DIMS, preferred_element_type=F32)
            dq = [dq[hh] + jnp.dot(dzb[hh], k_ref[hh, rows, :], preferred_element_type=F32) for hh in heads]
            pc = [pc[hh] + jnp.sum(log1m[hh], axis=1, keepdims=True) for hh in heads]
            ec = [ec[hh] + jnp.sum(e[hh], axis=1, keepdims=True) for hh in heads]
            return tuple(dq) + tuple(pc) + tuple(ec)

        zcol = jnp.zeros((bs, 1), F32)
        carry = lax.fori_loop(0, qi, lambda j, c: blocks(j, False, c),
                              (jnp.zeros((bs, d), F32),) * nh + (zcol,) * (2 * nh))
        carry = blocks(qi, True, carry)
        for hh in range(nh):
            dq_ref[hh] = carry[hh]

        @_when_step(ngroup - 1, nblock - 1)
        def _():
            _exchange_phase("finish", p_in, p_out, sems)

    qs = pl.BlockSpec((nh, bs, d), lambda hh, i: (hh, i, 0))
    ks = pl.BlockSpec((nh, t, d), lambda hh, i: (hh, 0, 0), pipeline_mode=pl.Buffered(1))
    res = pl.pallas_call(
        body, name="sb_attn_bwd", grid=(ngroup, nblock),
        in_specs=[qs, ks, ks, qs, qs] + [ANY] * nw, out_specs=[qs, ks, ks] + [ANY] * nw,
        out_shape=[jax.ShapeDtypeStruct((h, t, d), F32)] * 3 + [jax.ShapeDtypeStruct(p.shape, p.dtype) for p in parts],
        scratch_shapes=_exchange_sems(nw),
        compiler_params=_params(("arbitrary", "arbitrary")),
    )(q, k, v, lsum, do, *parts)
    return res[0], res[1], res[2], res[3:]


ANY = pl.BlockSpec(memory_space=pl.ANY)
IN_VMEM = pl.BlockSpec(memory_space=pltpu.VMEM)


def _coords():
    return lax.axis_index("x"), lax.axis_index("y"), lax.axis_index("c")


def _flip(v, bit):
    return 1 - v if bit else v


def _remote(src, dst, send_sem, recv_sem, device):
    return pltpu.make_async_remote_copy(src_ref=src, dst_ref=dst, send_sem=send_sem, recv_sem=recv_sem,
                                        device_id=device, device_id_type=MESH)


def _all_gather8(name, blk):
    m, n = blk.shape

    def body(x_ref, o_ref, send_sems, recv_sems, local_sem):
        x, y, c = _coords()
        own = pltpu.make_async_copy(x_ref, o_ref.at[4 * x + 2 * y + c], local_sem)
        own.start()
        peers = []
        for bits in range(1, N_DEV):
            px, py, pc = _flip(x, (bits >> 2) & 1), _flip(y, (bits >> 1) & 1), _flip(c, bits & 1)
            peers.append((px, py, pc))
        sends = []
        for k, peer in enumerate(peers):
            cp = _remote(x_ref, o_ref.at[4 * x + 2 * y + c], send_sems.at[k], recv_sems.at[k], peer)
            cp.start()
            sends.append(cp)
        for k, (px, py, pc) in enumerate(peers):
            slot = o_ref.at[4 * px + 2 * py + pc]
            _remote(slot, slot, send_sems.at[k], recv_sems.at[k], (px, py, pc)).wait_recv()
        for cp in sends:
            cp.wait_send()
        own.wait()

    return pl.pallas_call(
        body, name=name, out_shape=jax.ShapeDtypeStruct((N_DEV, m, n), blk.dtype),
        in_specs=[IN_VMEM], out_specs=IN_VMEM,
        scratch_shapes=[pltpu.SemaphoreType.DMA((N_DEV - 1,)), pltpu.SemaphoreType.DMA((N_DEV - 1,)),
                        pltpu.SemaphoreType.DMA],
        compiler_params=_params(),
    )(blk)


def _gather_weights(shards):
    nw = len(shards)
    heights = [s.shape[0] for s in shards]

    def body(*refs):
        ins, outs, sems = refs[:nw], refs[nw:2 * nw], refs[2 * nw:]
        _gather_phase("issue", ins, outs, sems, heights)
        _gather_phase("forward", ins, outs, sems, heights)
        _gather_phase("finish", ins, outs, sems, heights)

    return pl.pallas_call(
        body, name="gather_weights",
        out_shape=_gather_out_shapes(shards),
        in_specs=[ANY] * nw, out_specs=[ANY] * nw,
        scratch_shapes=_gather_sems(nw),
        compiler_params=_params(),
    )(*shards)


def _gather_out_shapes(shards):
    return [jax.ShapeDtypeStruct((N_CHIPS,) + s.shape, s.dtype) for s in shards]


def _gather_sems(nw):
    return [pltpu.SemaphoreType.DMA((3 * nw,))] * 4


def _gather_phase(phase, ins, outs, sems, heights):
    ici_send, ici_recv, d2d_send, d2d_recv = sems
    x, y, c = _coords()
    chip = 2 * x + y
    sibling = (x, y, 1 - c)
    for w, height in enumerate(heights):
        half = height // 2
        mine, theirs = pl.ds(c * half, half), pl.ds((1 - c) * half, half)
        for j, (a, b) in enumerate(XY_MASKS):
            px, py = _flip(x, a), _flip(y, b)
            k = 3 * w + j
            over_ici = _remote(ins[w].at[mine], outs[w].at[chip, mine], ici_send.at[k], ici_recv.at[k], (px, py, c))
            landed = outs[w].at[2 * px + py, mine]
            onward = _remote(landed, landed, d2d_send.at[k], d2d_recv.at[k], sibling)
            if phase == "issue":
                over_ici.start()
            elif phase == "forward":
                _remote(landed, landed, ici_send.at[k], ici_recv.at[k], (px, py, c)).wait_recv()
                onward.start()
            else:
                slot = outs[w].at[2 * px + py, theirs]
                _remote(slot, slot, d2d_send.at[k], d2d_recv.at[k], sibling).wait_recv()
                over_ici.wait_send()
                onward.wait_send()


def _pair_exchange(name, grads):
    nw = len(grads)

    def body(*refs):
        ins, outs = refs[:nw], refs[nw:2 * nw]
        send_sems, recv_sems = refs[2 * nw:]
        x, y, c = _coords()
        sibling = (x, y, 1 - c)
        sends = []
        for w in range(nw):
            half = grads[w].shape[1] // 2
            cp = _remote(ins[w].at[:, pl.ds((1 - c) * half, half)], outs[w], send_sems.at[w], recv_sems.at[w], sibling)
            cp.start()
            sends.append(cp)
        for w in range(nw):
            _remote(outs[w], outs[w], send_sems.at[w], recv_sems.at[w], sibling).wait_recv()
        for cp in sends:
            cp.wait_send()

    return pl.pallas_call(
        body, name=name,
        out_shape=[jax.ShapeDtypeStruct((N_CHIPS, g.shape[1] // 2, g.shape[2]), g.dtype) for g in grads],
        in_specs=[ANY] * nw, out_specs=[ANY] * nw,
        scratch_shapes=[pltpu.SemaphoreType.DMA((nw,))] * 2,
        compiler_params=_params(),
    )(*grads)


def _chip_exchange(name, parts):
    nw = len(parts)

    def body(*refs):
        ins, outs, sems = refs[:nw], refs[nw:2 * nw], refs[2 * nw:]
        _exchange_phase("issue", ins, outs, sems)
        _exchange_phase("finish", ins, outs, sems)

    return pl.pallas_call(
        body, name=name,
        out_shape=[jax.ShapeDtypeStruct(p.shape, p.dtype) for p in parts],
        in_specs=[ANY] * nw, out_specs=[ANY] * nw,
        scratch_shapes=_exchange_sems(nw),
        compiler_params=_params(),
    )(*parts)


def _exchange_sems(nw):
    return [pltpu.SemaphoreType.DMA((3 * nw,))] * 2


def _exchange_phase(phase, ins, outs, sems):
    send_sems, recv_sems = sems
    x, y, c = _coords()
    chip = 2 * x + y
    for w in range(len(ins)):
        for j, (a, b) in enumerate(XY_MASKS):
            px, py = _flip(x, a), _flip(y, b)
            k = 3 * w + j
            send = _remote(ins[w].at[2 * px + py], outs[w].at[chip], send_sems.at[k], recv_sems.at[k], (px, py, c))
            if phase == "issue":
                send.start()
            else:
                slot = outs[w].at[2 * px + py]
                _remote(slot, slot, send_sems.at[k], recv_sems.at[k], (px, py, c)).wait_recv()
                send.wait_send()


def _pair_share(shards):
    nw = len(shards)

    def body(*refs):
        bufs = refs[nw:2 * nw]
        send_sems, recv_sems = refs[2 * nw:]
        x, y, c = _coords()
        sibling = (x, y, 1 - c)
        sends = []
        for w in range(nw):
            half = shards[w].shape[0] // 2
            mine = bufs[w].at[pl.ds(c * half, half)]
            cp = _remote(mine, mine, send_sems.at[w], recv_sems.at[w], sibling)
            cp.start()
            sends.append(cp)
        for w in range(nw):
            half = shards[w].shape[0] // 2
            theirs = bufs[w].at[pl.ds((1 - c) * half, half)]
            _remote(theirs, theirs, send_sems.at[w], recv_sems.at[w], sibling).wait_recv()
        for cp in sends:
            cp.wait_send()

    return pl.pallas_call(
        body, name="grad_pair_share",
        out_shape=[jax.ShapeDtypeStruct(s.shape, s.dtype) for s in shards],
        in_specs=[ANY] * nw, out_specs=[ANY] * nw,
        input_output_aliases={w: w for w in range(nw)},
        scratch_shapes=[pltpu.SemaphoreType.DMA((nw,))] * 2,
        compiler_params=_params(),
    )(*shards)


TILE_BYTES = 2 * 1024 * 1024


def _row_tile(rows, cols, mult=8):
    best = None
    for tr in range(mult, rows + 1, mult):
        if rows % tr == 0 and tr * cols * 4 <= TILE_BYTES:
            best = tr
    return best if best is not None else rows


def _pair_add(name, grad, other, place):
    _, half, cols = other.shape
    tr = _row_tile(half, cols, mult=16)
    nb = half // tr

    def body(place_ref, g_ref, o_ref, out_ref):
        out_ref[...] = (g_ref[...] + o_ref[...]).astype(out_ref.dtype)

    return pl.pallas_call(
        body, name=name, out_shape=jax.ShapeDtypeStruct(other.shape, BF16),
        grid_spec=pltpu.PrefetchScalarGridSpec(
            num_scalar_prefetch=1, grid=(N_CHIPS, nb),
            in_specs=[pl.BlockSpec((None, tr, cols), lambda s, i, place_ref: (s, place_ref[0] * nb + i, 0)),
                      pl.BlockSpec((None, tr, cols), lambda s, i, place_ref: (s, i, 0))],
            out_specs=pl.BlockSpec((None, tr, cols), lambda s, i, place_ref: (s, i, 0))),
        compiler_params=_params(("parallel", "parallel")),
    )(place, grad, other)


def _sum_chips(name, own, others, place):
    _, half, cols = own.shape
    tr = _row_tile(half, cols, mult=16)
    nb = half // tr

    def body(place_ref, own_ref, a_ref, b_ref, c_ref, out_ref):
        total = own_ref[...].astype(F32) + a_ref[...].astype(F32)
        out_ref[...] = (total + b_ref[...].astype(F32)) + c_ref[...].astype(F32)

    def peer(mask):
        return pl.BlockSpec((None, tr, cols), lambda i, place_ref: (place_ref[1] ^ mask, i, 0))

    return pl.pallas_call(
        body, name=name, out_shape=jax.ShapeDtypeStruct((2 * half, cols), F32),
        grid_spec=pltpu.PrefetchScalarGridSpec(
            num_scalar_prefetch=1, grid=(nb,),
            in_specs=[peer(0), peer(1), peer(2), peer(3)],
            out_specs=pl.BlockSpec((tr, cols), lambda i, place_ref: (place_ref[0] * nb + i, 0))),
        compiler_params=_params(("parallel",)),
    )(place, own, others, others, others)


def _adamw_math(w, g, m, v):
    m2 = ADAM_B1 * m + (1.0 - ADAM_B1) * g
    v2 = ADAM_B2 * v + (1.0 - ADAM_B2) * (g * g)
    m_hat = m2 / (1.0 - ADAM_B1 ** ADAM_STEP)
    v_hat = v2 / (1.0 - ADAM_B2 ** ADAM_STEP)
    delta = -ADAM_LR * (m_hat / (jnp.sqrt(v_hat) + ADAM_EPS) + ADAM_WD * w)
    return delta, m2, v2


def _adamw(name, w, g, m, v):
    rows, cols = w.shape
    tr = _row_tile(rows, cols)

    def body(w_ref, g_ref, m_ref, v_ref, d_ref, m2_ref, v2_ref):
        d_ref[...], m2_ref[...], v2_ref[...] = _adamw_math(w_ref[...], g_ref[...], m_ref[...], v_ref[...])

    blk = pl.BlockSpec((tr, cols), lambda i: (i, 0))
    return pl.pallas_call(
        body, name=name, out_shape=[jax.ShapeDtypeStruct(w.shape, F32)] * 3, grid=(rows // tr,),
        in_specs=[blk] * 4, out_specs=[blk] * 3,
        compiler_params=_params(("parallel",)),
    )(w, g, m, v)


def _small_update(gathered, w, m, v):
    def body(gs_ref, w_ref, m_ref, v_ref, g_ref, d_ref, m2_ref, v2_ref):
        g = gs_ref[0]
        for dev in range(1, N_DEV):
            g = g + gs_ref[dev]
        g_ref[...] = g
        d_ref[...], m2_ref[...], v2_ref[...] = _adamw_math(w_ref[...], g, m_ref[...], v_ref[...])

    return pl.pallas_call(
        body, name="small_update", out_shape=[jax.ShapeDtypeStruct(w.shape, F32)] * 4,
        compiler_params=_params(),
    )(gathered, w, m, v)


def _ada_mod(c_all, w_shard, b_shard):
    d, n = w_shard.shape
    tn = _pick(n, (512, 256, 128))

    def body(c_ref, w_ref, b_ref, o_ref):
        act = _silu(c_ref[...]).astype(BF16)
        o_ref[...] = jnp.dot(act, w_ref[...].astype(BF16), preferred_element_type=F32) + b_ref[...]

    return pl.pallas_call(
        body, name="ada_mod", out_shape=jax.ShapeDtypeStruct((c_all.shape[0], n), F32), grid=(n // tn,),
        in_specs=[pl.BlockSpec(c_all.shape, lambda j: (0, 0)), pl.BlockSpec((d, tn), lambda j: (0, j)),
                  pl.BlockSpec((1, tn), lambda j: (0, j))],
        out_specs=pl.BlockSpec((c_all.shape[0], tn), lambda j: (0, j)),
        compiler_params=_params(("parallel",)),
    )(c_all, w_shard, b_shard)


def _ada_grad(c_pad, dmod_pad):
    rows, d = c_pad.shape
    n = dmod_pad.shape[1]
    tn = _pick(n, (512, 256, 128))

    def body(c_ref, g_ref, o_ref):
        act = _silu(c_ref[...]).astype(BF16)
        o_ref[...] = lax.dot_general(act, g_ref[...].astype(BF16), TN_DIMS, preferred_element_type=F32)

    return pl.pallas_call(
        body, name="ada_grad", out_shape=jax.ShapeDtypeStruct((d, n), F32), grid=(n // tn,),
        in_specs=[pl.BlockSpec((rows, d), lambda j: (0, 0)), pl.BlockSpec((rows, tn), lambda j: (0, j))],
        out_specs=pl.BlockSpec((d, tn), lambda j: (0, j)),
        compiler_params=_params(("parallel",)),
    )(c_pad, dmod_pad)


WEIGHTS = ['w_ada', 'b_ada', 'norm1_gain', 'norm2_gain', 'w_in', 'mu_rkv', 'mu_w', 'mu_a', 'mu_g', 'w0', 'w1',
           'w2', 'a0', 'a1', 'a2', 'g1', 'g2', 'k_k', 'k_a', 'r_k', 'ln_x_gain', 'ln_x_bias', 'q_norm_gain',
           'k_norm_gain', 'w_out', 'w_gate_up', 'w_down']
SMALL = ['b_ada', 'norm1_gain', 'norm2_gain', 'mu_rkv', 'mu_w', 'mu_a', 'mu_g', 'w0', 'a0', 'k_k', 'k_a', 'r_k',
         'ln_x_gain', 'ln_x_bias', 'q_norm_gain', 'k_norm_gain']
PACK_ROWS = 8
LOSS_SLOT = LANES


def _shift_down(a):
    return jnp.pad(a[:-1], ((1, 0), (0, 0)))


def _shift_up(a):
    return jnp.pad(a[1:], ((0, 1), (0, 0)))


def _pack_small(vals):
    flat = jnp.concatenate([v.reshape(1, -1) for v in vals], axis=1)
    unit = PACK_ROWS * LANES
    total = -(-flat.shape[1] // unit) * unit
    flat = jnp.pad(flat, ((0, 0), (0, total - flat.shape[1])))
    return flat.reshape(PACK_ROWS, total // PACK_ROWS)


def kernel(x, c, w_ada, b_ada, norm1_gain, norm2_gain, w_in, mu_rkv, mu_w, mu_a, mu_g, w0, w1, w2, a0, a1, a2, g1, g2, k_k, k_a, r_k, ln_x_gain, ln_x_bias, q_norm_gain, k_norm_gain, w_out, w_gate_up, w_down, loss_target, m_w_ada, m_b_ada, m_norm1_gain, m_norm2_gain, m_w_in, m_mu_rkv, m_mu_w, m_mu_a, m_mu_g, m_w0, m_w1, m_w2, m_a0, m_a1, m_a2, m_g1, m_g2, m_k_k, m_k_a, m_r_k, m_ln_x_gain, m_ln_x_bias, m_q_norm_gain, m_k_norm_gain, m_w_out, m_w_gate_up, m_w_down, v_w_ada, v_b_ada, v_norm1_gain, v_norm2_gain, v_w_in, v_mu_rkv, v_mu_w, v_mu_a, v_mu_g, v_w0, v_w1, v_w2, v_a0, v_a1, v_a2, v_g1, v_g2, v_k_k, v_k_a, v_r_k, v_ln_x_gain, v_ln_x_bias, v_q_norm_gain, v_k_norm_gain, v_w_out, v_w_gate_up, v_w_down):
    given = dict(locals())
    wt = {n: given[n][0] for n in WEIGHTS}
    mom = {n: given["m_" + n][0] for n in WEIGHTS}
    var = {n: given["v_" + n][0] for n in WEIGHTS}
    for tree in (wt, mom, var):
        tree["b_ada"] = tree["b_ada"].reshape(1, -1)
        for n in SMALL[1:]:
            tree[n] = tree[n].reshape(1, -1)

    ax, ay, ac = _coords()
    chip = 2 * ax + ay
    dev = 4 * ax + 2 * ay + ac
    xs, target = x[0], loss_target[0]
    t, d = xs.shape
    dr = wt["w0"].shape[1]
    ds = d - dr
    nh = ds // HEAD_DIM
    dff = wt["w_down"].shape[0] * N_CHIPS
    n_ada = wt["w_ada"].shape[1]
    lw, la, lg = wt["w1"].shape[1], wt["a1"].shape[1], wt["g1"].shape[1]

    def lora_a(tree):
        return jnp.concatenate([tree["w1"], tree["a1"], tree["g1"]], axis=1)

    def lora_b(tree):
        return jnp.concatenate([tree["w2"], tree["a2"], tree["g2"]], axis=0)

    def with_own_slot(gathered, own):
        return [lax.dynamic_update_slice(full, shard[None], (chip, 0, 0)) for full, shard in zip(gathered, own)]

    first_shards = [s.astype(BF16) for s in (wt["w_in"], lora_a(wt), lora_b(wt))]
    later_shards = [s.astype(BF16) for s in (wt["w_out"], wt["w_gate_up"], wt["w_down"])]
    full_in, full_la, full_lb = with_own_slot(_gather_weights(first_shards), first_shards)
    full_la = full_la.reshape(d, lw + la + lg).astype(F32)
    full_lb = full_lb.transpose(1, 0, 2).reshape(lw + la + lg, dr).astype(F32)
    w1f, a1f, g1f = full_la[:, :lw], full_la[:, lw:lw + la], full_la[:, lw + la:]
    w2f, a2f, g2f = full_lb[:lw], full_lb[lw:lw + la], full_lb[lw + la:]

    c_all = _all_gather8("gather_c", c.reshape(PACK_ROWS, d // PACK_ROWS)).reshape(N_DEV, d)
    b_shard = lax.dynamic_slice(wt["b_ada"], (0, chip * n_ada), (1, n_ada))
    mod_part = _ada_mod(c_all, wt["w_ada"], b_shard)
    mod_all = _all_gather8("gather_mod", mod_part)[::2]
    mod = lax.dynamic_slice(mod_all, (0, dev, 0), (N_CHIPS, 1, n_ada)).reshape(1, N_CHIPS * n_ada)
    sh1, sc1, gt1, sh2, sc2, gt2 = [mod[:, i * d:(i + 1) * d] for i in range(6)]

    h, h_bf = _rowwise("norm1", _fn_norm1, [xs], [wt["norm1_gain"], sc1, sh1], [(d, F32), (d, BF16)], 256)
    hp = _shift_down(h)
    p = _matmul("mm_in", h_bf, full_in, b_shards=True, tm=512, tn=1536, tk=2048, n_outer=True)
    p_rkv, p_sb = (p, 3 * dr, 0), (p, 3 * ds, 1)
    pp = _shift_down(p[:, :3 * dr])
    pre_rows = [h, hp, p_rkv, pp]
    pre_params = [wt["mu_rkv"], wt["mu_w"], wt["mu_a"], wt["mu_g"], wt["w0"], wt["a0"], wt["k_k"], wt["k_a"],
                  w1f, w2f, a1f, a2f, g1f, g2f]
    pre = _rowwise("rwkv_pre", _fn_rwkv_pre, pre_rows, pre_params, [(dr, F32)] * 7, 128)
    r_, w_, k2, v_, rem, wr, g_ = pre
    y_raw, hist, s_last = _scan_fwd(rem, w_, wr, k2, r_, v_)
    post_rows = [y_raw, r_, k2, v_, g_]
    post_params = [wt["ln_x_gain"], wt["ln_x_bias"], wt["r_k"]]

    qg = jnp.tile(wt["q_norm_gain"], (1, nh))
    kg = jnp.tile(wt["k_norm_gain"], (1, nh))
    qn, kn, vs = _rowwise("qk_norm", _fn_qk_norm, [p_sb], [qg, kg], [(ds, BF16)] * 3, 256)

    def to_heads(a):
        return a.reshape(t, nh, HEAD_DIM).transpose(1, 0, 2)

    def from_heads(a):
        return a.transpose(1, 0, 2).reshape(t, ds)

    qh, kh, vh = to_heads(qn), to_heads(kn), to_heads(vs)
    o_h, lsum, later_full = _sb_fwd(qh, kh, vh, later_shards)
    full_out, full_gu, full_down = with_own_slot(later_full, later_shards)
    full_out = full_out.reshape(d, d)
    full_down = full_down.reshape(dff, d)
    (ycat,) = _rowwise("rwkv_post", _fn_rwkv_post_cat, post_rows + [from_heads(o_h)], post_params, [(d, BF16)], 256)
    mix = _matmul("mm_out", ycat, full_out, tm=512, tn=1024, tk=2048, n_outer=True)
    norm2_params = [gt1, wt["norm2_gain"], sc2, sh2]
    x1, h2 = _rowwise("mix_norm2", _fn_mix_norm2, [xs, mix], norm2_params, [(d, F32), (d, BF16)], 256)
    gu = _matmul("mm_gate_up", h2, full_gu, b_shards=True, tm=512, tn=1408, tk=2048, n_outer=True)
    gate_up = [(gu, dff, 0), (gu, dff, 1)]
    (act,) = _rowwise("swiglu", _fn_swiglu, gate_up, [], [(dff, BF16)], 256)
    dn = _matmul("mm_down", act, full_down, tm=1024, tn=1024, tk=2816)
    loss_vec, dout, ddn, dgt2 = _loss_head("loss_head", x1, dn, target, gt2)

    dact = _matmul("mm_down_dx", ddn, full_down, tb=True, tm=512, tn=1408, tk=2048, n_outer=True)
    gw_down = _matmul("mm_down_dw", act, ddn, ta=True, tm=1408, tn=1024, tk=1024)
    dgu = _swiglu_bwd(gu, dact)
    dh2 = _matmul("mm_gate_up_dx", dgu, full_gu, tb=True, b_shards=True, tm=1024, tn=1024, tk=2816)
    gw_gu = _matmul("mm_gate_up_dw", h2, dgu, ta=True, out_shards=True, tm=1024, tn=1408, tk=1024)
    (dx_a, dmix), (dgt1, dgain2, dsc2, dsh2) = _rowwise_bwd(
        "mix_norm2_bwd", _fn_mix_norm2, [xs, mix], norm2_params, [[dout], [dh2]], [F32, BF16], [True] * 4, 128)
    dycat = _matmul("mm_out_dx", dmix, full_out, tb=True, tm=512, tn=1024, tk=2048, n_outer=True)
    gw_out = _matmul("mm_out_dw", ycat, dmix, ta=True, tm=1024, tn=1024, tk=1024)
    (dy_raw, dr_f, dk_f, dv_f, dg), (dlng, dlnb, drk) = _rowwise_bwd(
        "rwkv_post_bwd", _fn_rwkv_post, post_rows, post_params, [[(dycat, dr, 0)]], [F32] * 5, [True] * 3, 128)
    place = jnp.stack([ac, chip]).astype(jnp.int32)

    def pair_reduce(tag, names, grads):
        from_sibling = _pair_exchange("grad_pair_exchange_" + tag, grads)
        return [_pair_add("pair_add_" + n, g, o, place) for n, g, o in zip(names, grads, from_sibling)]

    early_names = ["w_out", "w_gate_up", "w_down"]
    early_sums = pair_reduce("early", early_names, [gw_out.reshape(N_CHIPS, d // N_CHIPS, d), gw_gu,
                                                    gw_down.reshape(N_CHIPS, dff // N_CHIPS, d)])
    do_h = to_heads(dycat[:, dr:])
    dqh, dkh, dvh, early_from_chips = _sb_bwd(qh, kh, vh, lsum, do_h, early_sums)
    drem_s, dw_s, dwr_s, dk_s, dr_s, dv_s = _scan_bwd(rem, w_, wr, k2, r_, v_, hist, s_last, dy_raw)
    (dp_sb,), (dqg, dkg) = _rowwise_bwd(
        "qk_norm_bwd", _fn_qk_norm, [p_sb], [qg, kg], [[from_heads(dqh)], [from_heads(dkh)], [from_heads(dvh)]],
        [F32], [True, True], 128)
    pre_cts = [[dr_s, dr_f], [dw_s], [dk_s, dk_f], [dv_s, dv_f], [drem_s], [dwr_s], [dg]]
    (dh_a, dhp, dp_rkv, dpp), pre_g = _rowwise_bwd(
        "rwkv_pre_bwd", _fn_rwkv_pre, pre_rows, pre_params, pre_cts, [F32] * 4, [True] * 14, 128)
    dp = jnp.concatenate([dp_rkv + _shift_up(dpp), dp_sb], axis=1).astype(BF16)
    dh_mm = _matmul("mm_in_dx", dp, full_in, tb=True, b_shards=True, tm=1024, tn=1024, tk=1536)
    gw_in = _matmul("mm_in_dw", h_bf, dp, ta=True, out_shards=True, tm=1024, tn=1536, tk=1024)
    (grad_x,), (dgain1, dsc1, dsh1) = _rowwise_bwd(
        "norm1_bwd", _fn_norm1, [xs], [wt["norm1_gain"], sc1, sh1], [[dh_a, dh_mm, _shift_up(dhp)], []],
        [F32], [True] * 3, 128, add_to_first=[dx_a])

    g_mu_rkv, g_mu_w, g_mu_a, g_mu_g, g_w0, g_a0, g_kk, g_ka, gw1, gw2, ga1, ga2, gg1, gg2 = pre_g
    g_la = jnp.concatenate([gw1, ga1, gg1], axis=1).reshape(N_CHIPS, d // N_CHIPS, lw + la + lg)
    g_lb = jnp.concatenate([gw2, ga2, gg2], axis=0)
    g_lb = g_lb.reshape(lw + la + lg, N_CHIPS, dr // N_CHIPS).transpose(1, 0, 2)
    late_names = ["w_in", "lora_a", "lora_b"]
    late_sums = pair_reduce("late", late_names, [gw_in, g_la, g_lb])
    late_from_chips = _chip_exchange("grad_chip_exchange", late_sums)
    halves = [_sum_chips("chip_sum_" + n, p, q, place)
              for n, p, q in zip(early_names + late_names, early_sums + late_sums,
                                 list(early_from_chips) + list(late_from_chips))]
    r_out, r_gu, r_down, r_in, r_la, r_lb = _pair_share(halves)

    dmod = jnp.concatenate([dsh1, dsc1, dgt1, dsh2, dsc2, dgt2], axis=1)
    dqg = dqg.reshape(nh, HEAD_DIM).sum(axis=0, keepdims=True)
    dkg = dkg.reshape(nh, HEAD_DIM).sum(axis=0, keepdims=True)
    small_g = [dmod, dgain1, dgain2, g_mu_rkv, g_mu_w, g_mu_a, g_mu_g, g_w0, g_a0, g_kk, g_ka, drk, dlng, dlnb,
               dqg, dkg]
    lead = jnp.zeros((1, LOSS_SLOT), F32)
    packed = _pack_small([loss_vec] + small_g)
    gathered = _all_gather8("gather_small", packed)
    sm_g, sm_d, sm_m, sm_v = _small_update(gathered, _pack_small([lead] + [wt[n] for n in SMALL]),
                                           _pack_small([lead] + [mom[n] for n in SMALL]),
                                           _pack_small([lead] + [var[n] for n in SMALL]))
    loss = sm_g.reshape(-1)[0]

    def unpack(packed_arr):
        flat, out, pos = packed_arr.reshape(-1), {}, LOSS_SLOT
        for n in SMALL:
            size = wt[n].size
            out[n] = flat[pos:pos + size]
            pos += size
        return out

    res = {"grad": unpack(sm_g), "delta": unpack(sm_d), "m": unpack(sm_m), "v": unpack(sm_v)}

    dmod_all = gathered.reshape(N_DEV, -1)[:, LOSS_SLOT:LOSS_SLOT + N_CHIPS * n_ada]
    dmod_cols = lax.dynamic_slice(dmod_all, (0, chip * n_ada), (N_DEV, n_ada))
    pad8 = ((0, N_DEV), (0, 0))
    res["grad"]["w_ada"] = _ada_grad(jnp.pad(c_all, pad8), jnp.pad(dmod_cols, pad8))

    res["grad"].update(w_in=r_in, w_out=r_out, w_gate_up=r_gu, w_down=r_down)
    for n in ("w_ada", "w_in", "w_out", "w_gate_up", "w_down"):
        res["delta"][n], res["m"][n], res["v"][n] = _adamw("adamw_" + n, wt[n], res["grad"][n], mom[n], var[n])
    la_d, la_m, la_v = _adamw("adamw_lora_a", lora_a(wt), r_la, lora_a(mom), lora_a(var))
    lb_d, lb_m, lb_v = _adamw("adamw_lora_b", lora_b(wt), r_lb, lora_b(mom), lora_b(var))
    for key, pa, pb in (("grad", r_la, r_lb), ("delta", la_d, lb_d), ("m", la_m, lb_m), ("v", la_v, lb_v)):
        res[key].update(w1=pa[:, :lw], a1=pa[:, lw:lw + la], g1=pa[:, lw + la:],
                        w2=pb[:lw], a2=pb[lw:lw + la], g2=pb[lw + la:])

    outs = [loss, grad_x[None]]
    for key in ("grad", "delta", "m", "v"):
        outs += [res[key][n].reshape(given[n].shape) for n in WEIGHTS]
    return tuple(outs)
```

```python
import functools
import math

import jax
import jax.numpy as jnp
from jax import lax
from jax.experimental import pallas as pl
from jax.experimental.pallas import tpu as pltpu

F32 = jnp.float32
BF16 = jnp.bfloat16
HEAD_DIM = 64
LANES = 128
RMS_EPS = 1e-6
GN_EPS = 64e-5
L2_EPS = 1e-12
ADAM_LR, ADAM_B1, ADAM_B2, ADAM_EPS, ADAM_WD, ADAM_STEP = 0.001, 0.9, 0.999, 1e-08, 0.01, 10
VMEM_LIMIT = 56 * 1024 * 1024
MESH = pl.DeviceIdType.MESH
HI = lax.Precision.HIGHEST
N_CHIPS = 4
N_DEV = 8
XY_MASKS = ((1, 0), (0, 1), (1, 1))


def _pick(dim, prefs):
    for p in prefs:
        if dim % p == 0:
            return p
    return dim


def _params(sem=None, vmem=VMEM_LIMIT):
    return pltpu.CompilerParams(dimension_semantics=sem, vmem_limit_bytes=vmem)


def _sigmoid(x):
    return 1.0 / (1.0 + jnp.exp(-x))


@jax.custom_vjp
def _softplus(x):
    return jnp.maximum(x, 0.0) + jnp.log(1.0 + jnp.exp(-jnp.abs(x)))


_softplus.defvjp(lambda x: (_softplus(x), x), lambda x, g: (g * _sigmoid(x),))


def _silu(x):
    return x * _sigmoid(x)


@jax.custom_vjp
def _bdot(a, b):
    return jnp.dot(a.astype(BF16), b.astype(BF16), preferred_element_type=F32)


def _bdot_bwd(res, g):
    a, b = res
    gb = g.astype(BF16)
    da = lax.dot_general(gb, b.astype(BF16), (((1,), (1,)), ((), ())), preferred_element_type=F32)
    db = lax.dot_general(a.astype(BF16), gb, (((0,), (0,)), ((), ())), preferred_element_type=F32)
    return da.astype(a.dtype), db.astype(b.dtype)


_bdot.defvjp(lambda a, b: (_bdot(a, b), (a, b)), _bdot_bwd)


def _head_ones():
    i = lax.broadcasted_iota(jnp.int32, (LANES, LANES), 0) // HEAD_DIM
    j = lax.broadcasted_iota(jnp.int32, (LANES, LANES), 1) // HEAD_DIM
    return (i == j).astype(F32)


def _hdot(x, ones_bf):
    hi = x.astype(BF16)
    rest = x - hi.astype(F32)
    mid = rest.astype(BF16)
    lo = (rest - mid.astype(F32)).astype(BF16)
    out = jnp.dot(hi, ones_bf, preferred_element_type=F32)
    out += jnp.dot(mid, ones_bf, preferred_element_type=F32)
    return out + jnp.dot(lo, ones_bf, preferred_element_type=F32)


@jax.custom_vjp
def _segsum(x):
    ones = _head_ones().astype(BF16)
    parts = [_hdot(x[:, LANES * j:LANES * (j + 1)], ones) for j in range(x.shape[1] // LANES)]
    return parts[0] if len(parts) == 1 else jnp.concatenate(parts, axis=1)


_segsum.defvjp(lambda x: (_segsum(x), None), lambda _, g: (_segsum(g),))


def _rms(x, gain):
    return x * lax.rsqrt(jnp.mean(x * x, axis=-1, keepdims=True) + RMS_EPS) * gain


def _matmul(name, a, b, *, ta=False, tb=False, b_shards=False, out_shards=False, out_dtype=F32,
            tm=512, tn=512, tk=512, n_outer=False):
    if ta:
        kdim, m = a.shape
    else:
        m, kdim = a.shape
    if b_shards:
        if tb:
            n, ks = b.shape[1], b.shape[2]
            assert ks * N_CHIPS == kdim
        else:
            ns = b.shape[2]
            n = ns * N_CHIPS
            assert b.shape[1] == kdim
    else:
        n = b.shape[0] if tb else b.shape[1]
    tm = _pick(m, (tm, 512, 256, 128))
    n_part = n // N_CHIPS if (out_shards or (b_shards and not tb)) else n
    tn = _pick(n_part, (tn, 512, 256, 128))
    k_part = kdim // N_CHIPS if (b_shards and tb) else kdim
    tk = _pick(k_part, (tk, 512, 256, 128))
    nb = n_part // tn
    kb = k_part // tk
    nk = kdim // tk
    grid = (n // tn, m // tm, nk) if n_outer else (m // tm, n // tn, nk)

    def spec(shape, index):
        if n_outer:
            return pl.BlockSpec(shape, lambda j, i, k: index(i, j, k))
        return pl.BlockSpec(shape, index)

    if ta:
        a_spec = spec((tk, tm), lambda i, j, k: (k, i))
    else:
        a_spec = spec((tm, tk), lambda i, j, k: (i, k))
    if b_shards and tb:
        b_spec = spec((None, tn, tk), lambda i, j, k: (k // kb, j, k % kb))
    elif b_shards:
        b_spec = spec((None, tk, tn), lambda i, j, k: (j // nb, k, j % nb))
    elif tb:
        b_spec = spec((tn, tk), lambda i, j, k: (j, k))
    else:
        b_spec = spec((tk, tn), lambda i, j, k: (k, j))
    if out_shards:
        o_spec = spec((None, tm, tn), lambda i, j, k: (j // nb, i, j % nb))
        o_shape = jax.ShapeDtypeStruct((N_CHIPS, m, n_part), out_dtype)
    else:
        o_spec = spec((tm, tn), lambda i, j, k: (i, j))
        o_shape = jax.ShapeDtypeStruct((m, n), out_dtype)
    dims = (((0 if ta else 1,), (1 if tb else 0,)), ((), ()))

    def product(a_ref, b_ref):
        return lax.dot_general(a_ref[...].astype(BF16), b_ref[...].astype(BF16), dims, preferred_element_type=F32)

    def body_one_step(a_ref, b_ref, o_ref):
        o_ref[...] = product(a_ref, b_ref).astype(o_ref.dtype)

    def body(a_ref, b_ref, o_ref, acc_ref):
        k = pl.program_id(2)

        @pl.when(k == 0)
        def _():
            acc_ref[...] = product(a_ref, b_ref)

        @pl.when(jnp.logical_and(k > 0, k < nk - 1))
        def _():
            acc_ref[...] += product(a_ref, b_ref)

        @pl.when(k == nk - 1)
        def _():
            o_ref[...] = (acc_ref[...] + product(a_ref, b_ref)).astype(o_ref.dtype)

    return pl.pallas_call(
        body_one_step if nk == 1 else body, name=name, grid=grid, in_specs=[a_spec, b_spec], out_specs=o_spec,
        out_shape=o_shape, scratch_shapes=[] if nk == 1 else [pltpu.VMEM((tm, tn), F32)],
        compiler_params=_params(("parallel", "parallel", "arbitrary")),
    )(a, b)


def _row_in(spec, tile):
    if isinstance(spec, tuple):
        arr, width, cb = spec
    else:
        arr, width, cb = spec, spec.shape[1], 0
    return arr, pl.BlockSpec((tile, width), lambda i, cb=cb: (i, cb))


def _full_spec(arr):
    nd = arr.ndim
    return pl.BlockSpec(arr.shape, lambda i, nd=nd: (0,) * nd, pipeline_mode=pl.Buffered(1))


def _rowwise(name, fn, rows, params, outs, tile):
    t = (rows[0][0] if isinstance(rows[0], tuple) else rows[0]).shape[0]
    tile = _pick(t, (tile,))
    arrs, specs = zip(*[_row_in(s, tile) for s in rows])
    nr, npar = len(rows), len(params)

    def body(*refs):
        rv = [r[...].astype(F32) for r in refs[:nr]]
        pv = [p[...] for p in refs[nr:nr + npar]]
        res = fn(*rv, *pv)
        for o_ref, val in zip(refs[nr + npar:], res):
            o_ref[...] = val.astype(o_ref.dtype)

    return pl.pallas_call(
        body, name=name, grid=(t // tile,),
        in_specs=list(specs) + [_full_spec(p) for p in params],
        out_specs=[pl.BlockSpec((tile, w), lambda i: (i, 0)) for w, _ in outs],
        out_shape=[jax.ShapeDtypeStruct((t, w), d) for w, d in outs],
        compiler_params=_params(("parallel",)),
    )(*arrs, *params)


def _rowwise_bwd(name, fn, rows, params, cts, row_grads, param_grads, tile, add_to_first=()):
    t = (rows[0][0] if isinstance(rows[0], tuple) else rows[0]).shape[0]
    tile = _pick(t, (tile,))
    arrs, specs = zip(*[_row_in(s, tile) for s in rows])
    n_add = len(add_to_first)
    flat_cts = [c for group in cts for c in group] + list(add_to_first)
    c_arrs, c_specs = zip(*[_row_in(s, tile) for s in flat_cts])
    nr, npar, nc = len(rows), len(params), len(flat_cts)
    rg_idx = [i for i, d in enumerate(row_grads) if d is not None]
    pg_idx = [i for i, d in enumerate(param_grads) if d]

    def body(*refs):
        rv = [r[...].astype(F32) for r in refs[:nr]]
        pv = [p[...] for p in refs[nr:nr + npar]]
        cv = [c[...].astype(F32) for c in refs[nr + npar:nr + npar + nc]]
        o_refs = refs[nr + npar + nc:]
        outs, vjp = jax.vjp(fn, *rv, *pv)
        ct, pos = [], 0
        for group, o in zip(cts, outs):
            if group:
                acc = cv[pos]
                for extra in cv[pos + 1:pos + len(group)]:
                    acc = acc + extra
                pos += len(group)
            else:
                acc = jnp.zeros_like(o)
            ct.append(acc)
        grads = list(vjp(tuple(ct)))
        for extra in cv[nc - n_add:]:
            grads[rg_idx[0]] = grads[rg_idx[0]] + extra
        for o_ref, i in zip(o_refs[:len(rg_idx)], rg_idx):
            o_ref[...] = grads[i].astype(o_ref.dtype)
        first = pl.program_id(0) == 0
        for o_ref, i in zip(o_refs[len(rg_idx):], pg_idx):
            g = grads[nr + i].astype(F32)

            @pl.when(first)
            def _(o_ref=o_ref, g=g):
                o_ref[...] = g

            @pl.when(jnp.logical_not(first))
            def _(o_ref=o_ref, g=g):
                o_ref[...] += g

    def width(i):
        s = rows[i]
        return s[1] if isinstance(s, tuple) else s.shape[1]

    out_specs = [pl.BlockSpec((tile, width(i)), lambda i_: (i_, 0)) for i in rg_idx]
    out_shape = [jax.ShapeDtypeStruct((t, width(i)), row_grads[i]) for i in rg_idx]
    out_specs += [_full_spec(params[i]) for i in pg_idx]
    out_shape += [jax.ShapeDtypeStruct(params[i].shape, F32) for i in pg_idx]
    res = pl.pallas_call(
        body, name=name, grid=(t // tile,),
        in_specs=list(specs) + [_full_spec(p) for p in params] + list(c_specs),
        out_specs=out_specs, out_shape=out_shape,
        compiler_params=_params(("arbitrary",)),
    )(*arrs, *params, *c_arrs)
    return res[:len(rg_idx)], res[len(rg_idx):]


def _fn_norm1(x, gain, sc, sh):
    h = _rms(x, gain) * (1.0 + sc) + sh
    return h, h


def _fn_rwkv_pre(h, hp, p, pp, mu_rkv, mu_w, mu_a, mu_g, w0, a0, k_k, k_a, w1, w2, a1, a2, g1, g2):
    d = p.shape[1] // 3
    dh = hp - h
    xw = h + dh * mu_w
    xa = h + dh * mu_a
    xg = h + dh * mu_g
    pr = p + (pp - p) * mu_rkv
    r, k, v = pr[:, :d], pr[:, d:2 * d], pr[:, 2 * d:]
    w_log = -_softplus(-(w0 + _bdot(jnp.tanh(_bdot(xw, w1)), w2))) - 0.5
    decay = jnp.exp(-jnp.exp(w_log))
    a = _sigmoid(a0 + _bdot(_bdot(xa, a1), a2))
    g = _bdot(_sigmoid(_bdot(xg, g1)), g2)
    kk = k * k_k
    kk = kk * lax.rsqrt(_segsum(kk * kk) + L2_EPS)
    k2 = k * (1.0 + (a - 1.0) * k_a)
    return r, decay, k2, v, -kk, kk * a, g


def _fn_rwkv_post(y, r, k2, v, g, ln_g, ln_b, r_k):
    inv = 1.0 / HEAD_DIM
    mean = _segsum(y) * inv
    yc = y - mean
    var = _segsum(yc * yc) * inv
    yn = yc * lax.rsqrt(var + GN_EPS) * ln_g + ln_b
    bonus = _segsum(r * k2 * r_k) * v
    return ((yn + bonus) * g,)


def _fn_rwkv_post_cat(y, r, k2, v, g, o_sb, ln_g, ln_b, r_k):
    return (jnp.concatenate([_fn_rwkv_post(y, r, k2, v, g, ln_g, ln_b, r_k)[0], o_sb], axis=1),)


def _fn_qk_norm(p, qg, kg):
    d = p.shape[1] // 3
    q, k, v = p[:, :d], p[:, d:2 * d], p[:, 2 * d:]
    inv = 1.0 / HEAD_DIM
    qn = q * lax.rsqrt(_segsum(q * q) * inv + RMS_EPS) * qg
    kn = k * lax.rsqrt(_segsum(k * k) * inv + RMS_EPS) * kg
    return qn * SB_SCALE, kn, v


def _fn_mix_norm2(x, mix, gt1, gain, sc, sh):
    x1 = x + gt1 * mix
    h2 = _rms(x1, gain) * (1.0 + sc) + sh
    return x1, h2


def _fn_swiglu(gate, up):
    return (_silu(gate) * up,)


def _swiglu_bwd(gu, dact, tile=128):
    t, dff = dact.shape
    tile = _pick(t, (tile,))

    def body(gate_ref, up_ref, d_ref, o_ref):
        _, vjp = jax.vjp(_fn_swiglu, gate_ref[...], up_ref[...])
        dgate, dup = vjp((d_ref[...],))
        o_ref[:, :dff] = dgate.astype(o_ref.dtype)
        o_ref[:, dff:] = dup.astype(o_ref.dtype)

    return pl.pallas_call(
        body, name="swiglu_bwd", grid=(t // tile,),
        in_specs=[pl.BlockSpec((tile, dff), lambda i: (i, 0)), pl.BlockSpec((tile, dff), lambda i: (i, 1)),
                  pl.BlockSpec((tile, dff), lambda i: (i, 0))],
        out_specs=pl.BlockSpec((tile, 2 * dff), lambda i: (i, 0)),
        out_shape=jax.ShapeDtypeStruct((t, 2 * dff), BF16),
        compiler_params=_params(("parallel",)),
    )(gu, gu, dact)


def _loss_head(name, x1, dn, target, gt2, tile=256):
    t, d = x1.shape
    tile = _pick(t, (tile,))

    def body(x1_ref, dn_ref, tg_ref, gt_ref, loss_ref, dout_ref, ddn_ref, dgt_ref):
        dnv = dn_ref[...]
        gt = gt_ref[...]
        err = x1_ref[...] + gt * dnv - tg_ref[...]
        dout = err * (1.0 / d)
        dout_ref[...] = dout
        ddn_ref[...] = (dout * gt).astype(ddn_ref.dtype)
        part = 0.5 * jnp.sum(jnp.sum(err * dout, axis=-1, keepdims=True), axis=0, keepdims=True)
        dgt = jnp.sum(dout * dnv, axis=0, keepdims=True)
        first = pl.program_id(0) == 0

        @pl.when(first)
        def _():
            loss_ref[...] = jnp.broadcast_to(part, loss_ref.shape)
            dgt_ref[...] = dgt

        @pl.when(jnp.logical_not(first))
        def _():
            loss_ref[...] += jnp.broadcast_to(part, loss_ref.shape)
            dgt_ref[...] += dgt

    row = pl.BlockSpec((tile, d), lambda i: (i, 0))
    vec = pl.BlockSpec((1, d), lambda i: (0, 0))
    return pl.pallas_call(
        body, name=name, grid=(t // tile,),
        in_specs=[row, row, row, vec],
        out_specs=[pl.BlockSpec((1, LANES), lambda i: (0, 0)), row, row, vec],
        out_shape=[jax.ShapeDtypeStruct((1, LANES), F32), jax.ShapeDtypeStruct((t, d), F32),
                   jax.ShapeDtypeStruct((t, d), BF16), jax.ShapeDtypeStruct((1, d), F32)],
        compiler_params=_params(("arbitrary",)),
    )(x1, dn, target, gt2)


SCAN_BLOCK = 32
N_COL = 5
WIDE = 2 * LANES


def _wide_eye():
    i = lax.broadcasted_iota(jnp.int32, (HEAD_DIM, WIDE), 0)
    j = lax.broadcasted_iota(jnp.int32, (HEAD_DIM, WIDE), 1) % HEAD_DIM
    return (i == j).astype(BF16)


def _wide_ones():
    i = lax.broadcasted_iota(jnp.int32, (WIDE, WIDE), 0) // HEAD_DIM
    j = lax.broadcasted_iota(jnp.int32, (WIDE, WIDE), 1) // HEAD_DIM
    return (i == j).astype(BF16)


COL_PIECES = (1, 2, 2, 2, 1)


def _col_tiles(refs, i, nq, eye, ones_bf):
    levels = max(COL_PIECES)
    pieces = [[] for _ in range(levels)]
    for ref, n_pieces in zip(refs, COL_PIECES):
        full = ref[pl.ds(i, 1), :]
        for q in range(nq):
            rest = full[:, q * WIDE:(q + 1) * WIDE]
            for level in range(n_pieces):
                part = rest.astype(BF16)
                pieces[level].append(part * eye)
                rest = rest - part.astype(F32)
    rows = nq * HEAD_DIM
    out = jnp.dot(jnp.concatenate(sum(pieces, []), axis=0), ones_bf, preferred_element_type=F32)
    start, place = 0, {}
    for level in range(levels):
        for n, n_pieces in enumerate(COL_PIECES):
            if n_pieces > level:
                place[level, n] = start
                start += rows
    tiles = []
    for n, n_pieces in enumerate(COL_PIECES):
        acc = out[place[0, n]:place[0, n] + rows]
        for level in range(1, n_pieces):
            acc = acc + out[place[level, n]:place[level, n] + rows]
        tiles.append(acc)
    return jnp.concatenate(tiles, axis=0).reshape(N_COL * nq, HEAD_DIM, WIDE)


def _bf16_round(x):
    bits = lax.bitcast_convert_type(x, jnp.uint32)
    bits = (bits + jnp.uint32(0x7FFF) + ((bits >> 16) & jnp.uint32(1))) & jnp.uint32(0xFFFF0000)
    return lax.bitcast_convert_type(bits, F32)


def _pair_tile(tiles_ref, n, p, nq):
    return tiles_ref[n * nq + p // 2, :, (p % 2) * LANES:(p % 2 + 1) * LANES]


def _scan_fwd(rem, w, wr, k, r, v):
    t, dr = v.shape
    npair, nq = dr // LANES, dr // WIDE
    tb = _pick(t, (SCAN_BLOCK,))

    def body(rem_ref, w_ref, wr_ref, k_ref, r_ref, v_ref, *rest):
        next_refs, (y_ref, hist_ref, last_ref, s_ref, sb_ref, tiles_a, tiles_b) = rest[:N_COL], rest[N_COL:]
        eye, ones_bf = _wide_eye(), _wide_ones()
        col_refs = (rem_ref, w_ref, wr_ref, k_ref, r_ref)

        @pl.when(pl.program_id(0) == 0)
        def _():
            s_ref[...] = jnp.zeros_like(s_ref)
            sb_ref[...] = jnp.zeros_like(sb_ref)
            tiles_a[...] = _col_tiles(col_refs, 0, nq, eye, ones_bf)

        def step(i, tiles_ref):
            v_full, y_rows = v_ref[pl.ds(i, 1), :], []
            for p in range(npair):
                s = s_ref[p]
                hist_ref[i, p] = s
                c_rem, c_w, c_wr, c_k, c_r = [_pair_tile(tiles_ref, n, p, nq) for n in range(N_COL)]
                sa = jnp.sum(sb_ref[p] * c_rem, axis=0, keepdims=True)
                s2 = s * c_w + c_wr * sa + c_k * v_full[:, p * LANES:(p + 1) * LANES]
                s2_b = _bf16_round(s2)
                y_rows.append(jnp.sum(s2_b * c_r, axis=0, keepdims=True))
                s_ref[p] = s2
                sb_ref[p] = s2_b
            y_ref[pl.ds(i, 1), :] = jnp.concatenate(y_rows, axis=1)

        def two_steps(i, ahead_refs, ahead_row):
            tiles_b[...] = _col_tiles(col_refs, i + 1, nq, eye, ones_bf)
            step(i, tiles_a)
            tiles_a[...] = _col_tiles(ahead_refs, ahead_row, nq, eye, ones_bf)
            step(i + 1, tiles_b)

        def loop_body(m, carry):
            two_steps(2 * m, col_refs, 2 * m + 2)
            return carry

        lax.fori_loop(0, tb // 2 - 1, loop_body, 0, unroll=3)
        two_steps(tb - 2, next_refs, 0)
        last_ref[...] = sb_ref[...]

    nblk = t // tb
    blk = pl.BlockSpec((tb, dr), lambda i: (i, 0))
    nxt = pl.BlockSpec((8, dr), lambda i: (jnp.minimum(i + 1, nblk - 1) * (tb // 8), 0))
    tiles = pltpu.VMEM((N_COL * nq, HEAD_DIM, WIDE), F32)
    state = pltpu.VMEM((npair, HEAD_DIM, LANES), F32)
    return pl.pallas_call(
        body, name="rwkv_scan_fwd", grid=(nblk,),
        in_specs=[blk] * 6 + [nxt] * N_COL,
        out_specs=[blk, pl.BlockSpec((tb, npair, HEAD_DIM, LANES), lambda i: (i, 0, 0, 0)),
                   pl.BlockSpec((npair, HEAD_DIM, LANES), lambda i: (0, 0, 0))],
        out_shape=[jax.ShapeDtypeStruct((t, dr), F32), jax.ShapeDtypeStruct((t, npair, HEAD_DIM, LANES), F32),
                   jax.ShapeDtypeStruct((npair, HEAD_DIM, LANES), F32)],
        scratch_shapes=[state, state, tiles, tiles],
        compiler_params=_params(("arbitrary",)),
    )(rem, w, wr, k, r, v, rem, w, wr, k, r)


def _scan_bwd(rem, w, wr, k, r, v, hist, last, dy):
    t, dr = v.shape
    npair, nq = dr // LANES, dr // WIDE
    tb = _pick(t, (SCAN_BLOCK,))
    nblk = t // tb

    def body(rem_ref, w_ref, wr_ref, k_ref, r_ref, v_ref, hist_ref, last_ref, dy_ref, *rest):
        prev_refs, out_refs = rest[:N_COL], rest[N_COL:2 * N_COL]
        dv_ref, ds_ref, next_ref, tiles_a, tiles_b = rest[2 * N_COL:]
        eye, ones_bf = _wide_eye(), _wide_ones()
        col_refs = (rem_ref, w_ref, wr_ref, k_ref, r_ref)

        @pl.when(pl.program_id(0) == 0)
        def _():
            ds_ref[...] = jnp.zeros_like(ds_ref)
            next_ref[...] = last_ref[...]
            tiles_a[...] = _col_tiles(col_refs, tb - 1, nq, eye, ones_bf)

        def step(i, tiles_ref):
            grads = [[None] * npair for _ in range(N_COL)]
            v_full, dy_full, dv_rows = v_ref[pl.ds(i, 1), :], dy_ref[pl.ds(i, 1), :], []
            for p in range(npair):
                lanes = slice(p * LANES, (p + 1) * LANES)
                s = hist_ref[i, p]
                c_rem, c_w, c_wr, c_k, c_r = [_pair_tile(tiles_ref, n, p, nq) for n in range(N_COL)]
                v_row, dy_row = v_full[:, lanes], dy_full[:, lanes]
                s_b = _bf16_round(s)
                s2_b = next_ref[p]
                next_ref[p] = s_b
                sa = jnp.sum(s_b * c_rem, axis=0, keepdims=True)
                d2 = ds_ref[p] + c_r * dy_row
                dsa = jnp.sum(d2 * c_wr, axis=0, keepdims=True)
                dv_rows.append(jnp.sum(d2 * c_k, axis=0, keepdims=True))
                ds_ref[p] = d2 * c_w + c_rem * dsa
                for n, tile in enumerate((s_b * dsa, d2 * s, d2 * sa, d2 * v_row, s2_b * dy_row)):
                    grads[n][p] = tile.astype(BF16)
            wide = [jnp.concatenate(grads[n][2 * q:2 * q + 2], axis=1) for n in range(N_COL) for q in range(nq)]
            sums = jnp.dot(jnp.concatenate(wide, axis=0), ones_bf, preferred_element_type=F32)
            sums = sums.reshape(N_COL * nq, HEAD_DIM, WIDE)
            rows = jnp.sum(sums * eye.astype(F32)[None], axis=1)
            dv_ref[pl.ds(i, 1), :] = jnp.concatenate(dv_rows, axis=1)
            for n, o_ref in enumerate(out_refs):
                o_ref[pl.ds(i, 1), :] = jnp.concatenate([rows[n * nq + q:n * nq + q + 1] for q in range(nq)], axis=1)

        def two_steps(i, ahead_refs, ahead_row):
            tiles_b[...] = _col_tiles(col_refs, i - 1, nq, eye, ones_bf)
            step(i, tiles_a)
            tiles_a[...] = _col_tiles(ahead_refs, ahead_row, nq, eye, ones_bf)
            step(i - 1, tiles_b)

        def loop_body(m, carry):
            i = tb - 1 - 2 * m
            two_steps(i, col_refs, i - 2)
            return carry

        lax.fori_loop(0, tb // 2 - 1, loop_body, 0, unroll=3)
        two_steps(1, prev_refs, 7)

    blk = pl.BlockSpec((tb, dr), lambda i: (nblk - 1 - i, 0))
    prv = pl.BlockSpec((8, dr), lambda i: (jnp.maximum((nblk - 1 - i) * (tb // 8) - 1, 0), 0))
    tiles = pltpu.VMEM((N_COL * nq, HEAD_DIM, WIDE), F32)
    state = pltpu.VMEM((npair, HEAD_DIM, LANES), F32)
    return pl.pallas_call(
        body, name="rwkv_scan_bwd", grid=(nblk,),
        in_specs=[blk] * 6 + [pl.BlockSpec((tb, npair, HEAD_DIM, LANES), lambda i: (nblk - 1 - i, 0, 0, 0)),
                              pl.BlockSpec((npair, HEAD_DIM, LANES), lambda i: (0, 0, 0)), blk] + [prv] * N_COL,
        out_specs=[blk] * 6,
        out_shape=[jax.ShapeDtypeStruct((t, dr), F32)] * 6,
        scratch_shapes=[state, state, tiles, tiles],
        compiler_params=_params(("arbitrary",)),
    )(rem, w, wr, k, r, v, hist, last, dy, rem, w, wr, k, r)


SB_BLOCK = 256
SB_HEADS = 8
SB_HEADS_BWD = 4
NT_DIMS = (((1,), (1,)), ((), ()))
TN_DIMS = (((0,), (0,)), ((), ()))
SB_SCALE = 1.0 / math.sqrt(HEAD_DIM)


def _dot2(x, tri):
    hi = x.astype(BF16)
    mid = (x - hi.astype(F32)).astype(BF16)
    return jnp.dot(hi, tri, preferred_element_type=F32) + jnp.dot(mid, tri, preferred_element_type=F32)


def _sb_block_iotas(bs):
    return lax.broadcasted_iota(jnp.int32, (bs, bs), 0), lax.broadcasted_iota(jnp.int32, (bs, bs), 1)


def _when_step(group, block):
    return pl.when(jnp.logical_and(pl.program_id(0) == group, pl.program_id(1) == block))


def _sb_fwd(q, k, v, shards):
    h, t, d = q.shape
    bs = _pick(t, (SB_BLOCK,))
    nh = _pick(h, (SB_HEADS,))
    ngroup, nblock = h // nh, t // bs
    nw = len(shards)
    heights = [s.shape[0] for s in shards]

    def body(q_ref, k_ref, v_ref, *rest):
        w_in, (o_ref, l_ref), w_out, sems = rest[:nw], rest[nw:nw + 2], rest[nw + 2:2 * nw + 2], rest[2 * nw + 2:]

        @_when_step(0, 0)
        def _():
            _gather_phase("issue", w_in, w_out, sems, heights)

        @_when_step(ngroup - 1, (2 * nblock) // 3)
        def _():
            _gather_phase("forward", w_in, w_out, sems, heights)

        qi = pl.program_id(1)
        ri, ci = _sb_block_iotas(bs)
        tri_ge = (ri >= ci).astype(BF16)
        causal = ci < ri
        qv = [q_ref[hh] for hh in range(nh)]

        def blocks(j, masked, carry):
            accs, tails = carry[:nh], carry[nh:]
            heads = range(nh)
            rows = pl.ds(pl.multiple_of(j * bs, bs), bs)
            z = [lax.dot_general(qv[hh], k_ref[hh, rows, :], NT_DIMS, preferred_element_type=F32) for hh in heads]
            log1m = [-_softplus(z[hh]) for hh in heads]
            if masked:
                log1m = [jnp.where(causal, x, 0.0) for x in log1m]
            cs = [_dot2(log1m[hh], tri_ge) for hh in heads]
            a = [jnp.exp(z[hh] + cs[hh] + tails[hh]) for hh in heads]
            if masked:
                a = [jnp.where(causal, x, 0.0) for x in a]
            accs = [accs[hh] + jnp.dot(a[hh].astype(BF16), v_ref[hh, rows, :], preferred_element_type=F32)
                    for hh in heads]
            return tuple(accs) + tuple(tails[hh] + cs[hh][:, 0:1] for hh in heads)

        carry = blocks(qi, True, (jnp.zeros((bs, d), F32),) * nh + (jnp.zeros((bs, 1), F32),) * nh)
        carry = lax.fori_loop(0, qi, lambda n, c: blocks(qi - 1 - n, False, c), carry)
        for hh in range(nh):
            o_ref[hh] = carry[hh]
            l_ref[hh] = jnp.broadcast_to(carry[nh + hh], (bs, d))

        @_when_step(ngroup - 1, nblock - 1)
        def _():
            _gather_phase("finish", w_in, w_out, sems, heights)

    qs = pl.BlockSpec((nh, bs, d), lambda hh, i: (hh, i, 0))
    ks = pl.BlockSpec((nh, t, d), lambda hh, i: (hh, 0, 0), pipeline_mode=pl.Buffered(1))
    res = pl.pallas_call(
        body, name="sb_attn_fwd", grid=(ngroup, nblock),
        in_specs=[qs, ks, ks] + [ANY] * nw, out_specs=[qs, qs] + [ANY] * nw,
        out_shape=[jax.ShapeDtypeStruct((h, t, d), F32)] * 2 + _gather_out_shapes(shards),
        scratch_shapes=_gather_sems(nw),
        compiler_params=_params(("arbitrary", "arbitrary")),
    )(q, k, v, *shards)
    return res[0], res[1], res[2:]


def _sb_bwd(q, k, v, lsum, do, parts):
    h, t, d = q.shape
    bs = _pick(t, (SB_BLOCK,))
    nh = _pick(h, (SB_HEADS_BWD,))
    ngroup, nblock = h // nh, t // bs
    nw = len(parts)

    def body(q_ref, k_ref, v_ref, l_ref, do_ref, *rest):
        p_in, (dq_ref, dk_ref, dv_ref), p_out, sems = rest[:nw], rest[nw:nw + 3], rest[nw + 3:2 * nw + 3], rest[2 * nw + 3:]

        @_when_step(0, 0)
        def _():
            _exchange_phase("issue", p_in, p_out, sems)

        qi = pl.program_id(1)

        @pl.when(qi == 0)
        def _():
            dk_ref[...] = jnp.zeros_like(dk_ref)
            dv_ref[...] = jnp.zeros_like(dv_ref)

        ri, ci = _sb_block_iotas(bs)
        tri_lt = (ri < ci).astype(BF16)
        causal = ci < ri
        qv = [q_ref[hh] for hh in range(nh)]
        dob = [do_ref[hh].astype(BF16) for hh in range(nh)]
        ltot = [l_ref[hh][:, 0:1] for hh in range(nh)]

        def blocks(j, masked, carry):
            dq, pc, ec = carry[:nh], carry[nh:2 * nh], carry[2 * nh:]
            heads = range(nh)
            rows = pl.ds(pl.multiple_of(j * bs, bs), bs)
            z = [lax.dot_general(qv[hh], k_ref[hh, rows, :], NT_DIMS, preferred_element_type=F32) for hh in heads]
            da = [lax.dot_general(dob[hh], v_ref[hh, rows, :], NT_DIMS, preferred_element_type=F32) for hh in heads]
            nsp = [-_softplus(z[hh]) for hh in heads]
            log1m = [jnp.where(causal, x, 0.0) for x in nsp] if masked else nsp
            below = [_dot2(log1m[hh], tri_lt) + pc[hh] for hh in heads]
            a = [jnp.exp(z[hh] + (ltot[hh] - below[hh])) for hh in heads]
            if masked:
                a = [jnp.where(causal, x, 0.0) for x in a]
            e = [a[hh] * da[hh] for hh in heads]
            ebelow = [_dot2(e[hh], tri_lt) + ec[hh] for hh in heads]
            dz = [e[hh] * jnp.exp(nsp[hh]) - jnp.exp(z[hh] + nsp[hh]) * ebelow[hh] for hh in heads]
            if masked:
                dz = [jnp.where(causal, x, 0.0) for x in dz]
            dzb = [x.astype(BF16) for x in dz]
            for hh in heads:
                dv_ref[hh, rows, :] += lax.dot_general(a[hh].astype(BF16), dob[hh], TN_DIMS,
                                                       preferred_element_type=F32)
            for hh in heads:
                dk_ref[hh, rows, :] += lax.dot_general(dzb[hh], qv[hh], TN_DIMS, preferred_element_type=F32)
            dq = [dq[hh] + jnp.dot(dzb[hh], k_ref[hh, rows, :], preferred_element_type=F32) for hh in heads]
            pc = [pc[hh] + jnp.sum(log1m[hh], axis=1, keepdims=True) for hh in heads]
            ec = [ec[hh] + jnp.sum(e[hh], axis=1, keepdims=True) for hh in heads]
            return tuple(dq) + tuple(pc) + tuple(ec)

        zcol = jnp.zeros((bs, 1), F32)
        carry = lax.fori_loop(0, qi, lambda j, c: blocks(j, False, c),
                              (jnp.zeros((bs, d), F32),) * nh + (zcol,) * (2 * nh))
        carry = blocks(qi, True, carry)
        for hh in range(nh):
            dq_ref[hh] = carry[hh]

        @_when_step(ngroup - 1, nblock - 1)
        def _():
            _exchange_phase("finish", p_in, p_out, sems)

    qs = pl.BlockSpec((nh, bs, d), lambda hh, i: (hh, i, 0))
    ks = pl.BlockSpec((nh, t, d), lambda hh, i: (hh, 0, 0), pipeline_mode=pl.Buffered(1))
    res = pl.pallas_call(
        body, name="sb_attn_bwd", grid=(ngroup, nblock),
        in_specs=[qs, ks, ks, qs, qs] + [ANY] * nw, out_specs=[qs, ks, ks] + [ANY] * nw,
        out_shape=[jax.ShapeDtypeStruct((h, t, d), F32)] * 3 + [jax.ShapeDtypeStruct(p.shape, p.dtype) for p in parts],
        scratch_shapes=_exchange_sems(nw),
        compiler_params=_params(("arbitrary", "arbitrary")),
    )(q, k, v, lsum, do, *parts)
    return res[0], res[1], res[2], res[3:]


ANY = pl.BlockSpec(memory_space=pl.ANY)
IN_VMEM = pl.BlockSpec(memory_space=pltpu.VMEM)


def _coords():
    return lax.axis_index("x"), lax.axis_index("y"), lax.axis_index("c")


def _flip(v, bit):
    return 1 - v if bit else v


def _remote(src, dst, send_sem, recv_sem, device):
    return pltpu.make_async_remote_copy(src_ref=src, dst_ref=dst, send_sem=send_sem, recv_sem=recv_sem,
                                        device_id=device, device_id_type=MESH)


def _all_gather8(name, blk):
    m, n = blk.shape

    def body(x_ref, o_ref, send_sems, recv_sems, local_sem):
        x, y, c = _coords()
        own = pltpu.make_async_copy(x_ref, o_ref.at[4 * x + 2 * y + c], local_sem)
        own.start()
        peers = []
        for bits in range(1, N_DEV):
            px, py, pc = _flip(x, (bits >> 2) & 1), _flip(y, (bits >> 1) & 1), _flip(c, bits & 1)
            peers.append((px, py, pc))
        sends = []
        for k, peer in enumerate(peers):
            cp = _remote(x_ref, o_ref.at[4 * x + 2 * y + c], send_sems.at[k], recv_sems.at[k], peer)
            cp.start()
            sends.append(cp)
        for k, (px, py, pc) in enumerate(peers):
            slot = o_ref.at[4 * px + 2 * py + pc]
            _remote(slot, slot, send_sems.at[k], recv_sems.at[k], (px, py, pc)).wait_recv()
        for cp in sends:
            cp.wait_send()
        own.wait()

    return pl.pallas_call(
        body, name=name, out_shape=jax.ShapeDtypeStruct((N_DEV, m, n), blk.dtype),
        in_specs=[IN_VMEM], out_specs=IN_VMEM,
        scratch_shapes=[pltpu.SemaphoreType.DMA((N_DEV - 1,)), pltpu.SemaphoreType.DMA((N_DEV - 1,)),
                        pltpu.SemaphoreType.DMA],
        compiler_params=_params(),
    )(blk)


def _gather_weights(shards):
    nw = len(shards)
    heights = [s.shape[0] for s in shards]

    def body(*refs):
        ins, outs, sems = refs[:nw], refs[nw:2 * nw], refs[2 * nw:]
        _gather_phase("issue", ins, outs, sems, heights)
        _gather_phase("forward", ins, outs, sems, heights)
        _gather_phase("finish", ins, outs, sems, heights)

    return pl.pallas_call(
        body, name="gather_weights",
        out_shape=_gather_out_shapes(shards),
        in_specs=[ANY] * nw, out_specs=[ANY] * nw,
        scratch_shapes=_gather_sems(nw),
        compiler_params=_params(),
    )(*shards)


def _gather_out_shapes(shards):
    return [jax.ShapeDtypeStruct((N_CHIPS,) + s.shape, s.dtype) for s in shards]


def _gather_sems(nw):
    return [pltpu.SemaphoreType.DMA((3 * nw,))] * 4


def _gather_phase(phase, ins, outs, sems, heights):
    ici_send, ici_recv, d2d_send, d2d_recv = sems
    x, y, c = _coords()
    chip = 2 * x + y
    sibling = (x, y, 1 - c)
    for w, height in enumerate(heights):
        half = height // 2
        mine, theirs = pl.ds(c * half, half), pl.ds((1 - c) * half, half)
        for j, (a, b) in enumerate(XY_MASKS):
            px, py = _flip(x, a), _flip(y, b)
            k = 3 * w + j
            over_ici = _remote(ins[w].at[mine], outs[w].at[chip, mine], ici_send.at[k], ici_recv.at[k], (px, py, c))
            landed = outs[w].at[2 * px + py, mine]
            onward = _remote(landed, landed, d2d_send.at[k], d2d_recv.at[k], sibling)
            if phase == "issue":
                over_ici.start()
            elif phase == "forward":
                _remote(landed, landed, ici_send.at[k], ici_recv.at[k], (px, py, c)).wait_recv()
                onward.start()
            else:
                slot = outs[w].at[2 * px + py, theirs]
                _remote(slot, slot, d2d_send.at[k], d2d_recv.at[k], sibling).wait_recv()
                over_ici.wait_send()
                onward.wait_send()


def _pair_exchange(name, grads):
    nw = len(grads)

    def body(*refs):
        ins, outs = refs[:nw], refs[nw:2 * nw]
        send_sems, recv_sems = refs[2 * nw:]
        x, y, c = _coords()
        sibling = (x, y, 1 - c)
        sends = []
        for w in range(nw):
            half = grads[w].shape[1] // 2
            cp = _remote(ins[w].at[:, pl.ds((1 - c) * half, half)], outs[w], send_sems.at[w], recv_sems.at[w], sibling)
            cp.start()
            sends.append(cp)
        for w in range(nw):
            _remote(outs[w], outs[w], send_sems.at[w], recv_sems.at[w], sibling).wait_recv()
        for cp in sends:
            cp.wait_send()

    return pl.pallas_call(
        body, name=name,
        out_shape=[jax.ShapeDtypeStruct((N_CHIPS, g.shape[1] // 2, g.shape[2]), g.dtype) for g in grads],
        in_specs=[ANY] * nw, out_specs=[ANY] * nw,
        scratch_shapes=[pltpu.SemaphoreType.DMA((nw,))] * 2,
        compiler_params=_params(),
    )(*grads)


def _chip_exchange(name, parts):
    nw = len(parts)

    def body(*refs):
        ins, outs, sems = refs[:nw], refs[nw:2 * nw], refs[2 * nw:]
        _exchange_phase("issue", ins, outs, sems)
        _exchange_phase("finish", ins, outs, sems)

    return pl.pallas_call(
        body, name=name,
        out_shape=[jax.ShapeDtypeStruct(p.shape, p.dtype) for p in parts],
        in_specs=[ANY] * nw, out_specs=[ANY] * nw,
        scratch_shapes=_exchange_sems(nw),
        compiler_params=_params(),
    )(*parts)


def _exchange_sems(nw):
    return [pltpu.SemaphoreType.DMA((3 * nw,))] * 2


def _exchange_phase(phase, ins, outs, sems):
    send_sems, recv_sems = sems
    x, y, c = _coords()
    chip = 2 * x + y
    for w in range(len(ins)):
        for j, (a, b) in enumerate(XY_MASKS):
            px, py = _flip(x, a), _flip(y, b)
            k = 3 * w + j
            send = _remote(ins[w].at[2 * px + py], outs[w].at[chip], send_sems.at[k], recv_sems.at[k], (px, py, c))
            if phase == "issue":
                send.start()
            else:
                slot = outs[w].at[2 * px + py]
                _remote(slot, slot, send_sems.at[k], recv_sems.at[k], (px, py, c)).wait_recv()
                send.wait_send()


def _pair_share(shards):
    nw = len(shards)

    def body(*refs):
        bufs = refs[nw:2 * nw]
        send_sems, recv_sems = refs[2 * nw:]
        x, y, c = _coords()
        sibling = (x, y, 1 - c)
        sends = []
        for w in range(nw):
            half = shards[w].shape[0] // 2
            mine = bufs[w].at[pl.ds(c * half, half)]
            cp = _remote(mine, mine, send_sems.at[w], recv_sems.at[w], sibling)
            cp.start()
            sends.append(cp)
        for w in range(nw):
            half = shards[w].shape[0] // 2
            theirs = bufs[w].at[pl.ds((1 - c) * half, half)]
            _remote(theirs, theirs, send_sems.at[w], recv_sems.at[w], sibling).wait_recv()
        for cp in sends:
            cp.wait_send()

    return pl.pallas_call(
        body, name="grad_pair_share",
        out_shape=[jax.ShapeDtypeStruct(s.shape, s.dtype) for s in shards],
        in_specs=[ANY] * nw, out_specs=[ANY] * nw,
        input_output_aliases={w: w for w in range(nw)},
        scratch_shapes=[pltpu.SemaphoreType.DMA((nw,))] * 2,
        compiler_params=_params(),
    )(*shards)


TILE_BYTES = 2 * 1024 * 1024


def _row_tile(rows, cols, mult=8):
    best = None
    for tr in range(mult, rows + 1, mult):
        if rows % tr == 0 and tr * cols * 4 <= TILE_BYTES:
            best = tr
    return best if best is not None else rows


def _pair_add(name, grad, other, place):
    _, half, cols = other.shape
    tr = _row_tile(half, cols, mult=16)
    nb = half // tr

    def body(place_ref, g_ref, o_ref, out_ref):
        out_ref[...] = (g_ref[...] + o_ref[...]).astype(out_ref.dtype)

    return pl.pallas_call(
        body, name=name, out_shape=jax.ShapeDtypeStruct(other.shape, BF16),
        grid_spec=pltpu.PrefetchScalarGridSpec(
            num_scalar_prefetch=1, grid=(N_CHIPS, nb),
            in_specs=[pl.BlockSpec((None, tr, cols), lambda s, i, place_ref: (s, place_ref[0] * nb + i, 0)),
                      pl.BlockSpec((None, tr, cols), lambda s, i, place_ref: (s, i, 0))],
            out_specs=pl.BlockSpec((None, tr, cols), lambda s, i, place_ref: (s, i, 0))),
        compiler_params=_params(("parallel", "parallel")),
    )(place, grad, other)


def _sum_chips(name, own, others, place):
    _, half, cols = own.shape
    tr = _row_tile(half, cols, mult=16)
    nb = half // tr

    def body(place_ref, own_ref, a_ref, b_ref, c_ref, out_ref):
        total = own_ref[...].astype(F32) + a_ref[...].astype(F32)
        out_ref[...] = (total + b_ref[...].astype(F32)) + c_ref[...].astype(F32)

    def peer(mask):
        return pl.BlockSpec((None, tr, cols), lambda i, place_ref: (place_ref[1] ^ mask, i, 0))

    return pl.pallas_call(
        body, name=name, out_shape=jax.ShapeDtypeStruct((2 * half, cols), F32),
        grid_spec=pltpu.PrefetchScalarGridSpec(
            num_scalar_prefetch=1, grid=(nb,),
            in_specs=[peer(0), peer(1), peer(2), peer(3)],
            out_specs=pl.BlockSpec((tr, cols), lambda i, place_ref: (place_ref[0] * nb + i, 0))),
        compiler_params=_params(("parallel",)),
    )(place, own, others, others, others)


def _adamw_math(w, g, m, v):
    m2 = ADAM_B1 * m + (1.0 - ADAM_B1) * g
    v2 = ADAM_B2 * v + (1.0 - ADAM_B2) * (g * g)
    m_hat = m2 / (1.0 - ADAM_B1 ** ADAM_STEP)
    v_hat = v2 / (1.0 - ADAM_B2 ** ADAM_STEP)
    delta = -ADAM_LR * (m_hat / (jnp.sqrt(v_hat) + ADAM_EPS) + ADAM_WD * w)
    return delta, m2, v2


def _adamw(name, w, g, m, v):
    rows, cols = w.shape
    tr = _row_tile(rows, cols)

    def body(w_ref, g_ref, m_ref, v_ref, d_ref, m2_ref, v2_ref):
        d_ref[...], m2_ref[...], v2_ref[...] = _adamw_math(w_ref[...], g_ref[...], m_ref[...], v_ref[...])

    blk = pl.BlockSpec((tr, cols), lambda i: (i, 0))
    return pl.pallas_call(
        body, name=name, out_shape=[jax.ShapeDtypeStruct(w.shape, F32)] * 3, grid=(rows // tr,),
        in_specs=[blk] * 4, out_specs=[blk] * 3,
        compiler_params=_params(("parallel",)),
    )(w, g, m, v)


def _small_update(gathered, w, m, v):
    def body(gs_ref, w_ref, m_ref, v_ref, g_ref, d_ref, m2_ref, v2_ref):
        g = gs_ref[0]
        for dev in range(1, N_DEV):
            g = g + gs_ref[dev]
        g_ref[...] = g
        d_ref[...], m2_ref[...], v2_ref[...] = _adamw_math(w_ref[...], g, m_ref[...], v_ref[...])

    return pl.pallas_call(
        body, name="small_update", out_shape=[jax.ShapeDtypeStruct(w.shape, F32)] * 4,
        compiler_params=_params(),
    )(gathered, w, m, v)


def _ada_mod(c_all, w_shard, b_shard):
    d, n = w_shard.shape
    tn = _pick(n, (512, 256, 128))

    def body(c_ref, w_ref, b_ref, o_ref):
        act = _silu(c_ref[...]).astype(BF16)
        o_ref[...] = jnp.dot(act, w_ref[...].astype(BF16), preferred_element_type=F32) + b_ref[...]

    return pl.pallas_call(
        body, name="ada_mod", out_shape=jax.ShapeDtypeStruct((c_all.shape[0], n), F32), grid=(n // tn,),
        in_specs=[pl.BlockSpec(c_all.shape, lambda j: (0, 0)), pl.BlockSpec((d, tn), lambda j: (0, j)),
                  pl.BlockSpec((1, tn), lambda j: (0, j))],
        out_specs=pl.BlockSpec((c_all.shape[0], tn), lambda j: (0, j)),
        compiler_params=_params(("parallel",)),
    )(c_all, w_shard, b_shard)


def _ada_grad(c_pad, dmod_pad):
    rows, d = c_pad.shape
    n = dmod_pad.shape[1]
    tn = _pick(n, (512, 256, 128))

    def body(c_ref, g_ref, o_ref):
        act = _silu(c_ref[...]).astype(BF16)
        o_ref[...] = lax.dot_general(act, g_ref[...].astype(BF16), TN_DIMS, preferred_element_type=F32)

    return pl.pallas_call(
        body, name="ada_grad", out_shape=jax.ShapeDtypeStruct((d, n), F32), grid=(n // tn,),
        in_specs=[pl.BlockSpec((rows, d), lambda j: (0, 0)), pl.BlockSpec((rows, tn), lambda j: (0, j))],
        out_specs=pl.BlockSpec((d, tn), lambda j: (0, j)),
        compiler_params=_params(("parallel",)),
    )(c_pad, dmod_pad)


WEIGHTS = ['w_ada', 'b_ada', 'norm1_gain', 'norm2_gain', 'w_in', 'mu_rkv', 'mu_w', 'mu_a', 'mu_g', 'w0', 'w1',
           'w2', 'a0', 'a1', 'a2', 'g1', 'g2', 'k_k', 'k_a', 'r_k', 'ln_x_gain', 'ln_x_bias', 'q_norm_gain',
           'k_norm_gain', 'w_out', 'w_gate_up', 'w_down']
SMALL = ['b_ada', 'norm1_gain', 'norm2_gain', 'mu_rkv', 'mu_w', 'mu_a', 'mu_g', 'w0', 'a0', 'k_k', 'k_a', 'r_k',
         'ln_x_gain', 'ln_x_bias', 'q_norm_gain', 'k_norm_gain']
PACK_ROWS = 8
LOSS_SLOT = LANES


def _shift_down(a):
    return jnp.pad(a[:-1], ((1, 0), (0, 0)))


def _shift_up(a):
    return jnp.pad(a[1:], ((0, 1), (0, 0)))


def _pack_small(vals):
    flat = jnp.concatenate([v.reshape(1, -1) for v in vals], axis=1)
    unit = PACK_ROWS * LANES
    total = -(-flat.shape[1] // unit) * unit
    flat = jnp.pad(flat, ((0, 0), (0, total - flat.shape[1])))
    return flat.reshape(PACK_ROWS, total // PACK_ROWS)


def kernel(x, c, w_ada, b_ada, norm1_gain, norm2_gain, w_in, mu_rkv, mu_w, mu_a, mu_g, w0, w1, w2, a0, a1, a2, g1, g2, k_k, k_a, r_k, ln_x_gain, ln_x_bias, q_norm_gain, k_norm_gain, w_out, w_gate_up, w_down, loss_target, m_w_ada, m_b_ada, m_norm1_gain, m_norm2_gain, m_w_in, m_mu_rkv, m_mu_w, m_mu_a, m_mu_g, m_w0, m_w1, m_w2, m_a0, m_a1, m_a2, m_g1, m_g2, m_k_k, m_k_a, m_r_k, m_ln_x_gain, m_ln_x_bias, m_q_norm_gain, m_k_norm_gain, m_w_out, m_w_gate_up, m_w_down, v_w_ada, v_b_ada, v_norm1_gain, v_norm2_gain, v_w_in, v_mu_rkv, v_mu_w, v_mu_a, v_mu_g, v_w0, v_w1, v_w2, v_a0, v_a1, v_a2, v_g1, v_g2, v_k_k, v_k_a, v_r_k, v_ln_x_gain, v_ln_x_bias, v_q_norm_gain, v_k_norm_gain, v_w_out, v_w_gate_up, v_w_down):
    given = dict(locals())
    wt = {n: given[n][0] for n in WEIGHTS}
    mom = {n: given["m_" + n][0] for n in WEIGHTS}
    var = {n: given["v_" + n][0] for n in WEIGHTS}
    for tree in (wt, mom, var):
        tree["b_ada"] = tree["b_ada"].reshape(1, -1)
        for n in SMALL[1:]:
            tree[n] = tree[n].reshape(1, -1)

    ax, ay, ac = _coords()
    chip = 2 * ax + ay
    dev = 4 * ax + 2 * ay + ac
    xs, target = x[0], loss_target[0]
    t, d = xs.shape
    dr = wt["w0"].shape[1]
    ds = d - dr
    nh = ds // HEAD_DIM
    dff = wt["w_down"].shape[0] * N_CHIPS
    n_ada = wt["w_ada"].shape[1]
    lw, la, lg = wt["w1"].shape[1], wt["a1"].shape[1], wt["g1"].shape[1]

    def lora_a(tree):
        return jnp.concatenate([tree["w1"], tree["a1"], tree["g1"]], axis=1)

    def lora_b(tree):
        return jnp.concatenate([tree["w2"], tree["a2"], tree["g2"]], axis=0)

    def with_own_slot(gathered, own):
        return [lax.dynamic_update_slice(full, shard[None], (chip, 0, 0)) for full, shard in zip(gathered, own)]

    first_shards = [s.astype(BF16) for s in (wt["w_in"], lora_a(wt), lora_b(wt))]
    later_shards = [s.astype(BF16) for s in (wt["w_out"], wt["w_gate_up"], wt["w_down"])]
    full_in, full_la, full_lb = with_own_slot(_gather_weights(first_shards), first_shards)
    full_la = full_la.reshape(d, lw + la + lg).astype(F32)
    full_lb = full_lb.transpose(1, 0, 2).reshape(lw + la + lg, dr).astype(F32)
    w1f, a1f, g1f = full_la[:, :lw], full_la[:, lw:lw + la], full_la[:, lw + la:]
    w2f, a2f, g2f = full_lb[:lw], full_lb[lw:lw + la], full_lb[lw + la:]

    c_all = _all_gather8("gather_c", c.reshape(PACK_ROWS, d // PACK_ROWS)).reshape(N_DEV, d)
    b_shard = lax.dynamic_slice(wt["b_ada"], (0, chip * n_ada), (1, n_ada))
    mod_part = _ada_mod(c_all, wt["w_ada"], b_shard)
    mod_all = _all_gather8("gather_mod", mod_part)[::2]
    mod = lax.dynamic_slice(mod_all, (0, dev, 0), (N_CHIPS, 1, n_ada)).reshape(1, N_CHIPS * n_ada)
    sh1, sc1, gt1, sh2, sc2, gt2 = [mod[:, i * d:(i + 1) * d] for i in range(6)]

    h, h_bf = _rowwise("norm1", _fn_norm1, [xs], [wt["norm1_gain"], sc1, sh1], [(d, F32), (d, BF16)], 256)
    hp = _shift_down(h)
    p = _matmul("mm_in", h_bf, full_in, b_shards=True, tm=512, tn=1536, tk=2048, n_outer=True)
    p_rkv, p_sb = (p, 3 * dr, 0), (p, 3 * ds, 1)
    pp = _shift_down(p[:, :3 * dr])
    pre_rows = [h, hp, p_rkv, pp]
    pre_params = [wt["mu_rkv"], wt["mu_w"], wt["mu_a"], wt["mu_g"], wt["w0"], wt["a0"], wt["k_k"], wt["k_a"],
                  w1f, w2f, a1f, a2f, g1f, g2f]
    pre = _rowwise("rwkv_pre", _fn_rwkv_pre, pre_rows, pre_params, [(dr, F32)] * 7, 128)
    r_, w_, k2, v_, rem, wr, g_ = pre
    y_raw, hist, s_last = _scan_fwd(rem, w_, wr, k2, r_, v_)
    post_rows = [y_raw, r_, k2, v_, g_]
    post_params = [wt["ln_x_gain"], wt["ln_x_bias"], wt["r_k"]]

    qg = jnp.tile(wt["q_norm_gain"], (1, nh))
    kg = jnp.tile(wt["k_norm_gain"], (1, nh))
    qn, kn, vs = _rowwise("qk_norm", _fn_qk_norm, [p_sb], [qg, kg], [(ds, BF16)] * 3, 256)

    def to_heads(a):
        return a.reshape(t, nh, HEAD_DIM).transpose(1, 0, 2)

    def from_heads(a):
        return a.transpose(1, 0, 2).reshape(t, ds)

    qh, kh, vh = to_heads(qn), to_heads(kn), to_heads(vs)
    o_h, lsum, later_full = _sb_fwd(qh, kh, vh, later_shards)
    full_out, full_gu, full_down = with_own_slot(later_full, later_shards)
    full_out = full_out.reshape(d, d)
    full_down = full_down.reshape(dff, d)
    (ycat,) = _rowwise("rwkv_post", _fn_rwkv_post_cat, post_rows + [from_heads(o_h)], post_params, [(d, BF16)], 256)
    mix = _matmul("mm_out", ycat, full_out, tm=512, tn=1024, tk=2048, n_outer=True)
    norm2_params = [gt1, wt["norm2_gain"], sc2, sh2]
    x1, h2 = _rowwise("mix_norm2", _fn_mix_norm2, [xs, mix], norm2_params, [(d, F32), (d, BF16)], 256)
    gu = _matmul("mm_gate_up", h2, full_gu, b_shards=True, tm=512, tn=1408, tk=2048, n_outer=True)
    gate_up = [(gu, dff, 0), (gu, dff, 1)]
    (act,) = _rowwise("swiglu", _fn_swiglu, gate_up, [], [(dff, BF16)], 256)
    dn = _matmul("mm_down", act, full_down, tm=1024, tn=1024, tk=2816)
    loss_vec, dout, ddn, dgt2 = _loss_head("loss_head", x1, dn, target, gt2)

    dact = _matmul("mm_down_dx", ddn, full_down, tb=True, tm=512, tn=1408, tk=2048, n_outer=True)
    gw_down = _matmul("mm_down_dw", act, ddn, ta=True, tm=1408, tn=1024, tk=1024)
    dgu = _swiglu_bwd(gu, dact)
    dh2 = _matmul("mm_gate_up_dx", dgu, full_gu, tb=True, b_shards=True, tm=1024, tn=1024, tk=2816)
    gw_gu = _matmul("mm_gate_up_dw", h2, dgu, ta=True, out_shards=True, tm=1024, tn=1408, tk=1024)
    (dx_a, dmix), (dgt1, dgain2, dsc2, dsh2) = _rowwise_bwd(
        "mix_norm2_bwd", _fn_mix_norm2, [xs, mix], norm2_params, [[dout], [dh2]], [F32, BF16], [True] * 4, 128)
    dycat = _matmul("mm_out_dx", dmix, full_out, tb=True, tm=512, tn=1024, tk=2048, n_outer=True)
    gw_out = _matmul("mm_out_dw", ycat, dmix, ta=True, tm=1024, tn=1024, tk=1024)
    (dy_raw, dr_f, dk_f, dv_f, dg), (dlng, dlnb, drk) = _rowwise_bwd(
        "rwkv_post_bwd", _fn_rwkv_post, post_rows, post_params, [[(dycat, dr, 0)]], [F32] * 5, [True] * 3, 128)
    place = jnp.stack([ac, chip]).astype(jnp.int32)

    def pair_reduce(tag, names, grads):
        from_sibling = _pair_exchange("grad_pair_exchange_" + tag, grads)
        return [_pair_add("pair_add_" + n, g, o, place) for n, g, o in zip(names, grads, from_sibling)]

    early_names = ["w_out", "w_gate_up", "w_down"]
    early_sums = pair_reduce("early", early_names, [gw_out.reshape(N_CHIPS, d // N_CHIPS, d), gw_gu,
                                                    gw_down.reshape(N_CHIPS, dff // N_CHIPS, d)])
    do_h = to_heads(dycat[:, dr:])
    dqh, dkh, dvh, early_from_chips = _sb_bwd(qh, kh, vh, lsum, do_h, early_sums)
    drem_s, dw_s, dwr_s, dk_s, dr_s, dv_s = _scan_bwd(rem, w_, wr, k2, r_, v_, hist, s_last, dy_raw)
    (dp_sb,), (dqg, dkg) = _rowwise_bwd(
        "qk_norm_bwd", _fn_qk_norm, [p_sb], [qg, kg], [[from_heads(dqh)], [from_heads(dkh)], [from_heads(dvh)]],
        [F32], [True, True], 128)
    pre_cts = [[dr_s, dr_f], [dw_s], [dk_s, dk_f], [dv_s, dv_f], [drem_s], [dwr_s], [dg]]
    (dh_a, dhp, dp_rkv, dpp), pre_g = _rowwise_bwd(
        "rwkv_pre_bwd", _fn_rwkv_pre, pre_rows, pre_params, pre_cts, [F32] * 4, [True] * 14, 128)
    dp = jnp.concatenate([dp_rkv + _shift_up(dpp), dp_sb], axis=1).astype(BF16)
    dh_mm = _matmul("mm_in_dx", dp, full_in, tb=True, b_shards=True, tm=1024, tn=1024, tk=1536)
    gw_in = _matmul("mm_in_dw", h_bf, dp, ta=True, out_shards=True, tm=1024, tn=1536, tk=1024)
    (grad_x,), (dgain1, dsc1, dsh1) = _rowwise_bwd(
        "norm1_bwd", _fn_norm1, [xs], [wt["norm1_gain"], sc1, sh1], [[dh_a, dh_mm, _shift_up(dhp)], []],
        [F32], [True] * 3, 128, add_to_first=[dx_a])

    g_mu_rkv, g_mu_w, g_mu_a, g_mu_g, g_w0, g_a0, g_kk, g_ka, gw1, gw2, ga1, ga2, gg1, gg2 = pre_g
    g_la = jnp.concatenate([gw1, ga1, gg1], axis=1).reshape(N_CHIPS, d // N_CHIPS, lw + la + lg)
    g_lb = jnp.concatenate([gw2, ga2, gg2], axis=0)
    g_lb = g_lb.reshape(lw + la + lg, N_CHIPS, dr // N_CHIPS).transpose(1, 0, 2)
    late_names = ["w_in", "lora_a", "lora_b"]
    late_sums = pair_reduce("late", late_names, [gw_in, g_la, g_lb])
    late_from_chips = _chip_exchange("grad_chip_exchange", late_sums)
    halves = [_sum_chips("chip_sum_" + n, p, q, place)
              for n, p, q in zip(early_names + late_names, early_sums + late_sums,
                                 list(early_from_chips) + list(late_from_chips))]
    r_out, r_gu, r_down, r_in, r_la, r_lb = _pair_share(halves)

    dmod = jnp.concatenate([dsh1, dsc1, dgt1, dsh2, dsc2, dgt2], axis=1)
    dqg = dqg.reshape(nh, HEAD_DIM).sum(axis=0, keepdims=True)
    dkg = dkg.reshape(nh, HEAD_DIM).sum(axis=0, keepdims=True)
    small_g = [dmod, dgain1, dgain2, g_mu_rkv, g_mu_w, g_mu_a, g_mu_g, g_w0, g_a0, g_kk, g_ka, drk, dlng, dlnb,
               dqg, dkg]
    lead = jnp.zeros((1, LOSS_SLOT), F32)
    packed = _pack_small([loss_vec] + small_g)
    gathered = _all_gather8("gather_small", packed)
    sm_g, sm_d, sm_m, sm_v = _small_update(gathered, _pack_small([lead] + [wt[n] for n in SMALL]),
                                           _pack_small([lead] + [mom[n] for n in SMALL]),
                                           _pack_small([lead] + [var[n] for n in SMALL]))
    loss = sm_g.reshape(-1)[0]

    def unpack(packed_arr):
        flat, out, pos = packed_arr.reshape(-1), {}, LOSS_SLOT
        for n in SMALL:
            size = wt[n].size
            out[n] = flat[pos:pos + size]
            pos += size
        return out

    res = {"grad": unpack(sm_g), "delta": unpack(sm_d), "m": unpack(sm_m), "v": unpack(sm_v)}

    dmod_all = gathered.reshape(N_DEV, -1)[:, LOSS_SLOT:LOSS_SLOT + N_CHIPS * n_ada]
    dmod_cols = lax.dynamic_slice(dmod_all, (0, chip * n_ada), (N_DEV, n_ada))
    pad8 = ((0, N_DEV), (0, 0))
    res["grad"]["w_ada"] = _ada_grad(jnp.pad(c_all, pad8), jnp.pad(dmod_cols, pad8))

    res["grad"].update(w_in=r_in, w_out=r_out, w_gate_up=r_gu, w_down=r_down)
    for n in ("w_ada", "w_in", "w_out", "w_gate_up", "w_down"):
        res["delta"][n], res["m"][n], res["v"][n] = _adamw("adamw_" + n, wt[n], res["grad"][n], mom[n], var[n])
    la_d, la_m, la_v = _adamw("adamw_lora_a", lora_a(wt), r_la, lora_a(mom), lora_a(var))
    lb_d, lb_m, lb_v = _adamw("adamw_lora_b", lora_b(wt), r_lb, lora_b(mom), lora_b(var))
    for key, pa, pb in (("grad", r_la, r_lb), ("delta", la_d, lb_d), ("m", la_m, lb_m), ("v", la_v, lb_v)):
        res[key].update(w1=pa[:, :lw], a1=pa[:, lw:lw + la], g1=pa[:, lw + la:],
                        w2=pb[:lw], a2=pb[lw:lw + la], g2=pb[lw + la:])

    outs = [loss, grad_x[None]]
    for key in ("grad", "delta", "m", "v"):
        outs += [res[key][n].reshape(given[n].shape) for n in WEIGHTS]
    return tuple(outs)
```

```python
import functools
import math

import jax
import jax.numpy as jnp
from jax import lax
from jax.experimental import pallas as pl
from jax.experimental.pallas import tpu as pltpu

F32 = jnp.float32
BF16 = jnp.bfloat16
HEAD_DIM = 64
LANES = 128
RMS_EPS = 1e-6
GN_EPS = 64e-5
L2_EPS = 1e-12
ADAM_LR, ADAM_B1, ADAM_B2, ADAM_EPS, ADAM_WD, ADAM_STEP = 0.001, 0.9, 0.999, 1e-08, 0.01, 10
VMEM_LIMIT = 56 * 1024 * 1024
MESH = pl.DeviceIdType.MESH
HI = lax.Precision.HIGHEST
N_CHIPS = 4
N_DEV = 8
XY_MASKS = ((1, 0), (0, 1), (1, 1))


def _pick(dim, prefs):
    for p in prefs:
        if dim % p == 0:
            return p
    return dim


def _params(sem=None, vmem=VMEM_LIMIT):
    return pltpu.CompilerParams(dimension_semantics=sem, vmem_limit_bytes=vmem)


def _sigmoid(x):
    return 1.0 / (1.0 + jnp.exp(-x))


@jax.custom_vjp
def _softplus(x):
    return jnp.maximum(x, 0.0) + jnp.log(1.0 + jnp.exp(-jnp.abs(x)))


_softplus.defvjp(lambda x: (_softplus(x), x), lambda x, g: (g * _sigmoid(x),))


def _silu(x):
    return x * _sigmoid(x)


@jax.custom_vjp
def _bdot(a, b):
    return jnp.dot(a.astype(BF16), b.astype(BF16), preferred_element_type=F32)


def _bdot_bwd(res, g):
    a, b = res
    gb = g.astype(BF16)
    da = lax.dot_general(gb, b.astype(BF16), (((1,), (1,)), ((), ())), preferred_element_type=F32)
    db = lax.dot_general(a.astype(BF16), gb, (((0,), (0,)), ((), ())), preferred_element_type=F32)
    return da.astype(a.dtype), db.astype(b.dtype)


_bdot.defvjp(lambda a, b: (_bdot(a, b), (a, b)), _bdot_bwd)


def _head_ones():
    i = lax.broadcasted_iota(jnp.int32, (LANES, LANES), 0) // HEAD_DIM
    j = lax.broadcasted_iota(jnp.int32, (LANES, LANES), 1) // HEAD_DIM
    return (i == j).astype(F32)


def _hdot(x, ones_bf):
    hi = x.astype(BF16)
    rest = x - hi.astype(F32)
    mid = rest.astype(BF16)
    lo = (rest - mid.astype(F32)).astype(BF16)
    out = jnp.dot(hi, ones_bf, preferred_element_type=F32)
    out += jnp.dot(mid, ones_bf, preferred_element_type=F32)
    return out + jnp.dot(lo, ones_bf, preferred_element_type=F32)


@jax.custom_vjp
def _segsum(x):
    ones = _head_ones().astype(BF16)
    parts = [_hdot(x[:, LANES * j:LANES * (j + 1)], ones) for j in range(x.shape[1] // LANES)]
    return parts[0] if len(parts) == 1 else jnp.concatenate(parts, axis=1)


_segsum.defvjp(lambda x: (_segsum(x), None), lambda _, g: (_segsum(g),))


def _rms(x, gain):
    return x * lax.rsqrt(jnp.mean(x * x, axis=-1, keepdims=True) + RMS_EPS) * gain


def _matmul(name, a, b, *, ta=False, tb=False, b_shards=False, out_shards=False, out_dtype=F32,
            tm=512, tn=512, tk=512, n_outer=False, exchange=()):
    if ta:
        kdim, m = a.shape
    else:
        m, kdim = a.shape
    if b_shards:
        if tb:
            n, ks = b.shape[1], b.shape[2]
            assert ks * N_CHIPS == kdim
        else:
            ns = b.shape[2]
            n = ns * N_CHIPS
            assert b.shape[1] == kdim
    else:
        n = b.shape[0] if tb else b.shape[1]
    tm = _pick(m, (tm, 512, 256, 128))
    n_part = n // N_CHIPS if (out_shards or (b_shards and not tb)) else n
    tn = _pick(n_part, (tn, 512, 256, 128))
    k_part = kdim // N_CHIPS if (b_shards and tb) else kdim
    tk = _pick(k_part, (tk, 512, 256, 128))
    nb = n_part // tn
    kb = k_part // tk
    nk = kdim // tk
    grid = (n // tn, m // tm, nk) if n_outer else (m // tm, n // tn, nk)

    def spec(shape, index):
        if n_outer:
            return pl.BlockSpec(shape, lambda j, i, k: index(i, j, k))
        return pl.BlockSpec(shape, index)

    if ta:
        a_spec = spec((tk, tm), lambda i, j, k: (k, i))
    else:
        a_spec = spec((tm, tk), lambda i, j, k: (i, k))
    if b_shards and tb:
        b_spec = spec((None, tn, tk), lambda i, j, k: (k // kb, j, k % kb))
    elif b_shards:
        b_spec = spec((None, tk, tn), lambda i, j, k: (j // nb, k, j % nb))
    elif tb:
        b_spec = spec((tn, tk), lambda i, j, k: (j, k))
    else:
        b_spec = spec((tk, tn), lambda i, j, k: (k, j))
    if out_shards:
        o_spec = spec((None, tm, tn), lambda i, j, k: (j // nb, i, j % nb))
        o_shape = jax.ShapeDtypeStruct((N_CHIPS, m, n_part), out_dtype)
    else:
        o_spec = spec((tm, tn), lambda i, j, k: (i, j))
        o_shape = jax.ShapeDtypeStruct((m, n), out_dtype)
    dims = (((0 if ta else 1,), (1 if tb else 0,)), ((), ()))

    def product(a_ref, b_ref):
        return lax.dot_general(a_ref[...].astype(BF16), b_ref[...].astype(BF16), dims, preferred_element_type=F32)

    def body_one_step(a_ref, b_ref, o_ref):
        o_ref[...] = product(a_ref, b_ref).astype(o_ref.dtype)

    def body(a_ref, b_ref, o_ref, acc_ref):
        k = pl.program_id(2)

        @pl.when(k == 0)
        def _():
            acc_ref[...] = product(a_ref, b_ref)

        @pl.when(jnp.logical_and(k > 0, k < nk - 1))
        def _():
            acc_ref[...] += product(a_ref, b_ref)

        @pl.when(k == nk - 1)
        def _():
            o_ref[...] = (acc_ref[...] + product(a_ref, b_ref)).astype(o_ref.dtype)

    acc_scratch = [] if nk == 1 else [pltpu.VMEM((tm, tn), F32)]
    if not exchange:
        return pl.pallas_call(
            body_one_step if nk == 1 else body, name=name, grid=grid, in_specs=[a_spec, b_spec], out_specs=o_spec,
            out_shape=o_shape, scratch_shapes=acc_scratch,
            compiler_params=_params(("parallel", "parallel", "arbitrary")),
        )(a, b)

    nx = len(exchange)

    def body_with_exchange(a_ref, b_ref, *rest):
        p_in, o_ref, p_out = rest[:nx], rest[nx], rest[nx + 1:2 * nx + 1]
        scratch = rest[2 * nx + 1:]
        acc, sems = scratch[:-2], scratch[-2:]
        at = [pl.program_id(axis) for axis in range(3)]

        @pl.when(jnp.logical_and(jnp.logical_and(at[0] == 0, at[1] == 0), at[2] == 0))
        def _():
            _exchange_phase("issue", p_in, p_out, sems)

        (body_one_step if nk == 1 else body)(a_ref, b_ref, o_ref, *acc)

        @pl.when(jnp.logical_and(jnp.logical_and(at[0] == grid[0] - 1, at[1] == grid[1] - 1), at[2] == grid[2] - 1))
        def _():
            _exchange_phase("finish", p_in, p_out, sems)

    res = pl.pallas_call(
        body_with_exchange, name=name, grid=grid, in_specs=[a_spec, b_spec] + [ANY] * nx,
        out_specs=[o_spec] + [ANY] * nx,
        out_shape=[o_shape] + [jax.ShapeDtypeStruct(p.shape, p.dtype) for p in exchange],
        scratch_shapes=acc_scratch + _exchange_sems(nx),
        compiler_params=_params(("arbitrary", "arbitrary", "arbitrary")),
    )(a, b, *exchange)
    return res[0], res[1:]


def _row_in(spec, tile):
    if isinstance(spec, tuple):
        arr, width, cb = spec
    else:
        arr, width, cb = spec, spec.shape[1], 0
    return arr, pl.BlockSpec((tile, width), lambda i, cb=cb: (i, cb))


def _full_spec(arr):
    nd = arr.ndim
    return pl.BlockSpec(arr.shape, lambda i, nd=nd: (0,) * nd, pipeline_mode=pl.Buffered(1))


def _rowwise(name, fn, rows, params, outs, tile):
    t = (rows[0][0] if isinstance(rows[0], tuple) else rows[0]).shape[0]
    tile = _pick(t, (tile,))
    arrs, specs = zip(*[_row_in(s, tile) for s in rows])
    nr, npar = len(rows), len(params)

    def body(*refs):
        rv = [r[...].astype(F32) for r in refs[:nr]]
        pv = [p[...] for p in refs[nr:nr + npar]]
        res = fn(*rv, *pv)
        for o_ref, val in zip(refs[nr + npar:], res):
            o_ref[...] = val.astype(o_ref.dtype)

    return pl.pallas_call(
        body, name=name, grid=(t // tile,),
        in_specs=list(specs) + [_full_spec(p) for p in params],
        out_specs=[pl.BlockSpec((tile, w), lambda i: (i, 0)) for w, _ in outs],
        out_shape=[jax.ShapeDtypeStruct((t, w), d) for w, d in outs],
        compiler_params=_params(("parallel",)),
    )(*arrs, *params)


def _rowwise_bwd(name, fn, rows, params, cts, row_grads, param_grads, tile, add_to_first=()):
    t = (rows[0][0] if isinstance(rows[0], tuple) else rows[0]).shape[0]
    tile = _pick(t, (tile,))
    arrs, specs = zip(*[_row_in(s, tile) for s in rows])
    n_add = len(add_to_first)
    flat_cts = [c for group in cts for c in group] + list(add_to_first)
    c_arrs, c_specs = zip(*[_row_in(s, tile) for s in flat_cts])
    nr, npar, nc = len(rows), len(params), len(flat_cts)
    rg_idx = [i for i, d in enumerate(row_grads) if d is not None]
    pg_idx = [i for i, d in enumerate(param_grads) if d]

    def body(*refs):
        rv = [r[...].astype(F32) for r in refs[:nr]]
        pv = [p[...] for p in refs[nr:nr + npar]]
        cv = [c[...].astype(F32) for c in refs[nr + npar:nr + npar + nc]]
        o_refs = refs[nr + npar + nc:]
        outs, vjp = jax.vjp(fn, *rv, *pv)
        ct, pos = [], 0
        for group, o in zip(cts, outs):
            if group:
                acc = cv[pos]
                for extra in cv[pos + 1:pos + len(group)]:
                    acc = acc + extra
                pos += len(group)
            else:
                acc = jnp.zeros_like(o)
            ct.append(acc)
        grads = list(vjp(tuple(ct)))
        for extra in cv[nc - n_add:]:
            grads[rg_idx[0]] = grads[rg_idx[0]] + extra
        for o_ref, i in zip(o_refs[:len(rg_idx)], rg_idx):
            o_ref[...] = grads[i].astype(o_ref.dtype)
        first = pl.program_id(0) == 0
        for o_ref, i in zip(o_refs[len(rg_idx):], pg_idx):
            g = grads[nr + i].astype(F32)

            @pl.when(first)
            def _(o_ref=o_ref, g=g):
                o_ref[...] = g

            @pl.when(jnp.logical_not(first))
            def _(o_ref=o_ref, g=g):
                o_ref[...] += g

    def width(i):
        s = rows[i]
        return s[1] if isinstance(s, tuple) else s.shape[1]

    out_specs = [pl.BlockSpec((tile, width(i)), lambda i_: (i_, 0)) for i in rg_idx]
    out_shape = [jax.ShapeDtypeStruct((t, width(i)), row_grads[i]) for i in rg_idx]
    out_specs += [_full_spec(params[i]) for i in pg_idx]
    out_shape += [jax.ShapeDtypeStruct(params[i].shape, F32) for i in pg_idx]
    res = pl.pallas_call(
        body, name=name, grid=(t // tile,),
        in_specs=list(specs) + [_full_spec(p) for p in params] + list(c_specs),
        out_specs=out_specs, out_shape=out_shape,
        compiler_params=_params(("arbitrary",)),
    )(*arrs, *params, *c_arrs)
    return res[:len(rg_idx)], res[len(rg_idx):]


def _fn_norm1(x, gain, sc, sh):
    h = _rms(x, gain) * (1.0 + sc) + sh
    return h, h


def _fn_rwkv_pre(h, hp, p, pp, mu_rkv, mu_w, mu_a, mu_g, w0, a0, k_k, k_a, w1, w2, a1, a2, g1, g2):
    d = p.shape[1] // 3
    dh = hp - h
    xw = h + dh * mu_w
    xa = h + dh * mu_a
    xg = h + dh * mu_g
    pr = p + (pp - p) * mu_rkv
    r, k, v = pr[:, :d], pr[:, d:2 * d], pr[:, 2 * d:]
    w_log = -_softplus(-(w0 + _bdot(jnp.tanh(_bdot(xw, w1)), w2))) - 0.5
    decay = jnp.exp(-jnp.exp(w_log))
    a = _sigmoid(a0 + _bdot(_bdot(xa, a1), a2))
    g = _bdot(_sigmoid(_bdot(xg, g1)), g2)
    kk = k * k_k
    kk = kk * lax.rsqrt(_segsum(kk * kk) + L2_EPS)
    k2 = k * (1.0 + (a - 1.0) * k_a)
    return r, decay, k2, v, -kk, kk * a, g


def _fn_rwkv_post(y, r, k2, v, g, ln_g, ln_b, r_k):
    inv = 1.0 / HEAD_DIM
    mean = _segsum(y) * inv
    yc = y - mean
    var = _segsum(yc * yc) * inv
    yn = yc * lax.rsqrt(var + GN_EPS) * ln_g + ln_b
    bonus = _segsum(r * k2 * r_k) * v
    return ((yn + bonus) * g,)


def _fn_rwkv_post_cat(y, r, k2, v, g, o_sb, ln_g, ln_b, r_k):
    return (jnp.concatenate([_fn_rwkv_post(y, r, k2, v, g, ln_g, ln_b, r_k)[0], o_sb], axis=1),)


def _fn_qk_norm(p, qg, kg):
    d = p.shape[1] // 3
    q, k, v = p[:, :d], p[:, d:2 * d], p[:, 2 * d:]
    inv = 1.0 / HEAD_DIM
    qn = q * lax.rsqrt(_segsum(q * q) * inv + RMS_EPS) * qg
    kn = k * lax.rsqrt(_segsum(k * k) * inv + RMS_EPS) * kg
    return qn * SB_SCALE, kn, v


def _fn_mix_norm2(x, mix, gt1, gain, sc, sh):
    x1 = x + gt1 * mix
    h2 = _rms(x1, gain) * (1.0 + sc) + sh
    return x1, h2


def _fn_swiglu(gate, up):
    return (_silu(gate) * up,)


def _swiglu_bwd(gu, dact, tile=128):
    t, dff = dact.shape
    tile = _pick(t, (tile,))

    def body(gate_ref, up_ref, d_ref, o_ref):
        _, vjp = jax.vjp(_fn_swiglu, gate_ref[...], up_ref[...])
        dgate, dup = vjp((d_ref[...],))
        o_ref[:, :dff] = dgate.astype(o_ref.dtype)
        o_ref[:, dff:] = dup.astype(o_ref.dtype)

    return pl.pallas_call(
        body, name="swiglu_bwd", grid=(t // tile,),
        in_specs=[pl.BlockSpec((tile, dff), lambda i: (i, 0)), pl.BlockSpec((tile, dff), lambda i: (i, 1)),
                  pl.BlockSpec((tile, dff), lambda i: (i, 0))],
        out_specs=pl.BlockSpec((tile, 2 * dff), lambda i: (i, 0)),
        out_shape=jax.ShapeDtypeStruct((t, 2 * dff), BF16),
        compiler_params=_params(("parallel",)),
    )(gu, gu, dact)


def _loss_head(name, x1, dn, target, gt2, tile=256):
    t, d = x1.shape
    tile = _pick(t, (tile,))

    def body(x1_ref, dn_ref, tg_ref, gt_ref, loss_ref, dout_ref, ddn_ref, dgt_ref):
        dnv = dn_ref[...]
        gt = gt_ref[...]
        err = x1_ref[...] + gt * dnv - tg_ref[...]
        dout = err * (1.0 / d)
        dout_ref[...] = dout
        ddn_ref[...] = (dout * gt).astype(ddn_ref.dtype)
        part = 0.5 * jnp.sum(jnp.sum(err * dout, axis=-1, keepdims=True), axis=0, keepdims=True)
        dgt = jnp.sum(dout * dnv, axis=0, keepdims=True)
        first = pl.program_id(0) == 0

        @pl.when(first)
        def _():
            loss_ref[...] = jnp.broadcast_to(part, loss_ref.shape)
            dgt_ref[...] = dgt

        @pl.when(jnp.logical_not(first))
        def _():
            loss_ref[...] += jnp.broadcast_to(part, loss_ref.shape)
            dgt_ref[...] += dgt

    row = pl.BlockSpec((tile, d), lambda i: (i, 0))
    vec = pl.BlockSpec((1, d), lambda i: (0, 0))
    return pl.pallas_call(
        body, name=name, grid=(t // tile,),
        in_specs=[row, row, row, vec],
        out_specs=[pl.BlockSpec((1, LANES), lambda i: (0, 0)), row, row, vec],
        out_shape=[jax.ShapeDtypeStruct((1, LANES), F32), jax.ShapeDtypeStruct((t, d), F32),
                   jax.ShapeDtypeStruct((t, d), BF16), jax.ShapeDtypeStruct((1, d), F32)],
        compiler_params=_params(("arbitrary",)),
    )(x1, dn, target, gt2)


SCAN_BLOCK = 32
N_COL = 5
WIDE = 2 * LANES


def _wide_eye():
    i = lax.broadcasted_iota(jnp.int32, (HEAD_DIM, WIDE), 0)
    j = lax.broadcasted_iota(jnp.int32, (HEAD_DIM, WIDE), 1) % HEAD_DIM
    return (i == j).astype(BF16)


def _wide_ones():
    i = lax.broadcasted_iota(jnp.int32, (WIDE, WIDE), 0) // HEAD_DIM
    j = lax.broadcasted_iota(jnp.int32, (WIDE, WIDE), 1) // HEAD_DIM
    return (i == j).astype(BF16)


COL_PIECES = (1, 2, 2, 2, 1)


def _col_tiles(refs, i, nq, eye, ones_bf):
    levels = max(COL_PIECES)
    pieces = [[] for _ in range(levels)]
    for ref, n_pieces in zip(refs, COL_PIECES):
        full = ref[pl.ds(i, 1), :]
        for q in range(nq):
            rest = full[:, q * WIDE:(q + 1) * WIDE]
            for level in range(n_pieces):
                part = rest.astype(BF16)
                pieces[level].append(part * eye)
                rest = rest - part.astype(F32)
    rows = nq * HEAD_DIM
    out = jnp.dot(jnp.concatenate(sum(pieces, []), axis=0), ones_bf, preferred_element_type=F32)
    start, place = 0, {}
    for level in range(levels):
        for n, n_pieces in enumerate(COL_PIECES):
            if n_pieces > level:
                place[level, n] = start
                start += rows
    tiles = []
    for n, n_pieces in enumerate(COL_PIECES):
        acc = out[place[0, n]:place[0, n] + rows]
        for level in range(1, n_pieces):
            acc = acc + out[place[level, n]:place[level, n] + rows]
        tiles.append(acc)
    return jnp.concatenate(tiles, axis=0).reshape(N_COL * nq, HEAD_DIM, WIDE)


def _bf16_round(x):
    bits = lax.bitcast_convert_type(x, jnp.uint32)
    bits = (bits + jnp.uint32(0x7FFF) + ((bits >> 16) & jnp.uint32(1))) & jnp.uint32(0xFFFF0000)
    return lax.bitcast_convert_type(bits, F32)


def _pair_tile(tiles_ref, n, p, nq):
    return tiles_ref[n * nq + p // 2, :, (p % 2) * LANES:(p % 2 + 1) * LANES]


def _scan_fwd(rem, w, wr, k, r, v):
    t, dr = v.shape
    npair, nq = dr // LANES, dr // WIDE
    tb = _pick(t, (SCAN_BLOCK,))

    def body(rem_ref, w_ref, wr_ref, k_ref, r_ref, v_ref, *rest):
        next_refs, (y_ref, hist_ref, last_ref, s_ref, sb_ref, tiles_a, tiles_b) = rest[:N_COL], rest[N_COL:]
        eye, ones_bf = _wide_eye(), _wide_ones()
        col_refs = (rem_ref, w_ref, wr_ref, k_ref, r_ref)

        @pl.when(pl.program_id(0) == 0)
        def _():
            s_ref[...] = jnp.zeros_like(s_ref)
            sb_ref[...] = jnp.zeros_like(sb_ref)
            tiles_a[...] = _col_tiles(col_refs, 0, nq, eye, ones_bf)

        def step(i, tiles_ref):
            v_full, y_rows = v_ref[pl.ds(i, 1), :], []
            for p in range(npair):
                s = s_ref[p]
                hist_ref[i, p] = s
                c_rem, c_w, c_wr, c_k, c_r = [_pair_tile(tiles_ref, n, p, nq) for n in range(N_COL)]
                sa = jnp.sum(sb_ref[p] * c_rem, axis=0, keepdims=True)
                s2 = s * c_w + c_wr * sa + c_k * v_full[:, p * LANES:(p + 1) * LANES]
                s2_b = _bf16_round(s2)
                y_rows.append(jnp.sum(s2_b * c_r, axis=0, keepdims=True))
                s_ref[p] = s2
                sb_ref[p] = s2_b
            y_ref[pl.ds(i, 1), :] = jnp.concatenate(y_rows, axis=1)

        def two_steps(i, ahead_refs, ahead_row):
            tiles_b[...] = _col_tiles(col_refs, i + 1, nq, eye, ones_bf)
            step(i, tiles_a)
            tiles_a[...] = _col_tiles(ahead_refs, ahead_row, nq, eye, ones_bf)
            step(i + 1, tiles_b)

        def loop_body(m, carry):
            two_steps(2 * m, col_refs, 2 * m + 2)
            return carry

        lax.fori_loop(0, tb // 2 - 1, loop_body, 0, unroll=3)
        two_steps(tb - 2, next_refs, 0)
        last_ref[...] = sb_ref[...]

    nblk = t // tb
    blk = pl.BlockSpec((tb, dr), lambda i: (i, 0))
    nxt = pl.BlockSpec((8, dr), lambda i: (jnp.minimum(i + 1, nblk - 1) * (tb // 8), 0))
    tiles = pltpu.VMEM((N_COL * nq, HEAD_DIM, WIDE), F32)
    state = pltpu.VMEM((npair, HEAD_DIM, LANES), F32)
    return pl.pallas_call(
        body, name="rwkv_scan_fwd", grid=(nblk,),
        in_specs=[blk] * 6 + [nxt] * N_COL,
        out_specs=[blk, pl.BlockSpec((tb, npair, HEAD_DIM, LANES), lambda i: (i, 0, 0, 0)),
                   pl.BlockSpec((npair, HEAD_DIM, LANES), lambda i: (0, 0, 0))],
        out_shape=[jax.ShapeDtypeStruct((t, dr), F32), jax.ShapeDtypeStruct((t, npair, HEAD_DIM, LANES), F32),
                   jax.ShapeDtypeStruct((npair, HEAD_DIM, LANES), F32)],
        scratch_shapes=[state, state, tiles, tiles],
        compiler_params=_params(("arbitrary",)),
    )(rem, w, wr, k, r, v, rem, w, wr, k, r)


def _scan_bwd(rem, w, wr, k, r, v, hist, last, dy):
    t, dr = v.shape
    npair, nq = dr // LANES, dr // WIDE
    tb = _pick(t, (SCAN_BLOCK,))
    nblk = t // tb

    def body(rem_ref, w_ref, wr_ref, k_ref, r_ref, v_ref, hist_ref, last_ref, dy_ref, *rest):
        prev_refs, out_refs = rest[:N_COL], rest[N_COL:2 * N_COL]
        dv_ref, ds_ref, next_ref, tiles_a, tiles_b = rest[2 * N_COL:]
        eye, ones_bf = _wide_eye(), _wide_ones()
        col_refs = (rem_ref, w_ref, wr_ref, k_ref, r_ref)

        @pl.when(pl.program_id(0) == 0)
        def _():
            ds_ref[...] = jnp.zeros_like(ds_ref)
            next_ref[...] = last_ref[...]
            tiles_a[...] = _col_tiles(col_refs, tb - 1, nq, eye, ones_bf)

        def step(i, tiles_ref):
            grads = [[None] * npair for _ in range(N_COL)]
            v_full, dy_full, dv_rows = v_ref[pl.ds(i, 1), :], dy_ref[pl.ds(i, 1), :], []
            for p in range(npair):
                lanes = slice(p * LANES, (p + 1) * LANES)
                s = hist_ref[i, p]
                c_rem, c_w, c_wr, c_k, c_r = [_pair_tile(tiles_ref, n, p, nq) for n in range(N_COL)]
                v_row, dy_row = v_full[:, lanes], dy_full[:, lanes]
                s_b = _bf16_round(s)
                s2_b = next_ref[p]
                next_ref[p] = s_b
                sa = jnp.sum(s_b * c_rem, axis=0, keepdims=True)
                d2 = ds_ref[p] + c_r * dy_row
                dsa = jnp.sum(d2 * c_wr, axis=0, keepdims=True)
                dv_rows.append(jnp.sum(d2 * c_k, axis=0, keepdims=True))
                ds_ref[p] = d2 * c_w + c_rem * dsa
                for n, tile in enumerate((s_b * dsa, d2 * s, d2 * sa, d2 * v_row, s2_b * dy_row)):
                    grads[n][p] = tile.astype(BF16)
            wide = [jnp.concatenate(grads[n][2 * q:2 * q + 2], axis=1) for n in range(N_COL) for q in range(nq)]
            sums = jnp.dot(jnp.concatenate(wide, axis=0), ones_bf, preferred_element_type=F32)
            sums = sums.reshape(N_COL * nq, HEAD_DIM, WIDE)
            rows = jnp.sum(sums * eye.astype(F32)[None], axis=1)
            dv_ref[pl.ds(i, 1), :] = jnp.concatenate(dv_rows, axis=1)
            for n, o_ref in enumerate(out_refs):
                o_ref[pl.ds(i, 1), :] = jnp.concatenate([rows[n * nq + q:n * nq + q + 1] for q in range(nq)], axis=1)

        def two_steps(i, ahead_refs, ahead_row):
            tiles_b[...] = _col_tiles(col_refs, i - 1, nq, eye, ones_bf)
            step(i, tiles_a)
            tiles_a[...] = _col_tiles(ahead_refs, ahead_row, nq, eye, ones_bf)
            step(i - 1, tiles_b)

        def loop_body(m, carry):
            i = tb - 1 - 2 * m
            two_steps(i, col_refs, i - 2)
            return carry

        lax.fori_loop(0, tb // 2 - 1, loop_body, 0, unroll=3)
        two_steps(1, prev_refs, 7)

    blk = pl.BlockSpec((tb, dr), lambda i: (nblk - 1 - i, 0))
    prv = pl.BlockSpec((8, dr), lambda i: (jnp.maximum((nblk - 1 - i) * (tb // 8) - 1, 0), 0))
    tiles = pltpu.VMEM((N_COL * nq, HEAD_DIM, WIDE), F32)
    state = pltpu.VMEM((npair, HEAD_DIM, LANES), F32)
    return pl.pallas_call(
        body, name="rwkv_scan_bwd", grid=(nblk,),
        in_specs=[blk] * 6 + [pl.BlockSpec((tb, npair, HEAD_DIM, LANES), lambda i: (nblk - 1 - i, 0, 0, 0)),
                              pl.BlockSpec((npair, HEAD_DIM, LANES), lambda i: (0, 0, 0)), blk] + [prv] * N_COL,
        out_specs=[blk] * 6,
        out_shape=[jax.ShapeDtypeStruct((t, dr), F32)] * 6,
        scratch_shapes=[state, state, tiles, tiles],
        compiler_params=_params(("arbitrary",)),
    )(rem, w, wr, k, r, v, hist, last, dy, rem, w, wr, k, r)


SB_BLOCK = 256
SB_HEADS = 8
SB_HEADS_BWD = 4
NT_DIMS = (((1,), (1,)), ((), ()))
TN_DIMS = (((0,), (0,)), ((), ()))
SB_SCALE = 1.0 / math.sqrt(HEAD_DIM)


def _dot2(x, tri):
    hi = x.astype(BF16)
    mid = (x - hi.astype(F32)).astype(BF16)
    return jnp.dot(hi, tri, preferred_element_type=F32) + jnp.dot(mid, tri, preferred_element_type=F32)


def _sb_block_iotas(bs):
    return lax.broadcasted_iota(jnp.int32, (bs, bs), 0), lax.broadcasted_iota(jnp.int32, (bs, bs), 1)


def _when_step(group, block):
    return pl.when(jnp.logical_and(pl.program_id(0) == group, pl.program_id(1) == block))


def _sb_fwd(q, k, v, shards):
    h, t, d = q.shape
    bs = _pick(t, (SB_BLOCK,))
    nh = _pick(h, (SB_HEADS,))
    ngroup, nblock = h // nh, t // bs
    nw = len(shards)
    heights = [s.shape[0] for s in shards]

    def body(q_ref, k_ref, v_ref, *rest):
        w_in, (o_ref, l_ref), w_out, sems = rest[:nw], rest[nw:nw + 2], rest[nw + 2:2 * nw + 2], rest[2 * nw + 2:]

        @_when_step(0, 0)
        def _():
            _gather_phase("issue", w_in, w_out, sems, heights)

        @_when_step(ngroup - 1, (2 * nblock) // 3)
        def _():
            _gather_phase("forward", w_in, w_out, sems, heights)

        qi = pl.program_id(1)
        ri, ci = _sb_block_iotas(bs)
        tri_ge = (ri >= ci).astype(BF16)
        causal = ci < ri
        qv = [q_ref[hh] for hh in range(nh)]

        def blocks(j, masked, carry):
            accs, tails = carry[:nh], carry[nh:]
            heads = range(nh)
            rows = pl.ds(pl.multiple_of(j * bs, bs), bs)
            z = [lax.dot_general(qv[hh], k_ref[hh, rows, :], NT_DIMS, preferred_element_type=F32) for hh in heads]
            log1m = [-_softplus(z[hh]) for hh in heads]
            if masked:
                log1m = [jnp.where(causal, x, 0.0) for x in log1m]
            cs = [_dot2(log1m[hh], tri_ge) for hh in heads]
            a = [jnp.exp(z[hh] + cs[hh] + tails[hh]) for hh in heads]
            if masked:
                a = [jnp.where(causal, x, 0.0) for x in a]
            accs = [accs[hh] + jnp.dot(a[hh].astype(BF16), v_ref[hh, rows, :], preferred_element_type=F32)
                    for hh in heads]
            return tuple(accs) + tuple(tails[hh] + cs[hh][:, 0:1] for hh in heads)

        carry = blocks(qi, True, (jnp.zeros((bs, d), F32),) * nh + (jnp.zeros((bs, 1), F32),) * nh)
        carry = lax.fori_loop(0, qi, lambda n, c: blocks(qi - 1 - n, False, c), carry)
        for hh in range(nh):
            o_ref[hh] = carry[hh]
            l_ref[hh] = jnp.broadcast_to(carry[nh + hh], (bs, d))

        @_when_step(ngroup - 1, nblock - 1)
        def _():
            _gather_phase("finish", w_in, w_out, sems, heights)

    qs = pl.BlockSpec((nh, bs, d), lambda hh, i: (hh, i, 0))
    ks = pl.BlockSpec((nh, t, d), lambda hh, i: (hh, 0, 0), pipeline_mode=pl.Buffered(1))
    res = pl.pallas_call(
        body, name="sb_attn_fwd", grid=(ngroup, nblock),
        in_specs=[qs, ks, ks] + [ANY] * nw, out_specs=[qs, qs] + [ANY] * nw,
        out_shape=[jax.ShapeDtypeStruct((h, t, d), F32)] * 2 + _gather_out_shapes(shards),
        scratch_shapes=_gather_sems(nw),
        compiler_params=_params(("arbitrary", "arbitrary")),
    )(q, k, v, *shards)
    return res[0], res[1], res[2:]


def _sb_bwd(q, k, v, lsum, do, parts):
    h, t, d = q.shape
    bs = _pick(t, (SB_BLOCK,))
    nh = _pick(h, (SB_HEADS_BWD,))
    ngroup, nblock = h // nh, t // bs
    nw = len(parts)

    def body(q_ref, k_ref, v_ref, l_ref, do_ref, *rest):
        p_in, (dq_ref, dk_ref, dv_ref), p_out, sems = rest[:nw], rest[nw:nw + 3], rest[nw + 3:2 * nw + 3], rest[2 * nw + 3:]

        @_when_step(0, 0)
        def _():
            _exchange_phase("issue", p_in, p_out, sems)

        qi = pl.program_id(1)

        @pl.when(qi == 0)
        def _():
            dk_ref[...] = jnp.zeros_like(dk_ref)
            dv_ref[...] = jnp.zeros_like(dv_ref)

        ri, ci = _sb_block_iotas(bs)
        tri_lt = (ri < ci).astype(BF16)
        causal = ci < ri
        qv = [q_ref[hh] for hh in range(nh)]
        dob = [do_ref[hh].astype(BF16) for hh in range(nh)]
        ltot = [l_ref[hh][:, 0:1] for hh in range(nh)]

        def blocks(j, masked, carry):
            dq, pc, ec = carry[:nh], carry[nh:2 * nh], carry[2 * nh:]
            heads = range(nh)
            rows = pl.ds(pl.multiple_of(j * bs, bs), bs)
            z = [lax.dot_general(qv[hh], k_ref[hh, rows, :], NT_DIMS, preferred_element_type=F32) for hh in heads]
            da = [lax.dot_general(dob[hh], v_ref[hh, rows, :], NT_DIMS, preferred_element_type=F32) for hh in heads]
            nsp = [-_softplus(z[hh]) for hh in heads]
            log1m = [jnp.where(causal, x, 0.0) for x in nsp] if masked else nsp
            below = [_dot2(log1m[hh], tri_lt) + pc[hh] for hh in heads]
            a = [jnp.exp(z[hh] + (ltot[hh] - below[hh])) for hh in heads]
            if masked:
                a = [jnp.where(causal, x, 0.0) for x in a]
            e = [a[hh] * da[hh] for hh in heads]
            ebelow = [_dot2(e[hh], tri_lt) + ec[hh] for hh in heads]
            dz = [e[hh] * jnp.exp(nsp[hh]) - jnp.exp(z[hh] + nsp[hh]) * ebelow[hh] for hh in heads]
            if masked:
                dz = [jnp.where(causal, x, 0.0) for x in dz]
            dzb = [x.astype(BF16) for x in dz]
            for hh in heads:
                dv_ref[hh, rows, :] += lax.dot_general(a[hh].astype(BF16), dob[hh], TN_DIMS,
                                                       preferred_element_type=F32)
            for hh in heads:
                dk_ref[hh, rows, :] += lax.dot_general(dzb[hh], qv[hh], TN_DIMS, preferred_element_type=F32)
            dq = [dq[hh] + jnp.dot(dzb[hh], k_ref[hh, rows, :], preferred_element_type=F32) for hh in heads]
            pc = [pc[hh] + jnp.sum(log1m[hh], axis=1, keepdims=True) for hh in heads]
            ec = [ec[hh] + jnp.sum(e[hh], axis=1, keepdims=True) for hh in heads]
            return tuple(dq) + tuple(pc) + tuple(ec)

        zcol = jnp.zeros((bs, 1), F32)
        carry = lax.fori_loop(0, qi, lambda j, c: blocks(j, False, c),
                              (jnp.zeros((bs, d), F32),) * nh + (zcol,) * (2 * nh))
        carry = blocks(qi, True, carry)
        for hh in range(nh):
            dq_ref[hh] = carry[hh]

        @_when_step(ngroup - 1, nblock - 1)
        def _():
            _exchange_phase("finish", p_in, p_out, sems)

    qs = pl.BlockSpec((nh, bs, d), lambda hh, i: (hh, i, 0))
    ks = pl.BlockSpec((nh, t, d), lambda hh, i: (hh, 0, 0), pipeline_mode=pl.Buffered(1))
    res = pl.pallas_call(
        body, name="sb_attn_bwd", grid=(ngroup, nblock),
        in_specs=[qs, ks, ks, qs, qs] + [ANY] * nw, out_specs=[qs, ks, ks] + [ANY] * nw,
        out_shape=[jax.ShapeDtypeStruct((h, t, d), F32)] * 3 + [jax.ShapeDtypeStruct(p.shape, p.dtype) for p in parts],
        scratch_shapes=_exchange_sems(nw),
        compiler_params=_params(("arbitrary", "arbitrary")),
    )(q, k, v, lsum, do, *parts)
    return res[0], res[1], res[2], res[3:]


ANY = pl.BlockSpec(memory_space=pl.ANY)
IN_VMEM = pl.BlockSpec(memory_space=pltpu.VMEM)


def _coords():
    return lax.axis_index("x"), lax.axis_index("y"), lax.axis_index("c")


def _flip(v, bit):
    return 1 - v if bit else v


def _remote(src, dst, send_sem, recv_sem, device):
    return pltpu.make_async_remote_copy(src_ref=src, dst_ref=dst, send_sem=send_sem, recv_sem=recv_sem,
                                        device_id=device, device_id_type=MESH)


def _all_gather8(name, blk):
    m, n = blk.shape

    def body(x_ref, o_ref, send_sems, recv_sems, local_sem):
        x, y, c = _coords()
        own = pltpu.make_async_copy(x_ref, o_ref.at[4 * x + 2 * y + c], local_sem)
        own.start()
        peers = []
        for bits in range(1, N_DEV):
            px, py, pc = _flip(x, (bits >> 2) & 1), _flip(y, (bits >> 1) & 1), _flip(c, bits & 1)
            peers.append((px, py, pc))
        sends = []
        for k, peer in enumerate(peers):
            cp = _remote(x_ref, o_ref.at[4 * x + 2 * y + c], send_sems.at[k], recv_sems.at[k], peer)
            cp.start()
            sends.append(cp)
        for k, (px, py, pc) in enumerate(peers):
            slot = o_ref.at[4 * px + 2 * py + pc]
            _remote(slot, slot, send_sems.at[k], recv_sems.at[k], (px, py, pc)).wait_recv()
        for cp in sends:
            cp.wait_send()
        own.wait()

    return pl.pallas_call(
        body, name=name, out_shape=jax.ShapeDtypeStruct((N_DEV, m, n), blk.dtype),
        in_specs=[IN_VMEM], out_specs=IN_VMEM,
        scratch_shapes=[pltpu.SemaphoreType.DMA((N_DEV - 1,)), pltpu.SemaphoreType.DMA((N_DEV - 1,)),
                        pltpu.SemaphoreType.DMA],
        compiler_params=_params(),
    )(blk)


def _gather_weights(shards):
    nw = len(shards)
    heights = [s.shape[0] for s in shards]

    def body(*refs):
        ins, outs, sems = refs[:nw], refs[nw:2 * nw], refs[2 * nw:]
        _gather_phase("issue", ins, outs, sems, heights)
        _gather_phase("forward", ins, outs, sems, heights)
        _gather_phase("finish", ins, outs, sems, heights)

    return pl.pallas_call(
        body, name="gather_weights",
        out_shape=_gather_out_shapes(shards),
        in_specs=[ANY] * nw, out_specs=[ANY] * nw,
        scratch_shapes=_gather_sems(nw),
        compiler_params=_params(),
    )(*shards)


def _gather_out_shapes(shards):
    return [jax.ShapeDtypeStruct((N_CHIPS,) + s.shape, s.dtype) for s in shards]


def _gather_sems(nw):
    return [pltpu.SemaphoreType.DMA((3 * nw,))] * 4


def _gather_phase(phase, ins, outs, sems, heights):
    ici_send, ici_recv, d2d_send, d2d_recv = sems
    x, y, c = _coords()
    chip = 2 * x + y
    sibling = (x, y, 1 - c)
    for w, height in enumerate(heights):
        half = height // 2
        mine, theirs = pl.ds(c * half, half), pl.ds((1 - c) * half, half)
        for j, (a, b) in enumerate(XY_MASKS):
            px, py = _flip(x, a), _flip(y, b)
            k = 3 * w + j
            over_ici = _remote(ins[w].at[mine], outs[w].at[chip, mine], ici_send.at[k], ici_recv.at[k], (px, py, c))
            landed = outs[w].at[2 * px + py, mine]
            onward = _remote(landed, landed, d2d_send.at[k], d2d_recv.at[k], sibling)
            if phase == "issue":
                over_ici.start()
            elif phase == "forward":
                _remote(landed, landed, ici_send.at[k], ici_recv.at[k], (px, py, c)).wait_recv()
                onward.start()
            else:
                slot = outs[w].at[2 * px + py, theirs]
                _remote(slot, slot, d2d_send.at[k], d2d_recv.at[k], sibling).wait_recv()
                over_ici.wait_send()
                onward.wait_send()


def _pair_exchange(name, grads):
    nw = len(grads)

    def body(*refs):
        ins, outs = refs[:nw], refs[nw:2 * nw]
        send_sems, recv_sems = refs[2 * nw:]
        x, y, c = _coords()
        sibling = (x, y, 1 - c)
        sends = []
        for w in range(nw):
            half = grads[w].shape[1] // 2
            cp = _remote(ins[w].at[:, pl.ds((1 - c) * half, half)], outs[w], send_sems.at[w], recv_sems.at[w], sibling)
            cp.start()
            sends.append(cp)
        for w in range(nw):
            _remote(outs[w], outs[w], send_sems.at[w], recv_sems.at[w], sibling).wait_recv()
        for cp in sends:
            cp.wait_send()

    return pl.pallas_call(
        body, name=name,
        out_shape=[jax.ShapeDtypeStruct((N_CHIPS, g.shape[1] // 2, g.shape[2]), g.dtype) for g in grads],
        in_specs=[ANY] * nw, out_specs=[ANY] * nw,
        scratch_shapes=[pltpu.SemaphoreType.DMA((nw,))] * 2,
        compiler_params=_params(),
    )(*grads)


def _chip_exchange(name, parts):
    nw = len(parts)

    def body(*refs):
        ins, outs, sems = refs[:nw], refs[nw:2 * nw], refs[2 * nw:]
        _exchange_phase("issue", ins, outs, sems)
        _exchange_phase("finish", ins, outs, sems)

    return pl.pallas_call(
        body, name=name,
        out_shape=[jax.ShapeDtypeStruct(p.shape, p.dtype) for p in parts],
        in_specs=[ANY] * nw, out_specs=[ANY] * nw,
        scratch_shapes=_exchange_sems(nw),
        compiler_params=_params(),
    )(*parts)


def _exchange_sems(nw):
    return [pltpu.SemaphoreType.DMA((3 * nw,))] * 2


def _exchange_phase(phase, ins, outs, sems):
    send_sems, recv_sems = sems
    x, y, c = _coords()
    chip = 2 * x + y
    for w in range(len(ins)):
        for j, (a, b) in enumerate(XY_MASKS):
            px, py = _flip(x, a), _flip(y, b)
            k = 3 * w + j
            send = _remote(ins[w].at[2 * px + py], outs[w].at[chip], send_sems.at[k], recv_sems.at[k], (px, py, c))
            if phase == "issue":
                send.start()
            else:
                slot = outs[w].at[2 * px + py]
                _remote(slot, slot, send_sems.at[k], recv_sems.at[k], (px, py, c)).wait_recv()
                send.wait_send()


def _pair_share(shards):
    nw = len(shards)

    def body(*refs):
        bufs = refs[nw:2 * nw]
        send_sems, recv_sems = refs[2 * nw:]
        x, y, c = _coords()
        sibling = (x, y, 1 - c)
        sends = []
        for w in range(nw):
            half = shards[w].shape[0] // 2
            mine = bufs[w].at[pl.ds(c * half, half)]
            cp = _remote(mine, mine, send_sems.at[w], recv_sems.at[w], sibling)
            cp.start()
            sends.append(cp)
        for w in range(nw):
            half = shards[w].shape[0] // 2
            theirs = bufs[w].at[pl.ds((1 - c) * half, half)]
            _remote(theirs, theirs, send_sems.at[w], recv_sems.at[w], sibling).wait_recv()
        for cp in sends:
            cp.wait_send()

    return pl.pallas_call(
        body, name="grad_pair_share",
        out_shape=[jax.ShapeDtypeStruct(s.shape, s.dtype) for s in shards],
        in_specs=[ANY] * nw, out_specs=[ANY] * nw,
        input_output_aliases={w: w for w in range(nw)},
        scratch_shapes=[pltpu.SemaphoreType.DMA((nw,))] * 2,
        compiler_params=_params(),
    )(*shards)


TILE_BYTES = 2 * 1024 * 1024


def _row_tile(rows, cols, mult=8):
    best = None
    for tr in range(mult, rows + 1, mult):
        if rows % tr == 0 and tr * cols * 4 <= TILE_BYTES:
            best = tr
    return best if best is not None else rows


def _pair_add(name, grad, other, place):
    _, half, cols = other.shape
    tr = _row_tile(half, cols, mult=16)
    nb = half // tr

    def body(place_ref, g_ref, o_ref, out_ref):
        out_ref[...] = (g_ref[...] + o_ref[...]).astype(out_ref.dtype)

    return pl.pallas_call(
        body, name=name, out_shape=jax.ShapeDtypeStruct(other.shape, BF16),
        grid_spec=pltpu.PrefetchScalarGridSpec(
            num_scalar_prefetch=1, grid=(N_CHIPS, nb),
            in_specs=[pl.BlockSpec((None, tr, cols), lambda s, i, place_ref: (s, place_ref[0] * nb + i, 0)),
                      pl.BlockSpec((None, tr, cols), lambda s, i, place_ref: (s, i, 0))],
            out_specs=pl.BlockSpec((None, tr, cols), lambda s, i, place_ref: (s, i, 0))),
        compiler_params=_params(("parallel", "parallel")),
    )(place, grad, other)


def _sum_chips(name, own, others, place):
    _, half, cols = own.shape
    tr = _row_tile(half, cols, mult=16)
    nb = half // tr

    def body(place_ref, own_ref, a_ref, b_ref, c_ref, out_ref):
        total = own_ref[...].astype(F32) + a_ref[...].astype(F32)
        out_ref[...] = (total + b_ref[...].astype(F32)) + c_ref[...].astype(F32)

    def peer(mask):
        return pl.BlockSpec((None, tr, cols), lambda i, place_ref: (place_ref[1] ^ mask, i, 0))

    return pl.pallas_call(
        body, name=name, out_shape=jax.ShapeDtypeStruct((2 * half, cols), F32),
        grid_spec=pltpu.PrefetchScalarGridSpec(
            num_scalar_prefetch=1, grid=(nb,),
            in_specs=[peer(0), peer(1), peer(2), peer(3)],
            out_specs=pl.BlockSpec((tr, cols), lambda i, place_ref: (place_ref[0] * nb + i, 0))),
        compiler_params=_params(("parallel",)),
    )(place, own, others, others, others)


def _adamw_math(w, g, m, v):
    m2 = ADAM_B1 * m + (1.0 - ADAM_B1) * g
    v2 = ADAM_B2 * v + (1.0 - ADAM_B2) * (g * g)
    m_hat = m2 / (1.0 - ADAM_B1 ** ADAM_STEP)
    v_hat = v2 / (1.0 - ADAM_B2 ** ADAM_STEP)
    delta = -ADAM_LR * (m_hat / (jnp.sqrt(v_hat) + ADAM_EPS) + ADAM_WD * w)
    return delta, m2, v2


def _adamw(name, w, g, m, v):
    rows, cols = w.shape
    tr = _row_tile(rows, cols)

    def body(w_ref, g_ref, m_ref, v_ref, d_ref, m2_ref, v2_ref):
        d_ref[...], m2_ref[...], v2_ref[...] = _adamw_math(w_ref[...], g_ref[...], m_ref[...], v_ref[...])

    blk = pl.BlockSpec((tr, cols), lambda i: (i, 0))
    return pl.pallas_call(
        body, name=name, out_shape=[jax.ShapeDtypeStruct(w.shape, F32)] * 3, grid=(rows // tr,),
        in_specs=[blk] * 4, out_specs=[blk] * 3,
        compiler_params=_params(("parallel",)),
    )(w, g, m, v)


def _small_update(gathered, w, m, v):
    def body(gs_ref, w_ref, m_ref, v_ref, g_ref, d_ref, m2_ref, v2_ref):
        g = gs_ref[0]
        for dev in range(1, N_DEV):
            g = g + gs_ref[dev]
        g_ref[...] = g
        d_ref[...], m2_ref[...], v2_ref[...] = _adamw_math(w_ref[...], g, m_ref[...], v_ref[...])

    return pl.pallas_call(
        body, name="small_update", out_shape=[jax.ShapeDtypeStruct(w.shape, F32)] * 4,
        compiler_params=_params(),
    )(gathered, w, m, v)


def _ada_mod(c_all, w_shard, b_shard):
    d, n = w_shard.shape
    tn = _pick(n, (512, 256, 128))

    def body(c_ref, w_ref, b_ref, o_ref):
        act = _silu(c_ref[...]).astype(BF16)
        o_ref[...] = jnp.dot(act, w_ref[...].astype(BF16), preferred_element_type=F32) + b_ref[...]

    return pl.pallas_call(
        body, name="ada_mod", out_shape=jax.ShapeDtypeStruct((c_all.shape[0], n), F32), grid=(n // tn,),
        in_specs=[pl.BlockSpec(c_all.shape, lambda j: (0, 0)), pl.BlockSpec((d, tn), lambda j: (0, j)),
                  pl.BlockSpec((1, tn), lambda j: (0, j))],
        out_specs=pl.BlockSpec((c_all.shape[0], tn), lambda j: (0, j)),
        compiler_params=_params(("parallel",)),
    )(c_all, w_shard, b_shard)


def _ada_grad(c_pad, dmod_pad):
    rows, d = c_pad.shape
    n = dmod_pad.shape[1]
    tn = _pick(n, (512, 256, 128))

    def body(c_ref, g_ref, o_ref):
        act = _silu(c_ref[...]).astype(BF16)
        o_ref[...] = lax.dot_general(act, g_ref[...].astype(BF16), TN_DIMS, preferred_element_type=F32)

    return pl.pallas_call(
        body, name="ada_grad", out_shape=jax.ShapeDtypeStruct((d, n), F32), grid=(n // tn,),
        in_specs=[pl.BlockSpec((rows, d), lambda j: (0, 0)), pl.BlockSpec((rows, tn), lambda j: (0, j))],
        out_specs=pl.BlockSpec((d, tn), lambda j: (0, j)),
        compiler_params=_params(("parallel",)),
    )(c_pad, dmod_pad)


WEIGHTS = ['w_ada', 'b_ada', 'norm1_gain', 'norm2_gain', 'w_in', 'mu_rkv', 'mu_w', 'mu_a', 'mu_g', 'w0', 'w1',
           'w2', 'a0', 'a1', 'a2', 'g1', 'g2', 'k_k', 'k_a', 'r_k', 'ln_x_gain', 'ln_x_bias', 'q_norm_gain',
           'k_norm_gain', 'w_out', 'w_gate_up', 'w_down']
SMALL = ['b_ada', 'norm1_gain', 'norm2_gain', 'mu_rkv', 'mu_w', 'mu_a', 'mu_g', 'w0', 'a0', 'k_k', 'k_a', 'r_k',
         'ln_x_gain', 'ln_x_bias', 'q_norm_gain', 'k_norm_gain']
PACK_ROWS = 8
LOSS_SLOT = LANES


def _shift_down(a):
    return jnp.pad(a[:-1], ((1, 0), (0, 0)))


def _shift_up(a):
    return jnp.pad(a[1:], ((0, 1), (0, 0)))


def _pack_small(vals):
    flat = jnp.concatenate([v.reshape(1, -1) for v in vals], axis=1)
    unit = PACK_ROWS * LANES
    total = -(-flat.shape[1] // unit) * unit
    flat = jnp.pad(flat, ((0, 0), (0, total - flat.shape[1])))
    return flat.reshape(PACK_ROWS, total // PACK_ROWS)


def kernel(x, c, w_ada, b_ada, norm1_gain, norm2_gain, w_in, mu_rkv, mu_w, mu_a, mu_g, w0, w1, w2, a0, a1, a2, g1, g2, k_k, k_a, r_k, ln_x_gain, ln_x_bias, q_norm_gain, k_norm_gain, w_out, w_gate_up, w_down, loss_target, m_w_ada, m_b_ada, m_norm1_gain, m_norm2_gain, m_w_in, m_mu_rkv, m_mu_w, m_mu_a, m_mu_g, m_w0, m_w1, m_w2, m_a0, m_a1, m_a2, m_g1, m_g2, m_k_k, m_k_a, m_r_k, m_ln_x_gain, m_ln_x_bias, m_q_norm_gain, m_k_norm_gain, m_w_out, m_w_gate_up, m_w_down, v_w_ada, v_b_ada, v_norm1_gain, v_norm2_gain, v_w_in, v_mu_rkv, v_mu_w, v_mu_a, v_mu_g, v_w0, v_w1, v_w2, v_a0, v_a1, v_a2, v_g1, v_g2, v_k_k, v_k_a, v_r_k, v_ln_x_gain, v_ln_x_bias, v_q_norm_gain, v_k_norm_gain, v_w_out, v_w_gate_up, v_w_down):
    given = dict(locals())
    wt = {n: given[n][0] for n in WEIGHTS}
    mom = {n: given["m_" + n][0] for n in WEIGHTS}
    var = {n: given["v_" + n][0] for n in WEIGHTS}
    for tree in (wt, mom, var):
        tree["b_ada"] = tree["b_ada"].reshape(1, -1)
        for n in SMALL[1:]:
            tree[n] = tree[n].reshape(1, -1)

    ax, ay, ac = _coords()
    chip = 2 * ax + ay
    dev = 4 * ax + 2 * ay + ac
    xs, target = x[0], loss_target[0]
    t, d = xs.shape
    dr = wt["w0"].shape[1]
    ds = d - dr
    nh = ds // HEAD_DIM
    dff = wt["w_down"].shape[0] * N_CHIPS
    n_ada = wt["w_ada"].shape[1]
    lw, la, lg = wt["w1"].shape[1], wt["a1"].shape[1], wt["g1"].shape[1]

    def lora_a(tree):
        return jnp.concatenate([tree["w1"], tree["a1"], tree["g1"]], axis=1)

    def lora_b(tree):
        return jnp.concatenate([tree["w2"], tree["a2"], tree["g2"]], axis=0)

    def with_own_slot(gathered, own):
        return [lax.dynamic_update_slice(full, shard[None], (chip, 0, 0)) for full, shard in zip(gathered, own)]

    first_shards = [s.astype(BF16) for s in (wt["w_in"], lora_a(wt), lora_b(wt))]
    later_shards = [s.astype(BF16) for s in (wt["w_out"], wt["w_gate_up"], wt["w_down"])]
    full_in, full_la, full_lb = with_own_slot(_gather_weights(first_shards), first_shards)
    full_la = full_la.reshape(d, lw + la + lg).astype(F32)
    full_lb = full_lb.transpose(1, 0, 2).reshape(lw + la + lg, dr).astype(F32)
    w1f, a1f, g1f = full_la[:, :lw], full_la[:, lw:lw + la], full_la[:, lw + la:]
    w2f, a2f, g2f = full_lb[:lw], full_lb[lw:lw + la], full_lb[lw + la:]

    c_all = _all_gather8("gather_c", c.reshape(PACK_ROWS, d // PACK_ROWS)).reshape(N_DEV, d)
    b_shard = lax.dynamic_slice(wt["b_ada"], (0, chip * n_ada), (1, n_ada))
    mod_part = _ada_mod(c_all, wt["w_ada"], b_shard)
    mod_all = _all_gather8("gather_mod", mod_part)[::2]
    mod = lax.dynamic_slice(mod_all, (0, dev, 0), (N_CHIPS, 1, n_ada)).reshape(1, N_CHIPS * n_ada)
    sh1, sc1, gt1, sh2, sc2, gt2 = [mod[:, i * d:(i + 1) * d] for i in range(6)]

    h, h_bf = _rowwise("norm1", _fn_norm1, [xs], [wt["norm1_gain"], sc1, sh1], [(d, F32), (d, BF16)], 256)
    hp = _shift_down(h)
    p = _matmul("mm_in", h_bf, full_in, b_shards=True, tm=512, tn=1536, tk=2048, n_outer=True)
    p_rkv, p_sb = (p, 3 * dr, 0), (p, 3 * ds, 1)
    pp = _shift_down(p[:, :3 * dr])
    pre_rows = [h, hp, p_rkv, pp]
    pre_params = [wt["mu_rkv"], wt["mu_w"], wt["mu_a"], wt["mu_g"], wt["w0"], wt["a0"], wt["k_k"], wt["k_a"],
                  w1f, w2f, a1f, a2f, g1f, g2f]
    pre = _rowwise("rwkv_pre", _fn_rwkv_pre, pre_rows, pre_params, [(dr, F32)] * 7, 128)
    r_, w_, k2, v_, rem, wr, g_ = pre
    y_raw, hist, s_last = _scan_fwd(rem, w_, wr, k2, r_, v_)
    post_rows = [y_raw, r_, k2, v_, g_]
    post_params = [wt["ln_x_gain"], wt["ln_x_bias"], wt["r_k"]]

    qg = jnp.tile(wt["q_norm_gain"], (1, nh))
    kg = jnp.tile(wt["k_norm_gain"], (1, nh))
    qn, kn, vs = _rowwise("qk_norm", _fn_qk_norm, [p_sb], [qg, kg], [(ds, BF16)] * 3, 256)

    def to_heads(a):
        return a.reshape(t, nh, HEAD_DIM).transpose(1, 0, 2)

    def from_heads(a):
        return a.transpose(1, 0, 2).reshape(t, ds)

    qh, kh, vh = to_heads(qn), to_heads(kn), to_heads(vs)
    o_h, lsum, later_full = _sb_fwd(qh, kh, vh, later_shards)
    full_out, full_gu, full_down = with_own_slot(later_full, later_shards)
    full_out = full_out.reshape(d, d)
    full_down = full_down.reshape(dff, d)
    (ycat,) = _rowwise("rwkv_post", _fn_rwkv_post_cat, post_rows + [from_heads(o_h)], post_params, [(d, BF16)], 256)
    mix = _matmul("mm_out", ycat, full_out, tm=512, tn=1024, tk=2048, n_outer=True)
    norm2_params = [gt1, wt["norm2_gain"], sc2, sh2]
    x1, h2 = _rowwise("mix_norm2", _fn_mix_norm2, [xs, mix], norm2_params, [(d, F32), (d, BF16)], 256)
    gu = _matmul("mm_gate_up", h2, full_gu, b_shards=True, tm=512, tn=1408, tk=2048, n_outer=True)
    gate_up = [(gu, dff, 0), (gu, dff, 1)]
    (act,) = _rowwise("swiglu", _fn_swiglu, gate_up, [], [(dff, BF16)], 256)
    dn = _matmul("mm_down", act, full_down, tm=1024, tn=1024, tk=2816)
    loss_vec, dout, ddn, dgt2 = _loss_head("loss_head", x1, dn, target, gt2)

    dact = _matmul("mm_down_dx", ddn, full_down, tb=True, tm=512, tn=1408, tk=2048, n_outer=True)
    gw_down = _matmul("mm_down_dw", act, ddn, ta=True, tm=1408, tn=1024, tk=1024)
    dgu = _swiglu_bwd(gu, dact)
    dh2 = _matmul("mm_gate_up_dx", dgu, full_gu, tb=True, b_shards=True, tm=1024, tn=1024, tk=2816)
    gw_gu = _matmul("mm_gate_up_dw", h2, dgu, ta=True, out_shards=True, tm=1024, tn=1408, tk=1024)
    (dx_a, dmix), (dgt1, dgain2, dsc2, dsh2) = _rowwise_bwd(
        "mix_norm2_bwd", _fn_mix_norm2, [xs, mix], norm2_params, [[dout], [dh2]], [F32, BF16], [True] * 4, 128)
    dycat = _matmul("mm_out_dx", dmix, full_out, tb=True, tm=512, tn=1024, tk=2048, n_outer=True)
    gw_out = _matmul("mm_out_dw", ycat, dmix, ta=True, tm=1024, tn=1024, tk=1024)
    (dy_raw, dr_f, dk_f, dv_f, dg), (dlng, dlnb, drk) = _rowwise_bwd(
        "rwkv_post_bwd", _fn_rwkv_post, post_rows, post_params, [[(dycat, dr, 0)]], [F32] * 5, [True] * 3, 128)
    place = jnp.stack([ac, chip]).astype(jnp.int32)

    def pair_reduce(tag, names, grads):
        from_sibling = _pair_exchange("grad_pair_exchange_" + tag, grads)
        return [_pair_add("pair_add_" + n, g, o, place) for n, g, o in zip(names, grads, from_sibling)]

    early_names = ["w_out", "w_gate_up", "w_down"]
    early_sums = pair_reduce("early", early_names, [gw_out.reshape(N_CHIPS, d // N_CHIPS, d), gw_gu,
                                                    gw_down.reshape(N_CHIPS, dff // N_CHIPS, d)])
    do_h = to_heads(dycat[:, dr:])
    dqh, dkh, dvh, early_from_chips = _sb_bwd(qh, kh, vh, lsum, do_h, early_sums)
    drem_s, dw_s, dwr_s, dk_s, dr_s, dv_s = _scan_bwd(rem, w_, wr, k2, r_, v_, hist, s_last, dy_raw)
    (dp_sb,), (dqg, dkg) = _rowwise_bwd(
        "qk_norm_bwd", _fn_qk_norm, [p_sb], [qg, kg], [[from_heads(dqh)], [from_heads(dkh)], [from_heads(dvh)]],
        [F32], [True, True], 128)
    pre_cts = [[dr_s, dr_f], [dw_s], [dk_s, dk_f], [dv_s, dv_f], [drem_s], [dwr_s], [dg]]
    (dh_a, dhp, dp_rkv, dpp), pre_g = _rowwise_bwd(
        "rwkv_pre_bwd", _fn_rwkv_pre, pre_rows, pre_params, pre_cts, [F32] * 4, [True] * 14, 128)
    dp = jnp.concatenate([dp_rkv + _shift_up(dpp), dp_sb], axis=1).astype(BF16)
    gw_in = _matmul("mm_in_dw", h_bf, dp, ta=True, out_shards=True, tm=1024, tn=1536, tk=1024)

    g_mu_rkv, g_mu_w, g_mu_a, g_mu_g, g_w0, g_a0, g_kk, g_ka, gw1, gw2, ga1, ga2, gg1, gg2 = pre_g
    g_la = jnp.concatenate([gw1, ga1, gg1], axis=1).reshape(N_CHIPS, d // N_CHIPS, lw + la + lg)
    g_lb = jnp.concatenate([gw2, ga2, gg2], axis=0)
    g_lb = g_lb.reshape(lw + la + lg, N_CHIPS, dr // N_CHIPS).transpose(1, 0, 2)
    late_names = ["w_in", "lora_a", "lora_b"]
    late_sums = pair_reduce("late", late_names, [gw_in, g_la, g_lb])
    dh_mm, late_from_chips = _matmul("mm_in_dx", dp, full_in, tb=True, b_shards=True, tm=1024, tn=1024, tk=1536,
                                     exchange=late_sums)
    (grad_x,), (dgain1, dsc1, dsh1) = _rowwise_bwd(
        "norm1_bwd", _fn_norm1, [xs], [wt["norm1_gain"], sc1, sh1], [[dh_a, dh_mm, _shift_up(dhp)], []],
        [F32], [True] * 3, 128, add_to_first=[dx_a])

    halves = [_sum_chips("chip_sum_" + n, p, q, place)
              for n, p, q in zip(early_names + late_names, early_sums + late_sums,
                                 list(early_from_chips) + list(late_from_chips))]
    r_out, r_gu, r_down, r_in, r_la, r_lb = _pair_share(halves)

    dmod = jnp.concatenate([dsh1, dsc1, dgt1, dsh2, dsc2, dgt2], axis=1)
    dqg = dqg.reshape(nh, HEAD_DIM).sum(axis=0, keepdims=True)
    dkg = dkg.reshape(nh, HEAD_DIM).sum(axis=0, keepdims=True)
    small_g = [dmod, dgain1, dgain2, g_mu_rkv, g_mu_w, g_mu_a, g_mu_g, g_w0, g_a0, g_kk, g_ka, drk, dlng, dlnb,
               dqg, dkg]
    lead = jnp.zeros((1, LOSS_SLOT), F32)
    packed = _pack_small([loss_vec] + small_g)
    gathered = _all_gather8("gather_small", packed)
    sm_g, sm_d, sm_m, sm_v = _small_update(gathered, _pack_small([lead] + [wt[n] for n in SMALL]),
                                           _pack_small([lead] + [mom[n] for n in SMALL]),
                                           _pack_small([lead] + [var[n] for n in SMALL]))
    loss = sm_g.reshape(-1)[0]

    def unpack(packed_arr):
        flat, out, pos = packed_arr.reshape(-1), {}, LOSS_SLOT
        for n in SMALL:
            size = wt[n].size
            out[n] = flat[pos:pos + size]
            pos += size
        return out

    res = {"grad": unpack(sm_g), "delta": unpack(sm_d), "m": unpack(sm_m), "v": unpack(sm_v)}

    dmod_all = gathered.reshape(N_DEV, -1)[:, LOSS_SLOT:LOSS_SLOT + N_CHIPS * n_ada]
    dmod_cols = lax.dynamic_slice(dmod_all, (0, chip * n_ada), (N_DEV, n_ada))
    pad8 = ((0, N_DEV), (0, 0))
    res["grad"]["w_ada"] = _ada_grad(jnp.pad(c_all, pad8), jnp.pad(dmod_cols, pad8))

    res["grad"].update(w_in=r_in, w_out=r_out, w_gate_up=r_gu, w_down=r_down)
    for n in ("w_ada", "w_in", "w_out", "w_gate_up", "w_down"):
        res["delta"][n], res["m"][n], res["v"][n] = _adamw("adamw_" + n, wt[n], res["grad"][n], mom[n], var[n])
    la_d, la_m, la_v = _adamw("adamw_lora_a", lora_a(wt), r_la, lora_a(mom), lora_a(var))
    lb_d, lb_m, lb_v = _adamw("adamw_lora_b", lora_b(wt), r_lb, lora_b(mom), lora_b(var))
    for key, pa, pb in (("grad", r_la, r_lb), ("delta", la_d, lb_d), ("m", la_m, lb_m), ("v", la_v, lb_v)):
        res[key].update(w1=pa[:, :lw], a1=pa[:, lw:lw + la], g1=pa[:, lw + la:],
                        w2=pb[:lw], a2=pb[lw:lw + la], g2=pb[lw + la:])

    outs = [loss, grad_x[None]]
    for key in ("grad", "delta", "m", "v"):
        outs += [res[key][n].reshape(given[n].shape) for n in WEIGHTS]
    return tuple(outs)
```

```python
import functools
import math

import jax
import jax.numpy as jnp
from jax import lax
from jax.experimental import pallas as pl
from jax.experimental.pallas import tpu as pltpu

F32 = jnp.float32
BF16 = jnp.bfloat16
HEAD_DIM = 64
LANES = 128
RMS_EPS = 1e-6
GN_EPS = 64e-5
L2_EPS = 1e-12
ADAM_LR, ADAM_B1, ADAM_B2, ADAM_EPS, ADAM_WD, ADAM_STEP = 0.001, 0.9, 0.999, 1e-08, 0.01, 10
VMEM_LIMIT = 56 * 1024 * 1024
MESH = pl.DeviceIdType.MESH
HI = lax.Precision.HIGHEST
N_CHIPS = 4
N_DEV = 8
XY_MASKS = ((1, 0), (0, 1), (1, 1))


def _pick(dim, prefs):
    for p in prefs:
        if dim % p == 0:
            return p
    return dim


def _params(sem=None, vmem=VMEM_LIMIT):
    return pltpu.CompilerParams(dimension_semantics=sem, vmem_limit_bytes=vmem)


def _sigmoid(x):
    return 1.0 / (1.0 + jnp.exp(-x))


@jax.custom_vjp
def _softplus(x):
    return jnp.maximum(x, 0.0) + jnp.log(1.0 + jnp.exp(-jnp.abs(x)))


_softplus.defvjp(lambda x: (_softplus(x), x), lambda x, g: (g * _sigmoid(x),))


def _silu(x):
    return x * _sigmoid(x)


@jax.custom_vjp
def _bdot(a, b):
    return jnp.dot(a.astype(BF16), b.astype(BF16), preferred_element_type=F32)


def _bdot_bwd(res, g):
    a, b = res
    gb = g.astype(BF16)
    da = lax.dot_general(gb, b.astype(BF16), (((1,), (1,)), ((), ())), preferred_element_type=F32)
    db = lax.dot_general(a.astype(BF16), gb, (((0,), (0,)), ((), ())), preferred_element_type=F32)
    return da.astype(a.dtype), db.astype(b.dtype)


_bdot.defvjp(lambda a, b: (_bdot(a, b), (a, b)), _bdot_bwd)


def _head_ones():
    i = lax.broadcasted_iota(jnp.int32, (LANES, LANES), 0) // HEAD_DIM
    j = lax.broadcasted_iota(jnp.int32, (LANES, LANES), 1) // HEAD_DIM
    return (i == j).astype(F32)


def _hdot(x, ones_bf):
    hi = x.astype(BF16)
    rest = x - hi.astype(F32)
    mid = rest.astype(BF16)
    lo = (rest - mid.astype(F32)).astype(BF16)
    out = jnp.dot(hi, ones_bf, preferred_element_type=F32)
    out += jnp.dot(mid, ones_bf, preferred_element_type=F32)
    return out + jnp.dot(lo, ones_bf, preferred_element_type=F32)


@jax.custom_vjp
def _segsum(x):
    ones = _head_ones().astype(BF16)
    parts = [_hdot(x[:, LANES * j:LANES * (j + 1)], ones) for j in range(x.shape[1] // LANES)]
    return parts[0] if len(parts) == 1 else jnp.concatenate(parts, axis=1)


_segsum.defvjp(lambda x: (_segsum(x), None), lambda _, g: (_segsum(g),))


def _rms(x, gain):
    return x * lax.rsqrt(jnp.mean(x * x, axis=-1, keepdims=True) + RMS_EPS) * gain


def _matmul(name, a, b, *, ta=False, tb=False, b_shards=False, out_shards=False, out_dtype=F32,
            tm=512, tn=512, tk=512, n_outer=False, exchange=(), pair=()):
    if ta:
        kdim, m = a.shape
    else:
        m, kdim = a.shape
    if b_shards:
        if tb:
            n, ks = b.shape[1], b.shape[2]
            assert ks * N_CHIPS == kdim
        else:
            ns = b.shape[2]
            n = ns * N_CHIPS
            assert b.shape[1] == kdim
    else:
        n = b.shape[0] if tb else b.shape[1]
    tm = _pick(m, (tm, 512, 256, 128))
    n_part = n // N_CHIPS if (out_shards or (b_shards and not tb)) else n
    tn = _pick(n_part, (tn, 512, 256, 128))
    k_part = kdim // N_CHIPS if (b_shards and tb) else kdim
    tk = _pick(k_part, (tk, 512, 256, 128))
    nb = n_part // tn
    kb = k_part // tk
    nk = kdim // tk
    grid = (n // tn, m // tm, nk) if n_outer else (m // tm, n // tn, nk)

    def spec(shape, index):
        if n_outer:
            return pl.BlockSpec(shape, lambda j, i, k: index(i, j, k))
        return pl.BlockSpec(shape, index)

    if ta:
        a_spec = spec((tk, tm), lambda i, j, k: (k, i))
    else:
        a_spec = spec((tm, tk), lambda i, j, k: (i, k))
    if b_shards and tb:
        b_spec = spec((None, tn, tk), lambda i, j, k: (k // kb, j, k % kb))
    elif b_shards:
        b_spec = spec((None, tk, tn), lambda i, j, k: (j // nb, k, j % nb))
    elif tb:
        b_spec = spec((tn, tk), lambda i, j, k: (j, k))
    else:
        b_spec = spec((tk, tn), lambda i, j, k: (k, j))
    if out_shards:
        o_spec = spec((None, tm, tn), lambda i, j, k: (j // nb, i, j % nb))
        o_shape = jax.ShapeDtypeStruct((N_CHIPS, m, n_part), out_dtype)
    else:
        o_spec = spec((tm, tn), lambda i, j, k: (i, j))
        o_shape = jax.ShapeDtypeStruct((m, n), out_dtype)
    dims = (((0 if ta else 1,), (1 if tb else 0,)), ((), ()))

    def product(a_ref, b_ref):
        return lax.dot_general(a_ref[...].astype(BF16), b_ref[...].astype(BF16), dims, preferred_element_type=F32)

    def body_one_step(a_ref, b_ref, o_ref):
        o_ref[...] = product(a_ref, b_ref).astype(o_ref.dtype)

    def body(a_ref, b_ref, o_ref, acc_ref):
        k = pl.program_id(2)

        @pl.when(k == 0)
        def _():
            acc_ref[...] = product(a_ref, b_ref)

        @pl.when(jnp.logical_and(k > 0, k < nk - 1))
        def _():
            acc_ref[...] += product(a_ref, b_ref)

        @pl.when(k == nk - 1)
        def _():
            o_ref[...] = (acc_ref[...] + product(a_ref, b_ref)).astype(o_ref.dtype)

    acc_scratch = [] if nk == 1 else [pltpu.VMEM((tm, tn), F32)]
    if pair:
        heights = [g.shape[1] for g in pair]
        carried_out = [jax.ShapeDtypeStruct((N_CHIPS, g.shape[1] // 2, g.shape[2]), g.dtype) for g in pair]
        carried_sems = [pltpu.SemaphoreType.DMA((len(pair),))] * 2

        def _exchange_phase(phase, ins, outs, sems):
            _pair_phase(phase, ins, outs, sems, heights)

        exchange = pair
    elif exchange:
        carried_out = [jax.ShapeDtypeStruct(p.shape, p.dtype) for p in exchange]
        carried_sems = _exchange_sems(len(exchange))
        _exchange_phase = globals()["_exchange_phase"]
    if not exchange:
        return pl.pallas_call(
            body_one_step if nk == 1 else body, name=name, grid=grid, in_specs=[a_spec, b_spec], out_specs=o_spec,
            out_shape=o_shape, scratch_shapes=acc_scratch,
            compiler_params=_params(("parallel", "parallel", "arbitrary")),
        )(a, b)

    nx = len(exchange)

    def body_with_exchange(a_ref, b_ref, *rest):
        p_in, o_ref, p_out = rest[:nx], rest[nx], rest[nx + 1:2 * nx + 1]
        scratch = rest[2 * nx + 1:]
        acc, sems = scratch[:-2], scratch[-2:]
        at = [pl.program_id(axis) for axis in range(3)]

        @pl.when(jnp.logical_and(jnp.logical_and(at[0] == 0, at[1] == 0), at[2] == 0))
        def _():
            _exchange_phase("issue", p_in, p_out, sems)

        (body_one_step if nk == 1 else body)(a_ref, b_ref, o_ref, *acc)

        @pl.when(jnp.logical_and(jnp.logical_and(at[0] == grid[0] - 1, at[1] == grid[1] - 1), at[2] == grid[2] - 1))
        def _():
            _exchange_phase("finish", p_in, p_out, sems)

    res = pl.pallas_call(
        body_with_exchange, name=name, grid=grid, in_specs=[a_spec, b_spec] + [ANY] * nx,
        out_specs=[o_spec] + [ANY] * nx,
        out_shape=[o_shape] + carried_out,
        scratch_shapes=acc_scratch + carried_sems,
        compiler_params=_params(("arbitrary", "arbitrary", "arbitrary")),
    )(a, b, *exchange)
    return res[0], res[1:]


def _row_in(spec, tile):
    if isinstance(spec, tuple):
        arr, width, cb = spec
    else:
        arr, width, cb = spec, spec.shape[1], 0
    return arr, pl.BlockSpec((tile, width), lambda i, cb=cb: (i, cb))


def _full_spec(arr):
    nd = arr.ndim
    return pl.BlockSpec(arr.shape, lambda i, nd=nd: (0,) * nd, pipeline_mode=pl.Buffered(1))


def _rowwise(name, fn, rows, params, outs, tile):
    t = (rows[0][0] if isinstance(rows[0], tuple) else rows[0]).shape[0]
    tile = _pick(t, (tile,))
    arrs, specs = zip(*[_row_in(s, tile) for s in rows])
    nr, npar = len(rows), len(params)

    def body(*refs):
        rv = [r[...].astype(F32) for r in refs[:nr]]
        pv = [p[...] for p in refs[nr:nr + npar]]
        res = fn(*rv, *pv)
        for o_ref, val in zip(refs[nr + npar:], res):
            o_ref[...] = val.astype(o_ref.dtype)

    return pl.pallas_call(
        body, name=name, grid=(t // tile,),
        in_specs=list(specs) + [_full_spec(p) for p in params],
        out_specs=[pl.BlockSpec((tile, w), lambda i: (i, 0)) for w, _ in outs],
        out_shape=[jax.ShapeDtypeStruct((t, w), d) for w, d in outs],
        compiler_params=_params(("parallel",)),
    )(*arrs, *params)


def _rowwise_bwd(name, fn, rows, params, cts, row_grads, param_grads, tile, add_to_first=()):
    t = (rows[0][0] if isinstance(rows[0], tuple) else rows[0]).shape[0]
    tile = _pick(t, (tile,))
    arrs, specs = zip(*[_row_in(s, tile) for s in rows])
    n_add = len(add_to_first)
    flat_cts = [c for group in cts for c in group] + list(add_to_first)
    c_arrs, c_specs = zip(*[_row_in(s, tile) for s in flat_cts])
    nr, npar, nc = len(rows), len(params), len(flat_cts)
    rg_idx = [i for i, d in enumerate(row_grads) if d is not None]
    pg_idx = [i for i, d in enumerate(param_grads) if d]

    def body(*refs):
        rv = [r[...].astype(F32) for r in refs[:nr]]
        pv = [p[...] for p in refs[nr:nr + npar]]
        cv = [c[...].astype(F32) for c in refs[nr + npar:nr + npar + nc]]
        o_refs = refs[nr + npar + nc:]
        outs, vjp = jax.vjp(fn, *rv, *pv)
        ct, pos = [], 0
        for group, o in zip(cts, outs):
            if group:
                acc = cv[pos]
                for extra in cv[pos + 1:pos + len(group)]:
                    acc = acc + extra
                pos += len(group)
            else:
                acc = jnp.zeros_like(o)
            ct.append(acc)
        grads = list(vjp(tuple(ct)))
        for extra in cv[nc - n_add:]:
            grads[rg_idx[0]] = grads[rg_idx[0]] + extra
        for o_ref, i in zip(o_refs[:len(rg_idx)], rg_idx):
            o_ref[...] = grads[i].astype(o_ref.dtype)
        first = pl.program_id(0) == 0
        for o_ref, i in zip(o_refs[len(rg_idx):], pg_idx):
            g = grads[nr + i].astype(F32)

            @pl.when(first)
            def _(o_ref=o_ref, g=g):
                o_ref[...] = g

            @pl.when(jnp.logical_not(first))
            def _(o_ref=o_ref, g=g):
                o_ref[...] += g

    def width(i):
        s = rows[i]
        return s[1] if isinstance(s, tuple) else s.shape[1]

    out_specs = [pl.BlockSpec((tile, width(i)), lambda i_: (i_, 0)) for i in rg_idx]
    out_shape = [jax.ShapeDtypeStruct((t, width(i)), row_grads[i]) for i in rg_idx]
    out_specs += [_full_spec(params[i]) for i in pg_idx]
    out_shape += [jax.ShapeDtypeStruct(params[i].shape, F32) for i in pg_idx]
    res = pl.pallas_call(
        body, name=name, grid=(t // tile,),
        in_specs=list(specs) + [_full_spec(p) for p in params] + list(c_specs),
        out_specs=out_specs, out_shape=out_shape,
        compiler_params=_params(("arbitrary",)),
    )(*arrs, *params, *c_arrs)
    return res[:len(rg_idx)], res[len(rg_idx):]


def _fn_norm1(x, gain, sc, sh):
    h = _rms(x, gain) * (1.0 + sc) + sh
    return h, h


def _fn_rwkv_pre(h, hp, p, pp, mu_rkv, mu_w, mu_a, mu_g, w0, a0, k_k, k_a, w1, w2, a1, a2, g1, g2):
    d = p.shape[1] // 3
    dh = hp - h
    xw = h + dh * mu_w
    xa = h + dh * mu_a
    xg = h + dh * mu_g
    pr = p + (pp - p) * mu_rkv
    r, k, v = pr[:, :d], pr[:, d:2 * d], pr[:, 2 * d:]
    w_log = -_softplus(-(w0 + _bdot(jnp.tanh(_bdot(xw, w1)), w2))) - 0.5
    decay = jnp.exp(-jnp.exp(w_log))
    a = _sigmoid(a0 + _bdot(_bdot(xa, a1), a2))
    g = _bdot(_sigmoid(_bdot(xg, g1)), g2)
    kk = k * k_k
    kk = kk * lax.rsqrt(_segsum(kk * kk) + L2_EPS)
    k2 = k * (1.0 + (a - 1.0) * k_a)
    return r, decay, k2, v, -kk, kk * a, g


def _fn_rwkv_post(y, r, k2, v, g, ln_g, ln_b, r_k):
    inv = 1.0 / HEAD_DIM
    mean = _segsum(y) * inv
    yc = y - mean
    var = _segsum(yc * yc) * inv
    yn = yc * lax.rsqrt(var + GN_EPS) * ln_g + ln_b
    bonus = _segsum(r * k2 * r_k) * v
    return ((yn + bonus) * g,)


def _fn_rwkv_post_cat(y, r, k2, v, g, o_sb, ln_g, ln_b, r_k):
    return (jnp.concatenate([_fn_rwkv_post(y, r, k2, v, g, ln_g, ln_b, r_k)[0], o_sb], axis=1),)


def _fn_qk_norm(p, qg, kg):
    d = p.shape[1] // 3
    q, k, v = p[:, :d], p[:, d:2 * d], p[:, 2 * d:]
    inv = 1.0 / HEAD_DIM
    qn = q * lax.rsqrt(_segsum(q * q) * inv + RMS_EPS) * qg
    kn = k * lax.rsqrt(_segsum(k * k) * inv + RMS_EPS) * kg
    return qn * SB_SCALE, kn, v


def _fn_mix_norm2(x, mix, gt1, gain, sc, sh):
    x1 = x + gt1 * mix
    h2 = _rms(x1, gain) * (1.0 + sc) + sh
    return x1, h2


def _fn_swiglu(gate, up):
    return (_silu(gate) * up,)


def _swiglu_bwd(gu, dact, tile=128):
    t, dff = dact.shape
    tile = _pick(t, (tile,))

    def body(gate_ref, up_ref, d_ref, o_ref):
        _, vjp = jax.vjp(_fn_swiglu, gate_ref[...], up_ref[...])
        dgate, dup = vjp((d_ref[...],))
        o_ref[:, :dff] = dgate.astype(o_ref.dtype)
        o_ref[:, dff:] = dup.astype(o_ref.dtype)

    return pl.pallas_call(
        body, name="swiglu_bwd", grid=(t // tile,),
        in_specs=[pl.BlockSpec((tile, dff), lambda i: (i, 0)), pl.BlockSpec((tile, dff), lambda i: (i, 1)),
                  pl.BlockSpec((tile, dff), lambda i: (i, 0))],
        out_specs=pl.BlockSpec((tile, 2 * dff), lambda i: (i, 0)),
        out_shape=jax.ShapeDtypeStruct((t, 2 * dff), BF16),
        compiler_params=_params(("parallel",)),
    )(gu, gu, dact)


def _loss_head(name, x1, dn, target, gt2, tile=256):
    t, d = x1.shape
    tile = _pick(t, (tile,))

    def body(x1_ref, dn_ref, tg_ref, gt_ref, loss_ref, dout_ref, ddn_ref, dgt_ref):
        dnv = dn_ref[...]
        gt = gt_ref[...]
        err = x1_ref[...] + gt * dnv - tg_ref[...]
        dout = err * (1.0 / d)
        dout_ref[...] = dout
        ddn_ref[...] = (dout * gt).astype(ddn_ref.dtype)
        part = 0.5 * jnp.sum(jnp.sum(err * dout, axis=-1, keepdims=True), axis=0, keepdims=True)
        dgt = jnp.sum(dout * dnv, axis=0, keepdims=True)
        first = pl.program_id(0) == 0

        @pl.when(first)
        def _():
            loss_ref[...] = jnp.broadcast_to(part, loss_ref.shape)
            dgt_ref[...] = dgt

        @pl.when(jnp.logical_not(first))
        def _():
            loss_ref[...] += jnp.broadcast_to(part, loss_ref.shape)
            dgt_ref[...] += dgt

    row = pl.BlockSpec((tile, d), lambda i: (i, 0))
    vec = pl.BlockSpec((1, d), lambda i: (0, 0))
    return pl.pallas_call(
        body, name=name, grid=(t // tile,),
        in_specs=[row, row, row, vec],
        out_specs=[pl.BlockSpec((1, LANES), lambda i: (0, 0)), row, row, vec],
        out_shape=[jax.ShapeDtypeStruct((1, LANES), F32), jax.ShapeDtypeStruct((t, d), F32),
                   jax.ShapeDtypeStruct((t, d), BF16), jax.ShapeDtypeStruct((1, d), F32)],
        compiler_params=_params(("arbitrary",)),
    )(x1, dn, target, gt2)


SCAN_BLOCK = 32
N_COL = 5
WIDE = 2 * LANES


def _wide_eye():
    i = lax.broadcasted_iota(jnp.int32, (HEAD_DIM, WIDE), 0)
    j = lax.broadcasted_iota(jnp.int32, (HEAD_DIM, WIDE), 1) % HEAD_DIM
    return (i == j).astype(BF16)


def _wide_ones():
    i = lax.broadcasted_iota(jnp.int32, (WIDE, WIDE), 0) // HEAD_DIM
    j = lax.broadcasted_iota(jnp.int32, (WIDE, WIDE), 1) // HEAD_DIM
    return (i == j).astype(BF16)


COL_PIECES = (1, 2, 2, 2, 1)


def _col_tiles(refs, i, nq, eye, ones_bf):
    levels = max(COL_PIECES)
    pieces = [[] for _ in range(levels)]
    for ref, n_pieces in zip(refs, COL_PIECES):
        full = ref[pl.ds(i, 1), :]
        for q in range(nq):
            rest = full[:, q * WIDE:(q + 1) * WIDE]
            for level in range(n_pieces):
                part = rest.astype(BF16)
                pieces[level].append(part * eye)
                rest = rest - part.astype(F32)
    rows = nq * HEAD_DIM
    out = jnp.dot(jnp.concatenate(sum(pieces, []), axis=0), ones_bf, preferred_element_type=F32)
    start, place = 0, {}
    for level in range(levels):
        for n, n_pieces in enumerate(COL_PIECES):
            if n_pieces > level:
                place[level, n] = start
                start += rows
    tiles = []
    for n, n_pieces in enumerate(COL_PIECES):
        acc = out[place[0, n]:place[0, n] + rows]
        for level in range(1, n_pieces):
            acc = acc + out[place[level, n]:place[level, n] + rows]
        tiles.append(acc)
    return jnp.concatenate(tiles, axis=0).reshape(N_COL * nq, HEAD_DIM, WIDE)


def _bf16_round(x):
    bits = lax.bitcast_convert_type(x, jnp.uint32)
    bits = (bits + jnp.uint32(0x7FFF) + ((bits >> 16) & jnp.uint32(1))) & jnp.uint32(0xFFFF0000)
    return lax.bitcast_convert_type(bits, F32)


def _pair_tile(tiles_ref, n, p, nq):
    return tiles_ref[n * nq + p // 2, :, (p % 2) * LANES:(p % 2 + 1) * LANES]


def _scan_fwd(rem, w, wr, k, r, v):
    t, dr = v.shape
    npair, nq = dr // LANES, dr // WIDE
    tb = _pick(t, (SCAN_BLOCK,))

    def body(rem_ref, w_ref, wr_ref, k_ref, r_ref, v_ref, *rest):
        next_refs, (y_ref, hist_ref, last_ref, s_ref, sb_ref, tiles_a, tiles_b) = rest[:N_COL], rest[N_COL:]
        eye, ones_bf = _wide_eye(), _wide_ones()
        col_refs = (rem_ref, w_ref, wr_ref, k_ref, r_ref)

        @pl.when(pl.program_id(0) == 0)
        def _():
            s_ref[...] = jnp.zeros_like(s_ref)
            sb_ref[...] = jnp.zeros_like(sb_ref)
            tiles_a[...] = _col_tiles(col_refs, 0, nq, eye, ones_bf)

        def step(i, tiles_ref):
            v_full, y_rows = v_ref[pl.ds(i, 1), :], []
            for p in range(npair):
                s = s_ref[p]
                hist_ref[i, p] = s
                c_rem, c_w, c_wr, c_k, c_r = [_pair_tile(tiles_ref, n, p, nq) for n in range(N_COL)]
                sa = jnp.sum(sb_ref[p] * c_rem, axis=0, keepdims=True)
                s2 = s * c_w + c_wr * sa + c_k * v_full[:, p * LANES:(p + 1) * LANES]
                s2_b = _bf16_round(s2)
                y_rows.append(jnp.sum(s2_b * c_r, axis=0, keepdims=True))
                s_ref[p] = s2
                sb_ref[p] = s2_b
            y_ref[pl.ds(i, 1), :] = jnp.concatenate(y_rows, axis=1)

        def two_steps(i, ahead_refs, ahead_row):
            tiles_b[...] = _col_tiles(col_refs, i + 1, nq, eye, ones_bf)
            step(i, tiles_a)
            tiles_a[...] = _col_tiles(ahead_refs, ahead_row, nq, eye, ones_bf)
            step(i + 1, tiles_b)

        def loop_body(m, carry):
            two_steps(2 * m, col_refs, 2 * m + 2)
            return carry

        lax.fori_loop(0, tb // 2 - 1, loop_body, 0, unroll=3)
        two_steps(tb - 2, next_refs, 0)
        last_ref[...] = sb_ref[...]

    nblk = t // tb
    blk = pl.BlockSpec((tb, dr), lambda i: (i, 0))
    nxt = pl.BlockSpec((8, dr), lambda i: (jnp.minimum(i + 1, nblk - 1) * (tb // 8), 0))
    tiles = pltpu.VMEM((N_COL * nq, HEAD_DIM, WIDE), F32)
    state = pltpu.VMEM((npair, HEAD_DIM, LANES), F32)
    return pl.pallas_call(
        body, name="rwkv_scan_fwd", grid=(nblk,),
        in_specs=[blk] * 6 + [nxt] * N_COL,
        out_specs=[blk, pl.BlockSpec((tb, npair, HEAD_DIM, LANES), lambda i: (i, 0, 0, 0)),
                   pl.BlockSpec((npair, HEAD_DIM, LANES), lambda i: (0, 0, 0))],
        out_shape=[jax.ShapeDtypeStruct((t, dr), F32), jax.ShapeDtypeStruct((t, npair, HEAD_DIM, LANES), F32),
                   jax.ShapeDtypeStruct((npair, HEAD_DIM, LANES), F32)],
        scratch_shapes=[state, state, tiles, tiles],
        compiler_params=_params(("arbitrary",)),
    )(rem, w, wr, k, r, v, rem, w, wr, k, r)


def _scan_bwd(rem, w, wr, k, r, v, hist, last, dy):
    t, dr = v.shape
    npair, nq = dr // LANES, dr // WIDE
    tb = _pick(t, (SCAN_BLOCK,))
    nblk = t // tb

    def body(rem_ref, w_ref, wr_ref, k_ref, r_ref, v_ref, hist_ref, last_ref, dy_ref, *rest):
        prev_refs, out_refs = rest[:N_COL], rest[N_COL:2 * N_COL]
        dv_ref, ds_ref, next_ref, tiles_a, tiles_b = rest[2 * N_COL:]
        eye, ones_bf = _wide_eye(), _wide_ones()
        col_refs = (rem_ref, w_ref, wr_ref, k_ref, r_ref)

        @pl.when(pl.program_id(0) == 0)
        def _():
            ds_ref[...] = jnp.zeros_like(ds_ref)
            next_ref[...] = last_ref[...]
            tiles_a[...] = _col_tiles(col_refs, tb - 1, nq, eye, ones_bf)

        def step(i, tiles_ref):
            grads = [[None] * npair for _ in range(N_COL)]
            v_full, dy_full, dv_rows = v_ref[pl.ds(i, 1), :], dy_ref[pl.ds(i, 1), :], []
            for p in range(npair):
                lanes = slice(p * LANES, (p + 1) * LANES)
                s = hist_ref[i, p]
                c_rem, c_w, c_wr, c_k, c_r = [_pair_tile(tiles_ref, n, p, nq) for n in range(N_COL)]
                v_row, dy_row = v_full[:, lanes], dy_full[:, lanes]
                s_b = _bf16_round(s)
                s2_b = next_ref[p]
                next_ref[p] = s_b
                sa = jnp.sum(s_b * c_rem, axis=0, keepdims=True)
                d2 = ds_ref[p] + c_r * dy_row
                dsa = jnp.sum(d2 * c_wr, axis=0, keepdims=True)
                dv_rows.append(jnp.sum(d2 * c_k, axis=0, keepdims=True))
                ds_ref[p] = d2 * c_w + c_rem * dsa
                for n, tile in enumerate((s_b * dsa, d2 * s, d2 * sa, d2 * v_row, s2_b * dy_row)):
                    grads[n][p] = tile.astype(BF16)
            wide = [jnp.concatenate(grads[n][2 * q:2 * q + 2], axis=1) for n in range(N_COL) for q in range(nq)]
            sums = jnp.dot(jnp.concatenate(wide, axis=0), ones_bf, preferred_element_type=F32)
            sums = sums.reshape(N_COL * nq, HEAD_DIM, WIDE)
            rows = jnp.sum(sums * eye.astype(F32)[None], axis=1)
            dv_ref[pl.ds(i, 1), :] = jnp.concatenate(dv_rows, axis=1)
            for n, o_ref in enumerate(out_refs):
                o_ref[pl.ds(i, 1), :] = jnp.concatenate([rows[n * nq + q:n * nq + q + 1] for q in range(nq)], axis=1)

        def two_steps(i, ahead_refs, ahead_row):
            tiles_b[...] = _col_tiles(col_refs, i - 1, nq, eye, ones_bf)
            step(i, tiles_a)
            tiles_a[...] = _col_tiles(ahead_refs, ahead_row, nq, eye, ones_bf)
            step(i - 1, tiles_b)

        def loop_body(m, carry):
            i = tb - 1 - 2 * m
            two_steps(i, col_refs, i - 2)
            return carry

        lax.fori_loop(0, tb // 2 - 1, loop_body, 0, unroll=3)
        two_steps(1, prev_refs, 7)

    blk = pl.BlockSpec((tb, dr), lambda i: (nblk - 1 - i, 0))
    prv = pl.BlockSpec((8, dr), lambda i: (jnp.maximum((nblk - 1 - i) * (tb // 8) - 1, 0), 0))
    tiles = pltpu.VMEM((N_COL * nq, HEAD_DIM, WIDE), F32)
    state = pltpu.VMEM((npair, HEAD_DIM, LANES), F32)
    return pl.pallas_call(
        body, name="rwkv_scan_bwd", grid=(nblk,),
        in_specs=[blk] * 6 + [pl.BlockSpec((tb, npair, HEAD_DIM, LANES), lambda i: (nblk - 1 - i, 0, 0, 0)),
                              pl.BlockSpec((npair, HEAD_DIM, LANES), lambda i: (0, 0, 0)), blk] + [prv] * N_COL,
        out_specs=[blk] * 6,
        out_shape=[jax.ShapeDtypeStruct((t, dr), F32)] * 6,
        scratch_shapes=[state, state, tiles, tiles],
        compiler_params=_params(("arbitrary",)),
    )(rem, w, wr, k, r, v, hist, last, dy, rem, w, wr, k, r)


SB_BLOCK = 256
SB_HEADS = 8
SB_HEADS_BWD = 4
NT_DIMS = (((1,), (1,)), ((), ()))
TN_DIMS = (((0,), (0,)), ((), ()))
SB_SCALE = 1.0 / math.sqrt(HEAD_DIM)


def _dot2(x, tri):
    hi = x.astype(BF16)
    mid = (x - hi.astype(F32)).astype(BF16)
    return jnp.dot(hi, tri, preferred_element_type=F32) + jnp.dot(mid, tri, preferred_element_type=F32)


def _sb_block_iotas(bs):
    return lax.broadcasted_iota(jnp.int32, (bs, bs), 0), lax.broadcasted_iota(jnp.int32, (bs, bs), 1)


def _when_step(group, block):
    return pl.when(jnp.logical_and(pl.program_id(0) == group, pl.program_id(1) == block))


def _sb_fwd(q, k, v, shards):
    h, t, d = q.shape
    bs = _pick(t, (SB_BLOCK,))
    nh = _pick(h, (SB_HEADS,))
    ngroup, nblock = h // nh, t // bs
    nw = len(shards)
    heights = [s.shape[0] for s in shards]

    def body(q_ref, k_ref, v_ref, *rest):
        w_in, (o_ref, l_ref), w_out, sems = rest[:nw], rest[nw:nw + 2], rest[nw + 2:2 * nw + 2], rest[2 * nw + 2:]

        @_when_step(0, 0)
        def _():
            _gather_phase("issue", w_in, w_out, sems, heights)

        @_when_step(ngroup - 1, (2 * nblock) // 3)
        def _():
            _gather_phase("forward", w_in, w_out, sems, heights)

        qi = pl.program_id(1)
        ri, ci = _sb_block_iotas(bs)
        tri_ge = (ri >= ci).astype(BF16)
        causal = ci < ri
        qv = [q_ref[hh] for hh in range(nh)]

        def blocks(j, masked, carry):
            accs, tails = carry[:nh], carry[nh:]
            heads = range(nh)
            rows = pl.ds(pl.multiple_of(j * bs, bs), bs)
            z = [lax.dot_general(qv[hh], k_ref[hh, rows, :], NT_DIMS, preferred_element_type=F32) for hh in heads]
            log1m = [-_softplus(z[hh]) for hh in heads]
            if masked:
                log1m = [jnp.where(causal, x, 0.0) for x in log1m]
            cs = [_dot2(log1m[hh], tri_ge) for hh in heads]
            a = [jnp.exp(z[hh] + cs[hh] + tails[hh]) for hh in heads]
            if masked:
                a = [jnp.where(causal, x, 0.0) for x in a]
            accs = [accs[hh] + jnp.dot(a[hh].astype(BF16), v_ref[hh, rows, :], preferred_element_type=F32)
                    for hh in heads]
            return tuple(accs) + tuple(tails[hh] + cs[hh][:, 0:1] for hh in heads)

        carry = blocks(qi, True, (jnp.zeros((bs, d), F32),) * nh + (jnp.zeros((bs, 1), F32),) * nh)
        carry = lax.fori_loop(0, qi, lambda n, c: blocks(qi - 1 - n, False, c), carry)
        for hh in range(nh):
            o_ref[hh] = carry[hh]
            l_ref[hh] = jnp.broadcast_to(carry[nh + hh], (bs, d))

        @_when_step(ngroup - 1, nblock - 1)
        def _():
            _gather_phase("finish", w_in, w_out, sems, heights)

    qs = pl.BlockSpec((nh, bs, d), lambda hh, i: (hh, i, 0))
    ks = pl.BlockSpec((nh, t, d), lambda hh, i: (hh, 0, 0), pipeline_mode=pl.Buffered(1))
    res = pl.pallas_call(
        body, name="sb_attn_fwd", grid=(ngroup, nblock),
        in_specs=[qs, ks, ks] + [ANY] * nw, out_specs=[qs, qs] + [ANY] * nw,
        out_shape=[jax.ShapeDtypeStruct((h, t, d), F32)] * 2 + _gather_out_shapes(shards),
        scratch_shapes=_gather_sems(nw),
        compiler_params=_params(("arbitrary", "arbitrary")),
    )(q, k, v, *shards)
    return res[0], res[1], res[2:]


def _sb_bwd(q, k, v, lsum, do, parts):
    h, t, d = q.shape
    bs = _pick(t, (SB_BLOCK,))
    nh = _pick(h, (SB_HEADS_BWD,))
    ngroup, nblock = h // nh, t // bs
    nw = len(parts)

    def body(q_ref, k_ref, v_ref, l_ref, do_ref, *rest):
        p_in, (dq_ref, dk_ref, dv_ref), p_out, sems = rest[:nw], rest[nw:nw + 3], rest[nw + 3:2 * nw + 3], rest[2 * nw + 3:]

        @_when_step(0, 0)
        def _():
            _exchange_phase("issue", p_in, p_out, sems)

        qi = pl.program_id(1)

        @pl.when(qi == 0)
        def _():
            dk_ref[...] = jnp.zeros_like(dk_ref)
            dv_ref[...] = jnp.zeros_like(dv_ref)

        ri, ci = _sb_block_iotas(bs)
        tri_lt = (ri < ci).astype(BF16)
        causal = ci < ri
        qv = [q_ref[hh] for hh in range(nh)]
        dob = [do_ref[hh].astype(BF16) for hh in range(nh)]
        ltot = [l_ref[hh][:, 0:1] for hh in range(nh)]

        def blocks(j, masked, carry):
            dq, pc, ec = carry[:nh], carry[nh:2 * nh], carry[2 * nh:]
            heads = range(nh)
            rows = pl.ds(pl.multiple_of(j * bs, bs), bs)
            z = [lax.dot_general(qv[hh], k_ref[hh, rows, :], NT_DIMS, preferred_element_type=F32) for hh in heads]
            da = [lax.dot_general(dob[hh], v_ref[hh, rows, :], NT_DIMS, preferred_element_type=F32) for hh in heads]
            nsp = [-_softplus(z[hh]) for hh in heads]
            log1m = [jnp.where(causal, x, 0.0) for x in nsp] if masked else nsp
            below = [_dot2(log1m[hh], tri_lt) + pc[hh] for hh in heads]
            a = [jnp.exp(z[hh] + (ltot[hh] - below[hh])) for hh in heads]
            if masked:
                a = [jnp.where(causal, x, 0.0) for x in a]
            e = [a[hh] * da[hh] for hh in heads]
            ebelow = [_dot2(e[hh], tri_lt) + ec[hh] for hh in heads]
            dz = [e[hh] * jnp.exp(nsp[hh]) - jnp.exp(z[hh] + nsp[hh]) * ebelow[hh] for hh in heads]
            if masked:
                dz = [jnp.where(causal, x, 0.0) for x in dz]
            dzb = [x.astype(BF16) for x in dz]
            for hh in heads:
                dv_ref[hh, rows, :] += lax.dot_general(a[hh].astype(BF16), dob[hh], TN_DIMS,
                                                       preferred_element_type=F32)
            for hh in heads:
                dk_ref[hh, rows, :] += lax.dot_general(dzb[hh], qv[hh], TN_DIMS, preferred_element_type=F32)
            dq = [dq[hh] + jnp.dot(dzb[hh], k_ref[hh, rows, :], preferred_element_type=F32) for hh in heads]
            pc = [pc[hh] + jnp.sum(log1m[hh], axis=1, keepdims=True) for hh in heads]
            ec = [ec[hh] + jnp.sum(e[hh], axis=1, keepdims=True) for hh in heads]
            return tuple(dq) + tuple(pc) + tuple(ec)

        zcol = jnp.zeros((bs, 1), F32)
        carry = lax.fori_loop(0, qi, lambda j, c: blocks(j, False, c),
                              (jnp.zeros((bs, d), F32),) * nh + (zcol,) * (2 * nh))
        carry = blocks(qi, True, carry)
        for hh in range(nh):
            dq_ref[hh] = carry[hh]

        @_when_step(ngroup - 1, nblock - 1)
        def _():
            _exchange_phase("finish", p_in, p_out, sems)

    qs = pl.BlockSpec((nh, bs, d), lambda hh, i: (hh, i, 0))
    ks = pl.BlockSpec((nh, t, d), lambda hh, i: (hh, 0, 0), pipeline_mode=pl.Buffered(1))
    res = pl.pallas_call(
        body, name="sb_attn_bwd", grid=(ngroup, nblock),
        in_specs=[qs, ks, ks, qs, qs] + [ANY] * nw, out_specs=[qs, ks, ks] + [ANY] * nw,
        out_shape=[jax.ShapeDtypeStruct((h, t, d), F32)] * 3 + [jax.ShapeDtypeStruct(p.shape, p.dtype) for p in parts],
        scratch_shapes=_exchange_sems(nw),
        compiler_params=_params(("arbitrary", "arbitrary")),
    )(q, k, v, lsum, do, *parts)
    return res[0], res[1], res[2], res[3:]


ANY = pl.BlockSpec(memory_space=pl.ANY)
IN_VMEM = pl.BlockSpec(memory_space=pltpu.VMEM)


def _coords():
    return lax.axis_index("x"), lax.axis_index("y"), lax.axis_index("c")


def _flip(v, bit):
    return 1 - v if bit else v


def _remote(src, dst, send_sem, recv_sem, device):
    return pltpu.make_async_remote_copy(src_ref=src, dst_ref=dst, send_sem=send_sem, recv_sem=recv_sem,
                                        device_id=device, device_id_type=MESH)


def _all_gather8(name, blk):
    m, n = blk.shape

    def body(x_ref, o_ref, send_sems, recv_sems, local_sem):
        x, y, c = _coords()
        own = pltpu.make_async_copy(x_ref, o_ref.at[4 * x + 2 * y + c], local_sem)
        own.start()
        peers = []
        for bits in range(1, N_DEV):
            px, py, pc = _flip(x, (bits >> 2) & 1), _flip(y, (bits >> 1) & 1), _flip(c, bits & 1)
            peers.append((px, py, pc))
        sends = []
        for k, peer in enumerate(peers):
            cp = _remote(x_ref, o_ref.at[4 * x + 2 * y + c], send_sems.at[k], recv_sems.at[k], peer)
            cp.start()
            sends.append(cp)
        for k, (px, py, pc) in enumerate(peers):
            slot = o_ref.at[4 * px + 2 * py + pc]
            _remote(slot, slot, send_sems.at[k], recv_sems.at[k], (px, py, pc)).wait_recv()
        for cp in sends:
            cp.wait_send()
        own.wait()

    return pl.pallas_call(
        body, name=name, out_shape=jax.ShapeDtypeStruct((N_DEV, m, n), blk.dtype),
        in_specs=[IN_VMEM], out_specs=IN_VMEM,
        scratch_shapes=[pltpu.SemaphoreType.DMA((N_DEV - 1,)), pltpu.SemaphoreType.DMA((N_DEV - 1,)),
                        pltpu.SemaphoreType.DMA],
        compiler_params=_params(),
    )(blk)


def _gather_weights(shards):
    nw = len(shards)
    heights = [s.shape[0] for s in shards]

    def body(*refs):
        ins, outs, sems = refs[:nw], refs[nw:2 * nw], refs[2 * nw:]
        _gather_phase("issue", ins, outs, sems, heights)
        _gather_phase("forward", ins, outs, sems, heights)
        _gather_phase("finish", ins, outs, sems, heights)

    return pl.pallas_call(
        body, name="gather_weights",
        out_shape=_gather_out_shapes(shards),
        in_specs=[ANY] * nw, out_specs=[ANY] * nw,
        scratch_shapes=_gather_sems(nw),
        compiler_params=_params(),
    )(*shards)


def _gather_out_shapes(shards):
    return [jax.ShapeDtypeStruct((N_CHIPS,) + s.shape, s.dtype) for s in shards]


def _gather_sems(nw):
    return [pltpu.SemaphoreType.DMA((3 * nw,))] * 4


def _gather_phase(phase, ins, outs, sems, heights):
    ici_send, ici_recv, d2d_send, d2d_recv = sems
    x, y, c = _coords()
    chip = 2 * x + y
    sibling = (x, y, 1 - c)
    for w, height in enumerate(heights):
        half = height // 2
        mine, theirs = pl.ds(c * half, half), pl.ds((1 - c) * half, half)
        for j, (a, b) in enumerate(XY_MASKS):
            px, py = _flip(x, a), _flip(y, b)
            k = 3 * w + j
            over_ici = _remote(ins[w].at[mine], outs[w].at[chip, mine], ici_send.at[k], ici_recv.at[k], (px, py, c))
            landed = outs[w].at[2 * px + py, mine]
            onward = _remote(landed, landed, d2d_send.at[k], d2d_recv.at[k], sibling)
            if phase == "issue":
                over_ici.start()
            elif phase == "forward":
                _remote(landed, landed, ici_send.at[k], ici_recv.at[k], (px, py, c)).wait_recv()
                onward.start()
            else:
                slot = outs[w].at[2 * px + py, theirs]
                _remote(slot, slot, d2d_send.at[k], d2d_recv.at[k], sibling).wait_recv()
                over_ici.wait_send()
                onward.wait_send()


def _pair_exchange(name, grads):
    nw = len(grads)

    def body(*refs):
        ins, outs = refs[:nw], refs[nw:2 * nw]
        send_sems, recv_sems = refs[2 * nw:]
        x, y, c = _coords()
        sibling = (x, y, 1 - c)
        sends = []
        for w in range(nw):
            half = grads[w].shape[1] // 2
            cp = _remote(ins[w].at[:, pl.ds((1 - c) * half, half)], outs[w], send_sems.at[w], recv_sems.at[w], sibling)
            cp.start()
            sends.append(cp)
        for w in range(nw):
            _remote(outs[w], outs[w], send_sems.at[w], recv_sems.at[w], sibling).wait_recv()
        for cp in sends:
            cp.wait_send()

    return pl.pallas_call(
        body, name=name,
        out_shape=[jax.ShapeDtypeStruct((N_CHIPS, g.shape[1] // 2, g.shape[2]), g.dtype) for g in grads],
        in_specs=[ANY] * nw, out_specs=[ANY] * nw,
        scratch_shapes=[pltpu.SemaphoreType.DMA((nw,))] * 2,
        compiler_params=_params(),
    )(*grads)


def _pair_phase(phase, ins, outs, sems, heights):
    send_sems, recv_sems = sems
    x, y, c = _coords()
    sibling = (x, y, 1 - c)
    for w, height in enumerate(heights):
        half = height // 2
        send = _remote(ins[w].at[:, pl.ds((1 - c) * half, half)], outs[w], send_sems.at[w], recv_sems.at[w], sibling)
        if phase == "issue":
            send.start()
        else:
            _remote(outs[w], outs[w], send_sems.at[w], recv_sems.at[w], sibling).wait_recv()
            send.wait_send()


def _chip_exchange(name, parts):
    nw = len(parts)

    def body(*refs):
        ins, outs, sems = refs[:nw], refs[nw:2 * nw], refs[2 * nw:]
        _exchange_phase("issue", ins, outs, sems)
        _exchange_phase("finish", ins, outs, sems)

    return pl.pallas_call(
        body, name=name,
        out_shape=[jax.ShapeDtypeStruct(p.shape, p.dtype) for p in parts],
        in_specs=[ANY] * nw, out_specs=[ANY] * nw,
        scratch_shapes=_exchange_sems(nw),
        compiler_params=_params(),
    )(*parts)


def _exchange_sems(nw):
    return [pltpu.SemaphoreType.DMA((3 * nw,))] * 2


def _exchange_phase(phase, ins, outs, sems):
    send_sems, recv_sems = sems
    x, y, c = _coords()
    chip = 2 * x + y
    for w in range(len(ins)):
        for j, (a, b) in enumerate(XY_MASKS):
            px, py = _flip(x, a), _flip(y, b)
            k = 3 * w + j
            send = _remote(ins[w].at[2 * px + py], outs[w].at[chip], send_sems.at[k], recv_sems.at[k], (px, py, c))
            if phase == "issue":
                send.start()
            else:
                slot = outs[w].at[2 * px + py]
                _remote(slot, slot, send_sems.at[k], recv_sems.at[k], (px, py, c)).wait_recv()
                send.wait_send()


def _pair_share(shards):
    nw = len(shards)

    def body(*refs):
        bufs = refs[nw:2 * nw]
        send_sems, recv_sems = refs[2 * nw:]
        x, y, c = _coords()
        sibling = (x, y, 1 - c)
        sends = []
        for w in range(nw):
            half = shards[w].shape[0] // 2
            mine = bufs[w].at[pl.ds(c * half, half)]
            cp = _remote(mine, mine, send_sems.at[w], recv_sems.at[w], sibling)
            cp.start()
            sends.append(cp)
        for w in range(nw):
            half = shards[w].shape[0] // 2
            theirs = bufs[w].at[pl.ds((1 - c) * half, half)]
            _remote(theirs, theirs, send_sems.at[w], recv_sems.at[w], sibling).wait_recv()
        for cp in sends:
            cp.wait_send()

    return pl.pallas_call(
        body, name="grad_pair_share",
        out_shape=[jax.ShapeDtypeStruct(s.shape, s.dtype) for s in shards],
        in_specs=[ANY] * nw, out_specs=[ANY] * nw,
        input_output_aliases={w: w for w in range(nw)},
        scratch_shapes=[pltpu.SemaphoreType.DMA((nw,))] * 2,
        compiler_params=_params(),
    )(*shards)


TILE_BYTES = 2 * 1024 * 1024


def _row_tile(rows, cols, mult=8):
    best = None
    for tr in range(mult, rows + 1, mult):
        if rows % tr == 0 and tr * cols * 4 <= TILE_BYTES:
            best = tr
    return best if best is not None else rows


def _pair_add(name, grad, other, place):
    _, half, cols = other.shape
    tr = _row_tile(half, cols, mult=16)
    nb = half // tr

    def body(place_ref, g_ref, o_ref, out_ref):
        out_ref[...] = (g_ref[...] + o_ref[...]).astype(out_ref.dtype)

    return pl.pallas_call(
        body, name=name, out_shape=jax.ShapeDtypeStruct(other.shape, BF16),
        grid_spec=pltpu.PrefetchScalarGridSpec(
            num_scalar_prefetch=1, grid=(N_CHIPS, nb),
            in_specs=[pl.BlockSpec((None, tr, cols), lambda s, i, place_ref: (s, place_ref[0] * nb + i, 0)),
                      pl.BlockSpec((None, tr, cols), lambda s, i, place_ref: (s, i, 0))],
            out_specs=pl.BlockSpec((None, tr, cols), lambda s, i, place_ref: (s, i, 0))),
        compiler_params=_params(("parallel", "parallel")),
    )(place, grad, other)


def _sum_chips(name, own, others, place):
    _, half, cols = own.shape
    tr = _row_tile(half, cols, mult=16)
    nb = half // tr

    def body(place_ref, own_ref, a_ref, b_ref, c_ref, out_ref):
        total = own_ref[...].astype(F32) + a_ref[...].astype(F32)
        out_ref[...] = (total + b_ref[...].astype(F32)) + c_ref[...].astype(F32)

    def peer(mask):
        return pl.BlockSpec((None, tr, cols), lambda i, place_ref: (place_ref[1] ^ mask, i, 0))

    return pl.pallas_call(
        body, name=name, out_shape=jax.ShapeDtypeStruct((2 * half, cols), F32),
        grid_spec=pltpu.PrefetchScalarGridSpec(
            num_scalar_prefetch=1, grid=(nb,),
            in_specs=[peer(0), peer(1), peer(2), peer(3)],
            out_specs=pl.BlockSpec((tr, cols), lambda i, place_ref: (place_ref[0] * nb + i, 0))),
        compiler_params=_params(("parallel",)),
    )(place, own, others, others, others)


def _adamw_math(w, g, m, v):
    m2 = ADAM_B1 * m + (1.0 - ADAM_B1) * g
    v2 = ADAM_B2 * v + (1.0 - ADAM_B2) * (g * g)
    m_hat = m2 / (1.0 - ADAM_B1 ** ADAM_STEP)
    v_hat = v2 / (1.0 - ADAM_B2 ** ADAM_STEP)
    delta = -ADAM_LR * (m_hat / (jnp.sqrt(v_hat) + ADAM_EPS) + ADAM_WD * w)
    return delta, m2, v2


def _adamw(name, w, g, m, v):
    rows, cols = w.shape
    tr = _row_tile(rows, cols)

    def body(w_ref, g_ref, m_ref, v_ref, d_ref, m2_ref, v2_ref):
        d_ref[...], m2_ref[...], v2_ref[...] = _adamw_math(w_ref[...], g_ref[...], m_ref[...], v_ref[...])

    blk = pl.BlockSpec((tr, cols), lambda i: (i, 0))
    return pl.pallas_call(
        body, name=name, out_shape=[jax.ShapeDtypeStruct(w.shape, F32)] * 3, grid=(rows // tr,),
        in_specs=[blk] * 4, out_specs=[blk] * 3,
        compiler_params=_params(("parallel",)),
    )(w, g, m, v)


def _small_update(gathered, w, m, v):
    def body(gs_ref, w_ref, m_ref, v_ref, g_ref, d_ref, m2_ref, v2_ref):
        g = gs_ref[0]
        for dev in range(1, N_DEV):
            g = g + gs_ref[dev]
        g_ref[...] = g
        d_ref[...], m2_ref[...], v2_ref[...] = _adamw_math(w_ref[...], g, m_ref[...], v_ref[...])

    return pl.pallas_call(
        body, name="small_update", out_shape=[jax.ShapeDtypeStruct(w.shape, F32)] * 4,
        compiler_params=_params(),
    )(gathered, w, m, v)


def _ada_mod(c_all, w_shard, b_shard):
    d, n = w_shard.shape
    tn = _pick(n, (512, 256, 128))

    def body(c_ref, w_ref, b_ref, o_ref):
        act = _silu(c_ref[...]).astype(BF16)
        o_ref[...] = jnp.dot(act, w_ref[...].astype(BF16), preferred_element_type=F32) + b_ref[...]

    return pl.pallas_call(
        body, name="ada_mod", out_shape=jax.ShapeDtypeStruct((c_all.shape[0], n), F32), grid=(n // tn,),
        in_specs=[pl.BlockSpec(c_all.shape, lambda j: (0, 0)), pl.BlockSpec((d, tn), lambda j: (0, j)),
                  pl.BlockSpec((1, tn), lambda j: (0, j))],
        out_specs=pl.BlockSpec((c_all.shape[0], tn), lambda j: (0, j)),
        compiler_params=_params(("parallel",)),
    )(c_all, w_shard, b_shard)


def _ada_grad(c_pad, dmod_pad):
    rows, d = c_pad.shape
    n = dmod_pad.shape[1]
    tn = _pick(n, (512, 256, 128))

    def body(c_ref, g_ref, o_ref):
        act = _silu(c_ref[...]).astype(BF16)
        o_ref[...] = lax.dot_general(act, g_ref[...].astype(BF16), TN_DIMS, preferred_element_type=F32)

    return pl.pallas_call(
        body, name="ada_grad", out_shape=jax.ShapeDtypeStruct((d, n), F32), grid=(n // tn,),
        in_specs=[pl.BlockSpec((rows, d), lambda j: (0, 0)), pl.BlockSpec((rows, tn), lambda j: (0, j))],
        out_specs=pl.BlockSpec((d, tn), lambda j: (0, j)),
        compiler_params=_params(("parallel",)),
    )(c_pad, dmod_pad)


WEIGHTS = ['w_ada', 'b_ada', 'norm1_gain', 'norm2_gain', 'w_in', 'mu_rkv', 'mu_w', 'mu_a', 'mu_g', 'w0', 'w1',
           'w2', 'a0', 'a1', 'a2', 'g1', 'g2', 'k_k', 'k_a', 'r_k', 'ln_x_gain', 'ln_x_bias', 'q_norm_gain',
           'k_norm_gain', 'w_out', 'w_gate_up', 'w_down']
SMALL = ['b_ada', 'norm1_gain', 'norm2_gain', 'mu_rkv', 'mu_w', 'mu_a', 'mu_g', 'w0', 'a0', 'k_k', 'k_a', 'r_k',
         'ln_x_gain', 'ln_x_bias', 'q_norm_gain', 'k_norm_gain']
PACK_ROWS = 8
LOSS_SLOT = LANES


def _shift_down(a):
    return jnp.pad(a[:-1], ((1, 0), (0, 0)))


def _shift_up(a):
    return jnp.pad(a[1:], ((0, 1), (0, 0)))


def _pack_small(vals):
    flat = jnp.concatenate([v.reshape(1, -1) for v in vals], axis=1)
    unit = PACK_ROWS * LANES
    total = -(-flat.shape[1] // unit) * unit
    flat = jnp.pad(flat, ((0, 0), (0, total - flat.shape[1])))
    return flat.reshape(PACK_ROWS, total // PACK_ROWS)


def kernel(x, c, w_ada, b_ada, norm1_gain, norm2_gain, w_in, mu_rkv, mu_w, mu_a, mu_g, w0, w1, w2, a0, a1, a2, g1, g2, k_k, k_a, r_k, ln_x_gain, ln_x_bias, q_norm_gain, k_norm_gain, w_out, w_gate_up, w_down, loss_target, m_w_ada, m_b_ada, m_norm1_gain, m_norm2_gain, m_w_in, m_mu_rkv, m_mu_w, m_mu_a, m_mu_g, m_w0, m_w1, m_w2, m_a0, m_a1, m_a2, m_g1, m_g2, m_k_k, m_k_a, m_r_k, m_ln_x_gain, m_ln_x_bias, m_q_norm_gain, m_k_norm_gain, m_w_out, m_w_gate_up, m_w_down, v_w_ada, v_b_ada, v_norm1_gain, v_norm2_gain, v_w_in, v_mu_rkv, v_mu_w, v_mu_a, v_mu_g, v_w0, v_w1, v_w2, v_a0, v_a1, v_a2, v_g1, v_g2, v_k_k, v_k_a, v_r_k, v_ln_x_gain, v_ln_x_bias, v_q_norm_gain, v_k_norm_gain, v_w_out, v_w_gate_up, v_w_down):
    given = dict(locals())
    wt = {n: given[n][0] for n in WEIGHTS}
    mom = {n: given["m_" + n][0] for n in WEIGHTS}
    var = {n: given["v_" + n][0] for n in WEIGHTS}
    for tree in (wt, mom, var):
        tree["b_ada"] = tree["b_ada"].reshape(1, -1)
        for n in SMALL[1:]:
            tree[n] = tree[n].reshape(1, -1)

    ax, ay, ac = _coords()
    chip = 2 * ax + ay
    dev = 4 * ax + 2 * ay + ac
    xs, target = x[0], loss_target[0]
    t, d = xs.shape
    dr = wt["w0"].shape[1]
    ds = d - dr
    nh = ds // HEAD_DIM
    dff = wt["w_down"].shape[0] * N_CHIPS
    n_ada = wt["w_ada"].shape[1]
    lw, la, lg = wt["w1"].shape[1], wt["a1"].shape[1], wt["g1"].shape[1]

    def lora_a(tree):
        return jnp.concatenate([tree["w1"], tree["a1"], tree["g1"]], axis=1)

    def lora_b(tree):
        return jnp.concatenate([tree["w2"], tree["a2"], tree["g2"]], axis=0)

    def with_own_slot(gathered, own):
        return [lax.dynamic_update_slice(full, shard[None], (chip, 0, 0)) for full, shard in zip(gathered, own)]

    first_shards = [s.astype(BF16) for s in (wt["w_in"], lora_a(wt), lora_b(wt))]
    later_shards = [s.astype(BF16) for s in (wt["w_out"], wt["w_gate_up"], wt["w_down"])]
    full_in, full_la, full_lb = with_own_slot(_gather_weights(first_shards), first_shards)
    full_la = full_la.reshape(d, lw + la + lg).astype(F32)
    full_lb = full_lb.transpose(1, 0, 2).reshape(lw + la + lg, dr).astype(F32)
    w1f, a1f, g1f = full_la[:, :lw], full_la[:, lw:lw + la], full_la[:, lw + la:]
    w2f, a2f, g2f = full_lb[:lw], full_lb[lw:lw + la], full_lb[lw + la:]

    c_all = _all_gather8("gather_c", c.reshape(PACK_ROWS, d // PACK_ROWS)).reshape(N_DEV, d)
    b_shard = lax.dynamic_slice(wt["b_ada"], (0, chip * n_ada), (1, n_ada))
    mod_part = _ada_mod(c_all, wt["w_ada"], b_shard)
    mod_all = _all_gather8("gather_mod", mod_part)[::2]
    mod = lax.dynamic_slice(mod_all, (0, dev, 0), (N_CHIPS, 1, n_ada)).reshape(1, N_CHIPS * n_ada)
    sh1, sc1, gt1, sh2, sc2, gt2 = [mod[:, i * d:(i + 1) * d] for i in range(6)]

    h, h_bf = _rowwise("norm1", _fn_norm1, [xs], [wt["norm1_gain"], sc1, sh1], [(d, F32), (d, BF16)], 256)
    hp = _shift_down(h)
    p = _matmul("mm_in", h_bf, full_in, b_shards=True, tm=512, tn=1536, tk=2048, n_outer=True)
    p_rkv, p_sb = (p, 3 * dr, 0), (p, 3 * ds, 1)
    pp = _shift_down(p[:, :3 * dr])
    pre_rows = [h, hp, p_rkv, pp]
    pre_params = [wt["mu_rkv"], wt["mu_w"], wt["mu_a"], wt["mu_g"], wt["w0"], wt["a0"], wt["k_k"], wt["k_a"],
                  w1f, w2f, a1f, a2f, g1f, g2f]
    pre = _rowwise("rwkv_pre", _fn_rwkv_pre, pre_rows, pre_params, [(dr, F32)] * 7, 128)
    r_, w_, k2, v_, rem, wr, g_ = pre
    y_raw, hist, s_last = _scan_fwd(rem, w_, wr, k2, r_, v_)
    post_rows = [y_raw, r_, k2, v_, g_]
    post_params = [wt["ln_x_gain"], wt["ln_x_bias"], wt["r_k"]]

    qg = jnp.tile(wt["q_norm_gain"], (1, nh))
    kg = jnp.tile(wt["k_norm_gain"], (1, nh))
    qn, kn, vs = _rowwise("qk_norm", _fn_qk_norm, [p_sb], [qg, kg], [(ds, BF16)] * 3, 256)

    def to_heads(a):
        return a.reshape(t, nh, HEAD_DIM).transpose(1, 0, 2)

    def from_heads(a):
        return a.transpose(1, 0, 2).reshape(t, ds)

    qh, kh, vh = to_heads(qn), to_heads(kn), to_heads(vs)
    o_h, lsum, later_full = _sb_fwd(qh, kh, vh, later_shards)
    full_out, full_gu, full_down = with_own_slot(later_full, later_shards)
    full_out = full_out.reshape(d, d)
    full_down = full_down.reshape(dff, d)
    (ycat,) = _rowwise("rwkv_post", _fn_rwkv_post_cat, post_rows + [from_heads(o_h)], post_params, [(d, BF16)], 256)
    mix = _matmul("mm_out", ycat, full_out, tm=512, tn=1024, tk=2048, n_outer=True)
    norm2_params = [gt1, wt["norm2_gain"], sc2, sh2]
    x1, h2 = _rowwise("mix_norm2", _fn_mix_norm2, [xs, mix], norm2_params, [(d, F32), (d, BF16)], 256)
    gu = _matmul("mm_gate_up", h2, full_gu, b_shards=True, tm=512, tn=1408, tk=2048, n_outer=True)
    gate_up = [(gu, dff, 0), (gu, dff, 1)]
    (act,) = _rowwise("swiglu", _fn_swiglu, gate_up, [], [(dff, BF16)], 256)
    dn = _matmul("mm_down", act, full_down, tm=1024, tn=1024, tk=2816)
    loss_vec, dout, ddn, dgt2 = _loss_head("loss_head", x1, dn, target, gt2)

    dact = _matmul("mm_down_dx", ddn, full_down, tb=True, tm=512, tn=1408, tk=2048, n_outer=True)
    gw_down = _matmul("mm_down_dw", act, ddn, ta=True, tm=1408, tn=1024, tk=1024)
    dgu = _swiglu_bwd(gu, dact)
    dh2 = _matmul("mm_gate_up_dx", dgu, full_gu, tb=True, b_shards=True, tm=1024, tn=1024, tk=2816)
    gw_gu = _matmul("mm_gate_up_dw", h2, dgu, ta=True, out_shards=True, tm=1024, tn=1408, tk=1024)
    (dx_a, dmix), (dgt1, dgain2, dsc2, dsh2) = _rowwise_bwd(
        "mix_norm2_bwd", _fn_mix_norm2, [xs, mix], norm2_params, [[dout], [dh2]], [F32, BF16], [True] * 4, 128)
    gw_out = _matmul("mm_out_dw", ycat, dmix, ta=True, tm=1024, tn=1024, tk=1024)
    early_grads = [gw_out.reshape(N_CHIPS, d // N_CHIPS, d), gw_gu, gw_down.reshape(N_CHIPS, dff // N_CHIPS, d)]
    dycat, early_from_sibling = _matmul("mm_out_dx", dmix, full_out, tb=True, tm=512, tn=1024, tk=2048,
                                        n_outer=True, pair=early_grads)
    (dy_raw, dr_f, dk_f, dv_f, dg), (dlng, dlnb, drk) = _rowwise_bwd(
        "rwkv_post_bwd", _fn_rwkv_post, post_rows, post_params, [[(dycat, dr, 0)]], [F32] * 5, [True] * 3, 128)
    place = jnp.stack([ac, chip]).astype(jnp.int32)

    def pair_reduce(tag, names, grads):
        from_sibling = _pair_exchange("grad_pair_exchange_" + tag, grads)
        return [_pair_add("pair_add_" + n, g, o, place) for n, g, o in zip(names, grads, from_sibling)]

    early_names = ["w_out", "w_gate_up", "w_down"]
    early_sums = [_pair_add("pair_add_" + n, g, o, place)
                  for n, g, o in zip(early_names, early_grads, early_from_sibling)]
    do_h = to_heads(dycat[:, dr:])
    dqh, dkh, dvh, early_from_chips = _sb_bwd(qh, kh, vh, lsum, do_h, early_sums)
    drem_s, dw_s, dwr_s, dk_s, dr_s, dv_s = _scan_bwd(rem, w_, wr, k2, r_, v_, hist, s_last, dy_raw)
    (dp_sb,), (dqg, dkg) = _rowwise_bwd(
        "qk_norm_bwd", _fn_qk_norm, [p_sb], [qg, kg], [[from_heads(dqh)], [from_heads(dkh)], [from_heads(dvh)]],
        [F32], [True, True], 128)
    pre_cts = [[dr_s, dr_f], [dw_s], [dk_s, dk_f], [dv_s, dv_f], [drem_s], [dwr_s], [dg]]
    (dh_a, dhp, dp_rkv, dpp), pre_g = _rowwise_bwd(
        "rwkv_pre_bwd", _fn_rwkv_pre, pre_rows, pre_params, pre_cts, [F32] * 4, [True] * 14, 128)
    dp = jnp.concatenate([dp_rkv + _shift_up(dpp), dp_sb], axis=1).astype(BF16)
    gw_in = _matmul("mm_in_dw", h_bf, dp, ta=True, out_shards=True, tm=1024, tn=1536, tk=1024)

    g_mu_rkv, g_mu_w, g_mu_a, g_mu_g, g_w0, g_a0, g_kk, g_ka, gw1, gw2, ga1, ga2, gg1, gg2 = pre_g
    g_la = jnp.concatenate([gw1, ga1, gg1], axis=1).reshape(N_CHIPS, d // N_CHIPS, lw + la + lg)
    g_lb = jnp.concatenate([gw2, ga2, gg2], axis=0)
    g_lb = g_lb.reshape(lw + la + lg, N_CHIPS, dr // N_CHIPS).transpose(1, 0, 2)
    late_names = ["w_in", "lora_a", "lora_b"]
    late_sums = pair_reduce("late", late_names, [gw_in, g_la, g_lb])
    dh_mm, late_from_chips = _matmul("mm_in_dx", dp, full_in, tb=True, b_shards=True, tm=1024, tn=1024, tk=1536,
                                     exchange=late_sums)
    (grad_x,), (dgain1, dsc1, dsh1) = _rowwise_bwd(
        "norm1_bwd", _fn_norm1, [xs], [wt["norm1_gain"], sc1, sh1], [[dh_a, dh_mm, _shift_up(dhp)], []],
        [F32], [True] * 3, 128, add_to_first=[dx_a])

    halves = [_sum_chips("chip_sum_" + n, p, q, place)
              for n, p, q in zip(early_names + late_names, early_sums + late_sums,
                                 list(early_from_chips) + list(late_from_chips))]
    r_out, r_gu, r_down, r_in, r_la, r_lb = _pair_share(halves)

    dmod = jnp.concatenate([dsh1, dsc1, dgt1, dsh2, dsc2, dgt2], axis=1)
    dqg = dqg.reshape(nh, HEAD_DIM).sum(axis=0, keepdims=True)
    dkg = dkg.reshape(nh, HEAD_DIM).sum(axis=0, keepdims=True)
    small_g = [dmod, dgain1, dgain2, g_mu_rkv, g_mu_w, g_mu_a, g_mu_g, g_w0, g_a0, g_kk, g_ka, drk, dlng, dlnb,
               dqg, dkg]
    lead = jnp.zeros((1, LOSS_SLOT), F32)
    packed = _pack_small([loss_vec] + small_g)
    gathered = _all_gather8("gather_small", packed)
    sm_g, sm_d, sm_m, sm_v = _small_update(gathered, _pack_small([lead] + [wt[n] for n in SMALL]),
                                           _pack_small([lead] + [mom[n] for n in SMALL]),
                                           _pack_small([lead] + [var[n] for n in SMALL]))
    loss = sm_g.reshape(-1)[0]

    def unpack(packed_arr):
        flat, out, pos = packed_arr.reshape(-1), {}, LOSS_SLOT
        for n in SMALL:
            size = wt[n].size
            out[n] = flat[pos:pos + size]
            pos += size
        return out

    res = {"grad": unpack(sm_g), "delta": unpack(sm_d), "m": unpack(sm_m), "v": unpack(sm_v)}

    dmod_all = gathered.reshape(N_DEV, -1)[:, LOSS_SLOT:LOSS_SLOT + N_CHIPS * n_ada]
    dmod_cols = lax.dynamic_slice(dmod_all, (0, chip * n_ada), (N_DEV, n_ada))
    pad8 = ((0, N_DEV), (0, 0))
    res["grad"]["w_ada"] = _ada_grad(jnp.pad(c_all, pad8), jnp.pad(dmod_cols, pad8))

    res["grad"].update(w_in=r_in, w_out=r_out, w_gate_up=r_gu, w_down=r_down)
    for n in ("w_ada", "w_in", "w_out", "w_gate_up", "w_down"):
        res["delta"][n], res["m"][n], res["v"][n] = _adamw("adamw_" + n, wt[n], res["grad"][n], mom[n], var[n])
    la_d, la_m, la_v = _adamw("adamw_lora_a", lora_a(wt), r_la, lora_a(mom), lora_a(var))
    lb_d, lb_m, lb_v = _adamw("adamw_lora_b", lora_b(wt), r_lb, lora_b(mom), lora_b(var))
    for key, pa, pb in (("grad", r_la, r_lb), ("delta", la_d, lb_d), ("m", la_m, lb_m), ("v", la_v, lb_v)):
        res[key].update(w1=pa[:, :lw], a1=pa[:, lw:lw + la], g1=pa[:, lw + la:],
                        w2=pb[:lw], a2=pb[lw:lw + la], g2=pb[lw + la:])

    outs = [loss, grad_x[None]]
    for key in ("grad", "delta", "m", "v"):
        outs += [res[key][n].reshape(given[n].shape) for n in WEIGHTS]
    return tuple(outs)
```
